```python
import math
import jax
import jax.numpy as jnp
from jax import lax
import numpy as np

D_MODEL = 2048
BATCH = 2
SEQ = 4096
DEPTH = 2

GRID_W = 64
WIN_ROWS = 8
WIN_COLS = 16
Q_COLS = 16
BAND_COLS = Q_COLS + WIN_COLS
NA_HEADS = 8
NA_HEAD_DIM = 128
NA_WIDTH = NA_HEADS * NA_HEAD_DIM
CONV_CH = D_MODEL // 2
CONV_WIDTH = 3
MLA_HEADS = 8
Q_LORA = 512
KV_LORA = 512
QK_NOPE = 128
QK_ROPE = 64
V_DIM = 128
ROPE_THETA = 10000.0
Q_BLOCK = 128
SG_CH = D_MODEL // 2
SG_GROUPS = 8
CHUNK = 128
N_GROUPS = 8
EXPERTS_PER_GROUP = 8
N_EXPERTS = N_GROUPS * EXPERTS_PER_GROUP
TOP_K = 2
D_EXPERT = 768
MOE_BLOCK = 128
EPS = 1e-6
NEG_INF = -1e30
AB_IN = 3 * NA_WIDTH + 3 * CONV_CH
AB_OUT = NA_WIDTH + CONV_CH
CD_IN = Q_LORA + KV_LORA + QK_ROPE + 2 * SG_CH
CD_OUT = MLA_HEADS * V_DIM + SG_CH

kernel_name = 'hybrid_natten_conv_mla_sgu_hmoe_encoder'


def rms_norm(x, g):
    xf = x.astype(jnp.float32)
    y = xf * lax.rsqrt(jnp.mean(xf * xf, axis=-1, keepdims=True) + EPS)
    return (y * g.astype(jnp.float32)).astype(x.dtype)


def neighbourhood_attention(q, k, v, rpb):
    bsz, seq, heads, dh = q.shape
    rows = seq // GRID_W
    kr = min(WIN_ROWS, rows)
    n_cb = GRID_W // Q_COLS
    n_keys = kr * BAND_COLS
    r = np.arange(rows)
    key_rows = np.clip(r - kr // 2, 0, rows - kr)[:, None] + np.arange(kr)[None, :]
    band_start = np.clip(np.arange(n_cb) * Q_COLS - WIN_COLS // 2, 0, GRID_W - BAND_COLS)
    key_cols = band_start[:, None] + np.arange(BAND_COLS)[None, :]
    q_cols = np.arange(n_cb)[:, None] * Q_COLS + np.arange(Q_COLS)[None, :]
    col_start = np.clip(q_cols - WIN_COLS // 2, 0, GRID_W - WIN_COLS)
    in_win = (key_cols[:, None, :] >= col_start[:, :, None]) & (key_cols[:, None, :] < col_start[:, :, None] + WIN_COLS)
    valid = np.broadcast_to(in_win[:, :, None, :], (n_cb, Q_COLS, kr, BAND_COLS)).reshape(n_cb, Q_COLS, n_keys)
    dr_idx = (key_rows - r[:, None] + WIN_ROWS - 1).astype(np.int32)
    dc_idx = np.clip(key_cols[:, None, :] - q_cols[:, :, None] + WIN_COLS - 1, 0, 2 * WIN_COLS - 2).astype(np.int32)
    bias = rpb[:, dr_idx[:, None, None, :, None], dc_idx[None, :, :, None, :]]
    bias = bias.reshape(heads, rows, n_cb, Q_COLS, n_keys)
    key_tok = (key_rows[:, None, :, None] * GRID_W + key_cols[None, :, None, :]).reshape(-1).astype(np.int32)
    qg = q.transpose(0, 2, 1, 3).reshape(bsz, heads, rows, n_cb, Q_COLS, dh)
    kg = jnp.take(k.transpose(0, 2, 1, 3), key_tok, axis=2).reshape(bsz, heads, rows, n_cb, n_keys, dh)
    vg = jnp.take(v.transpose(0, 2, 1, 3), key_tok, axis=2).reshape(bsz, heads, rows, n_cb, n_keys, dh)
    s = jnp.einsum('bhrjqd,bhrjkd->bhrjqk', qg, kg).astype(jnp.float32) * (dh ** -0.5) + bias.astype(jnp.float32)
    s = jnp.where(valid, s, NEG_INF)
    p = jax.nn.softmax(s, axis=-1).astype(v.dtype)
    o = jnp.einsum('bhrjqk,bhrjkd->bhrjqd', p, vg)
    return o.reshape(bsz, heads, seq, dh).transpose(0, 2, 1, 3).reshape(bsz, seq, heads * dh)


def short_conv(z, w, b):
    seq = z.shape[1]
    pad = CONV_WIDTH // 2
    zp = jnp.pad(z, ((0, 0), (pad, CONV_WIDTH - 1 - pad), (0, 0)))
    y = b
    for tap in range(CONV_WIDTH):
        y = y + zp[:, tap:tap + seq] * w[tap]
    return y


def rope_angles(seq):
    pos = jnp.arange(seq, dtype=jnp.int32)
    row = (pos // GRID_W).astype(jnp.float32)
    col = (pos % GRID_W).astype(jnp.float32)
    half = QK_ROPE // 2
    inv = ROPE_THETA ** (-jnp.arange(0, half, 2, dtype=jnp.float32) / half)
    return row[:, None] * inv, col[:, None] * inv


def rotate(x, ang):
    m = x.shape[-1] // 2
    x1, x2 = x[..., :m], x[..., m:]
    c = jnp.cos(ang).astype(x.dtype)
    s = jnp.sin(ang).astype(x.dtype)
    return jnp.concatenate([x1 * c - x2 * s, x1 * s + x2 * c], axis=-1)


def axial_rope(x, ang_row, ang_col):
    m = x.shape[-1] // 2
    return jnp.concatenate([rotate(x[..., :m], ang_row), rotate(x[..., m:], ang_col)], axis=-1)


def mla(c_q, c_kv, k_r, q_norm, kv_norm, w_uq, w_ukv, ang_row, ang_col):
    bsz, seq, _ = c_q.shape
    q = (rms_norm(c_q, q_norm) @ w_uq).reshape(bsz, seq, MLA_HEADS, QK_NOPE + QK_ROPE)
    q_nope = q[..., :QK_NOPE]
    q_rope = axial_rope(q[..., QK_NOPE:], ang_row[:, None], ang_col[:, None])
    kv = (rms_norm(c_kv, kv_norm) @ w_ukv).reshape(bsz, seq, MLA_HEADS, QK_NOPE + V_DIM)
    k_nope = kv[..., :QK_NOPE].transpose(0, 2, 1, 3)
    vh = kv[..., QK_NOPE:].transpose(0, 2, 1, 3)
    k_rope = axial_rope(k_r, ang_row, ang_col)
    scale = (QK_NOPE + QK_ROPE) ** -0.5
    nb = seq // Q_BLOCK
    qn_b = q_nope.reshape(bsz, nb, Q_BLOCK, MLA_HEADS, QK_NOPE).transpose(1, 0, 3, 2, 4)
    qr_b = q_rope.reshape(bsz, nb, Q_BLOCK, MLA_HEADS, QK_ROPE).transpose(1, 0, 3, 2, 4)

    def attend(blk):
        qn, qr = blk
        s = jnp.einsum('bhqd,bhkd->bhqk', qn, k_nope) + jnp.einsum('bhqr,bkr->bhqk', qr, k_rope)
        p = jax.nn.softmax(s.astype(jnp.float32) * scale, axis=-1).astype(vh.dtype)
        return jnp.einsum('bhqk,bhkd->bhqd', p, vh)

    o = lax.map(attend, (qn_b, qr_b))
    return o.transpose(1, 0, 3, 2, 4).reshape(bsz, seq, MLA_HEADS * V_DIM)


def spatial_gating(u, v, g_norm, w_s, b_s):
    u = jax.nn.gelu(u)
    v = rms_norm(jax.nn.gelu(v), g_norm)
    bsz, seq, ch = v.shape
    vr = v.reshape(bsz, seq // CHUNK, CHUNK, SG_GROUPS, ch // SG_GROUPS)
    mixed = jnp.einsum('gpq,bnqgc->bnpgc', w_s, vr) + b_s.T[None, None, :, :, None]
    return u * mixed.reshape(bsz, seq, ch)


def hier_moe(x, wg, bg, we, be, w1, w3, w2):
    bsz, seq, dm = x.shape
    xf = x.reshape(-1, dm)
    n_tok = xf.shape[0]
    g_logits = (xf @ wg).astype(jnp.float32) + bg.astype(jnp.float32)
    g_idx = jnp.argmax(g_logits, axis=-1)
    g_prob = jnp.take_along_axis(jax.nn.softmax(g_logits, axis=-1), g_idx[:, None], axis=-1)[:, 0]
    e_logits = ((xf @ we).astype(jnp.float32) + be.astype(jnp.float32)).reshape(n_tok, N_GROUPS, EXPERTS_PER_GROUP)
    e_in = jnp.take_along_axis(e_logits, g_idx[:, None, None], axis=1)[:, 0]
    top_v, top_i = lax.top_k(e_in, TOP_K)
    gate = g_prob[:, None] * jax.nn.softmax(top_v, axis=-1)
    eid = (g_idx[:, None] * EXPERTS_PER_GROUP + top_i).reshape(-1).astype(jnp.int32)
    tok = jnp.repeat(jnp.arange(n_tok, dtype=jnp.int32), TOP_K)
    gate_flat = gate.reshape(-1)
    order = jnp.argsort(eid)
    e_s, tok_s, w_s = eid[order], tok[order], gate_flat[order]
    counts = jnp.zeros((N_EXPERTS,), jnp.int32).at[eid].add(1)
    starts = jnp.cumsum(counts) - counts
    padded = (counts + MOE_BLOCK - 1) // MOE_BLOCK * MOE_BLOCK
    pad_ends = jnp.cumsum(padded)
    pad_starts = pad_ends - padded
    n_assign = n_tok * TOP_K
    dest = pad_starts[e_s] + (jnp.arange(n_assign, dtype=jnp.int32) - starts[e_s])
    n_blocks = (n_assign + MOE_BLOCK - 1) // MOE_BLOCK + N_EXPERTS
    n_rows = n_blocks * MOE_BLOCK
    row_tok = jnp.full((n_rows,), n_tok, jnp.int32).at[dest].set(tok_s)
    row_w = jnp.zeros((n_rows,), x.dtype).at[dest].set(w_s.astype(x.dtype))
    blk_expert = jnp.searchsorted(pad_ends, jnp.arange(n_blocks, dtype=jnp.int32) * MOE_BLOCK, side='right')
    blk_expert = jnp.minimum(blk_expert, N_EXPERTS - 1).astype(jnp.int32)
    x_pad = jnp.concatenate([xf, jnp.zeros((1, dm), xf.dtype)], axis=0)
    xb = x_pad[row_tok].reshape(n_blocks, MOE_BLOCK, dm)

    def expert_block(args):
        xblk, e = args
        h = jax.nn.silu(xblk @ w1[e]) * (xblk @ w3[e])
        return h @ w2[e]

    yb = lax.map(expert_block, (xb, blk_expert)).reshape(n_rows, dm)
    out = jnp.zeros((n_tok + 1, dm), x.dtype).at[row_tok].add(yb * row_w[:, None])[:n_tok]
    return out.reshape(bsz, seq, dm)


def setup_inputs(seed: int = 0) -> dict:
    key = jax.random.key(seed)
    ks = jax.random.split(key, 26)
    ne = (DEPTH + 1) // 2
    no = DEPTH // 2
    f32 = jnp.float32

    def nrm(k, shape, scale):
        return jax.random.normal(k, shape, f32) * scale

    def gain(k, shape):
        return 1.0 + 0.05 * jax.random.normal(k, shape, f32)

    return {
        'x': nrm(ks[0], (BATCH, SEQ, D_MODEL), 1.0),
        'norm_mix': gain(ks[1], (DEPTH, D_MODEL)),
        'norm_ffn': gain(ks[2], (DEPTH, D_MODEL)),
        'norm_final': gain(ks[3], (D_MODEL,)),
        'w_in_ab': nrm(ks[4], (ne, D_MODEL, AB_IN), D_MODEL ** -0.5),
        'na_rpb': nrm(ks[5], (ne, NA_HEADS, 2 * WIN_ROWS - 1, 2 * WIN_COLS - 1), 0.1),
        'conv_w': nrm(ks[6], (ne, CONV_WIDTH, CONV_CH), CONV_WIDTH ** -0.5),
        'conv_b': nrm(ks[7], (ne, CONV_CH), 0.02),
        'w_out_ab': nrm(ks[8], (ne, AB_OUT, D_MODEL), AB_OUT ** -0.5),
        'w_in_cd': nrm(ks[9], (no, D_MODEL, CD_IN), D_MODEL ** -0.5),
        'q_norm': gain(ks[10], (no, Q_LORA)),
        'kv_norm': gain(ks[11], (no, KV_LORA)),
        'w_uq': nrm(ks[12], (no, Q_LORA, MLA_HEADS * (QK_NOPE + QK_ROPE)), Q_LORA ** -0.5),
        'w_ukv': nrm(ks[13], (no, KV_LORA, MLA_HEADS * (QK_NOPE + V_DIM)), KV_LORA ** -0.5),
        'sg_norm': gain(ks[14], (no, SG_CH)),
        'sg_w': nrm(ks[15], (no, SG_GROUPS, CHUNK, CHUNK), CHUNK ** -0.5),
        'sg_b': nrm(ks[16], (no, SG_GROUPS, CHUNK), 0.02),
        'w_out_cd': nrm(ks[17], (no, CD_OUT, D_MODEL), CD_OUT ** -0.5),
        'router_group_w': nrm(ks[18], (DEPTH, D_MODEL, N_GROUPS), D_MODEL ** -0.5),
        'router_group_b': nrm(ks[19], (DEPTH, N_GROUPS), 0.01),
        'router_expert_w': nrm(ks[20], (DEPTH, D_MODEL, N_EXPERTS), D_MODEL ** -0.5),
        'router_expert_b': nrm(ks[21], (DEPTH, N_EXPERTS), 0.01),
        'w1': nrm(ks[22], (DEPTH, N_EXPERTS, D_MODEL, D_EXPERT), D_MODEL ** -0.5),
        'w3': nrm(ks[23], (DEPTH, N_EXPERTS, D_MODEL, D_EXPERT), D_MODEL ** -0.5),
        'w2': nrm(ks[24], (DEPTH, N_EXPERTS, D_EXPERT, D_MODEL), D_EXPERT ** -0.5),
    }


def reference(x, norm_mix, norm_ffn, norm_final, w_in_ab, na_rpb, conv_w, conv_b, w_out_ab,
              w_in_cd, q_norm, kv_norm, w_uq, w_ukv, sg_norm, sg_w, sg_b, w_out_cd,
              router_group_w, router_group_b, router_expert_w, router_expert_b, w1, w3, w2):
    bsz, seq, _ = x.shape
    ang_row, ang_col = rope_angles(seq)
    ab_split = [NA_WIDTH, 2 * NA_WIDTH, 3 * NA_WIDTH, 3 * NA_WIDTH + CONV_CH, 3 * NA_WIDTH + 2 * CONV_CH]
    cd_split = [Q_LORA, Q_LORA + KV_LORA, Q_LORA + KV_LORA + QK_ROPE, Q_LORA + KV_LORA + QK_ROPE + SG_CH]
    for layer in range(DEPTH):
        i = layer // 2
        h = rms_norm(x, norm_mix[layer])
        if layer % 2 == 0:
            p = h @ w_in_ab[i]
            q, k, v, gate_b, gate_c, hc = jnp.split(p, ab_split, axis=-1)
            heads = lambda t: t.reshape(bsz, seq, NA_HEADS, NA_HEAD_DIM)
            a_out = neighbourhood_attention(heads(q), heads(k), heads(v), na_rpb[i])
            b_out = gate_b * short_conv(gate_c * hc, conv_w[i], conv_b[i])
            x = x + jnp.concatenate([a_out, b_out], axis=-1) @ w_out_ab[i]
        else:
            p = h @ w_in_cd[i]
            c_q, c_kv, k_r, u, vv = jnp.split(p, cd_split, axis=-1)
            c_out = mla(c_q, c_kv, k_r, q_norm[i], kv_norm[i], w_uq[i], w_ukv[i], ang_row, ang_col)
            d_out = spatial_gating(u, vv, sg_norm[i], sg_w[i], sg_b[i])
            x = x + jnp.concatenate([c_out, d_out], axis=-1) @ w_out_cd[i]
        x = x + hier_moe(rms_norm(x, norm_ffn[layer]), router_group_w[layer], router_group_b[layer],
                         router_expert_w[layer], router_expert_b[layer], w1[layer], w3[layer], w2[layer])
    return rms_norm(x, norm_final)
```

```python
import functools

import numpy as np
import jax
import jax.numpy as jnp
from jax import lax
from jax.experimental import pallas as pl
from jax.experimental.pallas import tpu as pltpu

F32 = jnp.float32
BF16 = jnp.bfloat16

D_MODEL = 2048
BATCH = 2
SEQ = 4096
N_TOK = BATCH * SEQ
DEPTH = 2
GRID_W = 64
GRID_ROWS = SEQ // GRID_W
WIN_ROWS = 8
WIN_COLS = 16
NA_HEADS = 8
NA_HEAD_DIM = 128
NA_WIDTH = NA_HEADS * NA_HEAD_DIM
NA_KEYS = WIN_ROWS * GRID_W
CONV_CH = D_MODEL // 2
MLA_HEADS = 8
Q_LORA = 512
KV_LORA = 512
QK_NOPE = 128
QK_ROPE = 64
QK_DIM = QK_NOPE + QK_ROPE
V_DIM = 128
ROPE_THETA = 10000.0
SG_CH = D_MODEL // 2
SG_GROUPS = 8
CHUNK = 128
N_GROUPS = 8
EXPERTS_PER_GROUP = 8
N_EXPERTS = N_GROUPS * EXPERTS_PER_GROUP
TOP_K = 2
D_EXPERT = 768
EPS = 1e-6
NEG_INF = -1e30
AB_IN = 3 * NA_WIDTH + 3 * CONV_CH
CD_IN_PAD = 3200
KR_COL = Q_LORA + KV_LORA + 2 * SG_CH

LANES = 128
MOE_BLOCK = 512
MOE_SUB = 256
MOE_FT = 256
N_FT = D_EXPERT // MOE_FT
N_ASSIGN = N_TOK * TOP_K
MOE_NBLK = N_ASSIGN // MOE_BLOCK + N_EXPERTS
COMBINE_TOK = 256
VMEM_LIMIT = 52 * 1024 * 1024


def _params(*sem):
    return pltpu.CompilerParams(dimension_semantics=sem, vmem_limit_bytes=VMEM_LIMIT)


def _rms(x, g):
    return x * lax.rsqrt(jnp.mean(x * x, axis=-1, keepdims=True) + EPS) * g


def _norm_mm_kernel(x_ref, g_ref, w_ref, o_ref, xn_ref):
    @pl.when(pl.program_id(1) == 0)
    def _():
        xn_ref[...] = _rms(x_ref[...].astype(F32), g_ref[...]).astype(BF16)

    o_ref[...] = jnp.dot(xn_ref[...], w_ref[...], preferred_element_type=F32).astype(o_ref.dtype)


def _norm_matmul(x, g, w, tm, tn):
    m, k = x.shape
    nout = w.shape[1]
    return pl.pallas_call(
        _norm_mm_kernel,
        out_shape=jax.ShapeDtypeStruct((m, nout), BF16),
        grid=(m // tm, nout // tn),
        in_specs=[
            pl.BlockSpec((tm, k), lambda i, j: (i, 0)),
            pl.BlockSpec((1, k), lambda i, j: (0, 0)),
            pl.BlockSpec((k, tn), lambda i, j: (0, j)),
        ],
        out_specs=pl.BlockSpec((tm, tn), lambda i, j: (i, j)),
        scratch_shapes=[pltpu.VMEM((tm, k), BF16)],
        compiler_params=_params("parallel", "arbitrary"),
        name="norm_matmul",
    )(x, g.reshape(1, k), w)


def _natten_bias_table(rpb):
    offs = np.arange(WIN_ROWS) - (WIN_ROWS - 1)
    dr = offs[:, None] + np.arange(WIN_ROWS)[None, :] + WIN_ROWS - 1
    c = np.arange(GRID_W)
    col_start = np.clip(c - WIN_COLS // 2, 0, GRID_W - WIN_COLS)
    valid = (c[None, :] >= col_start[:, None]) & (c[None, :] < col_start[:, None] + WIN_COLS)
    dc = np.clip(c[None, :] - c[:, None] + WIN_COLS - 1, 0, 2 * WIN_COLS - 2)
    t = rpb[:, dr[:, None, :, None], dc[None, :, None, :]]
    t = jnp.where(valid[None, None, :, None, :], t, NEG_INF)
    return t.reshape(rpb.shape[0], WIN_ROWS, GRID_W, NA_KEYS).astype(F32)


def _natten_kernel(q_ref, k_ref, v_ref, t_ref, o_ref):
    scale = NA_HEAD_DIM ** -0.5

    def row(r, carry):
        kr0 = jnp.clip(r - WIN_ROWS // 2, 0, GRID_ROWS - WIN_ROWS)
        q0 = pl.multiple_of(r * GRID_W, GRID_W)
        k0 = pl.multiple_of(kr0 * GRID_W, GRID_W)
        q = q_ref[pl.ds(q0, GRID_W), :]
        k = k_ref[pl.ds(k0, NA_KEYS), :]
        v = v_ref[pl.ds(k0, NA_KEYS), :]
        s = lax.dot_general(q, k, (((1,), (1,)), ((), ())), preferred_element_type=F32)
        s = s * scale + t_ref[kr0 - r + WIN_ROWS - 1]
        p = jnp.exp(s - jnp.max(s, axis=-1, keepdims=True))
        l = jnp.sum(p, axis=-1, keepdims=True)
        o = jnp.dot(p.astype(BF16), v, preferred_element_type=F32) / l
        o_ref[pl.ds(q0, GRID_W), :] = o.astype(o_ref.dtype)
        return carry

    lax.fori_loop(0, GRID_ROWS, row, 0)


def _natten(p, table):
    blk = (SEQ, NA_HEAD_DIM)
    return pl.pallas_call(
        _natten_kernel,
        out_shape=jax.ShapeDtypeStruct((N_TOK, NA_WIDTH), BF16),
        grid=(BATCH, NA_HEADS),
        in_specs=[
            pl.BlockSpec(blk, lambda b, h: (b, h)),
            pl.BlockSpec(blk, lambda b, h: (b, NA_HEADS + h)),
            pl.BlockSpec(blk, lambda b, h: (b, 2 * NA_HEADS + h)),
            pl.BlockSpec((None, WIN_ROWS, GRID_W, NA_KEYS), lambda b, h: (h, 0, 0, 0)),
        ],
        out_specs=pl.BlockSpec(blk, lambda b, h: (b, h)),
        compiler_params=_params("parallel", "parallel"),
        name="natten",
    )(p, p, p, table)


CONV_CB = 128


def _conv_kernel(gb_ref, gc_ref, hc_ref, w_ref, b_ref, o_ref):
    z = gc_ref[...].astype(F32) * hc_ref[...].astype(F32)
    pos = lax.broadcasted_iota(jnp.int32, z.shape, 0)
    z_prev = jnp.where(pos == 0, 0.0, pltpu.roll(z, 1, 0))
    z_next = jnp.where(pos == SEQ - 1, 0.0, pltpu.roll(z, SEQ - 1, 0))
    y = b_ref[...] + z_prev * w_ref[0:1, :] + z * w_ref[1:2, :] + z_next * w_ref[2:3, :]
    o_ref[...] = (gb_ref[...].astype(F32) * y).astype(o_ref.dtype)


def _gated_conv(p, w, b):
    base = 3 * NA_WIDTH // CONV_CB
    step = CONV_CH // CONV_CB
    blk = (SEQ, CONV_CB)
    return pl.pallas_call(
        _conv_kernel,
        out_shape=jax.ShapeDtypeStruct((N_TOK, CONV_CH), BF16),
        grid=(BATCH, step),
        in_specs=[
            pl.BlockSpec(blk, lambda bi, c: (bi, base + c)),
            pl.BlockSpec(blk, lambda bi, c: (bi, base + step + c)),
            pl.BlockSpec(blk, lambda bi, c: (bi, base + 2 * step + c)),
            pl.BlockSpec((3, CONV_CB), lambda bi, c: (0, c)),
            pl.BlockSpec((1, CONV_CB), lambda bi, c: (0, c)),
        ],
        out_specs=pl.BlockSpec(blk, lambda bi, c: (bi, c)),
        compiler_params=_params("parallel", "parallel"),
        name="gated_conv",
    )(p, p, p, w, b.reshape(1, CONV_CH))


def _out_proj_kernel(a_ref, b_ref, wa_ref, wb_ref, r_ref, o_ref):
    acc = jnp.dot(a_ref[...], wa_ref[...], preferred_element_type=F32)
    acc = acc + jnp.dot(b_ref[...], wb_ref[...], preferred_element_type=F32)
    o_ref[...] = r_ref[...] + acc


def _out_proj(a, b, w, res, tm=512):
    ka, kb = a.shape[1], b.shape[1]
    assert ka == kb
    return pl.pallas_call(
        _out_proj_kernel,
        out_shape=jax.ShapeDtypeStruct((N_TOK, D_MODEL), F32),
        grid=(N_TOK // tm,),
        in_specs=[
            pl.BlockSpec((tm, ka), lambda i: (i, 0)),
            pl.BlockSpec((tm, kb), lambda i: (i, 0)),
            pl.BlockSpec((ka, D_MODEL), lambda i: (0, 0)),
            pl.BlockSpec((kb, D_MODEL), lambda i: (1, 0)),
            pl.BlockSpec((tm, D_MODEL), lambda i: (i, 0)),
        ],
        out_specs=pl.BlockSpec((tm, D_MODEL), lambda i: (i, 0)),
        compiler_params=_params("parallel"),
        name="out_proj",
    )(a, b, w, w, res)


MLA_TM = 512


def _rope_tables():
    pos = jnp.arange(SEQ, dtype=jnp.int32)
    row = (pos // GRID_W).astype(F32)
    col = (pos % GRID_W).astype(F32)
    half = QK_ROPE // 2
    inv = ROPE_THETA ** (-jnp.arange(0, half, 2, dtype=F32) / half)
    ar, ac = row[:, None] * inv, col[:, None] * inv
    cos_t = jnp.concatenate([jnp.cos(ar), jnp.cos(ar), jnp.cos(ac), jnp.cos(ac)], axis=-1)
    sin_t = jnp.concatenate([-jnp.sin(ar), jnp.sin(ar), -jnp.sin(ac), jnp.sin(ac)], axis=-1)
    quarter = half // 2
    src = np.arange(QK_ROPE) + np.where((np.arange(QK_ROPE) // quarter) % 2 == 0, quarter, -quarter)
    swap = np.zeros((QK_ROPE, QK_ROPE), np.float32)
    swap[src, np.arange(QK_ROPE)] = 1.0
    return cos_t, sin_t, jnp.asarray(swap)


def _rope(x, cos_t, sin_t, swap):
    xs = jnp.dot(x, swap, preferred_element_type=F32, precision=lax.Precision.HIGHEST)
    return x * cos_t + xs * sin_t


def _mla_prep_kernel(cq_ref, ckv_ref, kr_ref, gq_ref, gkv_ref, wq_ref, wkv_ref, cos_ref, sin_ref,
                     swap_ref, q_ref, k_ref, v_ref, cqn_ref, ckvn_ref, krot_ref):
    @pl.when(pl.program_id(1) == 0)
    def _():
        cqn_ref[...] = _rms(cq_ref[...].astype(F32), gq_ref[...]).astype(BF16)
        ckvn_ref[...] = _rms(ckv_ref[...].astype(F32), gkv_ref[...]).astype(BF16)
        kr = kr_ref[:, :QK_ROPE].astype(F32)
        krot_ref[...] = _rope(kr, cos_ref[...], sin_ref[...], swap_ref[...]).astype(BF16)

    scale = QK_DIM ** -0.5
    q = jnp.dot(cqn_ref[...], wq_ref[...], preferred_element_type=F32)
    q_ref[:, :QK_NOPE] = (q[:, :QK_NOPE] * scale).astype(BF16)
    qr = _rope(q[:, QK_NOPE:], cos_ref[...], sin_ref[...], swap_ref[...])
    q_ref[:, QK_NOPE:] = (qr * scale).astype(BF16)
    kv = jnp.dot(ckvn_ref[...], wkv_ref[...], preferred_element_type=F32)
    k_ref[:, :QK_NOPE] = kv[:, :QK_NOPE].astype(BF16)
    k_ref[:, QK_NOPE:] = krot_ref[...]
    v_ref[...] = kv[:, QK_NOPE:].astype(BF16)


def _mla_prep(p, q_norm, kv_norm, wq, wkv):
    cos_t, sin_t, swap = _rope_tables()
    tm = MLA_TM
    seq_blocks = SEQ // tm
    return pl.pallas_call(
        _mla_prep_kernel,
        out_shape=(
            jax.ShapeDtypeStruct((MLA_HEADS, N_TOK, QK_DIM), BF16),
            jax.ShapeDtypeStruct((MLA_HEADS, N_TOK, QK_DIM), BF16),
            jax.ShapeDtypeStruct((MLA_HEADS, N_TOK, V_DIM), BF16),
        ),
        grid=(N_TOK // tm, MLA_HEADS),
        in_specs=[
            pl.BlockSpec((tm, Q_LORA), lambda i, h: (i, 0)),
            pl.BlockSpec((tm, KV_LORA), lambda i, h: (i, 1)),
            pl.BlockSpec((tm, LANES), lambda i, h: (i, KR_COL // LANES)),
            pl.BlockSpec((1, Q_LORA), lambda i, h: (0, 0)),
            pl.BlockSpec((1, KV_LORA), lambda i, h: (0, 0)),
            pl.BlockSpec((None, Q_LORA, QK_DIM), lambda i, h: (h, 0, 0)),
            pl.BlockSpec((None, KV_LORA, QK_NOPE + V_DIM), lambda i, h: (h, 0, 0)),
            pl.BlockSpec((tm, QK_ROPE), lambda i, h: (i % seq_blocks, 0)),
            pl.BlockSpec((tm, QK_ROPE), lambda i, h: (i % seq_blocks, 0)),
            pl.BlockSpec((QK_ROPE, QK_ROPE), lambda i, h: (0, 0)),
        ],
        out_specs=(
            pl.BlockSpec((None, tm, QK_DIM), lambda i, h: (h, i, 0)),
            pl.BlockSpec((None, tm, QK_DIM), lambda i, h: (h, i, 0)),
            pl.BlockSpec((None, tm, V_DIM), lambda i, h: (h, i, 0)),
        ),
        scratch_shapes=[
            pltpu.VMEM((tm, Q_LORA), BF16),
            pltpu.VMEM((tm, KV_LORA), BF16),
            pltpu.VMEM((tm, QK_ROPE), BF16),
        ],
        compiler_params=_params("parallel", "arbitrary"),
        name="mla_prep",
    )(p, p, p, q_norm.reshape(1, Q_LORA), kv_norm.reshape(1, KV_LORA), wq, wkv, cos_t, sin_t, swap)


MLA_TQ = 512
MLA_TK = 512


def _mla_attn_kernel(q_ref, k_ref, v_ref, o_ref):
    q = q_ref[...]

    def chunk(c, carry):
        m, l, acc = carry
        k0 = pl.multiple_of(c * MLA_TK, MLA_TK)
        k = k_ref[pl.ds(k0, MLA_TK), :]
        v = v_ref[pl.ds(k0, MLA_TK), :]
        s = lax.dot_general(q, k, (((1,), (1,)), ((), ())), preferred_element_type=F32)
        m_new = jnp.maximum(m, jnp.max(s, axis=-1, keepdims=True))
        alpha = jnp.exp(m - m_new)
        p = jnp.exp(s - m_new)
        l = alpha * l + jnp.sum(p, axis=-1, keepdims=True)
        acc = alpha * acc + jnp.dot(p.astype(BF16), v, preferred_element_type=F32)
        return m_new, l, acc

    init = (jnp.full((MLA_TQ, 1), NEG_INF, F32), jnp.zeros((MLA_TQ, 1), F32),
            jnp.zeros((MLA_TQ, V_DIM), F32))
    _, l, acc = lax.fori_loop(0, SEQ // MLA_TK, chunk, init)
    o_ref[...] = (acc / l).astype(o_ref.dtype)


def _mla_attn(q, k, v):
    nq = SEQ // MLA_TQ
    return pl.pallas_call(
        _mla_attn_kernel,
        out_shape=jax.ShapeDtypeStruct((N_TOK, MLA_HEADS * V_DIM), BF16),
        grid=(BATCH, MLA_HEADS, nq),
        in_specs=[
            pl.BlockSpec((None, MLA_TQ, QK_DIM), lambda b, h, i: (h, b * nq + i, 0)),
            pl.BlockSpec((None, SEQ, QK_DIM), lambda b, h, i: (h, b, 0)),
            pl.BlockSpec((None, SEQ, V_DIM), lambda b, h, i: (h, b, 0)),
        ],
        out_specs=pl.BlockSpec((MLA_TQ, V_DIM), lambda b, h, i: (b * nq + i, h)),
        compiler_params=_params("parallel", "parallel", "parallel"),
        name="mla_attn",
    )(q, k, v)


SG_TM = 256


def _sg_kernel(u_ref, v_ref, g_ref, w_ref, bt_ref, o_ref):
    v = jax.nn.gelu(v_ref[...].astype(F32))
    vn = _rms(v, g_ref[...]).astype(BF16)
    u = jax.nn.gelu(u_ref[...].astype(F32))
    gc = SG_CH // SG_GROUPS
    for n in range(SG_TM // CHUNK):
        rows = slice(n * CHUNK, (n + 1) * CHUNK)
        for g in range(SG_GROUPS):
            cols = slice(g * gc, (g + 1) * gc)
            mixed = jnp.dot(w_ref[g], vn[rows, cols], preferred_element_type=F32) + bt_ref[:, g:g + 1]
            o_ref[rows, cols] = (u[rows, cols] * mixed).astype(o_ref.dtype)


def _spatial_gating(p, g_norm, w_s, b_s):
    u_blk = (Q_LORA + KV_LORA) // SG_CH
    return pl.pallas_call(
        _sg_kernel,
        out_shape=jax.ShapeDtypeStruct((N_TOK, SG_CH), BF16),
        grid=(N_TOK // SG_TM,),
        in_specs=[
            pl.BlockSpec((SG_TM, SG_CH), lambda i: (i, u_blk)),
            pl.BlockSpec((SG_TM, SG_CH), lambda i: (i, u_blk + 1)),
            pl.BlockSpec((1, SG_CH), lambda i: (0, 0)),
            pl.BlockSpec((SG_GROUPS, CHUNK, CHUNK), lambda i: (0, 0, 0)),
            pl.BlockSpec((CHUNK, SG_GROUPS), lambda i: (0, 0)),
        ],
        out_specs=pl.BlockSpec((SG_TM, SG_CH), lambda i: (i, 0)),
        compiler_params=_params("parallel"),
        name="spatial_gating",
    )(p, p, g_norm.reshape(1, SG_CH), w_s.astype(BF16), b_s.T)


ROUTER_TM = 512


def _router_kernel(x_ref, g_ref, w_ref, b_ref, o_ref):
    xn = _rms(x_ref[...], g_ref[...])
    logits = jnp.dot(xn, w_ref[...], preferred_element_type=F32,
                     precision=lax.Precision.HIGHEST) + b_ref[...]
    lane = lax.broadcasted_iota(jnp.int32, logits.shape, 1).astype(F32)
    low = jnp.float32(-3.0e38)

    def first_max(vals):
        top = jnp.max(vals, axis=-1, keepdims=True)
        idx = jnp.min(jnp.where(vals == top, lane, float(LANES)), axis=-1, keepdims=True)
        return top, idx

    is_group = lane < N_GROUPS
    g_top, g_idx = first_max(jnp.where(is_group, logits, low))
    g_prob = 1.0 / jnp.sum(jnp.where(is_group, jnp.exp(logits - g_top), 0.0), axis=-1, keepdims=True)
    lo = N_GROUPS + g_idx * EXPERTS_PER_GROUP
    e_vals = jnp.where(lane >= lo, jnp.where(lane < lo + EXPERTS_PER_GROUP, logits, low), low)
    v1, i1 = first_max(e_vals)
    v2, i2 = first_max(jnp.where(lane == i1, low, e_vals))
    e21 = jnp.exp(v2 - v1)
    w1 = g_prob / (1.0 + e21)
    w2 = g_prob * e21 / (1.0 + e21)
    out = jnp.where(lane == 0, i1 - N_GROUPS,
                    jnp.where(lane == 1, i2 - N_GROUPS,
                              jnp.where(lane == 2, w1, jnp.where(lane == 3, w2, 0.0))))
    o_ref[...] = out


def _router(x, g, w, b):
    tm = ROUTER_TM
    return pl.pallas_call(
        _router_kernel,
        out_shape=jax.ShapeDtypeStruct((N_TOK, LANES), F32),
        grid=(N_TOK // tm,),
        in_specs=[
            pl.BlockSpec((tm, D_MODEL), lambda i: (i, 0)),
            pl.BlockSpec((1, D_MODEL), lambda i: (0, 0)),
            pl.BlockSpec((D_MODEL, LANES), lambda i: (0, 0)),
            pl.BlockSpec((1, LANES), lambda i: (0, 0)),
        ],
        out_specs=pl.BlockSpec((tm, LANES), lambda i: (i, 0)),
        compiler_params=_params("parallel"),
        name="router",
    )(x, g.reshape(1, D_MODEL), w, b)


def _dispatch_plan(eid):
    eid_flat = eid.reshape(-1)
    onehot = eid_flat[:, None] == jnp.arange(N_EXPERTS, dtype=jnp.int32)[None, :]
    csum = jnp.cumsum(onehot.astype(jnp.int32), axis=0)
    rank = jnp.sum(jnp.where(onehot, csum, 0), axis=1) - 1
    counts = csum[-1]
    nblk_e = (counts + MOE_BLOCK - 1) // MOE_BLOCK
    blk_end = jnp.cumsum(nblk_e)
    blk_start = blk_end - nblk_e
    dest = (blk_start * MOE_BLOCK)[eid_flat] + rank
    n_active = blk_end[-1]
    bidx = jnp.arange(MOE_NBLK, dtype=jnp.int32)
    blk_x = jnp.minimum(bidx, n_active - 1)
    blk_e = jnp.minimum(jnp.searchsorted(blk_end, blk_x, side='right'), N_EXPERTS - 1).astype(jnp.int32)
    rows = jnp.clip(counts[blk_e] - (blk_x - blk_start[blk_e]) * MOE_BLOCK, 0, MOE_BLOCK)
    blk_rows = jnp.where(bidx < n_active, rows, 0).astype(jnp.int32)
    tok = jnp.arange(N_ASSIGN, dtype=jnp.int32) // TOP_K
    row_tok = jnp.zeros((MOE_NBLK * MOE_BLOCK,), jnp.int32).at[dest].set(tok)
    return dest.astype(jnp.int32), row_tok, blk_e, blk_x.astype(jnp.int32), blk_rows


def _gather_kernel(row_tok, blk_x, blk_rows, x_hbm, o_ref, sem):
    b = pl.program_id(0)
    rows = blk_rows[b]
    base = blk_x[b] * MOE_BLOCK

    def row_copy(i, t):
        return pltpu.make_async_copy(x_hbm.at[pl.ds(t, 1), :], o_ref.at[pl.ds(i, 1), :], sem)

    @pl.when(rows > 0)
    def _():
        n_fetch = jnp.where(rows > MOE_SUB, MOE_BLOCK, MOE_SUB)

        @pl.when(rows <= MOE_SUB)
        def _():
            o_ref[MOE_SUB:, :] = jnp.zeros((MOE_BLOCK - MOE_SUB, D_MODEL), F32)

        def start(i, c):
            row_copy(i, row_tok[base + i]).start()
            return c

        def wait(i, c):
            row_copy(0, 0).wait()
            return c

        lax.fori_loop(0, n_fetch, start, 0)
        lax.fori_loop(0, n_fetch, wait, 0)


def _moe_gather(x, row_tok, blk_x, blk_rows):
    return pl.pallas_call(
        _gather_kernel,
        out_shape=jax.ShapeDtypeStruct((MOE_NBLK * MOE_BLOCK, D_MODEL), F32),
        grid_spec=pltpu.PrefetchScalarGridSpec(
            num_scalar_prefetch=3,
            grid=(MOE_NBLK,),
            in_specs=[pl.BlockSpec(memory_space=pl.ANY)],
            out_specs=pl.BlockSpec((MOE_BLOCK, D_MODEL), lambda b, rt, bx, br: (bx[b], 0)),
            scratch_shapes=[pltpu.SemaphoreType.DMA(())],
        ),
        compiler_params=_params("arbitrary"),
        name="moe_gather",
    )(row_tok, blk_x, blk_rows, x)


def _expert_kernel(blk_e, blk_x, blk_rows, x_ref, g_ref, w1_ref, w3_ref, w2_ref, o_ref,
                   xn_ref, w1b_ref, w3b_ref, w2b_ref):
    b = pl.program_id(0)
    j = pl.program_id(1)
    rows = blk_rows[b]

    @pl.when(rows > 0)
    def _():
        @pl.when(j == 0)
        def _():
            xn_ref[...] = _rms(x_ref[...], g_ref[...]).astype(BF16)

        w1b_ref[...] = w1_ref[...].astype(BF16)
        w3b_ref[...] = w3_ref[...].astype(BF16)
        w2b_ref[...] = w2_ref[...].astype(BF16)

        def sub_block(s):
            sl = slice(s * MOE_SUB, (s + 1) * MOE_SUB)
            xs = xn_ref[sl, :]
            h1 = jnp.dot(xs, w1b_ref[...], preferred_element_type=F32)
            h3 = jnp.dot(xs, w3b_ref[...], preferred_element_type=F32)
            h = (jax.nn.silu(h1) * h3).astype(BF16)
            y = jnp.dot(h, w2b_ref[...], preferred_element_type=F32)

            @pl.when(j == 0)
            def _():
                o_ref[sl, :] = y

            @pl.when(j > 0)
            def _():
                o_ref[sl, :] += y

        sub_block(0)

        @pl.when(rows > MOE_SUB)
        def _():
            sub_block(1)

        @pl.when(jnp.logical_and(rows <= MOE_SUB, j == 0))
        def _():
            o_ref[MOE_SUB:, :] = jnp.zeros((MOE_BLOCK - MOE_SUB, D_MODEL), F32)


def _moe_experts(xs, g, w1, w3, w2, blk_e, blk_x, blk_rows):
    def w_col(b, j, be, bx, br):
        return (be[b], 0, jnp.where(br[b] > 0, j, N_FT - 1))

    def w_row(b, j, be, bx, br):
        return (be[b], jnp.where(br[b] > 0, j, N_FT - 1), 0)

    def x_map(b, j, be, bx, br):
        return (bx[b], 0)

    return pl.pallas_call(
        _expert_kernel,
        out_shape=jax.ShapeDtypeStruct((MOE_NBLK * MOE_BLOCK, D_MODEL), F32),
        grid_spec=pltpu.PrefetchScalarGridSpec(
            num_scalar_prefetch=3,
            grid=(MOE_NBLK, N_FT),
            in_specs=[
                pl.BlockSpec((MOE_BLOCK, D_MODEL), x_map),
                pl.BlockSpec((1, D_MODEL), lambda b, j, be, bx, br: (0, 0)),
                pl.BlockSpec((None, D_MODEL, MOE_FT), w_col),
                pl.BlockSpec((None, D_MODEL, MOE_FT), w_col),
                pl.BlockSpec((None, MOE_FT, D_MODEL), w_row),
            ],
            out_specs=pl.BlockSpec((MOE_BLOCK, D_MODEL), x_map),
            scratch_shapes=[
                pltpu.VMEM((MOE_BLOCK, D_MODEL), BF16),
                pltpu.VMEM((D_MODEL, MOE_FT), BF16),
                pltpu.VMEM((D_MODEL, MOE_FT), BF16),
                pltpu.VMEM((MOE_FT, D_MODEL), BF16),
            ],
        ),
        compiler_params=_params("arbitrary", "arbitrary"),
        name="moe_experts",
    )(blk_e, blk_x, blk_rows, xs, g.reshape(1, D_MODEL), w1, w3, w2)


def _combine_kernel(final_norm, dest, ys_hbm, x_ref, gate_ref, gf_ref, o_ref, ya_ref, yb_ref, sem):
    base = pl.program_id(0) * COMBINE_TOK * TOP_K

    def row_copy(dst_ref, t, d, s):
        return pltpu.make_async_copy(ys_hbm.at[pl.ds(d, 1), :], dst_ref.at[pl.ds(t, 1), :], sem.at[s])

    def start(t, c):
        row_copy(ya_ref, t, dest[base + TOP_K * t], 0).start()
        row_copy(yb_ref, t, dest[base + TOP_K * t + 1], 1).start()
        return c

    def wait(t, c):
        row_copy(ya_ref, 0, 0, 0).wait()
        row_copy(yb_ref, 0, 0, 1).wait()
        return c

    lax.fori_loop(0, COMBINE_TOK, start, 0)
    lax.fori_loop(0, COMBINE_TOK, wait, 0)
    gates = gate_ref[...]
    out = x_ref[...] + (gates[:, 2:3] * ya_ref[...] + gates[:, 3:4] * yb_ref[...])
    if final_norm:
        out = _rms(out, gf_ref[...])
    o_ref[...] = out


def _moe_combine(x, ys, route, dest, g_final, final_norm):
    tm = COMBINE_TOK
    return pl.pallas_call(
        functools.partial(_combine_kernel, final_norm),
        out_shape=jax.ShapeDtypeStruct((N_TOK, D_MODEL), F32),
        grid_spec=pltpu.PrefetchScalarGridSpec(
            num_scalar_prefetch=1,
            grid=(N_TOK // tm,),
            in_specs=[
                pl.BlockSpec(memory_space=pl.ANY),
                pl.BlockSpec((tm, D_MODEL), lambda i, d: (i, 0)),
                pl.BlockSpec((tm, LANES), lambda i, d: (i, 0)),
                pl.BlockSpec((1, D_MODEL), lambda i, d: (0, 0)),
            ],
            out_specs=pl.BlockSpec((tm, D_MODEL), lambda i, d: (i, 0)),
            scratch_shapes=[
                pltpu.VMEM((tm, D_MODEL), F32),
                pltpu.VMEM((tm, D_MODEL), F32),
                pltpu.SemaphoreType.DMA((2,)),
            ],
        ),
        compiler_params=_params("arbitrary"),
        name="moe_combine",
    )(dest, ys, x, route, g_final.reshape(1, D_MODEL))


def _hier_moe(x, g, wg, bg, we, be, w1, w3, w2, g_final, final_norm):
    pad = LANES - N_GROUPS - N_EXPERTS
    w_r = jnp.concatenate([wg, we, jnp.zeros((D_MODEL, pad), F32)], axis=1)
    b_r = jnp.concatenate([bg, be, jnp.zeros((pad,), F32)]).reshape(1, LANES)
    route = _router(x, g, w_r, b_r)
    eid = route[:, :TOP_K].astype(jnp.int32)
    dest, row_tok, blk_e, blk_x, blk_rows = _dispatch_plan(eid)
    xs = _moe_gather(x, row_tok, blk_x, blk_rows)
    ys = _moe_experts(xs, g, w1, w3, w2, blk_e, blk_x, blk_rows)
    return _moe_combine(x, ys, route, dest, g_final, final_norm)


def kernel(x, norm_mix, norm_ffn, norm_final, w_in_ab, na_rpb, conv_w, conv_b, w_out_ab, w_in_cd,
           q_norm, kv_norm, w_uq, w_ukv, sg_norm, sg_w, sg_b, w_out_cd, router_group_w,
           router_group_b, router_expert_w, router_expert_b, w1, w3, w2):
    xt = x.reshape(N_TOK, D_MODEL)
    for layer in range(DEPTH):
        i = layer // 2
        if layer % 2 == 0:
            p = _norm_matmul(xt, norm_mix[layer], w_in_ab[i].astype(BF16), tm=1024, tn=512)
            a_out = _natten(p, _natten_bias_table(na_rpb[i]))
            b_out = _gated_conv(p, conv_w[i], conv_b[i])
            xt = _out_proj(a_out, b_out, w_out_ab[i].astype(BF16), xt)
        else:
            w = w_in_cd[i]
            c0, c1, c2 = Q_LORA + KV_LORA, Q_LORA + KV_LORA + QK_ROPE, CD_IN_PAD - KR_COL - QK_ROPE
            w = jnp.concatenate([w[:, :c0], w[:, c1:], w[:, c0:c1], jnp.zeros((D_MODEL, c2), F32)], axis=1)
            p = _norm_matmul(xt, norm_mix[layer], w.astype(BF16), tm=1024, tn=640)
            wq = w_uq[i].reshape(Q_LORA, MLA_HEADS, QK_DIM).transpose(1, 0, 2).astype(BF16)
            wkv = w_ukv[i].reshape(KV_LORA, MLA_HEADS, QK_NOPE + V_DIM).transpose(1, 0, 2).astype(BF16)
            q, k, v = _mla_prep(p, q_norm[i], kv_norm[i], wq, wkv)
            c_out = _mla_attn(q, k, v)
            d_out = _spatial_gating(p, sg_norm[i], sg_w[i], sg_b[i])
            xt = _out_proj(c_out, d_out, w_out_cd[i].astype(BF16), xt)
        xt = _hier_moe(xt, norm_ffn[layer], router_group_w[layer], router_group_b[layer],
                       router_expert_w[layer], router_expert_b[layer], w1[layer], w3[layer], w2[layer],
                       norm_final, final_norm=(layer == DEPTH - 1))
    return xt.reshape(BATCH, SEQ, D_MODEL)
```

```python
import functools

import numpy as np
import jax
import jax.numpy as jnp
from jax import lax
from jax.experimental import pallas as pl
from jax.experimental.pallas import tpu as pltpu

F32 = jnp.float32
BF16 = jnp.bfloat16

D_MODEL = 2048
BATCH = 2
SEQ = 4096
N_TOK = BATCH * SEQ
DEPTH = 2
GRID_W = 64
GRID_ROWS = SEQ // GRID_W
WIN_ROWS = 8
WIN_COLS = 16
NA_HEADS = 8
NA_HEAD_DIM = 128
NA_WIDTH = NA_HEADS * NA_HEAD_DIM
NA_KEYS = WIN_ROWS * GRID_W
CONV_CH = D_MODEL // 2
MLA_HEADS = 8
Q_LORA = 512
KV_LORA = 512
QK_NOPE = 128
QK_ROPE = 64
QK_DIM = QK_NOPE + QK_ROPE
V_DIM = 128
ROPE_THETA = 10000.0
SG_CH = D_MODEL // 2
SG_GROUPS = 8
CHUNK = 128
N_GROUPS = 8
EXPERTS_PER_GROUP = 8
N_EXPERTS = N_GROUPS * EXPERTS_PER_GROUP
TOP_K = 2
D_EXPERT = 768
EPS = 1e-6
NEG_INF = -1e30
AB_IN = 3 * NA_WIDTH + 3 * CONV_CH
CD_IN_PAD = 3200
KR_COL = Q_LORA + KV_LORA + 2 * SG_CH

LANES = 128
MOE_BLOCK = 512
MOE_SUB = 256
MOE_FT = 256
N_FT = D_EXPERT // MOE_FT
N_ASSIGN = N_TOK * TOP_K
MOE_NBLK = N_ASSIGN // MOE_BLOCK + N_EXPERTS
COMBINE_TOK = 256
VMEM_LIMIT = 52 * 1024 * 1024


def _params(*sem):
    return pltpu.CompilerParams(dimension_semantics=sem, vmem_limit_bytes=VMEM_LIMIT)


def _rms(x, g):
    return x * lax.rsqrt(jnp.mean(x * x, axis=-1, keepdims=True) + EPS) * g


def _norm_mm_kernel(x_ref, g_ref, w_ref, o_ref, xn_ref):
    @pl.when(pl.program_id(1) == 0)
    def _():
        xn_ref[...] = _rms(x_ref[...].astype(F32), g_ref[...]).astype(BF16)

    o_ref[...] = jnp.dot(xn_ref[...], w_ref[...], preferred_element_type=F32).astype(o_ref.dtype)


def _norm_matmul(x, g, w, tm, tn):
    m, k = x.shape
    nout = w.shape[1]
    return pl.pallas_call(
        _norm_mm_kernel,
        out_shape=jax.ShapeDtypeStruct((m, nout), BF16),
        grid=(m // tm, nout // tn),
        in_specs=[
            pl.BlockSpec((tm, k), lambda i, j: (i, 0)),
            pl.BlockSpec((1, k), lambda i, j: (0, 0)),
            pl.BlockSpec((k, tn), lambda i, j: (0, j)),
        ],
        out_specs=pl.BlockSpec((tm, tn), lambda i, j: (i, j)),
        scratch_shapes=[pltpu.VMEM((tm, k), BF16)],
        compiler_params=_params("parallel", "arbitrary"),
        name="norm_matmul",
    )(x, g.reshape(1, k), w)


def _natten_bias_table(rpb):
    c = np.arange(GRID_W)
    col_start = np.clip(c - WIN_COLS // 2, 0, GRID_W - WIN_COLS)
    valid = (c[None, :] >= col_start[:, None]) & (c[None, :] < col_start[:, None] + WIN_COLS)
    dc = np.clip(c[None, :] - c[:, None] + WIN_COLS - 1, 0, 2 * WIN_COLS - 2)
    pick = (dc[:, :, None] == np.arange(2 * WIN_COLS - 1)).astype(np.float32)
    m = jnp.einsum('hrd,ckd->hrck', rpb, jnp.asarray(pick), precision=lax.Precision.HIGHEST)
    m = jnp.where(valid[None, None], m, NEG_INF)
    t = jnp.stack([m[:, o:o + WIN_ROWS] for o in range(WIN_ROWS)], axis=1)
    return t.transpose(0, 1, 3, 2, 4).reshape(rpb.shape[0], WIN_ROWS, GRID_W, NA_KEYS).astype(F32)


def _natten_kernel(q_ref, k_ref, v_ref, t_ref, o_ref):
    scale = NA_HEAD_DIM ** -0.5

    def row(r, carry):
        kr0 = jnp.clip(r - WIN_ROWS // 2, 0, GRID_ROWS - WIN_ROWS)
        q0 = pl.multiple_of(r * GRID_W, GRID_W)
        k0 = pl.multiple_of(kr0 * GRID_W, GRID_W)
        q = q_ref[pl.ds(q0, GRID_W), :]
        k = k_ref[pl.ds(k0, NA_KEYS), :]
        v = v_ref[pl.ds(k0, NA_KEYS), :]
        s = lax.dot_general(q, k, (((1,), (1,)), ((), ())), preferred_element_type=F32)
        s = s * scale + t_ref[kr0 - r + WIN_ROWS - 1]
        p = jnp.exp(s - jnp.max(s, axis=-1, keepdims=True))
        l = jnp.sum(p, axis=-1, keepdims=True)
        o = jnp.dot(p.astype(BF16), v, preferred_element_type=F32) / l
        o_ref[pl.ds(q0, GRID_W), :] = o.astype(o_ref.dtype)
        return carry

    lax.fori_loop(0, GRID_ROWS, row, 0)


def _natten(p, table):
    blk = (SEQ, NA_HEAD_DIM)
    return pl.pallas_call(
        _natten_kernel,
        out_shape=jax.ShapeDtypeStruct((N_TOK, NA_WIDTH), BF16),
        grid=(BATCH, NA_HEADS),
        in_specs=[
            pl.BlockSpec(blk, lambda b, h: (b, h)),
            pl.BlockSpec(blk, lambda b, h: (b, NA_HEADS + h)),
            pl.BlockSpec(blk, lambda b, h: (b, 2 * NA_HEADS + h)),
            pl.BlockSpec((None, WIN_ROWS, GRID_W, NA_KEYS), lambda b, h: (h, 0, 0, 0)),
        ],
        out_specs=pl.BlockSpec(blk, lambda b, h: (b, h)),
        compiler_params=_params("parallel", "parallel"),
        name="natten",
    )(p, p, p, table)


CONV_CB = 128


def _conv_kernel(gb_ref, gc_ref, hc_ref, w_ref, b_ref, o_ref):
    z = gc_ref[...].astype(F32) * hc_ref[...].astype(F32)
    pos = lax.broadcasted_iota(jnp.int32, z.shape, 0)
    z_prev = jnp.where(pos == 0, 0.0, pltpu.roll(z, 1, 0))
    z_next = jnp.where(pos == SEQ - 1, 0.0, pltpu.roll(z, SEQ - 1, 0))
    y = b_ref[...] + z_prev * w_ref[0:1, :] + z * w_ref[1:2, :] + z_next * w_ref[2:3, :]
    o_ref[...] = (gb_ref[...].astype(F32) * y).astype(o_ref.dtype)


def _gated_conv(p, w, b):
    base = 3 * NA_WIDTH // CONV_CB
    step = CONV_CH // CONV_CB
    blk = (SEQ, CONV_CB)
    return pl.pallas_call(
        _conv_kernel,
        out_shape=jax.ShapeDtypeStruct((N_TOK, CONV_CH), BF16),
        grid=(BATCH, step),
        in_specs=[
            pl.BlockSpec(blk, lambda bi, c: (bi, base + c)),
            pl.BlockSpec(blk, lambda bi, c: (bi, base + step + c)),
            pl.BlockSpec(blk, lambda bi, c: (bi, base + 2 * step + c)),
            pl.BlockSpec((3, CONV_CB), lambda bi, c: (0, c)),
            pl.BlockSpec((1, CONV_CB), lambda bi, c: (0, c)),
        ],
        out_specs=pl.BlockSpec(blk, lambda bi, c: (bi, c)),
        compiler_params=_params("parallel", "parallel"),
        name="gated_conv",
    )(p, p, p, w, b.reshape(1, CONV_CH))


def _out_proj_kernel(a_ref, b_ref, wa_ref, wb_ref, r_ref, o_ref):
    acc = jnp.dot(a_ref[...], wa_ref[...], preferred_element_type=F32)
    acc = acc + jnp.dot(b_ref[...], wb_ref[...], preferred_element_type=F32)
    o_ref[...] = r_ref[...] + acc


def _out_proj(a, b, w, res, tm=512):
    ka, kb = a.shape[1], b.shape[1]
    assert ka == kb
    return pl.pallas_call(
        _out_proj_kernel,
        out_shape=jax.ShapeDtypeStruct((N_TOK, D_MODEL), F32),
        grid=(N_TOK // tm,),
        in_specs=[
            pl.BlockSpec((tm, ka), lambda i: (i, 0)),
            pl.BlockSpec((tm, kb), lambda i: (i, 0)),
            pl.BlockSpec((ka, D_MODEL), lambda i: (0, 0)),
            pl.BlockSpec((kb, D_MODEL), lambda i: (1, 0)),
            pl.BlockSpec((tm, D_MODEL), lambda i: (i, 0)),
        ],
        out_specs=pl.BlockSpec((tm, D_MODEL), lambda i: (i, 0)),
        compiler_params=_params("parallel"),
        name="out_proj",
    )(a, b, w, w, res)


MLA_TM = 512


def _rope_tables():
    pos = jnp.arange(SEQ, dtype=jnp.int32)
    row = (pos // GRID_W).astype(F32)
    col = (pos % GRID_W).astype(F32)
    half = QK_ROPE // 2
    inv = ROPE_THETA ** (-jnp.arange(0, half, 2, dtype=F32) / half)
    ar, ac = row[:, None] * inv, col[:, None] * inv
    cos_t = jnp.concatenate([jnp.cos(ar), jnp.cos(ar), jnp.cos(ac), jnp.cos(ac)], axis=-1)
    sin_t = jnp.concatenate([-jnp.sin(ar), jnp.sin(ar), -jnp.sin(ac), jnp.sin(ac)], axis=-1)
    quarter = half // 2
    src = np.arange(QK_ROPE) + np.where((np.arange(QK_ROPE) // quarter) % 2 == 0, quarter, -quarter)
    swap = np.zeros((QK_ROPE, QK_ROPE), np.float32)
    swap[src, np.arange(QK_ROPE)] = 1.0
    return cos_t, sin_t, jnp.asarray(swap)


def _rope(x, cos_t, sin_t, swap):
    xs = jnp.dot(x, swap, preferred_element_type=F32, precision=lax.Precision.HIGHEST)
    return x * cos_t + xs * sin_t


def _mla_prep_kernel(cq_ref, ckv_ref, kr_ref, gq_ref, gkv_ref, wq_ref, wkv_ref, cos_ref, sin_ref,
                     swap_ref, q_ref, k_ref, v_ref, cqn_ref, ckvn_ref, krot_ref):
    @pl.when(pl.program_id(1) == 0)
    def _():
        cqn_ref[...] = _rms(cq_ref[...].astype(F32), gq_ref[...]).astype(BF16)
        ckvn_ref[...] = _rms(ckv_ref[...].astype(F32), gkv_ref[...]).astype(BF16)
        kr = kr_ref[:, :QK_ROPE].astype(F32)
        krot_ref[...] = _rope(kr, cos_ref[...], sin_ref[...], swap_ref[...]).astype(BF16)

    scale = QK_DIM ** -0.5
    q = jnp.dot(cqn_ref[...], wq_ref[...], preferred_element_type=F32)
    q_ref[:, :QK_NOPE] = (q[:, :QK_NOPE] * scale).astype(BF16)
    qr = _rope(q[:, QK_NOPE:], cos_ref[...], sin_ref[...], swap_ref[...])
    q_ref[:, QK_NOPE:] = (qr * scale).astype(BF16)
    kv = jnp.dot(ckvn_ref[...], wkv_ref[...], preferred_element_type=F32)
    k_ref[:, :QK_NOPE] = kv[:, :QK_NOPE].astype(BF16)
    k_ref[:, QK_NOPE:] = krot_ref[...]
    v_ref[...] = kv[:, QK_NOPE:].astype(BF16)


def _mla_prep(p, q_norm, kv_norm, wq, wkv):
    cos_t, sin_t, swap = _rope_tables()
    tm = MLA_TM
    seq_blocks = SEQ // tm
    return pl.pallas_call(
        _mla_prep_kernel,
        out_shape=(
            jax.ShapeDtypeStruct((MLA_HEADS, N_TOK, QK_DIM), BF16),
            jax.ShapeDtypeStruct((MLA_HEADS, N_TOK, QK_DIM), BF16),
            jax.ShapeDtypeStruct((MLA_HEADS, N_TOK, V_DIM), BF16),
        ),
        grid=(N_TOK // tm, MLA_HEADS),
        in_specs=[
            pl.BlockSpec((tm, Q_LORA), lambda i, h: (i, 0)),
            pl.BlockSpec((tm, KV_LORA), lambda i, h: (i, 1)),
            pl.BlockSpec((tm, LANES), lambda i, h: (i, KR_COL // LANES)),
            pl.BlockSpec((1, Q_LORA), lambda i, h: (0, 0)),
            pl.BlockSpec((1, KV_LORA), lambda i, h: (0, 0)),
            pl.BlockSpec((None, Q_LORA, QK_DIM), lambda i, h: (h, 0, 0)),
            pl.BlockSpec((None, KV_LORA, QK_NOPE + V_DIM), lambda i, h: (h, 0, 0)),
            pl.BlockSpec((tm, QK_ROPE), lambda i, h: (i % seq_blocks, 0)),
            pl.BlockSpec((tm, QK_ROPE), lambda i, h: (i % seq_blocks, 0)),
            pl.BlockSpec((QK_ROPE, QK_ROPE), lambda i, h: (0, 0)),
        ],
        out_specs=(
            pl.BlockSpec((None, tm, QK_DIM), lambda i, h: (h, i, 0)),
            pl.BlockSpec((None, tm, QK_DIM), lambda i, h: (h, i, 0)),
            pl.BlockSpec((None, tm, V_DIM), lambda i, h: (h, i, 0)),
        ),
        scratch_shapes=[
            pltpu.VMEM((tm, Q_LORA), BF16),
            pltpu.VMEM((tm, KV_LORA), BF16),
            pltpu.VMEM((tm, QK_ROPE), BF16),
        ],
        compiler_params=_params("parallel", "arbitrary"),
        name="mla_prep",
    )(p, p, p, q_norm.reshape(1, Q_LORA), kv_norm.reshape(1, KV_LORA), wq, wkv, cos_t, sin_t, swap)


MLA_TQ = 512
MLA_TK = 512


def _mla_attn_kernel(q_ref, k_ref, v_ref, o_ref):
    q = q_ref[...]

    def chunk(c, carry):
        m, l, acc = carry
        k0 = pl.multiple_of(c * MLA_TK, MLA_TK)
        k = k_ref[pl.ds(k0, MLA_TK), :]
        v = v_ref[pl.ds(k0, MLA_TK), :]
        s = lax.dot_general(q, k, (((1,), (1,)), ((), ())), preferred_element_type=F32)
        m_new = jnp.maximum(m, jnp.max(s, axis=-1, keepdims=True))
        alpha = jnp.exp(m - m_new)
        p = jnp.exp(s - m_new)
        l = alpha * l + jnp.sum(p, axis=-1, keepdims=True)
        acc = alpha * acc + jnp.dot(p.astype(BF16), v, preferred_element_type=F32)
        return m_new, l, acc

    init = (jnp.full((MLA_TQ, 1), NEG_INF, F32), jnp.zeros((MLA_TQ, 1), F32),
            jnp.zeros((MLA_TQ, V_DIM), F32))
    _, l, acc = lax.fori_loop(0, SEQ // MLA_TK, chunk, init)
    o_ref[...] = (acc / l).astype(o_ref.dtype)


def _mla_attn(q, k, v):
    nq = SEQ // MLA_TQ
    return pl.pallas_call(
        _mla_attn_kernel,
        out_shape=jax.ShapeDtypeStruct((N_TOK, MLA_HEADS * V_DIM), BF16),
        grid=(BATCH, MLA_HEADS, nq),
        in_specs=[
            pl.BlockSpec((None, MLA_TQ, QK_DIM), lambda b, h, i: (h, b * nq + i, 0)),
            pl.BlockSpec((None, SEQ, QK_DIM), lambda b, h, i: (h, b, 0)),
            pl.BlockSpec((None, SEQ, V_DIM), lambda b, h, i: (h, b, 0)),
        ],
        out_specs=pl.BlockSpec((MLA_TQ, V_DIM), lambda b, h, i: (b * nq + i, h)),
        compiler_params=_params("parallel", "parallel", "parallel"),
        name="mla_attn",
    )(q, k, v)


SG_TM = 256


def _sg_kernel(u_ref, v_ref, g_ref, w_ref, bt_ref, o_ref):
    v = jax.nn.gelu(v_ref[...].astype(F32))
    vn = _rms(v, g_ref[...]).astype(BF16)
    u = jax.nn.gelu(u_ref[...].astype(F32))
    gc = SG_CH // SG_GROUPS
    for n in range(SG_TM // CHUNK):
        rows = slice(n * CHUNK, (n + 1) * CHUNK)
        for g in range(SG_GROUPS):
            cols = slice(g * gc, (g + 1) * gc)
            mixed = jnp.dot(w_ref[g], vn[rows, cols], preferred_element_type=F32) + bt_ref[:, g:g + 1]
            o_ref[rows, cols] = (u[rows, cols] * mixed).astype(o_ref.dtype)


def _spatial_gating(p, g_norm, w_s, b_s):
    u_blk = (Q_LORA + KV_LORA) // SG_CH
    return pl.pallas_call(
        _sg_kernel,
        out_shape=jax.ShapeDtypeStruct((N_TOK, SG_CH), BF16),
        grid=(N_TOK // SG_TM,),
        in_specs=[
            pl.BlockSpec((SG_TM, SG_CH), lambda i: (i, u_blk)),
            pl.BlockSpec((SG_TM, SG_CH), lambda i: (i, u_blk + 1)),
            pl.BlockSpec((1, SG_CH), lambda i: (0, 0)),
            pl.BlockSpec((SG_GROUPS, CHUNK, CHUNK), lambda i: (0, 0, 0)),
            pl.BlockSpec((CHUNK, SG_GROUPS), lambda i: (0, 0)),
        ],
        out_specs=pl.BlockSpec((SG_TM, SG_CH), lambda i: (i, 0)),
        compiler_params=_params("parallel"),
        name="spatial_gating",
    )(p, p, g_norm.reshape(1, SG_CH), w_s.astype(BF16), b_s.T)


ROUTER_TM = 512


def _router_kernel(x_ref, g_ref, w_ref, b_ref, o_ref):
    xn = _rms(x_ref[...], g_ref[...])
    logits = jnp.dot(xn, w_ref[...], preferred_element_type=F32,
                     precision=lax.Precision.HIGHEST) + b_ref[...]
    lane = lax.broadcasted_iota(jnp.int32, logits.shape, 1).astype(F32)
    low = jnp.float32(-3.0e38)

    def first_max(vals):
        top = jnp.max(vals, axis=-1, keepdims=True)
        idx = jnp.min(jnp.where(vals == top, lane, float(LANES)), axis=-1, keepdims=True)
        return top, idx

    is_group = lane < N_GROUPS
    g_top, g_idx = first_max(jnp.where(is_group, logits, low))
    g_prob = 1.0 / jnp.sum(jnp.where(is_group, jnp.exp(logits - g_top), 0.0), axis=-1, keepdims=True)
    lo = N_GROUPS + g_idx * EXPERTS_PER_GROUP
    e_vals = jnp.where(lane >= lo, jnp.where(lane < lo + EXPERTS_PER_GROUP, logits, low), low)
    v1, i1 = first_max(e_vals)
    v2, i2 = first_max(jnp.where(lane == i1, low, e_vals))
    e21 = jnp.exp(v2 - v1)
    w1 = g_prob / (1.0 + e21)
    w2 = g_prob * e21 / (1.0 + e21)
    out = jnp.where(lane == 0, i1 - N_GROUPS,
                    jnp.where(lane == 1, i2 - N_GROUPS,
                              jnp.where(lane == 2, w1, jnp.where(lane == 3, w2, 0.0))))
    o_ref[...] = out


def _router(x, g, w, b):
    tm = ROUTER_TM
    return pl.pallas_call(
        _router_kernel,
        out_shape=jax.ShapeDtypeStruct((N_TOK, LANES), F32),
        grid=(N_TOK // tm,),
        in_specs=[
            pl.BlockSpec((tm, D_MODEL), lambda i: (i, 0)),
            pl.BlockSpec((1, D_MODEL), lambda i: (0, 0)),
            pl.BlockSpec((D_MODEL, LANES), lambda i: (0, 0)),
            pl.BlockSpec((1, LANES), lambda i: (0, 0)),
        ],
        out_specs=pl.BlockSpec((tm, LANES), lambda i: (i, 0)),
        compiler_params=_params("parallel"),
        name="router",
    )(x, g.reshape(1, D_MODEL), w, b)


def _dispatch_plan(eid):
    eid_flat = eid.reshape(-1)
    onehot = eid_flat[:, None] == jnp.arange(N_EXPERTS, dtype=jnp.int32)[None, :]
    csum = jnp.cumsum(onehot.astype(jnp.int32), axis=0)
    rank = jnp.sum(jnp.where(onehot, csum, 0), axis=1) - 1
    counts = csum[-1]
    nblk_e = (counts + MOE_BLOCK - 1) // MOE_BLOCK
    blk_end = jnp.cumsum(nblk_e)
    blk_start = blk_end - nblk_e
    dest = (blk_start * MOE_BLOCK)[eid_flat] + rank
    n_active = blk_end[-1]
    bidx = jnp.arange(MOE_NBLK, dtype=jnp.int32)
    blk_x = jnp.minimum(bidx, n_active - 1)
    blk_e = jnp.minimum(jnp.searchsorted(blk_end, blk_x, side='right'), N_EXPERTS - 1).astype(jnp.int32)
    rows = jnp.clip(counts[blk_e] - (blk_x - blk_start[blk_e]) * MOE_BLOCK, 0, MOE_BLOCK)
    blk_rows = jnp.where(bidx < n_active, rows, 0).astype(jnp.int32)
    tok = jnp.arange(N_ASSIGN, dtype=jnp.int32) // TOP_K
    row_tok = jnp.zeros((MOE_NBLK * MOE_BLOCK,), jnp.int32).at[dest].set(tok)
    return dest.astype(jnp.int32), row_tok, blk_e, blk_x.astype(jnp.int32), blk_rows


def _gather_kernel(row_tok, blk_x, blk_rows, x_hbm, o_ref, sem):
    b = pl.program_id(0)
    rows = blk_rows[b]
    base = blk_x[b] * MOE_BLOCK

    def row_copy(i, t):
        return pltpu.make_async_copy(x_hbm.at[pl.ds(t, 1), :], o_ref.at[pl.ds(i, 1), :], sem)

    @pl.when(rows > 0)
    def _():
        n_fetch = jnp.where(rows > MOE_SUB, MOE_BLOCK, MOE_SUB)

        @pl.when(rows <= MOE_SUB)
        def _():
            o_ref[MOE_SUB:, :] = jnp.zeros((MOE_BLOCK - MOE_SUB, D_MODEL), F32)

        def start(i, c):
            row_copy(i, row_tok[base + i]).start()
            return c

        def wait(i, c):
            row_copy(0, 0).wait()
            return c

        lax.fori_loop(0, n_fetch, start, 0)
        lax.fori_loop(0, n_fetch, wait, 0)


def _moe_gather(x, row_tok, blk_x, blk_rows):
    return pl.pallas_call(
        _gather_kernel,
        out_shape=jax.ShapeDtypeStruct((MOE_NBLK * MOE_BLOCK, D_MODEL), F32),
        grid_spec=pltpu.PrefetchScalarGridSpec(
            num_scalar_prefetch=3,
            grid=(MOE_NBLK,),
            in_specs=[pl.BlockSpec(memory_space=pl.ANY)],
            out_specs=pl.BlockSpec((MOE_BLOCK, D_MODEL), lambda b, rt, bx, br: (bx[b], 0)),
            scratch_shapes=[pltpu.SemaphoreType.DMA(())],
        ),
        compiler_params=_params("arbitrary"),
        name="moe_gather",
    )(row_tok, blk_x, blk_rows, x)


def _expert_kernel(blk_e, blk_x, blk_rows, x_ref, g_ref, w1_ref, w3_ref, w2_ref, o_ref,
                   xn_ref, w1b_ref, w3b_ref, w2b_ref):
    b = pl.program_id(0)
    j = pl.program_id(1)
    rows = blk_rows[b]

    @pl.when(rows > 0)
    def _():
        @pl.when(j == 0)
        def _():
            xn_ref[...] = _rms(x_ref[...], g_ref[...]).astype(BF16)

        w1b_ref[...] = w1_ref[...].astype(BF16)
        w3b_ref[...] = w3_ref[...].astype(BF16)
        w2b_ref[...] = w2_ref[...].astype(BF16)

        def sub_block(s):
            sl = slice(s * MOE_SUB, (s + 1) * MOE_SUB)
            xs = xn_ref[sl, :]
            h1 = jnp.dot(xs, w1b_ref[...], preferred_element_type=F32)
            h3 = jnp.dot(xs, w3b_ref[...], preferred_element_type=F32)
            h = (jax.nn.silu(h1) * h3).astype(BF16)
            y = jnp.dot(h, w2b_ref[...], preferred_element_type=F32)

            @pl.when(j == 0)
            def _():
                o_ref[sl, :] = y

            @pl.when(j > 0)
            def _():
                o_ref[sl, :] += y

        sub_block(0)

        @pl.when(rows > MOE_SUB)
        def _():
            sub_block(1)

        @pl.when(jnp.logical_and(rows <= MOE_SUB, j == 0))
        def _():
            o_ref[MOE_SUB:, :] = jnp.zeros((MOE_BLOCK - MOE_SUB, D_MODEL), F32)


def _moe_experts(xs, g, w1, w3, w2, layer, blk_e, blk_x, blk_rows):
    def w_col(b, j, be, bx, br):
        return (layer, be[b], 0, jnp.where(br[b] > 0, j, N_FT - 1))

    def w_row(b, j, be, bx, br):
        return (layer, be[b], jnp.where(br[b] > 0, j, N_FT - 1), 0)

    def x_map(b, j, be, bx, br):
        return (bx[b], 0)

    return pl.pallas_call(
        _expert_kernel,
        out_shape=jax.ShapeDtypeStruct((MOE_NBLK * MOE_BLOCK, D_MODEL), F32),
        grid_spec=pltpu.PrefetchScalarGridSpec(
            num_scalar_prefetch=3,
            grid=(MOE_NBLK, N_FT),
            in_specs=[
                pl.BlockSpec((MOE_BLOCK, D_MODEL), x_map),
                pl.BlockSpec((1, D_MODEL), lambda b, j, be, bx, br: (0, 0)),
                pl.BlockSpec((None, None, D_MODEL, MOE_FT), w_col),
                pl.BlockSpec((None, None, D_MODEL, MOE_FT), w_col),
                pl.BlockSpec((None, None, MOE_FT, D_MODEL), w_row),
            ],
            out_specs=pl.BlockSpec((MOE_BLOCK, D_MODEL), x_map),
            scratch_shapes=[
                pltpu.VMEM((MOE_BLOCK, D_MODEL), BF16),
                pltpu.VMEM((D_MODEL, MOE_FT), BF16),
                pltpu.VMEM((D_MODEL, MOE_FT), BF16),
                pltpu.VMEM((MOE_FT, D_MODEL), BF16),
            ],
        ),
        compiler_params=_params("arbitrary", "arbitrary"),
        name="moe_experts",
    )(blk_e, blk_x, blk_rows, xs, g.reshape(1, D_MODEL), w1, w3, w2)


def _combine_kernel(final_norm, dest, ys_hbm, x_ref, gate_ref, gf_ref, o_ref, ya_ref, yb_ref, sem):
    base = pl.program_id(0) * COMBINE_TOK * TOP_K

    def row_copy(dst_ref, t, d, s):
        return pltpu.make_async_copy(ys_hbm.at[pl.ds(d, 1), :], dst_ref.at[pl.ds(t, 1), :], sem.at[s])

    def start(t, c):
        row_copy(ya_ref, t, dest[base + TOP_K * t], 0).start()
        row_copy(yb_ref, t, dest[base + TOP_K * t + 1], 1).start()
        return c

    def wait(t, c):
        row_copy(ya_ref, 0, 0, 0).wait()
        row_copy(yb_ref, 0, 0, 1).wait()
        return c

    lax.fori_loop(0, COMBINE_TOK, start, 0)
    lax.fori_loop(0, COMBINE_TOK, wait, 0)
    gates = gate_ref[...]
    out = x_ref[...] + (gates[:, 2:3] * ya_ref[...] + gates[:, 3:4] * yb_ref[...])
    if final_norm:
        out = _rms(out, gf_ref[...])
    o_ref[...] = out


def _moe_combine(x, ys, route, dest, g_final, final_norm):
    tm = COMBINE_TOK
    return pl.pallas_call(
        functools.partial(_combine_kernel, final_norm),
        out_shape=jax.ShapeDtypeStruct((N_TOK, D_MODEL), F32),
        grid_spec=pltpu.PrefetchScalarGridSpec(
            num_scalar_prefetch=1,
            grid=(N_TOK // tm,),
            in_specs=[
                pl.BlockSpec(memory_space=pl.ANY),
                pl.BlockSpec((tm, D_MODEL), lambda i, d: (i, 0)),
                pl.BlockSpec((tm, LANES), lambda i, d: (i, 0)),
                pl.BlockSpec((1, D_MODEL), lambda i, d: (0, 0)),
            ],
            out_specs=pl.BlockSpec((tm, D_MODEL), lambda i, d: (i, 0)),
            scratch_shapes=[
                pltpu.VMEM((tm, D_MODEL), F32),
                pltpu.VMEM((tm, D_MODEL), F32),
                pltpu.SemaphoreType.DMA((2,)),
            ],
        ),
        compiler_params=_params("arbitrary"),
        name="moe_combine",
    )(dest, ys, x, route, g_final.reshape(1, D_MODEL))


def _hier_moe(x, g, wg, bg, we, be, w1, w3, w2, layer, g_final, final_norm):
    pad = LANES - N_GROUPS - N_EXPERTS
    w_r = jnp.concatenate([wg, we, jnp.zeros((D_MODEL, pad), F32)], axis=1)
    b_r = jnp.concatenate([bg, be, jnp.zeros((pad,), F32)]).reshape(1, LANES)
    route = _router(x, g, w_r, b_r)
    eid = route[:, :TOP_K].astype(jnp.int32)
    dest, row_tok, blk_e, blk_x, blk_rows = _dispatch_plan(eid)
    xs = _moe_gather(x, row_tok, blk_x, blk_rows)
    ys = _moe_experts(xs, g, w1, w3, w2, layer, blk_e, blk_x, blk_rows)
    return _moe_combine(x, ys, route, dest, g_final, final_norm)


def kernel(x, norm_mix, norm_ffn, norm_final, w_in_ab, na_rpb, conv_w, conv_b, w_out_ab, w_in_cd,
           q_norm, kv_norm, w_uq, w_ukv, sg_norm, sg_w, sg_b, w_out_cd, router_group_w,
           router_group_b, router_expert_w, router_expert_b, w1, w3, w2):
    xt = x.reshape(N_TOK, D_MODEL)
    for layer in range(DEPTH):
        i = layer // 2
        if layer % 2 == 0:
            p = _norm_matmul(xt, norm_mix[layer], w_in_ab[i].astype(BF16), tm=1024, tn=512)
            a_out = _natten(p, _natten_bias_table(na_rpb[i]))
            b_out = _gated_conv(p, conv_w[i], conv_b[i])
            xt = _out_proj(a_out, b_out, w_out_ab[i].astype(BF16), xt)
        else:
            w = w_in_cd[i]
            c0, c1, c2 = Q_LORA + KV_LORA, Q_LORA + KV_LORA + QK_ROPE, CD_IN_PAD - KR_COL - QK_ROPE
            w = jnp.concatenate([w[:, :c0], w[:, c1:], w[:, c0:c1], jnp.zeros((D_MODEL, c2), F32)], axis=1)
            p = _norm_matmul(xt, norm_mix[layer], w.astype(BF16), tm=1024, tn=640)
            wq = w_uq[i].reshape(Q_LORA, MLA_HEADS, QK_DIM).transpose(1, 0, 2).astype(BF16)
            wkv = w_ukv[i].reshape(KV_LORA, MLA_HEADS, QK_NOPE + V_DIM).transpose(1, 0, 2).astype(BF16)
            q, k, v = _mla_prep(p, q_norm[i], kv_norm[i], wq, wkv)
            c_out = _mla_attn(q, k, v)
            d_out = _spatial_gating(p, sg_norm[i], sg_w[i], sg_b[i])
            xt = _out_proj(c_out, d_out, w_out_cd[i].astype(BF16), xt)
        xt = _hier_moe(xt, norm_ffn[layer], router_group_w[layer], router_group_b[layer],
                       router_expert_w[layer], router_expert_b[layer], w1, w3, w2, layer,
                       norm_final, final_norm=(layer == DEPTH - 1))
    return xt.reshape(BATCH, SEQ, D_MODEL)
```

```python
import functools

import numpy as np
import jax
import jax.numpy as jnp
from jax import lax
from jax.experimental import pallas as pl
from jax.experimental.pallas import tpu as pltpu

F32 = jnp.float32
BF16 = jnp.bfloat16

D_MODEL = 2048
BATCH = 2
SEQ = 4096
N_TOK = BATCH * SEQ
DEPTH = 2
GRID_W = 64
GRID_ROWS = SEQ // GRID_W
WIN_ROWS = 8
WIN_COLS = 16
NA_HEADS = 8
NA_HEAD_DIM = 128
NA_WIDTH = NA_HEADS * NA_HEAD_DIM
NA_KEYS = WIN_ROWS * GRID_W
CONV_CH = D_MODEL // 2
MLA_HEADS = 8
Q_LORA = 512
KV_LORA = 512
QK_NOPE = 128
QK_ROPE = 64
QK_DIM = QK_NOPE + QK_ROPE
V_DIM = 128
ROPE_THETA = 10000.0
SG_CH = D_MODEL // 2
SG_GROUPS = 8
CHUNK = 128
N_GROUPS = 8
EXPERTS_PER_GROUP = 8
N_EXPERTS = N_GROUPS * EXPERTS_PER_GROUP
TOP_K = 2
D_EXPERT = 768
EPS = 1e-6
NEG_INF = -1e30
AB_IN = 3 * NA_WIDTH + 3 * CONV_CH
CD_IN_PAD = 3200
KR_COL = Q_LORA + KV_LORA + 2 * SG_CH

LANES = 128
MOE_BLOCK = 512
MOE_SUB = 256
MOE_FT = 256
N_FT = D_EXPERT // MOE_FT
N_ASSIGN = N_TOK * TOP_K
MOE_NBLK = N_ASSIGN // MOE_BLOCK + N_EXPERTS
COMBINE_TOK = 256
VMEM_LIMIT = 52 * 1024 * 1024


def _params(*sem):
    return pltpu.CompilerParams(dimension_semantics=sem, vmem_limit_bytes=VMEM_LIMIT)


def _rms(x, g):
    return x * lax.rsqrt(jnp.mean(x * x, axis=-1, keepdims=True) + EPS) * g


def _norm_mm_kernel(x_ref, g_ref, w_ref, o_ref, xn_ref):
    @pl.when(pl.program_id(1) == 0)
    def _():
        xn_ref[...] = _rms(x_ref[...].astype(F32), g_ref[...]).astype(BF16)

    o_ref[...] = jnp.dot(xn_ref[...], w_ref[...], preferred_element_type=F32).astype(o_ref.dtype)


def _norm_matmul(x, g, w, tm, tn):
    m, k = x.shape
    nout = w.shape[1]
    return pl.pallas_call(
        _norm_mm_kernel,
        out_shape=jax.ShapeDtypeStruct((m, nout), BF16),
        grid=(m // tm, nout // tn),
        in_specs=[
            pl.BlockSpec((tm, k), lambda i, j: (i, 0)),
            pl.BlockSpec((1, k), lambda i, j: (0, 0)),
            pl.BlockSpec((k, tn), lambda i, j: (0, j)),
        ],
        out_specs=pl.BlockSpec((tm, tn), lambda i, j: (i, j)),
        scratch_shapes=[pltpu.VMEM((tm, k), BF16)],
        compiler_params=_params("parallel", "arbitrary"),
        name="norm_matmul",
    )(x, g.reshape(1, k), w)


def _natten_bias_table(rpb):
    c = np.arange(GRID_W)
    col_start = np.clip(c - WIN_COLS // 2, 0, GRID_W - WIN_COLS)
    valid = (c[None, :] >= col_start[:, None]) & (c[None, :] < col_start[:, None] + WIN_COLS)
    dc = np.clip(c[None, :] - c[:, None] + WIN_COLS - 1, 0, 2 * WIN_COLS - 2)
    pick = (dc[:, :, None] == np.arange(2 * WIN_COLS - 1)).astype(np.float32)
    m = jnp.einsum('hrd,ckd->hrck', rpb, jnp.asarray(pick), precision=lax.Precision.HIGHEST)
    m = jnp.where(valid[None, None], m, NEG_INF)
    t = jnp.stack([m[:, o:o + WIN_ROWS] for o in range(WIN_ROWS)], axis=1)
    return t.transpose(0, 1, 3, 2, 4).reshape(rpb.shape[0], WIN_ROWS, GRID_W, NA_KEYS).astype(F32)


def _natten_kernel(q_ref, k_ref, v_ref, t_ref, o_ref):
    scale = NA_HEAD_DIM ** -0.5

    def row(r, carry):
        kr0 = jnp.clip(r - WIN_ROWS // 2, 0, GRID_ROWS - WIN_ROWS)
        q0 = pl.multiple_of(r * GRID_W, GRID_W)
        k0 = pl.multiple_of(kr0 * GRID_W, GRID_W)
        q = q_ref[pl.ds(q0, GRID_W), :]
        k = k_ref[pl.ds(k0, NA_KEYS), :]
        v = v_ref[pl.ds(k0, NA_KEYS), :]
        s = lax.dot_general(q, k, (((1,), (1,)), ((), ())), preferred_element_type=F32)
        s = s * scale + t_ref[kr0 - r + WIN_ROWS - 1]
        p = jnp.exp(s - jnp.max(s, axis=-1, keepdims=True))
        l = jnp.sum(p, axis=-1, keepdims=True)
        o = jnp.dot(p.astype(BF16), v, preferred_element_type=F32) / l
        o_ref[pl.ds(q0, GRID_W), :] = o.astype(o_ref.dtype)
        return carry

    lax.fori_loop(0, GRID_ROWS, row, 0)


def _natten(p, table):
    blk = (SEQ, NA_HEAD_DIM)
    return pl.pallas_call(
        _natten_kernel,
        out_shape=jax.ShapeDtypeStruct((N_TOK, NA_WIDTH), BF16),
        grid=(BATCH, NA_HEADS),
        in_specs=[
            pl.BlockSpec(blk, lambda b, h: (b, h)),
            pl.BlockSpec(blk, lambda b, h: (b, NA_HEADS + h)),
            pl.BlockSpec(blk, lambda b, h: (b, 2 * NA_HEADS + h)),
            pl.BlockSpec((None, WIN_ROWS, GRID_W, NA_KEYS), lambda b, h: (h, 0, 0, 0)),
        ],
        out_specs=pl.BlockSpec(blk, lambda b, h: (b, h)),
        compiler_params=_params("parallel", "parallel"),
        name="natten",
    )(p, p, p, table)


CONV_CB = 128


def _conv_kernel(gb_ref, gc_ref, hc_ref, w_ref, b_ref, o_ref):
    z = gc_ref[...].astype(F32) * hc_ref[...].astype(F32)
    pos = lax.broadcasted_iota(jnp.int32, z.shape, 0)
    z_prev = jnp.where(pos == 0, 0.0, pltpu.roll(z, 1, 0))
    z_next = jnp.where(pos == SEQ - 1, 0.0, pltpu.roll(z, SEQ - 1, 0))
    y = b_ref[...] + z_prev * w_ref[0:1, :] + z * w_ref[1:2, :] + z_next * w_ref[2:3, :]
    o_ref[...] = (gb_ref[...].astype(F32) * y).astype(o_ref.dtype)


def _gated_conv(p, w, b):
    base = 3 * NA_WIDTH // CONV_CB
    step = CONV_CH // CONV_CB
    blk = (SEQ, CONV_CB)
    return pl.pallas_call(
        _conv_kernel,
        out_shape=jax.ShapeDtypeStruct((N_TOK, CONV_CH), BF16),
        grid=(BATCH, step),
        in_specs=[
            pl.BlockSpec(blk, lambda bi, c: (bi, base + c)),
            pl.BlockSpec(blk, lambda bi, c: (bi, base + step + c)),
            pl.BlockSpec(blk, lambda bi, c: (bi, base + 2 * step + c)),
            pl.BlockSpec((3, CONV_CB), lambda bi, c: (0, c)),
            pl.BlockSpec((1, CONV_CB), lambda bi, c: (0, c)),
        ],
        out_specs=pl.BlockSpec(blk, lambda bi, c: (bi, c)),
        compiler_params=_params("parallel", "parallel"),
        name="gated_conv",
    )(p, p, p, w, b.reshape(1, CONV_CH))


def _out_proj_kernel(a_ref, b_ref, wa_ref, wb_ref, r_ref, o_ref):
    acc = jnp.dot(a_ref[...], wa_ref[...], preferred_element_type=F32)
    acc = acc + jnp.dot(b_ref[...], wb_ref[...], preferred_element_type=F32)
    o_ref[...] = r_ref[...] + acc


def _out_proj(a, b, w, res, tm=512):
    ka, kb = a.shape[1], b.shape[1]
    assert ka == kb
    return pl.pallas_call(
        _out_proj_kernel,
        out_shape=jax.ShapeDtypeStruct((N_TOK, D_MODEL), F32),
        grid=(N_TOK // tm,),
        in_specs=[
            pl.BlockSpec((tm, ka), lambda i: (i, 0)),
            pl.BlockSpec((tm, kb), lambda i: (i, 0)),
            pl.BlockSpec((ka, D_MODEL), lambda i: (0, 0)),
            pl.BlockSpec((kb, D_MODEL), lambda i: (1, 0)),
            pl.BlockSpec((tm, D_MODEL), lambda i: (i, 0)),
        ],
        out_specs=pl.BlockSpec((tm, D_MODEL), lambda i: (i, 0)),
        compiler_params=_params("parallel"),
        name="out_proj",
    )(a, b, w, w, res)


MLA_TM = 512


def _rope_tables():
    pos = jnp.arange(SEQ, dtype=jnp.int32)
    row = (pos // GRID_W).astype(F32)
    col = (pos % GRID_W).astype(F32)
    half = QK_ROPE // 2
    inv = ROPE_THETA ** (-jnp.arange(0, half, 2, dtype=F32) / half)
    ar, ac = row[:, None] * inv, col[:, None] * inv
    cos_t = jnp.concatenate([jnp.cos(ar), jnp.cos(ar), jnp.cos(ac), jnp.cos(ac)], axis=-1)
    sin_t = jnp.concatenate([-jnp.sin(ar), jnp.sin(ar), -jnp.sin(ac), jnp.sin(ac)], axis=-1)
    quarter = half // 2
    src = np.arange(QK_ROPE) + np.where((np.arange(QK_ROPE) // quarter) % 2 == 0, quarter, -quarter)
    swap = np.zeros((QK_ROPE, QK_ROPE), np.float32)
    swap[src, np.arange(QK_ROPE)] = 1.0
    return cos_t, sin_t, jnp.asarray(swap)


def _rope(x, cos_t, sin_t, swap):
    xs = jnp.dot(x, swap, preferred_element_type=F32, precision=lax.Precision.HIGHEST)
    return x * cos_t + xs * sin_t


def _mla_prep_kernel(cq_ref, ckv_ref, kr_ref, gq_ref, gkv_ref, wq_ref, wkv_ref, cos_ref, sin_ref,
                     swap_ref, q_ref, k_ref, v_ref, cqn_ref, ckvn_ref, krot_ref):
    @pl.when(pl.program_id(1) == 0)
    def _():
        cqn_ref[...] = _rms(cq_ref[...].astype(F32), gq_ref[...]).astype(BF16)
        ckvn_ref[...] = _rms(ckv_ref[...].astype(F32), gkv_ref[...]).astype(BF16)
        kr = kr_ref[:, :QK_ROPE].astype(F32)
        krot_ref[...] = _rope(kr, cos_ref[...], sin_ref[...], swap_ref[...]).astype(BF16)

    scale = QK_DIM ** -0.5
    q = jnp.dot(cqn_ref[...], wq_ref[...], preferred_element_type=F32)
    q_ref[:, :QK_NOPE] = (q[:, :QK_NOPE] * scale).astype(BF16)
    qr = _rope(q[:, QK_NOPE:], cos_ref[...], sin_ref[...], swap_ref[...])
    q_ref[:, QK_NOPE:] = (qr * scale).astype(BF16)
    kv = jnp.dot(ckvn_ref[...], wkv_ref[...], preferred_element_type=F32)
    k_ref[:, :QK_NOPE] = kv[:, :QK_NOPE].astype(BF16)
    k_ref[:, QK_NOPE:] = krot_ref[...]
    v_ref[...] = kv[:, QK_NOPE:].astype(BF16)


def _mla_prep(p, q_norm, kv_norm, wq, wkv):
    cos_t, sin_t, swap = _rope_tables()
    tm = MLA_TM
    seq_blocks = SEQ // tm
    return pl.pallas_call(
        _mla_prep_kernel,
        out_shape=(
            jax.ShapeDtypeStruct((MLA_HEADS, N_TOK, QK_DIM), BF16),
            jax.ShapeDtypeStruct((MLA_HEADS, N_TOK, QK_DIM), BF16),
            jax.ShapeDtypeStruct((MLA_HEADS, N_TOK, V_DIM), BF16),
        ),
        grid=(N_TOK // tm, MLA_HEADS),
        in_specs=[
            pl.BlockSpec((tm, Q_LORA), lambda i, h: (i, 0)),
            pl.BlockSpec((tm, KV_LORA), lambda i, h: (i, 1)),
            pl.BlockSpec((tm, LANES), lambda i, h: (i, KR_COL // LANES)),
            pl.BlockSpec((1, Q_LORA), lambda i, h: (0, 0)),
            pl.BlockSpec((1, KV_LORA), lambda i, h: (0, 0)),
            pl.BlockSpec((None, Q_LORA, QK_DIM), lambda i, h: (h, 0, 0)),
            pl.BlockSpec((None, KV_LORA, QK_NOPE + V_DIM), lambda i, h: (h, 0, 0)),
            pl.BlockSpec((tm, QK_ROPE), lambda i, h: (i % seq_blocks, 0)),
            pl.BlockSpec((tm, QK_ROPE), lambda i, h: (i % seq_blocks, 0)),
            pl.BlockSpec((QK_ROPE, QK_ROPE), lambda i, h: (0, 0)),
        ],
        out_specs=(
            pl.BlockSpec((None, tm, QK_DIM), lambda i, h: (h, i, 0)),
            pl.BlockSpec((None, tm, QK_DIM), lambda i, h: (h, i, 0)),
            pl.BlockSpec((None, tm, V_DIM), lambda i, h: (h, i, 0)),
        ),
        scratch_shapes=[
            pltpu.VMEM((tm, Q_LORA), BF16),
            pltpu.VMEM((tm, KV_LORA), BF16),
            pltpu.VMEM((tm, QK_ROPE), BF16),
        ],
        compiler_params=_params("parallel", "arbitrary"),
        name="mla_prep",
    )(p, p, p, q_norm.reshape(1, Q_LORA), kv_norm.reshape(1, KV_LORA), wq, wkv, cos_t, sin_t, swap)


MLA_TQ = 512
MLA_TK = 512


def _mla_attn_kernel(q_ref, k_ref, v_ref, o_ref):
    q = q_ref[...]

    def chunk(c, carry):
        m, l, acc = carry
        k0 = pl.multiple_of(c * MLA_TK, MLA_TK)
        k = k_ref[pl.ds(k0, MLA_TK), :]
        v = v_ref[pl.ds(k0, MLA_TK), :]
        s = lax.dot_general(q, k, (((1,), (1,)), ((), ())), preferred_element_type=F32)
        m_new = jnp.maximum(m, jnp.max(s, axis=-1, keepdims=True))
        alpha = jnp.exp(m - m_new)
        p = jnp.exp(s - m_new)
        l = alpha * l + jnp.sum(p, axis=-1, keepdims=True)
        acc = alpha * acc + jnp.dot(p.astype(BF16), v, preferred_element_type=F32)
        return m_new, l, acc

    init = (jnp.full((MLA_TQ, 1), NEG_INF, F32), jnp.zeros((MLA_TQ, 1), F32),
            jnp.zeros((MLA_TQ, V_DIM), F32))
    _, l, acc = lax.fori_loop(0, SEQ // MLA_TK, chunk, init)
    o_ref[...] = (acc / l).astype(o_ref.dtype)


def _mla_attn(q, k, v):
    nq = SEQ // MLA_TQ
    return pl.pallas_call(
        _mla_attn_kernel,
        out_shape=jax.ShapeDtypeStruct((N_TOK, MLA_HEADS * V_DIM), BF16),
        grid=(BATCH, MLA_HEADS, nq),
        in_specs=[
            pl.BlockSpec((None, MLA_TQ, QK_DIM), lambda b, h, i: (h, b * nq + i, 0)),
            pl.BlockSpec((None, SEQ, QK_DIM), lambda b, h, i: (h, b, 0)),
            pl.BlockSpec((None, SEQ, V_DIM), lambda b, h, i: (h, b, 0)),
        ],
        out_specs=pl.BlockSpec((MLA_TQ, V_DIM), lambda b, h, i: (b * nq + i, h)),
        compiler_params=_params("parallel", "parallel", "parallel"),
        name="mla_attn",
    )(q, k, v)


SG_TM = 256


def _sg_kernel(u_ref, v_ref, g_ref, w_ref, bt_ref, o_ref):
    v = jax.nn.gelu(v_ref[...].astype(F32))
    vn = _rms(v, g_ref[...]).astype(BF16)
    u = jax.nn.gelu(u_ref[...].astype(F32))
    gc = SG_CH // SG_GROUPS
    for n in range(SG_TM // CHUNK):
        rows = slice(n * CHUNK, (n + 1) * CHUNK)
        for g in range(SG_GROUPS):
            cols = slice(g * gc, (g + 1) * gc)
            mixed = jnp.dot(w_ref[g], vn[rows, cols], preferred_element_type=F32) + bt_ref[:, g:g + 1]
            o_ref[rows, cols] = (u[rows, cols] * mixed).astype(o_ref.dtype)


def _spatial_gating(p, g_norm, w_s, b_s):
    u_blk = (Q_LORA + KV_LORA) // SG_CH
    return pl.pallas_call(
        _sg_kernel,
        out_shape=jax.ShapeDtypeStruct((N_TOK, SG_CH), BF16),
        grid=(N_TOK // SG_TM,),
        in_specs=[
            pl.BlockSpec((SG_TM, SG_CH), lambda i: (i, u_blk)),
            pl.BlockSpec((SG_TM, SG_CH), lambda i: (i, u_blk + 1)),
            pl.BlockSpec((1, SG_CH), lambda i: (0, 0)),
            pl.BlockSpec((SG_GROUPS, CHUNK, CHUNK), lambda i: (0, 0, 0)),
            pl.BlockSpec((CHUNK, SG_GROUPS), lambda i: (0, 0)),
        ],
        out_specs=pl.BlockSpec((SG_TM, SG_CH), lambda i: (i, 0)),
        compiler_params=_params("parallel"),
        name="spatial_gating",
    )(p, p, g_norm.reshape(1, SG_CH), w_s.astype(BF16), b_s.T)


ROUTER_TM = 512


def _router_kernel(x_ref, g_ref, w_ref, b_ref, o_ref):
    xn = _rms(x_ref[...], g_ref[...])
    logits = jnp.dot(xn, w_ref[...], preferred_element_type=F32,
                     precision=lax.Precision.HIGHEST) + b_ref[...]
    lane = lax.broadcasted_iota(jnp.int32, logits.shape, 1).astype(F32)
    low = jnp.float32(-3.0e38)

    def first_max(vals):
        top = jnp.max(vals, axis=-1, keepdims=True)
        idx = jnp.min(jnp.where(vals == top, lane, float(LANES)), axis=-1, keepdims=True)
        return top, idx

    is_group = lane < N_GROUPS
    g_top, g_idx = first_max(jnp.where(is_group, logits, low))
    g_prob = 1.0 / jnp.sum(jnp.where(is_group, jnp.exp(logits - g_top), 0.0), axis=-1, keepdims=True)
    lo = N_GROUPS + g_idx * EXPERTS_PER_GROUP
    e_vals = jnp.where(lane >= lo, jnp.where(lane < lo + EXPERTS_PER_GROUP, logits, low), low)
    v1, i1 = first_max(e_vals)
    v2, i2 = first_max(jnp.where(lane == i1, low, e_vals))
    e21 = jnp.exp(v2 - v1)
    w1 = g_prob / (1.0 + e21)
    w2 = g_prob * e21 / (1.0 + e21)
    out = jnp.where(lane == 0, i1 - N_GROUPS,
                    jnp.where(lane == 1, i2 - N_GROUPS,
                              jnp.where(lane == 2, w1, jnp.where(lane == 3, w2, 0.0))))
    o_ref[...] = out


def _router(x, g, w, b):
    tm = ROUTER_TM
    return pl.pallas_call(
        _router_kernel,
        out_shape=jax.ShapeDtypeStruct((N_TOK, LANES), F32),
        grid=(N_TOK // tm,),
        in_specs=[
            pl.BlockSpec((tm, D_MODEL), lambda i: (i, 0)),
            pl.BlockSpec((1, D_MODEL), lambda i: (0, 0)),
            pl.BlockSpec((D_MODEL, LANES), lambda i: (0, 0)),
            pl.BlockSpec((1, LANES), lambda i: (0, 0)),
        ],
        out_specs=pl.BlockSpec((tm, LANES), lambda i: (i, 0)),
        compiler_params=_params("parallel"),
        name="router",
    )(x, g.reshape(1, D_MODEL), w, b)


def _dispatch_plan(eid):
    eid_flat = eid.reshape(-1)
    onehot = eid_flat[:, None] == jnp.arange(N_EXPERTS, dtype=jnp.int32)[None, :]
    csum = jnp.cumsum(onehot.astype(jnp.int32), axis=0)
    rank = jnp.sum(jnp.where(onehot, csum, 0), axis=1) - 1
    counts = csum[-1]
    nblk_e = (counts + MOE_BLOCK - 1) // MOE_BLOCK
    blk_end = jnp.cumsum(nblk_e)
    blk_start = blk_end - nblk_e
    dest = (blk_start * MOE_BLOCK)[eid_flat] + rank
    n_active = blk_end[-1]
    bidx = jnp.arange(MOE_NBLK, dtype=jnp.int32)
    blk_x = jnp.minimum(bidx, n_active - 1)
    blk_e = jnp.minimum(jnp.searchsorted(blk_end, blk_x, side='right'), N_EXPERTS - 1).astype(jnp.int32)
    rows = jnp.clip(counts[blk_e] - (blk_x - blk_start[blk_e]) * MOE_BLOCK, 0, MOE_BLOCK)
    blk_rows = jnp.where(bidx < n_active, rows, 0).astype(jnp.int32)
    row_asg = jnp.zeros((MOE_NBLK * MOE_BLOCK,), jnp.int32).at[dest].set(jnp.arange(N_ASSIGN, dtype=jnp.int32))
    return row_asg, blk_e, blk_rows


ROW_UNROLL = 8
_ROW_BITS = tuple(1 << k for k in range(MOE_BLOCK.bit_length() - 1, -1, -1))


def _expert_kernel(row_asg, blk_e, blk_rows, x_hbm, g_ref, w1_ref, w3_ref, w2_ref, ys_hbm,
                   xbuf, acc, xn_ref, w1b_ref, w3b_ref, w2b_ref, gsem, ssem):
    b = pl.program_id(0)
    j = pl.program_id(1)
    rows = blk_rows[b]
    slot = b % 2

    def n_fetch(blk):
        return (blk_rows[blk] + ROW_UNROLL - 1) // ROW_UNROLL * ROW_UNROLL

    def issue_gather(blk, s):
        base = blk * MOE_BLOCK

        def group(q, c):
            for u in range(ROW_UNROLL):
                i = q * ROW_UNROLL + u
                t = row_asg[base + i] // TOP_K
                pltpu.make_async_copy(x_hbm.at[pl.ds(t, 1), :], xbuf.at[s, pl.ds(i, 1), :], gsem.at[s]).start()
            return c

        lax.fori_loop(0, n_fetch(blk) // ROW_UNROLL, group, 0)

    def wait_rows(n, copy_of):
        for bit in _ROW_BITS:
            @pl.when((n & bit) != 0)
            def _():
                copy_of(bit).wait()

    def gather_block(s):
        return lambda k: pltpu.make_async_copy(x_hbm.at[pl.ds(0, k), :], xbuf.at[s, pl.ds(0, k), :], gsem.at[s])

    def scatter_block(s):
        return lambda k: pltpu.make_async_copy(acc.at[s, pl.ds(0, k), :], ys_hbm.at[pl.ds(0, k), :], ssem.at[s])

    def issue_scatter(blk, s, n):
        base = blk * MOE_BLOCK

        def one(i):
            a = row_asg[base + i]
            pltpu.make_async_copy(acc.at[s, pl.ds(i, 1), :], ys_hbm.at[pl.ds(a, 1), :], ssem.at[s]).start()

        def group(q, c):
            for u in range(ROW_UNROLL):
                one(q * ROW_UNROLL + u)
            return c

        def tail(i, c):
            one(i)
            return c

        full = n // ROW_UNROLL
        lax.fori_loop(0, full, group, 0)
        lax.fori_loop(full * ROW_UNROLL, n, tail, 0)

    nxt = jnp.minimum(b + 1, MOE_NBLK - 1)
    has_next = jnp.logical_and(b + 1 < MOE_NBLK, blk_rows[nxt] > 0)

    @pl.when(rows > 0)
    def _():
        @pl.when(j == 0)
        def _():
            @pl.when(b == 0)
            def _():
                xbuf[...] = jnp.zeros(xbuf.shape, F32)
                issue_gather(0, 0)

            wait_rows(n_fetch(b), gather_block(slot))

            @pl.when(has_next)
            def _():
                issue_gather(nxt, 1 - slot)

            xn_ref[...] = _rms(xbuf[slot], g_ref[...]).astype(BF16)

        w1b_ref[...] = w1_ref[...].astype(BF16)
        w3b_ref[...] = w3_ref[...].astype(BF16)
        w2b_ref[...] = w2_ref[...].astype(BF16)

        def sub_block(s):
            sl = slice(s * MOE_SUB, (s + 1) * MOE_SUB)
            xs = xn_ref[sl, :]
            h1 = jnp.dot(xs, w1b_ref[...], preferred_element_type=F32)
            h3 = jnp.dot(xs, w3b_ref[...], preferred_element_type=F32)
            h = (jax.nn.silu(h1) * h3).astype(BF16)
            y = jnp.dot(h, w2b_ref[...], preferred_element_type=F32)

            @pl.when(j == 0)
            def _():
                acc[slot, sl, :] = y

            @pl.when(j > 0)
            def _():
                acc[slot, sl, :] += y

        sub_block(0)

        @pl.when(rows > MOE_SUB)
        def _():
            sub_block(1)

        @pl.when(j == N_FT - 1)
        def _():
            @pl.when(b > 0)
            def _():
                wait_rows(blk_rows[jnp.maximum(b - 1, 0)], scatter_block(1 - slot))

            issue_scatter(b, slot, rows)

            @pl.when(jnp.logical_not(has_next))
            def _():
                wait_rows(rows, scatter_block(slot))


def _moe_experts(x, g, w1, w3, w2, layer, row_asg, blk_e, blk_rows):
    def w_col(b, j, ra, be, br):
        return (layer, be[b], 0, jnp.where(br[b] > 0, j, N_FT - 1))

    def w_row(b, j, ra, be, br):
        return (layer, be[b], jnp.where(br[b] > 0, j, N_FT - 1), 0)

    return pl.pallas_call(
        _expert_kernel,
        out_shape=jax.ShapeDtypeStruct((N_ASSIGN, D_MODEL), F32),
        grid_spec=pltpu.PrefetchScalarGridSpec(
            num_scalar_prefetch=3,
            grid=(MOE_NBLK, N_FT),
            in_specs=[
                pl.BlockSpec(memory_space=pl.ANY),
                pl.BlockSpec((1, D_MODEL), lambda b, j, ra, be, br: (0, 0)),
                pl.BlockSpec((None, None, D_MODEL, MOE_FT), w_col),
                pl.BlockSpec((None, None, D_MODEL, MOE_FT), w_col),
                pl.BlockSpec((None, None, MOE_FT, D_MODEL), w_row),
            ],
            out_specs=pl.BlockSpec(memory_space=pl.ANY),
            scratch_shapes=[
                pltpu.VMEM((2, MOE_BLOCK, D_MODEL), F32),
                pltpu.VMEM((2, MOE_BLOCK, D_MODEL), F32),
                pltpu.VMEM((MOE_BLOCK, D_MODEL), BF16),
                pltpu.VMEM((D_MODEL, MOE_FT), BF16),
                pltpu.VMEM((D_MODEL, MOE_FT), BF16),
                pltpu.VMEM((MOE_FT, D_MODEL), BF16),
                pltpu.SemaphoreType.DMA((2,)),
                pltpu.SemaphoreType.DMA((2,)),
            ],
        ),
        compiler_params=_params("arbitrary", "arbitrary"),
        name="moe_experts",
    )(row_asg, blk_e, blk_rows, x, g.reshape(1, D_MODEL), w1, w3, w2)


def _combine_kernel(final_norm, x_ref, ys_ref, gate_ref, gf_ref, o_ref):
    gates = gate_ref[...]
    out = x_ref[...] + (gates[:, 2:3] * ys_ref[:, :D_MODEL] + gates[:, 3:4] * ys_ref[:, D_MODEL:])
    if final_norm:
        out = _rms(out, gf_ref[...])
    o_ref[...] = out


def _moe_combine(x, ys, route, g_final, final_norm):
    tm = COMBINE_TOK
    return pl.pallas_call(
        functools.partial(_combine_kernel, final_norm),
        out_shape=jax.ShapeDtypeStruct((N_TOK, D_MODEL), F32),
        grid=(N_TOK // tm,),
        in_specs=[
            pl.BlockSpec((tm, D_MODEL), lambda i: (i, 0)),
            pl.BlockSpec((tm, TOP_K * D_MODEL), lambda i: (i, 0)),
            pl.BlockSpec((tm, LANES), lambda i: (i, 0)),
            pl.BlockSpec((1, D_MODEL), lambda i: (0, 0)),
        ],
        out_specs=pl.BlockSpec((tm, D_MODEL), lambda i: (i, 0)),
        compiler_params=_params("parallel"),
        name="moe_combine",
    )(x, ys.reshape(N_TOK, TOP_K * D_MODEL), route, g_final.reshape(1, D_MODEL))


def _hier_moe(x, g, wg, bg, we, be, w1, w3, w2, layer, g_final, final_norm):
    pad = LANES - N_GROUPS - N_EXPERTS
    w_r = jnp.concatenate([wg, we, jnp.zeros((D_MODEL, pad), F32)], axis=1)
    b_r = jnp.concatenate([bg, be, jnp.zeros((pad,), F32)]).reshape(1, LANES)
    route = _router(x, g, w_r, b_r)
    eid = route[:, :TOP_K].astype(jnp.int32)
    row_asg, blk_e, blk_rows = _dispatch_plan(eid)
    ys = _moe_experts(x, g, w1, w3, w2, layer, row_asg, blk_e, blk_rows)
    return _moe_combine(x, ys, route, g_final, final_norm)


def kernel(x, norm_mix, norm_ffn, norm_final, w_in_ab, na_rpb, conv_w, conv_b, w_out_ab, w_in_cd,
           q_norm, kv_norm, w_uq, w_ukv, sg_norm, sg_w, sg_b, w_out_cd, router_group_w,
           router_group_b, router_expert_w, router_expert_b, w1, w3, w2):
    xt = x.reshape(N_TOK, D_MODEL)
    for layer in range(DEPTH):
        i = layer // 2
        if layer % 2 == 0:
            p = _norm_matmul(xt, norm_mix[layer], w_in_ab[i].astype(BF16), tm=1024, tn=512)
            a_out = _natten(p, _natten_bias_table(na_rpb[i]))
            b_out = _gated_conv(p, conv_w[i], conv_b[i])
            xt = _out_proj(a_out, b_out, w_out_ab[i].astype(BF16), xt)
        else:
            w = w_in_cd[i]
            c0, c1, c2 = Q_LORA + KV_LORA, Q_LORA + KV_LORA + QK_ROPE, CD_IN_PAD - KR_COL - QK_ROPE
            w = jnp.concatenate([w[:, :c0], w[:, c1:], w[:, c0:c1], jnp.zeros((D_MODEL, c2), F32)], axis=1)
            p = _norm_matmul(xt, norm_mix[layer], w.astype(BF16), tm=1024, tn=640)
            wq = w_uq[i].reshape(Q_LORA, MLA_HEADS, QK_DIM).transpose(1, 0, 2).astype(BF16)
            wkv = w_ukv[i].reshape(KV_LORA, MLA_HEADS, QK_NOPE + V_DIM).transpose(1, 0, 2).astype(BF16)
            q, k, v = _mla_prep(p, q_norm[i], kv_norm[i], wq, wkv)
            c_out = _mla_attn(q, k, v)
            d_out = _spatial_gating(p, sg_norm[i], sg_w[i], sg_b[i])
            xt = _out_proj(c_out, d_out, w_out_cd[i].astype(BF16), xt)
        xt = _hier_moe(xt, norm_ffn[layer], router_group_w[layer], router_group_b[layer],
                       router_expert_w[layer], router_expert_b[layer], w1, w3, w2, layer,
                       norm_final, final_norm=(layer == DEPTH - 1))
    return xt.reshape(BATCH, SEQ, D_MODEL)
```

```python
import functools

import numpy as np
import jax
import jax.numpy as jnp
from jax import lax
from jax.experimental import pallas as pl
from jax.experimental.pallas import tpu as pltpu

F32 = jnp.float32
BF16 = jnp.bfloat16

D_MODEL = 2048
BATCH = 2
SEQ = 4096
N_TOK = BATCH * SEQ
DEPTH = 2
GRID_W = 64
GRID_ROWS = SEQ // GRID_W
WIN_ROWS = 8
WIN_COLS = 16
NA_HEADS = 8
NA_HEAD_DIM = 128
NA_WIDTH = NA_HEADS * NA_HEAD_DIM
NA_KEYS = WIN_ROWS * GRID_W
CONV_CH = D_MODEL // 2
MLA_HEADS = 8
Q_LORA = 512
KV_LORA = 512
QK_NOPE = 128
QK_ROPE = 64
QK_DIM = QK_NOPE + QK_ROPE
V_DIM = 128
ROPE_THETA = 10000.0
SG_CH = D_MODEL // 2
SG_GROUPS = 8
CHUNK = 128
N_GROUPS = 8
EXPERTS_PER_GROUP = 8
N_EXPERTS = N_GROUPS * EXPERTS_PER_GROUP
TOP_K = 2
D_EXPERT = 768
EPS = 1e-6
NEG_INF = -1e30
LOG2_E = 1.4426950408889634
AB_IN = 3 * NA_WIDTH + 3 * CONV_CH
CD_IN_PAD = 3200
KR_COL = Q_LORA + KV_LORA + 2 * SG_CH

LANES = 128
MOE_BLOCK = 512
MOE_SUB = 256
MOE_FT = 256
N_FT = D_EXPERT // MOE_FT
N_ASSIGN = N_TOK * TOP_K
MOE_NBLK = N_ASSIGN // MOE_BLOCK + N_EXPERTS
COMBINE_TOK = 256
VMEM_LIMIT = 52 * 1024 * 1024


def _params(*sem):
    return pltpu.CompilerParams(dimension_semantics=sem, vmem_limit_bytes=VMEM_LIMIT)


def _rms(x, g):
    return x * lax.rsqrt(jnp.mean(x * x, axis=-1, keepdims=True) + EPS) * g


def _norm_mm_kernel(x_ref, g_ref, w_ref, o_ref, xn_ref):
    @pl.when(pl.program_id(1) == 0)
    def _():
        xn_ref[...] = _rms(x_ref[...].astype(F32), g_ref[...]).astype(BF16)

    o_ref[...] = jnp.dot(xn_ref[...], w_ref[...], preferred_element_type=F32).astype(o_ref.dtype)


def _norm_matmul(x, g, w, tm, tn):
    m, k = x.shape
    nout = w.shape[1]
    return pl.pallas_call(
        _norm_mm_kernel,
        out_shape=jax.ShapeDtypeStruct((m, nout), BF16),
        grid=(m // tm, nout // tn),
        in_specs=[
            pl.BlockSpec((tm, k), lambda i, j: (i, 0)),
            pl.BlockSpec((1, k), lambda i, j: (0, 0)),
            pl.BlockSpec((k, tn), lambda i, j: (0, j)),
        ],
        out_specs=pl.BlockSpec((tm, tn), lambda i, j: (i, j)),
        scratch_shapes=[pltpu.VMEM((tm, k), BF16)],
        compiler_params=_params("parallel", "arbitrary"),
        name="norm_matmul",
    )(x, g.reshape(1, k), w)


def _natten_bias_table(rpb):
    c = np.arange(GRID_W)
    col_start = np.clip(c - WIN_COLS // 2, 0, GRID_W - WIN_COLS)
    valid = (c[None, :] >= col_start[:, None]) & (c[None, :] < col_start[:, None] + WIN_COLS)
    dc = np.clip(c[None, :] - c[:, None] + WIN_COLS - 1, 0, 2 * WIN_COLS - 2)
    pick = (dc[:, :, None] == np.arange(2 * WIN_COLS - 1)).astype(np.float32)
    m = jnp.einsum('hrd,ckd->hrck', rpb, jnp.asarray(pick), precision=lax.Precision.HIGHEST)
    m = jnp.where(valid[None, None], m, NEG_INF)
    t = jnp.stack([m[:, o:o + WIN_ROWS] for o in range(WIN_ROWS)], axis=1)
    return t.transpose(0, 1, 3, 2, 4).reshape(rpb.shape[0], WIN_ROWS, GRID_W, NA_KEYS).astype(F32)


NA_ROWS_PER_STEP = 4


def _natten_kernel(q_ref, k_ref, v_ref, t_ref, o_ref):
    scale = NA_HEAD_DIM ** -0.5

    def rows(i, carry):
        rs = [i * NA_ROWS_PER_STEP + u for u in range(NA_ROWS_PER_STEP)]
        kr0s = [jnp.clip(r - WIN_ROWS // 2, 0, GRID_ROWS - WIN_ROWS) for r in rs]
        q0s = [pl.multiple_of(r * GRID_W, GRID_W) for r in rs]
        k0s = [pl.multiple_of(kr0 * GRID_W, GRID_W) for kr0 in kr0s]
        ss = [lax.dot_general(q_ref[pl.ds(q0, GRID_W), :], k_ref[pl.ds(k0, NA_KEYS), :],
                              (((1,), (1,)), ((), ())), preferred_element_type=F32)
              for q0, k0 in zip(q0s, k0s)]
        ps, ls = [], []
        for s, r, kr0 in zip(ss, rs, kr0s):
            s = s * scale + t_ref[kr0 - r + WIN_ROWS - 1]
            p = jnp.exp(s - jnp.max(s, axis=-1, keepdims=True))
            ls.append(jnp.sum(p, axis=-1, keepdims=True))
            ps.append(p.astype(BF16))
        for p, l, q0, k0 in zip(ps, ls, q0s, k0s):
            o = jnp.dot(p, v_ref[pl.ds(k0, NA_KEYS), :], preferred_element_type=F32) / l
            o_ref[pl.ds(q0, GRID_W), :] = o.astype(o_ref.dtype)
        return carry

    lax.fori_loop(0, GRID_ROWS // NA_ROWS_PER_STEP, rows, 0)


def _natten(p, table):
    blk = (SEQ, NA_HEAD_DIM)
    return pl.pallas_call(
        _natten_kernel,
        out_shape=jax.ShapeDtypeStruct((N_TOK, NA_WIDTH), BF16),
        grid=(BATCH, NA_HEADS),
        in_specs=[
            pl.BlockSpec(blk, lambda b, h: (b, h)),
            pl.BlockSpec(blk, lambda b, h: (b, NA_HEADS + h)),
            pl.BlockSpec(blk, lambda b, h: (b, 2 * NA_HEADS + h)),
            pl.BlockSpec((None, WIN_ROWS, GRID_W, NA_KEYS), lambda b, h: (h, 0, 0, 0)),
        ],
        out_specs=pl.BlockSpec(blk, lambda b, h: (b, h)),
        compiler_params=_params("parallel", "parallel"),
        name="natten",
    )(p, p, p, table)


CONV_CB = 128


def _conv_kernel(gb_ref, gc_ref, hc_ref, w_ref, b_ref, o_ref):
    z = gc_ref[...].astype(F32) * hc_ref[...].astype(F32)
    pos = lax.broadcasted_iota(jnp.int32, z.shape, 0)
    z_prev = jnp.where(pos == 0, 0.0, pltpu.roll(z, 1, 0))
    z_next = jnp.where(pos == SEQ - 1, 0.0, pltpu.roll(z, SEQ - 1, 0))
    y = b_ref[...] + z_prev * w_ref[0:1, :] + z * w_ref[1:2, :] + z_next * w_ref[2:3, :]
    o_ref[...] = (gb_ref[...].astype(F32) * y).astype(o_ref.dtype)


def _gated_conv(p, w, b):
    base = 3 * NA_WIDTH // CONV_CB
    step = CONV_CH // CONV_CB
    blk = (SEQ, CONV_CB)
    return pl.pallas_call(
        _conv_kernel,
        out_shape=jax.ShapeDtypeStruct((N_TOK, CONV_CH), BF16),
        grid=(BATCH, step),
        in_specs=[
            pl.BlockSpec(blk, lambda bi, c: (bi, base + c)),
            pl.BlockSpec(blk, lambda bi, c: (bi, base + step + c)),
            pl.BlockSpec(blk, lambda bi, c: (bi, base + 2 * step + c)),
            pl.BlockSpec((3, CONV_CB), lambda bi, c: (0, c)),
            pl.BlockSpec((1, CONV_CB), lambda bi, c: (0, c)),
        ],
        out_specs=pl.BlockSpec(blk, lambda bi, c: (bi, c)),
        compiler_params=_params("parallel", "parallel"),
        name="gated_conv",
    )(p, p, p, w, b.reshape(1, CONV_CH))


def _out_proj_kernel(a_ref, b_ref, wa_ref, wb_ref, r_ref, o_ref):
    acc = jnp.dot(a_ref[...], wa_ref[...], preferred_element_type=F32)
    acc = acc + jnp.dot(b_ref[...], wb_ref[...], preferred_element_type=F32)
    o_ref[...] = r_ref[...] + acc


def _out_proj(a, b, w, res, tm=512):
    ka, kb = a.shape[1], b.shape[1]
    assert ka == kb
    return pl.pallas_call(
        _out_proj_kernel,
        out_shape=jax.ShapeDtypeStruct((N_TOK, D_MODEL), F32),
        grid=(N_TOK // tm,),
        in_specs=[
            pl.BlockSpec((tm, ka), lambda i: (i, 0)),
            pl.BlockSpec((tm, kb), lambda i: (i, 0)),
            pl.BlockSpec((ka, D_MODEL), lambda i: (0, 0)),
            pl.BlockSpec((kb, D_MODEL), lambda i: (1, 0)),
            pl.BlockSpec((tm, D_MODEL), lambda i: (i, 0)),
        ],
        out_specs=pl.BlockSpec((tm, D_MODEL), lambda i: (i, 0)),
        compiler_params=_params("parallel"),
        name="out_proj",
    )(a, b, w, w, res)


MLA_TM = 512


def _rope_tables():
    pos = jnp.arange(SEQ, dtype=jnp.int32)
    row = (pos // GRID_W).astype(F32)
    col = (pos % GRID_W).astype(F32)
    half = QK_ROPE // 2
    inv = ROPE_THETA ** (-jnp.arange(0, half, 2, dtype=F32) / half)
    ar, ac = row[:, None] * inv, col[:, None] * inv
    cos_t = jnp.concatenate([jnp.cos(ar), jnp.cos(ar), jnp.cos(ac), jnp.cos(ac)], axis=-1)
    sin_t = jnp.concatenate([-jnp.sin(ar), jnp.sin(ar), -jnp.sin(ac), jnp.sin(ac)], axis=-1)
    quarter = half // 2
    src = np.arange(QK_ROPE) + np.where((np.arange(QK_ROPE) // quarter) % 2 == 0, quarter, -quarter)
    swap = np.zeros((QK_ROPE, QK_ROPE), np.float32)
    swap[src, np.arange(QK_ROPE)] = 1.0
    return cos_t, sin_t, jnp.asarray(swap)


def _rope(x, cos_t, sin_t, swap):
    xs = jnp.dot(x, swap, preferred_element_type=F32, precision=lax.Precision.HIGHEST)
    return x * cos_t + xs * sin_t


def _mla_prep_kernel(cq_ref, ckv_ref, kr_ref, gq_ref, gkv_ref, wq_ref, wkv_ref, cos_ref, sin_ref,
                     swap_ref, q_ref, k_ref, v_ref, cqn_ref, ckvn_ref, krot_ref):
    @pl.when(pl.program_id(1) == 0)
    def _():
        cqn_ref[...] = _rms(cq_ref[...].astype(F32), gq_ref[...]).astype(BF16)
        ckvn_ref[...] = _rms(ckv_ref[...].astype(F32), gkv_ref[...]).astype(BF16)
        kr = kr_ref[:, :QK_ROPE].astype(F32)
        krot_ref[...] = _rope(kr, cos_ref[...], sin_ref[...], swap_ref[...]).astype(BF16)

    scale = QK_DIM ** -0.5 * LOG2_E
    q = jnp.dot(cqn_ref[...], wq_ref[...], preferred_element_type=F32)
    q_ref[:, :QK_NOPE] = (q[:, :QK_NOPE] * scale).astype(BF16)
    qr = _rope(q[:, QK_NOPE:], cos_ref[...], sin_ref[...], swap_ref[...])
    q_ref[:, QK_NOPE:] = (qr * scale).astype(BF16)
    kv = jnp.dot(ckvn_ref[...], wkv_ref[...], preferred_element_type=F32)
    k_ref[:, :QK_NOPE] = kv[:, :QK_NOPE].astype(BF16)
    k_ref[:, QK_NOPE:] = krot_ref[...]
    v_ref[:, :V_DIM] = kv[:, QK_NOPE:].astype(BF16)
    lane = lax.broadcasted_iota(jnp.int32, (v_ref.shape[0], V_DIM), 1)
    v_ref[:, V_DIM:] = jnp.where(lane == 0, 1.0, 0.0).astype(BF16)


def _mla_prep(p, q_norm, kv_norm, wq, wkv):
    cos_t, sin_t, swap = _rope_tables()
    tm = MLA_TM
    seq_blocks = SEQ // tm
    return pl.pallas_call(
        _mla_prep_kernel,
        out_shape=(
            jax.ShapeDtypeStruct((MLA_HEADS, N_TOK, QK_DIM), BF16),
            jax.ShapeDtypeStruct((MLA_HEADS, N_TOK, QK_DIM), BF16),
            jax.ShapeDtypeStruct((MLA_HEADS, N_TOK, 2 * V_DIM), BF16),
        ),
        grid=(N_TOK // tm, MLA_HEADS),
        in_specs=[
            pl.BlockSpec((tm, Q_LORA), lambda i, h: (i, 0)),
            pl.BlockSpec((tm, KV_LORA), lambda i, h: (i, 1)),
            pl.BlockSpec((tm, LANES), lambda i, h: (i, KR_COL // LANES)),
            pl.BlockSpec((1, Q_LORA), lambda i, h: (0, 0)),
            pl.BlockSpec((1, KV_LORA), lambda i, h: (0, 0)),
            pl.BlockSpec((None, Q_LORA, QK_DIM), lambda i, h: (h, 0, 0)),
            pl.BlockSpec((None, KV_LORA, QK_NOPE + V_DIM), lambda i, h: (h, 0, 0)),
            pl.BlockSpec((tm, QK_ROPE), lambda i, h: (i % seq_blocks, 0)),
            pl.BlockSpec((tm, QK_ROPE), lambda i, h: (i % seq_blocks, 0)),
            pl.BlockSpec((QK_ROPE, QK_ROPE), lambda i, h: (0, 0)),
        ],
        out_specs=(
            pl.BlockSpec((None, tm, QK_DIM), lambda i, h: (h, i, 0)),
            pl.BlockSpec((None, tm, QK_DIM), lambda i, h: (h, i, 0)),
            pl.BlockSpec((None, tm, 2 * V_DIM), lambda i, h: (h, i, 0)),
        ),
        scratch_shapes=[
            pltpu.VMEM((tm, Q_LORA), BF16),
            pltpu.VMEM((tm, KV_LORA), BF16),
            pltpu.VMEM((tm, QK_ROPE), BF16),
        ],
        compiler_params=_params("parallel", "arbitrary"),
        name="mla_prep",
    )(p, p, p, q_norm.reshape(1, Q_LORA), kv_norm.reshape(1, KV_LORA), wq, wkv, cos_t, sin_t, swap)


MLA_TQ = 512
MLA_TK = 512
MLA_QH = 2


def _mla_attn_kernel(q_ref, k_ref, v_ref, o_ref, s_ref, m_ref, acc_ref):
    n_chunks = SEQ // MLA_TK
    rows = MLA_TQ // MLA_QH

    def scores(h, c):
        k0 = pl.multiple_of(c * MLA_TK, MLA_TK)
        return lax.dot_general(q_ref[h * rows:(h + 1) * rows, :], k_ref[pl.ds(k0, MLA_TK), :],
                               (((1,), (1,)), ((), ())), preferred_element_type=F32)

    m_ref[...] = jnp.full(m_ref.shape, NEG_INF, F32)
    acc_ref[...] = jnp.zeros(acc_ref.shape, F32)
    for h in range(MLA_QH):
        s_ref[0, h] = scores(h, 0)

    def step(c, slot):
        c_next = jnp.where(c + 1 < n_chunks, c + 1, 0)
        k0 = pl.multiple_of(c * MLA_TK, MLA_TK)
        for h in range(MLA_QH):
            s_ref[1 - slot, h] = scores(h, c_next)
            s = s_ref[slot, h]
            m_old = m_ref[h]
            m_new = jnp.maximum(m_old, jnp.max(s, axis=-1, keepdims=True))
            p = jnp.exp2(s - m_new).astype(BF16)
            pv = jnp.dot(p, v_ref[pl.ds(k0, MLA_TK), :], preferred_element_type=F32)
            acc_ref[h] = jnp.exp2(m_old - m_new) * acc_ref[h] + pv
            m_ref[h] = m_new

    def pair(i, carry):
        step(2 * i, 0)
        step(2 * i + 1, 1)
        return carry

    lax.fori_loop(0, n_chunks // 2, pair, 0)
    for h in range(MLA_QH):
        acc = acc_ref[h]
        o_ref[h * rows:(h + 1) * rows, :] = (acc[:, :V_DIM] / acc[:, V_DIM:V_DIM + 1]).astype(o_ref.dtype)


def _mla_attn(q, k, v):
    nq = SEQ // MLA_TQ
    return pl.pallas_call(
        _mla_attn_kernel,
        out_shape=jax.ShapeDtypeStruct((N_TOK, MLA_HEADS * V_DIM), BF16),
        grid=(BATCH, MLA_HEADS, nq),
        in_specs=[
            pl.BlockSpec((None, MLA_TQ, QK_DIM), lambda b, h, i: (h, b * nq + i, 0)),
            pl.BlockSpec((None, SEQ, QK_DIM), lambda b, h, i: (h, b, 0)),
            pl.BlockSpec((None, SEQ, 2 * V_DIM), lambda b, h, i: (h, b, 0)),
        ],
        out_specs=pl.BlockSpec((MLA_TQ, V_DIM), lambda b, h, i: (b * nq + i, h)),
        scratch_shapes=[
            pltpu.VMEM((2, MLA_QH, MLA_TQ // MLA_QH, MLA_TK), F32),
            pltpu.VMEM((MLA_QH, MLA_TQ // MLA_QH, 1), F32),
            pltpu.VMEM((MLA_QH, MLA_TQ // MLA_QH, 2 * V_DIM), F32),
        ],
        compiler_params=_params("parallel", "parallel", "parallel"),
        name="mla_attn",
    )(q, k, v)


SG_TM = 256


def _sg_kernel(u_ref, v_ref, g_ref, w_ref, bt_ref, o_ref):
    v = jax.nn.gelu(v_ref[...].astype(F32))
    vn = _rms(v, g_ref[...]).astype(BF16)
    u = jax.nn.gelu(u_ref[...].astype(F32))
    gc = SG_CH // SG_GROUPS
    for n in range(SG_TM // CHUNK):
        rows = slice(n * CHUNK, (n + 1) * CHUNK)
        for g in range(SG_GROUPS):
            cols = slice(g * gc, (g + 1) * gc)
            mixed = jnp.dot(w_ref[g], vn[rows, cols], preferred_element_type=F32) + bt_ref[:, g:g + 1]
            o_ref[rows, cols] = (u[rows, cols] * mixed).astype(o_ref.dtype)


def _spatial_gating(p, g_norm, w_s, b_s):
    u_blk = (Q_LORA + KV_LORA) // SG_CH
    return pl.pallas_call(
        _sg_kernel,
        out_shape=jax.ShapeDtypeStruct((N_TOK, SG_CH), BF16),
        grid=(N_TOK // SG_TM,),
        in_specs=[
            pl.BlockSpec((SG_TM, SG_CH), lambda i: (i, u_blk)),
            pl.BlockSpec((SG_TM, SG_CH), lambda i: (i, u_blk + 1)),
            pl.BlockSpec((1, SG_CH), lambda i: (0, 0)),
            pl.BlockSpec((SG_GROUPS, CHUNK, CHUNK), lambda i: (0, 0, 0)),
            pl.BlockSpec((CHUNK, SG_GROUPS), lambda i: (0, 0)),
        ],
        out_specs=pl.BlockSpec((SG_TM, SG_CH), lambda i: (i, 0)),
        compiler_params=_params("parallel"),
        name="spatial_gating",
    )(p, p, g_norm.reshape(1, SG_CH), w_s.astype(BF16), b_s.T)


ROUTER_TM = 512


def _router_kernel(x_ref, g_ref, w_ref, b_ref, o_ref):
    xn = _rms(x_ref[...], g_ref[...])
    logits = jnp.dot(xn, w_ref[...], preferred_element_type=F32,
                     precision=lax.Precision.HIGHEST) + b_ref[...]
    lane = lax.broadcasted_iota(jnp.int32, logits.shape, 1).astype(F32)
    low = jnp.float32(-3.0e38)

    def first_max(vals):
        top = jnp.max(vals, axis=-1, keepdims=True)
        idx = jnp.min(jnp.where(vals == top, lane, float(LANES)), axis=-1, keepdims=True)
        return top, idx

    is_group = lane < N_GROUPS
    g_top, g_idx = first_max(jnp.where(is_group, logits, low))
    g_prob = 1.0 / jnp.sum(jnp.where(is_group, jnp.exp(logits - g_top), 0.0), axis=-1, keepdims=True)
    lo = N_GROUPS + g_idx * EXPERTS_PER_GROUP
    e_vals = jnp.where(lane >= lo, jnp.where(lane < lo + EXPERTS_PER_GROUP, logits, low), low)
    v1, i1 = first_max(e_vals)
    v2, i2 = first_max(jnp.where(lane == i1, low, e_vals))
    e21 = jnp.exp(v2 - v1)
    w1 = g_prob / (1.0 + e21)
    w2 = g_prob * e21 / (1.0 + e21)
    out = jnp.where(lane == 0, i1 - N_GROUPS,
                    jnp.where(lane == 1, i2 - N_GROUPS,
                              jnp.where(lane == 2, w1, jnp.where(lane == 3, w2, 0.0))))
    o_ref[...] = out


def _router(x, g, w, b):
    tm = ROUTER_TM
    return pl.pallas_call(
        _router_kernel,
        out_shape=jax.ShapeDtypeStruct((N_TOK, LANES), F32),
        grid=(N_TOK // tm,),
        in_specs=[
            pl.BlockSpec((tm, D_MODEL), lambda i: (i, 0)),
            pl.BlockSpec((1, D_MODEL), lambda i: (0, 0)),
            pl.BlockSpec((D_MODEL, LANES), lambda i: (0, 0)),
            pl.BlockSpec((1, LANES), lambda i: (0, 0)),
        ],
        out_specs=pl.BlockSpec((tm, LANES), lambda i: (i, 0)),
        compiler_params=_params("parallel"),
        name="router",
    )(x, g.reshape(1, D_MODEL), w, b)


def _dispatch_plan(eid):
    eid_flat = eid.reshape(-1)
    onehot = eid_flat[:, None] == jnp.arange(N_EXPERTS, dtype=jnp.int32)[None, :]
    csum = jnp.cumsum(onehot.astype(jnp.int32), axis=0)
    rank = jnp.sum(jnp.where(onehot, csum, 0), axis=1) - 1
    counts = csum[-1]
    nblk_e = (counts + MOE_BLOCK - 1) // MOE_BLOCK
    blk_end = jnp.cumsum(nblk_e)
    blk_start = blk_end - nblk_e
    dest = (blk_start * MOE_BLOCK)[eid_flat] + rank
    n_active = blk_end[-1]
    bidx = jnp.arange(MOE_NBLK, dtype=jnp.int32)
    blk_x = jnp.minimum(bidx, n_active - 1)
    blk_e = jnp.minimum(jnp.searchsorted(blk_end, blk_x, side='right'), N_EXPERTS - 1).astype(jnp.int32)
    rows = jnp.clip(counts[blk_e] - (blk_x - blk_start[blk_e]) * MOE_BLOCK, 0, MOE_BLOCK)
    blk_rows = jnp.where(bidx < n_active, rows, 0).astype(jnp.int32)
    asg = jnp.arange(N_ASSIGN, dtype=jnp.int32)
    row_asg = jnp.zeros((MOE_NBLK * MOE_BLOCK,), jnp.int32).at[dest].set((asg % TOP_K) * N_TOK + asg // TOP_K)
    return row_asg, blk_e, blk_rows


ROW_UNROLL = 8
_ROW_BITS = tuple(1 << k for k in range(MOE_BLOCK.bit_length() - 1, -1, -1))


def _expert_kernel(row_asg, blk_e, blk_rows, x_hbm, g_ref, w1_ref, w3_ref, w2_ref, ys_hbm,
                   xbuf, acc, xn_ref, w1b_ref, w3b_ref, w2b_ref, gsem, ssem):
    b = pl.program_id(0)
    j = pl.program_id(1)
    rows = blk_rows[b]
    slot = b % 2

    def n_fetch(blk):
        return (blk_rows[blk] + ROW_UNROLL - 1) & -ROW_UNROLL

    def issue_gather(blk, s):
        base = blk * MOE_BLOCK

        def group(q, c):
            for u in range(ROW_UNROLL):
                i = q * ROW_UNROLL + u
                t = row_asg[base + i] & (N_TOK - 1)
                pltpu.make_async_copy(x_hbm.at[pl.ds(t, 1), :], xbuf.at[s, pl.ds(i, 1), :], gsem.at[s]).start()
            return c

        lax.fori_loop(0, n_fetch(blk) >> (ROW_UNROLL.bit_length() - 1), group, 0)

    def wait_rows(n, copy_of):
        for bit in _ROW_BITS:
            @pl.when((n & bit) != 0)
            def _():
                copy_of(bit).wait()

    def gather_block(s):
        return lambda k: pltpu.make_async_copy(x_hbm.at[pl.ds(0, k), :], xbuf.at[s, pl.ds(0, k), :], gsem.at[s])

    def scatter_block(s):
        return lambda k: pltpu.make_async_copy(acc.at[s, pl.ds(0, k), :], ys_hbm.at[pl.ds(0, k), :], ssem.at[s])

    def issue_scatter(blk, s, n):
        base = blk * MOE_BLOCK

        def one(i):
            a = row_asg[base + i]
            pltpu.make_async_copy(acc.at[s, pl.ds(i, 1), :], ys_hbm.at[pl.ds(a, 1), :], ssem.at[s]).start()

        def group(q, c):
            for u in range(ROW_UNROLL):
                one(q * ROW_UNROLL + u)
            return c

        def tail(i, c):
            one(i)
            return c

        full = n >> (ROW_UNROLL.bit_length() - 1)
        lax.fori_loop(0, full, group, 0)
        lax.fori_loop(full * ROW_UNROLL, n, tail, 0)

    nxt = jnp.minimum(b + 1, MOE_NBLK - 1)
    has_next = jnp.logical_and(b + 1 < MOE_NBLK, blk_rows[nxt] > 0)

    @pl.when(rows > 0)
    def _():
        @pl.when(j == 0)
        def _():
            @pl.when(b == 0)
            def _():
                xbuf[...] = jnp.zeros(xbuf.shape, F32)
                acc[...] = jnp.zeros(acc.shape, F32)
                issue_gather(0, 0)

            wait_rows(n_fetch(b), gather_block(slot))

            @pl.when(has_next)
            def _():
                issue_gather(nxt, 1 - slot)

            xn_ref[...] = _rms(xbuf[slot], g_ref[...]).astype(BF16)

        def ffn_tile(m):
            xs = xn_ref[:m, :]
            w1b_ref[...] = w1_ref[...].astype(BF16)
            h1 = jnp.dot(xs, w1b_ref[...], preferred_element_type=F32)
            w3b_ref[...] = w3_ref[...].astype(BF16)
            h3 = jnp.dot(xs, w3b_ref[...], preferred_element_type=F32)
            w2b_ref[...] = w2_ref[...].astype(BF16)
            h = (jax.nn.silu(h1) * h3).astype(BF16)
            y = jnp.dot(h, w2b_ref[...], preferred_element_type=F32)
            acc[slot, :m, :] = jnp.where(j > 0, acc[slot, :m, :], 0.0) + y

        @pl.when(rows <= MOE_SUB)
        def _():
            ffn_tile(MOE_SUB)

        @pl.when(rows > MOE_SUB)
        def _():
            ffn_tile(MOE_BLOCK)

        @pl.when(j == N_FT - 1)
        def _():
            @pl.when(b > 0)
            def _():
                wait_rows(blk_rows[jnp.maximum(b - 1, 0)], scatter_block(1 - slot))

            issue_scatter(b, slot, rows)

            @pl.when(jnp.logical_not(has_next))
            def _():
                wait_rows(rows, scatter_block(slot))


def _moe_experts(x, g, w1, w3, w2, layer, row_asg, blk_e, blk_rows):
    def w_col(b, j, ra, be, br):
        return (layer, be[b], 0, jnp.where(br[b] > 0, j, N_FT - 1))

    def w_row(b, j, ra, be, br):
        return (layer, be[b], jnp.where(br[b] > 0, j, N_FT - 1), 0)

    return pl.pallas_call(
        _expert_kernel,
        out_shape=jax.ShapeDtypeStruct((N_ASSIGN, D_MODEL), F32),
        grid_spec=pltpu.PrefetchScalarGridSpec(
            num_scalar_prefetch=3,
            grid=(MOE_NBLK, N_FT),
            in_specs=[
                pl.BlockSpec(memory_space=pl.ANY),
                pl.BlockSpec((1, D_MODEL), lambda b, j, ra, be, br: (0, 0)),
                pl.BlockSpec((None, None, D_MODEL, MOE_FT), w_col),
                pl.BlockSpec((None, None, D_MODEL, MOE_FT), w_col),
                pl.BlockSpec((None, None, MOE_FT, D_MODEL), w_row),
            ],
            out_specs=pl.BlockSpec(memory_space=pl.ANY),
            scratch_shapes=[
                pltpu.VMEM((2, MOE_BLOCK, D_MODEL), F32),
                pltpu.VMEM((2, MOE_BLOCK, D_MODEL), F32),
                pltpu.VMEM((MOE_BLOCK, D_MODEL), BF16),
                pltpu.VMEM((D_MODEL, MOE_FT), BF16),
                pltpu.VMEM((D_MODEL, MOE_FT), BF16),
                pltpu.VMEM((MOE_FT, D_MODEL), BF16),
                pltpu.SemaphoreType.DMA((2,)),
                pltpu.SemaphoreType.DMA((2,)),
            ],
        ),
        compiler_params=_params("arbitrary", "arbitrary"),
        name="moe_experts",
    )(row_asg, blk_e, blk_rows, x, g.reshape(1, D_MODEL), w1, w3, w2)


def _combine_kernel(final_norm, x_ref, y0_ref, y1_ref, gate_ref, gf_ref, o_ref):
    gates = gate_ref[...]
    out = x_ref[...] + (gates[:, 2:3] * y0_ref[...] + gates[:, 3:4] * y1_ref[...])
    if final_norm:
        out = _rms(out, gf_ref[...])
    o_ref[...] = out


def _moe_combine(x, ys, route, g_final, final_norm):
    tm = COMBINE_TOK
    return pl.pallas_call(
        functools.partial(_combine_kernel, final_norm),
        out_shape=jax.ShapeDtypeStruct((N_TOK, D_MODEL), F32),
        grid=(N_TOK // tm,),
        in_specs=[
            pl.BlockSpec((tm, D_MODEL), lambda i: (i, 0)),
            pl.BlockSpec((tm, D_MODEL), lambda i: (i, 0)),
            pl.BlockSpec((tm, D_MODEL), lambda i: (N_TOK // tm + i, 0)),
            pl.BlockSpec((tm, LANES), lambda i: (i, 0)),
            pl.BlockSpec((1, D_MODEL), lambda i: (0, 0)),
        ],
        out_specs=pl.BlockSpec((tm, D_MODEL), lambda i: (i, 0)),
        compiler_params=_params("parallel"),
        name="moe_combine",
    )(x, ys, ys, route, g_final.reshape(1, D_MODEL))


def _hier_moe(x, g, wg, bg, we, be, w1, w3, w2, layer, g_final, final_norm):
    pad = LANES - N_GROUPS - N_EXPERTS
    w_r = jnp.concatenate([wg, we, jnp.zeros((D_MODEL, pad), F32)], axis=1)
    b_r = jnp.concatenate([bg, be, jnp.zeros((pad,), F32)]).reshape(1, LANES)
    route = _router(x, g, w_r, b_r)
    eid = route[:, :TOP_K].astype(jnp.int32)
    row_asg, blk_e, blk_rows = _dispatch_plan(eid)
    ys = _moe_experts(x, g, w1, w3, w2, layer, row_asg, blk_e, blk_rows)
    return _moe_combine(x, ys, route, g_final, final_norm)


def kernel(x, norm_mix, norm_ffn, norm_final, w_in_ab, na_rpb, conv_w, conv_b, w_out_ab, w_in_cd,
           q_norm, kv_norm, w_uq, w_ukv, sg_norm, sg_w, sg_b, w_out_cd, router_group_w,
           router_group_b, router_expert_w, router_expert_b, w1, w3, w2):
    xt = x.reshape(N_TOK, D_MODEL)
    for layer in range(DEPTH):
        i = layer // 2
        if layer % 2 == 0:
            p = _norm_matmul(xt, norm_mix[layer], w_in_ab[i].astype(BF16), tm=1024, tn=512)
            a_out = _natten(p, _natten_bias_table(na_rpb[i]))
            b_out = _gated_conv(p, conv_w[i], conv_b[i])
            xt = _out_proj(a_out, b_out, w_out_ab[i].astype(BF16), xt)
        else:
            w = w_in_cd[i]
            c0, c1, c2 = Q_LORA + KV_LORA, Q_LORA + KV_LORA + QK_ROPE, CD_IN_PAD - KR_COL - QK_ROPE
            w = jnp.concatenate([w[:, :c0], w[:, c1:], w[:, c0:c1], jnp.zeros((D_MODEL, c2), F32)], axis=1)
            p = _norm_matmul(xt, norm_mix[layer], w.astype(BF16), tm=1024, tn=640)
            wq = w_uq[i].reshape(Q_LORA, MLA_HEADS, QK_DIM).transpose(1, 0, 2).astype(BF16)
            wkv = w_ukv[i].reshape(KV_LORA, MLA_HEADS, QK_NOPE + V_DIM).transpose(1, 0, 2).astype(BF16)
            q, k, v = _mla_prep(p, q_norm[i], kv_norm[i], wq, wkv)
            c_out = _mla_attn(q, k, v)
            d_out = _spatial_gating(p, sg_norm[i], sg_w[i], sg_b[i])
            xt = _out_proj(c_out, d_out, w_out_cd[i].astype(BF16), xt)
        xt = _hier_moe(xt, norm_ffn[layer], router_group_w[layer], router_group_b[layer],
                       router_expert_w[layer], router_expert_b[layer], w1, w3, w2, layer,
                       norm_final, final_norm=(layer == DEPTH - 1))
    return xt.reshape(BATCH, SEQ, D_MODEL)
```

```python
import functools

import numpy as np
import jax
import jax.numpy as jnp
from jax import lax
from jax.experimental import pallas as pl
from jax.experimental.pallas import tpu as pltpu

F32 = jnp.float32
BF16 = jnp.bfloat16

D_MODEL = 2048
BATCH = 2
SEQ = 4096
N_TOK = BATCH * SEQ
DEPTH = 2
GRID_W = 64
GRID_ROWS = SEQ // GRID_W
WIN_ROWS = 8
WIN_COLS = 16
NA_HEADS = 8
NA_HEAD_DIM = 128
NA_WIDTH = NA_HEADS * NA_HEAD_DIM
NA_KEYS = WIN_ROWS * GRID_W
CONV_CH = D_MODEL // 2
MLA_HEADS = 8
Q_LORA = 512
KV_LORA = 512
QK_NOPE = 128
QK_ROPE = 64
QK_DIM = QK_NOPE + QK_ROPE
V_DIM = 128
ROPE_THETA = 10000.0
SG_CH = D_MODEL // 2
SG_GROUPS = 8
CHUNK = 128
N_GROUPS = 8
EXPERTS_PER_GROUP = 8
N_EXPERTS = N_GROUPS * EXPERTS_PER_GROUP
TOP_K = 2
D_EXPERT = 768
EPS = 1e-6
NEG_INF = -1e30
LOG2_E = 1.4426950408889634
AB_IN = 3 * NA_WIDTH + 3 * CONV_CH
CD_IN_PAD = 3200
KR_COL = Q_LORA + KV_LORA + 2 * SG_CH

LANES = 128
SUBLANES = 8
SUBLANE_SHIFT = 3
MOE_BLOCK = 512
MOE_ROW_TILES = (256, 320, 512)
MOE_FT = 256
N_FT = D_EXPERT // MOE_FT
N_ASSIGN = N_TOK * TOP_K
MOE_NBLK = N_ASSIGN // MOE_BLOCK + N_EXPERTS
COMBINE_TOK = 256
VMEM_LIMIT = 52 * 1024 * 1024


def _params(*sem):
    return pltpu.CompilerParams(dimension_semantics=sem, vmem_limit_bytes=VMEM_LIMIT)


def _rms(x, g):
    return x * lax.rsqrt(jnp.mean(x * x, axis=-1, keepdims=True) + EPS) * g


def _norm_mm_kernel(x_ref, g_ref, w_ref, o_ref, xn_ref):
    @pl.when(pl.program_id(1) == 0)
    def _():
        xn_ref[...] = _rms(x_ref[...].astype(F32), g_ref[...]).astype(BF16)

    o_ref[...] = jnp.dot(xn_ref[...], w_ref[...], preferred_element_type=F32).astype(o_ref.dtype)


def _norm_matmul(x, g, w, tm, tn):
    m, k = x.shape
    nout = w.shape[1]
    return pl.pallas_call(
        _norm_mm_kernel,
        out_shape=jax.ShapeDtypeStruct((m, nout), BF16),
        grid=(m // tm, nout // tn),
        in_specs=[
            pl.BlockSpec((tm, k), lambda i, j: (i, 0)),
            pl.BlockSpec((1, k), lambda i, j: (0, 0)),
            pl.BlockSpec((k, tn), lambda i, j: (0, j)),
        ],
        out_specs=pl.BlockSpec((tm, tn), lambda i, j: (i, j)),
        scratch_shapes=[pltpu.VMEM((tm, k), BF16)],
        compiler_params=_params("parallel", "arbitrary"),
        name="norm_matmul",
    )(x, g.reshape(1, k), w)


def _natten_bias_table(rpb):
    c = np.arange(GRID_W)
    col_start = np.clip(c - WIN_COLS // 2, 0, GRID_W - WIN_COLS)
    valid = (c[None, :] >= col_start[:, None]) & (c[None, :] < col_start[:, None] + WIN_COLS)
    dc = np.clip(c[None, :] - c[:, None] + WIN_COLS - 1, 0, 2 * WIN_COLS - 2)
    pick = (dc[:, :, None] == np.arange(2 * WIN_COLS - 1)).astype(np.float32)
    m = jnp.einsum('hrd,ckd->hrck', rpb, jnp.asarray(pick), precision=lax.Precision.HIGHEST)
    m = jnp.where(valid[None, None], m, NEG_INF)
    t = jnp.stack([m[:, o:o + WIN_ROWS] for o in range(WIN_ROWS)], axis=1)
    return t.transpose(0, 1, 3, 2, 4).reshape(rpb.shape[0], WIN_ROWS, GRID_W, NA_KEYS).astype(F32)


NA_ROWS_PER_STEP = 4


def _natten_kernel(q_ref, k_ref, v_ref, t_ref, o_ref):
    scale = NA_HEAD_DIM ** -0.5

    def rows(i, carry):
        rs = [i * NA_ROWS_PER_STEP + u for u in range(NA_ROWS_PER_STEP)]
        kr0s = [jnp.clip(r - WIN_ROWS // 2, 0, GRID_ROWS - WIN_ROWS) for r in rs]
        q0s = [pl.multiple_of(r * GRID_W, GRID_W) for r in rs]
        k0s = [pl.multiple_of(kr0 * GRID_W, GRID_W) for kr0 in kr0s]
        ss = [lax.dot_general(q_ref[pl.ds(q0, GRID_W), :], k_ref[pl.ds(k0, NA_KEYS), :],
                              (((1,), (1,)), ((), ())), preferred_element_type=F32)
              for q0, k0 in zip(q0s, k0s)]
        ps, ls = [], []
        for s, r, kr0 in zip(ss, rs, kr0s):
            s = s * scale + t_ref[kr0 - r + WIN_ROWS - 1]
            p = jnp.exp(s - jnp.max(s, axis=-1, keepdims=True))
            ls.append(jnp.sum(p, axis=-1, keepdims=True))
            ps.append(p.astype(BF16))
        for p, l, q0, k0 in zip(ps, ls, q0s, k0s):
            o = jnp.dot(p, v_ref[pl.ds(k0, NA_KEYS), :], preferred_element_type=F32) / l
            o_ref[pl.ds(q0, GRID_W), :] = o.astype(o_ref.dtype)
        return carry

    lax.fori_loop(0, GRID_ROWS // NA_ROWS_PER_STEP, rows, 0)


def _natten(p, table):
    blk = (SEQ, NA_HEAD_DIM)
    return pl.pallas_call(
        _natten_kernel,
        out_shape=jax.ShapeDtypeStruct((N_TOK, NA_WIDTH), BF16),
        grid=(BATCH, NA_HEADS),
        in_specs=[
            pl.BlockSpec(blk, lambda b, h: (b, h)),
            pl.BlockSpec(blk, lambda b, h: (b, NA_HEADS + h)),
            pl.BlockSpec(blk, lambda b, h: (b, 2 * NA_HEADS + h)),
            pl.BlockSpec((None, WIN_ROWS, GRID_W, NA_KEYS), lambda b, h: (h, 0, 0, 0)),
        ],
        out_specs=pl.BlockSpec(blk, lambda b, h: (b, h)),
        compiler_params=_params("parallel", "parallel"),
        name="natten",
    )(p, p, p, table)


CONV_CB = 128


def _conv_kernel(gb_ref, gc_ref, hc_ref, w_ref, b_ref, o_ref):
    z = gc_ref[...].astype(F32) * hc_ref[...].astype(F32)
    pos = lax.broadcasted_iota(jnp.int32, z.shape, 0)
    z_prev = jnp.where(pos == 0, 0.0, pltpu.roll(z, 1, 0))
    z_next = jnp.where(pos == SEQ - 1, 0.0, pltpu.roll(z, SEQ - 1, 0))
    y = b_ref[...] + z_prev * w_ref[0:1, :] + z * w_ref[1:2, :] + z_next * w_ref[2:3, :]
    o_ref[...] = (gb_ref[...].astype(F32) * y).astype(o_ref.dtype)


def _gated_conv(p, w, b):
    base = 3 * NA_WIDTH // CONV_CB
    step = CONV_CH // CONV_CB
    blk = (SEQ, CONV_CB)
    return pl.pallas_call(
        _conv_kernel,
        out_shape=jax.ShapeDtypeStruct((N_TOK, CONV_CH), BF16),
        grid=(BATCH, step),
        in_specs=[
            pl.BlockSpec(blk, lambda bi, c: (bi, base + c)),
            pl.BlockSpec(blk, lambda bi, c: (bi, base + step + c)),
            pl.BlockSpec(blk, lambda bi, c: (bi, base + 2 * step + c)),
            pl.BlockSpec((3, CONV_CB), lambda bi, c: (0, c)),
            pl.BlockSpec((1, CONV_CB), lambda bi, c: (0, c)),
        ],
        out_specs=pl.BlockSpec(blk, lambda bi, c: (bi, c)),
        compiler_params=_params("parallel", "parallel"),
        name="gated_conv",
    )(p, p, p, w, b.reshape(1, CONV_CH))


def _out_proj_kernel(a_ref, b_ref, wa_ref, wb_ref, r_ref, o_ref):
    acc = jnp.dot(a_ref[...], wa_ref[...], preferred_element_type=F32)
    acc = acc + jnp.dot(b_ref[...], wb_ref[...], preferred_element_type=F32)
    o_ref[...] = r_ref[...] + acc


def _out_proj(a, b, w, res, tm=512):
    ka, kb = a.shape[1], b.shape[1]
    assert ka == kb
    return pl.pallas_call(
        _out_proj_kernel,
        out_shape=jax.ShapeDtypeStruct((N_TOK, D_MODEL), F32),
        grid=(N_TOK // tm,),
        in_specs=[
            pl.BlockSpec((tm, ka), lambda i: (i, 0)),
            pl.BlockSpec((tm, kb), lambda i: (i, 0)),
            pl.BlockSpec((ka, D_MODEL), lambda i: (0, 0)),
            pl.BlockSpec((kb, D_MODEL), lambda i: (1, 0)),
            pl.BlockSpec((tm, D_MODEL), lambda i: (i, 0)),
        ],
        out_specs=pl.BlockSpec((tm, D_MODEL), lambda i: (i, 0)),
        compiler_params=_params("parallel"),
        name="out_proj",
    )(a, b, w, w, res)


MLA_TM = 512


def _rope_tables():
    pos = jnp.arange(SEQ, dtype=jnp.int32)
    row = (pos // GRID_W).astype(F32)
    col = (pos % GRID_W).astype(F32)
    half = QK_ROPE // 2
    inv = ROPE_THETA ** (-jnp.arange(0, half, 2, dtype=F32) / half)
    ar, ac = row[:, None] * inv, col[:, None] * inv
    cos_t = jnp.concatenate([jnp.cos(ar), jnp.cos(ar), jnp.cos(ac), jnp.cos(ac)], axis=-1)
    sin_t = jnp.concatenate([-jnp.sin(ar), jnp.sin(ar), -jnp.sin(ac), jnp.sin(ac)], axis=-1)
    quarter = half // 2
    src = np.arange(QK_ROPE) + np.where((np.arange(QK_ROPE) // quarter) % 2 == 0, quarter, -quarter)
    swap = np.zeros((QK_ROPE, QK_ROPE), np.float32)
    swap[src, np.arange(QK_ROPE)] = 1.0
    return cos_t, sin_t, jnp.asarray(swap)


def _rope(x, cos_t, sin_t, swap):
    xs = jnp.dot(x, swap, preferred_element_type=F32, precision=lax.Precision.HIGHEST)
    return x * cos_t + xs * sin_t


def _mla_prep_kernel(cq_ref, ckv_ref, kr_ref, gq_ref, gkv_ref, wq_ref, wkv_ref, cos_ref, sin_ref,
                     swap_ref, q_ref, k_ref, v_ref, cqn_ref, ckvn_ref, krot_ref):
    @pl.when(pl.program_id(1) == 0)
    def _():
        cqn_ref[...] = _rms(cq_ref[...].astype(F32), gq_ref[...]).astype(BF16)
        ckvn_ref[...] = _rms(ckv_ref[...].astype(F32), gkv_ref[...]).astype(BF16)
        kr = kr_ref[:, :QK_ROPE].astype(F32)
        krot_ref[...] = _rope(kr, cos_ref[...], sin_ref[...], swap_ref[...]).astype(BF16)

    scale = QK_DIM ** -0.5 * LOG2_E
    q = jnp.dot(cqn_ref[...], wq_ref[...], preferred_element_type=F32)
    q_ref[:, :QK_NOPE] = (q[:, :QK_NOPE] * scale).astype(BF16)
    qr = _rope(q[:, QK_NOPE:], cos_ref[...], sin_ref[...], swap_ref[...])
    q_ref[:, QK_NOPE:] = (qr * scale).astype(BF16)
    kv = jnp.dot(ckvn_ref[...], wkv_ref[...], preferred_element_type=F32)
    k_ref[:, :QK_NOPE] = kv[:, :QK_NOPE].astype(BF16)
    k_ref[:, QK_NOPE:] = krot_ref[...]
    v_ref[:, :V_DIM] = kv[:, QK_NOPE:].astype(BF16)
    lane = lax.broadcasted_iota(jnp.int32, (v_ref.shape[0], V_DIM), 1)
    v_ref[:, V_DIM:] = jnp.where(lane == 0, 1.0, 0.0).astype(BF16)


def _mla_prep(p, q_norm, kv_norm, wq, wkv):
    cos_t, sin_t, swap = _rope_tables()
    tm = MLA_TM
    seq_blocks = SEQ // tm
    return pl.pallas_call(
        _mla_prep_kernel,
        out_shape=(
            jax.ShapeDtypeStruct((MLA_HEADS, N_TOK, QK_DIM), BF16),
            jax.ShapeDtypeStruct((MLA_HEADS, N_TOK, QK_DIM), BF16),
            jax.ShapeDtypeStruct((MLA_HEADS, N_TOK, 2 * V_DIM), BF16),
        ),
        grid=(N_TOK // tm, MLA_HEADS),
        in_specs=[
            pl.BlockSpec((tm, Q_LORA), lambda i, h: (i, 0)),
            pl.BlockSpec((tm, KV_LORA), lambda i, h: (i, 1)),
            pl.BlockSpec((tm, LANES), lambda i, h: (i, KR_COL // LANES)),
            pl.BlockSpec((1, Q_LORA), lambda i, h: (0, 0)),
            pl.BlockSpec((1, KV_LORA), lambda i, h: (0, 0)),
            pl.BlockSpec((None, Q_LORA, QK_DIM), lambda i, h: (h, 0, 0)),
            pl.BlockSpec((None, KV_LORA, QK_NOPE + V_DIM), lambda i, h: (h, 0, 0)),
            pl.BlockSpec((tm, QK_ROPE), lambda i, h: (i % seq_blocks, 0)),
            pl.BlockSpec((tm, QK_ROPE), lambda i, h: (i % seq_blocks, 0)),
            pl.BlockSpec((QK_ROPE, QK_ROPE), lambda i, h: (0, 0)),
        ],
        out_specs=(
            pl.BlockSpec((None, tm, QK_DIM), lambda i, h: (h, i, 0)),
            pl.BlockSpec((None, tm, QK_DIM), lambda i, h: (h, i, 0)),
            pl.BlockSpec((None, tm, 2 * V_DIM), lambda i, h: (h, i, 0)),
        ),
        scratch_shapes=[
            pltpu.VMEM((tm, Q_LORA), BF16),
            pltpu.VMEM((tm, KV_LORA), BF16),
            pltpu.VMEM((tm, QK_ROPE), BF16),
        ],
        compiler_params=_params("parallel", "arbitrary"),
        name="mla_prep",
    )(p, p, p, q_norm.reshape(1, Q_LORA), kv_norm.reshape(1, KV_LORA), wq, wkv, cos_t, sin_t, swap)


MLA_TQ = 512
MLA_TK = 512
MLA_QH = 2


def _mla_attn_kernel(q_ref, k_ref, v_ref, o_ref, s_ref, m_ref, acc_ref):
    n_chunks = SEQ // MLA_TK
    rows = MLA_TQ // MLA_QH

    def scores(h, c):
        k0 = pl.multiple_of(c * MLA_TK, MLA_TK)
        return lax.dot_general(q_ref[h * rows:(h + 1) * rows, :], k_ref[pl.ds(k0, MLA_TK), :],
                               (((1,), (1,)), ((), ())), preferred_element_type=F32)

    m_ref[...] = jnp.full(m_ref.shape, NEG_INF, F32)
    acc_ref[...] = jnp.zeros(acc_ref.shape, F32)
    for h in range(MLA_QH):
        s_ref[0, h] = scores(h, 0)

    def step(c, slot):
        c_next = jnp.where(c + 1 < n_chunks, c + 1, 0)
        k0 = pl.multiple_of(c * MLA_TK, MLA_TK)
        for h in range(MLA_QH):
            s_ref[1 - slot, h] = scores(h, c_next)
            s = s_ref[slot, h]
            m_old = m_ref[h]
            m_new = jnp.maximum(m_old, jnp.max(s, axis=-1, keepdims=True))
            p = jnp.exp2(s - m_new).astype(BF16)
            pv = jnp.dot(p, v_ref[pl.ds(k0, MLA_TK), :], preferred_element_type=F32)
            acc_ref[h] = jnp.exp2(m_old - m_new) * acc_ref[h] + pv
            m_ref[h] = m_new

    def pair(i, carry):
        step(2 * i, 0)
        step(2 * i + 1, 1)
        return carry

    lax.fori_loop(0, n_chunks // 2, pair, 0)
    for h in range(MLA_QH):
        acc = acc_ref[h]
        o_ref[h * rows:(h + 1) * rows, :] = (acc[:, :V_DIM] / acc[:, V_DIM:V_DIM + 1]).astype(o_ref.dtype)


def _mla_attn(q, k, v):
    nq = SEQ // MLA_TQ
    return pl.pallas_call(
        _mla_attn_kernel,
        out_shape=jax.ShapeDtypeStruct((N_TOK, MLA_HEADS * V_DIM), BF16),
        grid=(BATCH, MLA_HEADS, nq),
        in_specs=[
            pl.BlockSpec((None, MLA_TQ, QK_DIM), lambda b, h, i: (h, b * nq + i, 0)),
            pl.BlockSpec((None, SEQ, QK_DIM), lambda b, h, i: (h, b, 0)),
            pl.BlockSpec((None, SEQ, 2 * V_DIM), lambda b, h, i: (h, b, 0)),
        ],
        out_specs=pl.BlockSpec((MLA_TQ, V_DIM), lambda b, h, i: (b * nq + i, h)),
        scratch_shapes=[
            pltpu.VMEM((2, MLA_QH, MLA_TQ // MLA_QH, MLA_TK), F32),
            pltpu.VMEM((MLA_QH, MLA_TQ // MLA_QH, 1), F32),
            pltpu.VMEM((MLA_QH, MLA_TQ // MLA_QH, 2 * V_DIM), F32),
        ],
        compiler_params=_params("parallel", "parallel", "parallel"),
        name="mla_attn",
    )(q, k, v)


SG_TM = 256


def _sg_kernel(u_ref, v_ref, g_ref, w_ref, bt_ref, o_ref):
    v = jax.nn.gelu(v_ref[...].astype(F32))
    vn = _rms(v, g_ref[...]).astype(BF16)
    u = jax.nn.gelu(u_ref[...].astype(F32))
    gc = SG_CH // SG_GROUPS
    for n in range(SG_TM // CHUNK):
        rows = slice(n * CHUNK, (n + 1) * CHUNK)
        for g in range(SG_GROUPS):
            cols = slice(g * gc, (g + 1) * gc)
            mixed = jnp.dot(w_ref[g], vn[rows, cols], preferred_element_type=F32) + bt_ref[:, g:g + 1]
            o_ref[rows, cols] = (u[rows, cols] * mixed).astype(o_ref.dtype)


def _spatial_gating(p, g_norm, w_s, b_s):
    u_blk = (Q_LORA + KV_LORA) // SG_CH
    return pl.pallas_call(
        _sg_kernel,
        out_shape=jax.ShapeDtypeStruct((N_TOK, SG_CH), BF16),
        grid=(N_TOK // SG_TM,),
        in_specs=[
            pl.BlockSpec((SG_TM, SG_CH), lambda i: (i, u_blk)),
            pl.BlockSpec((SG_TM, SG_CH), lambda i: (i, u_blk + 1)),
            pl.BlockSpec((1, SG_CH), lambda i: (0, 0)),
            pl.BlockSpec((SG_GROUPS, CHUNK, CHUNK), lambda i: (0, 0, 0)),
            pl.BlockSpec((CHUNK, SG_GROUPS), lambda i: (0, 0)),
        ],
        out_specs=pl.BlockSpec((SG_TM, SG_CH), lambda i: (i, 0)),
        compiler_params=_params("parallel"),
        name="spatial_gating",
    )(p, p, g_norm.reshape(1, SG_CH), w_s.astype(BF16), b_s.T)


ROUTER_TM = 512


def _router_kernel(x_ref, g_ref, w_ref, b_ref, o_ref):
    xn = _rms(x_ref[...], g_ref[...])
    logits = jnp.dot(xn, w_ref[...], preferred_element_type=F32,
                     precision=lax.Precision.HIGHEST) + b_ref[...]
    lane = lax.broadcasted_iota(jnp.int32, logits.shape, 1).astype(F32)
    low = jnp.float32(-3.0e38)

    def first_max(vals):
        top = jnp.max(vals, axis=-1, keepdims=True)
        idx = jnp.min(jnp.where(vals == top, lane, float(LANES)), axis=-1, keepdims=True)
        return top, idx

    is_group = lane < N_GROUPS
    g_top, g_idx = first_max(jnp.where(is_group, logits, low))
    g_prob = 1.0 / jnp.sum(jnp.where(is_group, jnp.exp(logits - g_top), 0.0), axis=-1, keepdims=True)
    lo = N_GROUPS + g_idx * EXPERTS_PER_GROUP
    e_vals = jnp.where(lane >= lo, jnp.where(lane < lo + EXPERTS_PER_GROUP, logits, low), low)
    v1, i1 = first_max(e_vals)
    v2, i2 = first_max(jnp.where(lane == i1, low, e_vals))
    e21 = jnp.exp(v2 - v1)
    w1 = g_prob / (1.0 + e21)
    w2 = g_prob * e21 / (1.0 + e21)
    out = jnp.where(lane == 0, i1 - N_GROUPS,
                    jnp.where(lane == 1, i2 - N_GROUPS,
                              jnp.where(lane == 2, w1, jnp.where(lane == 3, w2, 0.0))))
    o_ref[...] = out


def _router(x, g, w, b):
    tm = ROUTER_TM
    return pl.pallas_call(
        _router_kernel,
        out_shape=jax.ShapeDtypeStruct((N_TOK, LANES), F32),
        grid=(N_TOK // tm,),
        in_specs=[
            pl.BlockSpec((tm, D_MODEL), lambda i: (i, 0)),
            pl.BlockSpec((1, D_MODEL), lambda i: (0, 0)),
            pl.BlockSpec((D_MODEL, LANES), lambda i: (0, 0)),
            pl.BlockSpec((1, LANES), lambda i: (0, 0)),
        ],
        out_specs=pl.BlockSpec((tm, LANES), lambda i: (i, 0)),
        compiler_params=_params("parallel"),
        name="router",
    )(x, g.reshape(1, D_MODEL), w, b)


def _dispatch_plan(eid):
    tok = jnp.arange(N_TOK, dtype=jnp.int32)
    keys = jnp.concatenate([eid[:, k] * N_ASSIGN + (k * N_TOK + tok) for k in range(TOP_K)])
    row_asg = jnp.sort(keys) & (N_ASSIGN - 1)
    row_asg = jnp.concatenate([row_asg, jnp.zeros((SUBLANES,), jnp.int32)])
    counts = jnp.sum(eid.reshape(-1)[:, None] == jnp.arange(N_EXPERTS, dtype=jnp.int32)[None, :], axis=0,
                     dtype=jnp.int32)
    starts = jnp.cumsum(counts) - counts
    nblk_e = (counts + MOE_BLOCK - 1) // MOE_BLOCK
    blk_end = jnp.cumsum(nblk_e)
    blk_start = blk_end - nblk_e
    n_active = blk_end[-1]
    bidx = jnp.arange(MOE_NBLK, dtype=jnp.int32)
    blk_x = jnp.minimum(bidx, n_active - 1)
    blk_e = jnp.minimum(jnp.sum(blk_end[None, :] <= blk_x[:, None], axis=1, dtype=jnp.int32), N_EXPERTS - 1)
    first = (blk_x - blk_start[blk_e]) * MOE_BLOCK
    blk_rows = jnp.where(bidx < n_active, jnp.clip(counts[blk_e] - first, 0, MOE_BLOCK), 0).astype(jnp.int32)
    blk_src = (starts[blk_e] + first).astype(jnp.int32)
    return row_asg, blk_e, blk_rows, blk_src


_ROW_BITS = tuple(1 << k for k in range(MOE_BLOCK.bit_length() - 1, -1, -1))


def _expert_kernel(row_asg, blk_e, blk_rows, blk_src, x_hbm, g_ref, w1_ref, w3_ref, w2_ref, ys_hbm,
                   xbuf, acc, xn_ref, gsem, ssem):
    b = pl.program_id(0)
    j = pl.program_id(1)
    rows = blk_rows[b]
    slot = b % 2

    def n_fetch(blk):
        return (blk_rows[blk] + SUBLANES - 1) & -SUBLANES

    def issue_gather(blk, s):
        base = blk_src[blk]

        def group(q, c):
            for u in range(SUBLANES):
                t = row_asg[base + q * SUBLANES + u] & (N_TOK - 1)
                pltpu.make_async_copy(x_hbm.at[t >> SUBLANE_SHIFT, pl.ds(t & (SUBLANES - 1), 1), :],
                                      xbuf.at[s, q, pl.ds(u, 1), :], gsem.at[s]).start()
            return c

        lax.fori_loop(0, n_fetch(blk) >> SUBLANE_SHIFT, group, 0)

    def wait_rows(n, copy_of):
        for bit in _ROW_BITS:
            @pl.when((n & bit) != 0)
            def _():
                copy_of(bit).wait()

    def gather_block(s):
        return lambda k: pltpu.make_async_copy(x_hbm.at[pl.ds(0, k // SUBLANES)],
                                               xbuf.at[s, pl.ds(0, k // SUBLANES)], gsem.at[s])

    def scatter_block(s):
        def copy_of(k):
            if k >= SUBLANES:
                return pltpu.make_async_copy(acc.at[s, pl.ds(0, k // SUBLANES)],
                                             ys_hbm.at[pl.ds(0, k // SUBLANES)], ssem.at[s])
            return pltpu.make_async_copy(acc.at[s, 0, pl.ds(0, k), :], ys_hbm.at[0, pl.ds(0, k), :], ssem.at[s])
        return copy_of

    def issue_scatter(blk, s, n):
        base = blk_src[blk]

        def one(q, u, a):
            pltpu.make_async_copy(acc.at[s, q, pl.ds(u, 1), :],
                                  ys_hbm.at[a >> SUBLANE_SHIFT, pl.ds(a & (SUBLANES - 1), 1), :],
                                  ssem.at[s]).start()

        def group(q, c):
            for u in range(SUBLANES):
                one(q, u, row_asg[base + q * SUBLANES + u])
            return c

        def tail(i, c):
            one(i >> SUBLANE_SHIFT, i & (SUBLANES - 1), row_asg[base + i])
            return c

        full = n >> SUBLANE_SHIFT
        lax.fori_loop(0, full, group, 0)
        lax.fori_loop(full * SUBLANES, n, tail, 0)

    def for_row_tile(fn):
        lo = 0
        for m in MOE_ROW_TILES:
            @pl.when(jnp.logical_and(rows > lo, rows <= m))
            def _():
                fn(m)
            lo = m

    nxt = jnp.minimum(b + 1, MOE_NBLK - 1)
    has_next = jnp.logical_and(b + 1 < MOE_NBLK, blk_rows[nxt] > 0)

    @pl.when(rows > 0)
    def _():
        @pl.when(j == 0)
        def _():
            @pl.when(b == 0)
            def _():
                xbuf[...] = jnp.zeros(xbuf.shape, F32)
                acc[...] = jnp.zeros(acc.shape, F32)
                issue_gather(0, 0)

            wait_rows(n_fetch(b), gather_block(slot))

            @pl.when(has_next)
            def _():
                issue_gather(nxt, 1 - slot)

            def norm_tile(m):
                xt = xbuf[slot, :m // SUBLANES].reshape(m, D_MODEL)
                xn_ref[:m, :] = _rms(xt, g_ref[...]).astype(BF16)

            for_row_tile(norm_tile)

        def ffn_tile(m):
            xs = xn_ref[:m, :]
            h1 = jnp.dot(xs, w1_ref[...].astype(BF16), preferred_element_type=F32)
            h3 = jnp.dot(xs, w3_ref[...].astype(BF16), preferred_element_type=F32)
            h = (jax.nn.silu(h1) * h3).astype(BF16)
            y = jnp.dot(h, w2_ref[...].astype(BF16), preferred_element_type=F32)
            mt = m // SUBLANES
            acc[slot, :mt] = jnp.where(j > 0, acc[slot, :mt], 0.0) + y.reshape(mt, SUBLANES, D_MODEL)

        for_row_tile(ffn_tile)

        @pl.when(j == N_FT - 1)
        def _():
            @pl.when(b > 0)
            def _():
                wait_rows(blk_rows[jnp.maximum(b - 1, 0)], scatter_block(1 - slot))

            issue_scatter(b, slot, rows)

            @pl.when(jnp.logical_not(has_next))
            def _():
                wait_rows(rows, scatter_block(slot))


def _moe_experts(x, g, w1, w3, w2, layer, row_asg, blk_e, blk_rows, blk_src):
    def w_col(b, j, ra, be, br, bs):
        return (layer, be[b], 0, jnp.where(br[b] > 0, j, N_FT - 1))

    def w_row(b, j, ra, be, br, bs):
        return (layer, be[b], jnp.where(br[b] > 0, j, N_FT - 1), 0)

    return pl.pallas_call(
        _expert_kernel,
        out_shape=jax.ShapeDtypeStruct((N_ASSIGN // SUBLANES, SUBLANES, D_MODEL), F32),
        grid_spec=pltpu.PrefetchScalarGridSpec(
            num_scalar_prefetch=4,
            grid=(MOE_NBLK, N_FT),
            in_specs=[
                pl.BlockSpec(memory_space=pl.ANY),
                pl.BlockSpec((1, D_MODEL), lambda b, j, ra, be, br, bs: (0, 0)),
                pl.BlockSpec((None, None, D_MODEL, MOE_FT), w_col),
                pl.BlockSpec((None, None, D_MODEL, MOE_FT), w_col),
                pl.BlockSpec((None, None, MOE_FT, D_MODEL), w_row),
            ],
            out_specs=pl.BlockSpec(memory_space=pl.ANY),
            scratch_shapes=[
                pltpu.VMEM((2, MOE_BLOCK // SUBLANES, SUBLANES, D_MODEL), F32),
                pltpu.VMEM((2, MOE_BLOCK // SUBLANES, SUBLANES, D_MODEL), F32),
                pltpu.VMEM((MOE_BLOCK, D_MODEL), BF16),
                pltpu.SemaphoreType.DMA((2,)),
                pltpu.SemaphoreType.DMA((2,)),
            ],
        ),
        compiler_params=_params("arbitrary", "arbitrary"),
        name="moe_experts",
    )(row_asg, blk_e, blk_rows, blk_src, x.reshape(N_TOK // SUBLANES, SUBLANES, D_MODEL), g.reshape(1, D_MODEL),
      w1, w3, w2).reshape(N_ASSIGN, D_MODEL)


def _combine_kernel(final_norm, x_ref, y0_ref, y1_ref, gate_ref, gf_ref, o_ref):
    gates = gate_ref[...]
    out = x_ref[...] + (gates[:, 2:3] * y0_ref[...] + gates[:, 3:4] * y1_ref[...])
    if final_norm:
        out = _rms(out, gf_ref[...])
    o_ref[...] = out


def _moe_combine(x, ys, route, g_final, final_norm):
    tm = COMBINE_TOK
    return pl.pallas_call(
        functools.partial(_combine_kernel, final_norm),
        out_shape=jax.ShapeDtypeStruct((N_TOK, D_MODEL), F32),
        grid=(N_TOK // tm,),
        in_specs=[
            pl.BlockSpec((tm, D_MODEL), lambda i: (i, 0)),
            pl.BlockSpec((tm, D_MODEL), lambda i: (i, 0)),
            pl.BlockSpec((tm, D_MODEL), lambda i: (N_TOK // tm + i, 0)),
            pl.BlockSpec((tm, LANES), lambda i: (i, 0)),
            pl.BlockSpec((1, D_MODEL), lambda i: (0, 0)),
        ],
        out_specs=pl.BlockSpec((tm, D_MODEL), lambda i: (i, 0)),
        compiler_params=_params("parallel"),
        name="moe_combine",
    )(x, ys, ys, route, g_final.reshape(1, D_MODEL))


def _hier_moe(x, g, wg, bg, we, be, w1, w3, w2, layer, g_final, final_norm):
    pad = LANES - N_GROUPS - N_EXPERTS
    w_r = jnp.concatenate([wg, we, jnp.zeros((D_MODEL, pad), F32)], axis=1)
    b_r = jnp.concatenate([bg, be, jnp.zeros((pad,), F32)]).reshape(1, LANES)
    route = _router(x, g, w_r, b_r)
    eid = route[:, :TOP_K].astype(jnp.int32)
    row_asg, blk_e, blk_rows, blk_src = _dispatch_plan(eid)
    ys = _moe_experts(x, g, w1, w3, w2, layer, row_asg, blk_e, blk_rows, blk_src)
    return _moe_combine(x, ys, route, g_final, final_norm)


def kernel(x, norm_mix, norm_ffn, norm_final, w_in_ab, na_rpb, conv_w, conv_b, w_out_ab, w_in_cd,
           q_norm, kv_norm, w_uq, w_ukv, sg_norm, sg_w, sg_b, w_out_cd, router_group_w,
           router_group_b, router_expert_w, router_expert_b, w1, w3, w2):
    xt = x.reshape(N_TOK, D_MODEL)
    for layer in range(DEPTH):
        i = layer // 2
        if layer % 2 == 0:
            p = _norm_matmul(xt, norm_mix[layer], w_in_ab[i].astype(BF16), tm=1024, tn=512)
            a_out = _natten(p, _natten_bias_table(na_rpb[i]))
            b_out = _gated_conv(p, conv_w[i], conv_b[i])
            xt = _out_proj(a_out, b_out, w_out_ab[i].astype(BF16), xt)
        else:
            w = w_in_cd[i]
            c0, c1, c2 = Q_LORA + KV_LORA, Q_LORA + KV_LORA + QK_ROPE, CD_IN_PAD - KR_COL - QK_ROPE
            w = jnp.concatenate([w[:, :c0], w[:, c1:], w[:, c0:c1], jnp.zeros((D_MODEL, c2), F32)], axis=1)
            p = _norm_matmul(xt, norm_mix[layer], w.astype(BF16), tm=1024, tn=640)
            wq = w_uq[i].reshape(Q_LORA, MLA_HEADS, QK_DIM).transpose(1, 0, 2).astype(BF16)
            wkv = w_ukv[i].reshape(KV_LORA, MLA_HEADS, QK_NOPE + V_DIM).transpose(1, 0, 2).astype(BF16)
            q, k, v = _mla_prep(p, q_norm[i], kv_norm[i], wq, wkv)
            c_out = _mla_attn(q, k, v)
            d_out = _spatial_gating(p, sg_norm[i], sg_w[i], sg_b[i])
            xt = _out_proj(c_out, d_out, w_out_cd[i].astype(BF16), xt)
        xt = _hier_moe(xt, norm_ffn[layer], router_group_w[layer], router_group_b[layer],
                       router_expert_w[layer], router_expert_b[layer], w1, w3, w2, layer,
                       norm_final, final_norm=(layer == DEPTH - 1))
    return xt.reshape(BATCH, SEQ, D_MODEL)
```

```python
import functools

import numpy as np
import jax
import jax.numpy as jnp
from jax import lax
from jax.experimental import pallas as pl
from jax.experimental.pallas import tpu as pltpu

F32 = jnp.float32
BF16 = jnp.bfloat16

D_MODEL = 2048
BATCH = 2
SEQ = 4096
N_TOK = BATCH * SEQ
DEPTH = 2
GRID_W = 64
GRID_ROWS = SEQ // GRID_W
WIN_ROWS = 8
WIN_COLS = 16
NA_HEADS = 8
NA_HEAD_DIM = 128
NA_WIDTH = NA_HEADS * NA_HEAD_DIM
NA_KEYS = WIN_ROWS * GRID_W
CONV_CH = D_MODEL // 2
MLA_HEADS = 8
Q_LORA = 512
KV_LORA = 512
QK_NOPE = 128
QK_ROPE = 64
QK_DIM = QK_NOPE + QK_ROPE
V_DIM = 128
ROPE_THETA = 10000.0
SG_CH = D_MODEL // 2
SG_GROUPS = 8
CHUNK = 128
N_GROUPS = 8
EXPERTS_PER_GROUP = 8
N_EXPERTS = N_GROUPS * EXPERTS_PER_GROUP
TOP_K = 2
D_EXPERT = 768
EPS = 1e-6
NEG_INF = -1e30
LOG2_E = 1.4426950408889634
AB_IN = 3 * NA_WIDTH + 3 * CONV_CH
CD_IN_PAD = 3200
KR_COL = Q_LORA + KV_LORA + 2 * SG_CH

LANES = 128
SUBLANES = 8
SUBLANE_SHIFT = 3
MOE_BLOCK = 512
MOE_ROW_TILES = (256, 320, 512)
MOE_FT = 256
N_FT = D_EXPERT // MOE_FT
N_ASSIGN = N_TOK * TOP_K
MOE_NBLK = N_ASSIGN // MOE_BLOCK + N_EXPERTS
COMBINE_TOK = 256
VMEM_LIMIT = 52 * 1024 * 1024


def _params(*sem):
    return pltpu.CompilerParams(dimension_semantics=sem, vmem_limit_bytes=VMEM_LIMIT)


def _rms(x, g):
    return x * lax.rsqrt(jnp.mean(x * x, axis=-1, keepdims=True) + EPS) * g


def _norm_mm_kernel(x_ref, g_ref, w_ref, o_ref, xn_ref):
    @pl.when(pl.program_id(1) == 0)
    def _():
        xn_ref[...] = _rms(x_ref[...].astype(F32), g_ref[...]).astype(BF16)

    o_ref[...] = jnp.dot(xn_ref[...], w_ref[...], preferred_element_type=F32).astype(o_ref.dtype)


def _norm_matmul(x, g, w, tm, tn):
    m, k = x.shape
    nout = w.shape[1]
    return pl.pallas_call(
        _norm_mm_kernel,
        out_shape=jax.ShapeDtypeStruct((m, nout), BF16),
        grid=(m // tm, nout // tn),
        in_specs=[
            pl.BlockSpec((tm, k), lambda i, j: (i, 0)),
            pl.BlockSpec((1, k), lambda i, j: (0, 0)),
            pl.BlockSpec((k, tn), lambda i, j: (0, j)),
        ],
        out_specs=pl.BlockSpec((tm, tn), lambda i, j: (i, j)),
        scratch_shapes=[pltpu.VMEM((tm, k), BF16)],
        compiler_params=_params("parallel", "arbitrary"),
        name="norm_matmul",
    )(x, g.reshape(1, k), w)


def _natten_bias_table(rpb):
    c = np.arange(GRID_W)
    col_start = np.clip(c - WIN_COLS // 2, 0, GRID_W - WIN_COLS)
    valid = (c[None, :] >= col_start[:, None]) & (c[None, :] < col_start[:, None] + WIN_COLS)
    dc = np.clip(c[None, :] - c[:, None] + WIN_COLS - 1, 0, 2 * WIN_COLS - 2)
    pick = (dc[:, :, None] == np.arange(2 * WIN_COLS - 1)).astype(np.float32)
    m = jnp.einsum('hrd,ckd->hrck', rpb, jnp.asarray(pick), precision=lax.Precision.HIGHEST)
    m = jnp.where(valid[None, None], m, NEG_INF)
    t = jnp.stack([m[:, o:o + WIN_ROWS] for o in range(WIN_ROWS)], axis=1)
    return t.transpose(0, 1, 3, 2, 4).reshape(rpb.shape[0], WIN_ROWS, GRID_W, NA_KEYS).astype(F32)


NA_ROWS_PER_STEP = 4


def _natten_kernel(q_ref, k_ref, v_ref, t_ref, o_ref):
    scale = NA_HEAD_DIM ** -0.5

    def rows(i, carry):
        rs = [i * NA_ROWS_PER_STEP + u for u in range(NA_ROWS_PER_STEP)]
        kr0s = [jnp.clip(r - WIN_ROWS // 2, 0, GRID_ROWS - WIN_ROWS) for r in rs]
        q0s = [pl.multiple_of(r * GRID_W, GRID_W) for r in rs]
        k0s = [pl.multiple_of(kr0 * GRID_W, GRID_W) for kr0 in kr0s]
        ss = [lax.dot_general(q_ref[pl.ds(q0, GRID_W), :], k_ref[pl.ds(k0, NA_KEYS), :],
                              (((1,), (1,)), ((), ())), preferred_element_type=F32)
              for q0, k0 in zip(q0s, k0s)]
        ps, ls = [], []
        for s, r, kr0 in zip(ss, rs, kr0s):
            s = s * scale + t_ref[kr0 - r + WIN_ROWS - 1]
            p = jnp.exp(s - jnp.max(s, axis=-1, keepdims=True))
            ls.append(jnp.sum(p, axis=-1, keepdims=True))
            ps.append(p.astype(BF16))
        for p, l, q0, k0 in zip(ps, ls, q0s, k0s):
            o = jnp.dot(p, v_ref[pl.ds(k0, NA_KEYS), :], preferred_element_type=F32) / l
            o_ref[pl.ds(q0, GRID_W), :] = o.astype(o_ref.dtype)
        return carry

    lax.fori_loop(0, GRID_ROWS // NA_ROWS_PER_STEP, rows, 0)


def _natten(p, table):
    blk = (SEQ, NA_HEAD_DIM)
    return pl.pallas_call(
        _natten_kernel,
        out_shape=jax.ShapeDtypeStruct((N_TOK, NA_WIDTH), BF16),
        grid=(BATCH, NA_HEADS),
        in_specs=[
            pl.BlockSpec(blk, lambda b, h: (b, h)),
            pl.BlockSpec(blk, lambda b, h: (b, NA_HEADS + h)),
            pl.BlockSpec(blk, lambda b, h: (b, 2 * NA_HEADS + h)),
            pl.BlockSpec((None, WIN_ROWS, GRID_W, NA_KEYS), lambda b, h: (h, 0, 0, 0)),
        ],
        out_specs=pl.BlockSpec(blk, lambda b, h: (b, h)),
        compiler_params=_params("parallel", "parallel"),
        name="natten",
    )(p, p, p, table)


CONV_CB = 128


def _conv_kernel(gb_ref, gc_ref, hc_ref, w_ref, b_ref, o_ref):
    z = gc_ref[...].astype(F32) * hc_ref[...].astype(F32)
    pos = lax.broadcasted_iota(jnp.int32, z.shape, 0)
    z_prev = jnp.where(pos == 0, 0.0, pltpu.roll(z, 1, 0))
    z_next = jnp.where(pos == SEQ - 1, 0.0, pltpu.roll(z, SEQ - 1, 0))
    y = b_ref[...] + z_prev * w_ref[0:1, :] + z * w_ref[1:2, :] + z_next * w_ref[2:3, :]
    o_ref[...] = (gb_ref[...].astype(F32) * y).astype(o_ref.dtype)


def _gated_conv(p, w, b):
    base = 3 * NA_WIDTH // CONV_CB
    step = CONV_CH // CONV_CB
    blk = (SEQ, CONV_CB)
    return pl.pallas_call(
        _conv_kernel,
        out_shape=jax.ShapeDtypeStruct((N_TOK, CONV_CH), BF16),
        grid=(BATCH, step),
        in_specs=[
            pl.BlockSpec(blk, lambda bi, c: (bi, base + c)),
            pl.BlockSpec(blk, lambda bi, c: (bi, base + step + c)),
            pl.BlockSpec(blk, lambda bi, c: (bi, base + 2 * step + c)),
            pl.BlockSpec((3, CONV_CB), lambda bi, c: (0, c)),
            pl.BlockSpec((1, CONV_CB), lambda bi, c: (0, c)),
        ],
        out_specs=pl.BlockSpec(blk, lambda bi, c: (bi, c)),
        compiler_params=_params("parallel", "parallel"),
        name="gated_conv",
    )(p, p, p, w, b.reshape(1, CONV_CH))


def _out_proj_kernel(a_ref, b_ref, wa_ref, wb_ref, r_ref, o_ref):
    acc = jnp.dot(a_ref[...], wa_ref[...], preferred_element_type=F32)
    acc = acc + jnp.dot(b_ref[...], wb_ref[...], preferred_element_type=F32)
    o_ref[...] = r_ref[...] + acc


def _out_proj(a, b, w, res, tm=512):
    ka, kb = a.shape[1], b.shape[1]
    assert ka == kb
    return pl.pallas_call(
        _out_proj_kernel,
        out_shape=jax.ShapeDtypeStruct((N_TOK, D_MODEL), F32),
        grid=(N_TOK // tm,),
        in_specs=[
            pl.BlockSpec((tm, ka), lambda i: (i, 0)),
            pl.BlockSpec((tm, kb), lambda i: (i, 0)),
            pl.BlockSpec((ka, D_MODEL), lambda i: (0, 0)),
            pl.BlockSpec((kb, D_MODEL), lambda i: (1, 0)),
            pl.BlockSpec((tm, D_MODEL), lambda i: (i, 0)),
        ],
        out_specs=pl.BlockSpec((tm, D_MODEL), lambda i: (i, 0)),
        compiler_params=_params("parallel"),
        name="out_proj",
    )(a, b, w, w, res)


MLA_TM = 512


def _rope_tables():
    pos = jnp.arange(SEQ, dtype=jnp.int32)
    row = (pos // GRID_W).astype(F32)
    col = (pos % GRID_W).astype(F32)
    half = QK_ROPE // 2
    inv = ROPE_THETA ** (-jnp.arange(0, half, 2, dtype=F32) / half)
    ar, ac = row[:, None] * inv, col[:, None] * inv
    cos_t = jnp.concatenate([jnp.cos(ar), jnp.cos(ar), jnp.cos(ac), jnp.cos(ac)], axis=-1)
    sin_t = jnp.concatenate([-jnp.sin(ar), jnp.sin(ar), -jnp.sin(ac), jnp.sin(ac)], axis=-1)
    quarter = half // 2
    src = np.arange(QK_ROPE) + np.where((np.arange(QK_ROPE) // quarter) % 2 == 0, quarter, -quarter)
    swap = np.zeros((QK_ROPE, QK_ROPE), np.float32)
    swap[src, np.arange(QK_ROPE)] = 1.0
    swap_all = np.kron(np.eye(MLA_HEADS, dtype=np.float32), swap)
    return jnp.tile(cos_t, (1, MLA_HEADS)), jnp.tile(sin_t, (1, MLA_HEADS)), jnp.asarray(swap_all, BF16)


def _rope(x, cos_t, sin_t, swap):
    hi = x.astype(BF16)
    lo = (x - hi.astype(F32)).astype(BF16)
    xs = jnp.dot(hi, swap, preferred_element_type=F32) + jnp.dot(lo, swap, preferred_element_type=F32)
    return x * cos_t + xs * sin_t


ROPE_ALL = MLA_HEADS * QK_ROPE
NOPE_ALL = MLA_HEADS * QK_NOPE


def _mla_prep_kernel(cq_ref, ckv_ref, kr_ref, gq_ref, gkv_ref, wq_ref, wkv_ref, cos_ref, sin_ref,
                     swap_ref, q_ref, k_ref, v_ref):
    scale = QK_DIM ** -0.5 * LOG2_E
    cos_t, sin_t, swap = cos_ref[...], sin_ref[...], swap_ref[...]
    cqn = _rms(cq_ref[...].astype(F32), gq_ref[...]).astype(BF16)
    ckvn = _rms(ckv_ref[...].astype(F32), gkv_ref[...]).astype(BF16)
    kr = kr_ref[:, :QK_ROPE].astype(F32)
    krot = _rope(kr, cos_t[:, :QK_ROPE], sin_t[:, :QK_ROPE], swap[:QK_ROPE, :QK_ROPE]).astype(BF16)
    q = jnp.dot(cqn, wq_ref[...], preferred_element_type=F32)
    qn = (q[:, :NOPE_ALL] * scale).astype(BF16)
    qr = (_rope(q[:, NOPE_ALL:], cos_t, sin_t, swap) * scale).astype(BF16)
    kv = jnp.dot(ckvn, wkv_ref[...], preferred_element_type=F32)
    hw = QK_NOPE + V_DIM
    for h in range(MLA_HEADS):
        q_ref[h, :, :QK_NOPE] = qn[:, h * QK_NOPE:(h + 1) * QK_NOPE]
        q_ref[h, :, QK_NOPE:] = qr[:, h * QK_ROPE:(h + 1) * QK_ROPE]
        k_ref[h, :, :QK_NOPE] = kv[:, h * hw:h * hw + QK_NOPE].astype(BF16)
        k_ref[h, :, QK_NOPE:] = krot
        v_ref[h, 0] = kv[:, h * hw + QK_NOPE:(h + 1) * hw].T.astype(BF16)


def _mla_prep(p, q_norm, kv_norm, wq, wkv):
    cos_t, sin_t, swap = _rope_tables()
    tm = MLA_TM
    seq_blocks = SEQ // tm
    return pl.pallas_call(
        _mla_prep_kernel,
        out_shape=(
            jax.ShapeDtypeStruct((MLA_HEADS, N_TOK, QK_DIM), BF16),
            jax.ShapeDtypeStruct((MLA_HEADS, N_TOK, QK_DIM), BF16),
            jax.ShapeDtypeStruct((MLA_HEADS, N_TOK // tm, V_DIM, tm), BF16),
        ),
        grid=(N_TOK // tm,),
        in_specs=[
            pl.BlockSpec((tm, Q_LORA), lambda i: (i, 0)),
            pl.BlockSpec((tm, KV_LORA), lambda i: (i, 1)),
            pl.BlockSpec((tm, LANES), lambda i: (i, KR_COL // LANES)),
            pl.BlockSpec((1, Q_LORA), lambda i: (0, 0)),
            pl.BlockSpec((1, KV_LORA), lambda i: (0, 0)),
            pl.BlockSpec((Q_LORA, NOPE_ALL + ROPE_ALL), lambda i: (0, 0)),
            pl.BlockSpec((KV_LORA, MLA_HEADS * (QK_NOPE + V_DIM)), lambda i: (0, 0)),
            pl.BlockSpec((tm, ROPE_ALL), lambda i: (i % seq_blocks, 0)),
            pl.BlockSpec((tm, ROPE_ALL), lambda i: (i % seq_blocks, 0)),
            pl.BlockSpec((ROPE_ALL, ROPE_ALL), lambda i: (0, 0)),
        ],
        out_specs=(
            pl.BlockSpec((MLA_HEADS, tm, QK_DIM), lambda i: (0, i, 0)),
            pl.BlockSpec((MLA_HEADS, tm, QK_DIM), lambda i: (0, i, 0)),
            pl.BlockSpec((MLA_HEADS, 1, V_DIM, tm), lambda i: (0, i, 0, 0)),
        ),
        compiler_params=_params("parallel"),
        name="mla_prep",
    )(p, p, p, q_norm.reshape(1, Q_LORA), kv_norm.reshape(1, KV_LORA), wq, wkv, cos_t, sin_t, swap)


MLA_TQ = 4096
MLA_TK = 512
MLA_QH = 16
assert MLA_TK == MLA_TM


def _mla_attn_kernel(q_ref, k_ref, vt_ref, o_ref, s_ref, m_ref, l_ref, acc_ref):
    n_chunks = SEQ // MLA_TK
    cols = MLA_TQ // MLA_QH

    def scores(h, c):
        k0 = pl.multiple_of(c * MLA_TK, MLA_TK)
        return lax.dot_general(k_ref[pl.ds(k0, MLA_TK), :], q_ref[h * cols:(h + 1) * cols, :],
                               (((1,), (1,)), ((), ())), preferred_element_type=F32)

    m_ref[...] = jnp.full(m_ref.shape, NEG_INF, F32)
    l_ref[...] = jnp.zeros(l_ref.shape, F32)
    acc_ref[...] = jnp.zeros(acc_ref.shape, F32)
    for h in range(MLA_QH):
        s_ref[0, h] = scores(h, 0)

    def step(c, slot):
        c_next = jnp.where(c + 1 < n_chunks, c + 1, 0)
        for h in range(MLA_QH):
            s_ref[1 - slot, h] = scores(h, c_next)
            s = s_ref[slot, h]
            m_old = m_ref[h]
            m_new = jnp.maximum(m_old, jnp.max(s, axis=0, keepdims=True))
            p = jnp.exp2(s - m_new)
            alpha = jnp.exp2(m_old - m_new)
            l_ref[h] = alpha * l_ref[h] + jnp.sum(p, axis=0, keepdims=True)
            pv = jnp.dot(vt_ref[c], p.astype(BF16), preferred_element_type=F32)
            acc_ref[h] = alpha * acc_ref[h] + pv
            m_ref[h] = m_new

    def pair(i, carry):
        step(2 * i, 0)
        step(2 * i + 1, 1)
        return carry

    lax.fori_loop(0, n_chunks // 2, pair, 0)
    for h in range(MLA_QH):
        o_ref[h * cols:(h + 1) * cols, :] = (acc_ref[h] / l_ref[h]).T.astype(o_ref.dtype)


def _mla_attn(q, k, v):
    nq = SEQ // MLA_TQ
    return pl.pallas_call(
        _mla_attn_kernel,
        out_shape=jax.ShapeDtypeStruct((N_TOK, MLA_HEADS * V_DIM), BF16),
        grid=(BATCH, MLA_HEADS, nq),
        in_specs=[
            pl.BlockSpec((None, MLA_TQ, QK_DIM), lambda b, h, i: (h, b * nq + i, 0)),
            pl.BlockSpec((None, SEQ, QK_DIM), lambda b, h, i: (h, b, 0)),
            pl.BlockSpec((None, SEQ // MLA_TK, V_DIM, MLA_TK), lambda b, h, i: (h, b, 0, 0)),
        ],
        out_specs=pl.BlockSpec((MLA_TQ, V_DIM), lambda b, h, i: (b * nq + i, h)),
        scratch_shapes=[
            pltpu.VMEM((2, MLA_QH, MLA_TK, MLA_TQ // MLA_QH), F32),
            pltpu.VMEM((MLA_QH, 1, MLA_TQ // MLA_QH), F32),
            pltpu.VMEM((MLA_QH, 1, MLA_TQ // MLA_QH), F32),
            pltpu.VMEM((MLA_QH, V_DIM, MLA_TQ // MLA_QH), F32),
        ],
        compiler_params=_params("parallel", "parallel", "parallel"),
        name="mla_attn",
    )(q, k, v)


SG_TM = 256


def _sg_kernel(u_ref, v_ref, g_ref, w_ref, bt_ref, o_ref):
    v = jax.nn.gelu(v_ref[...].astype(F32))
    vn = _rms(v, g_ref[...]).astype(BF16)
    u = jax.nn.gelu(u_ref[...].astype(F32))
    gc = SG_CH // SG_GROUPS
    for n in range(SG_TM // CHUNK):
        rows = slice(n * CHUNK, (n + 1) * CHUNK)
        for g in range(SG_GROUPS):
            cols = slice(g * gc, (g + 1) * gc)
            mixed = jnp.dot(w_ref[g], vn[rows, cols], preferred_element_type=F32) + bt_ref[:, g:g + 1]
            o_ref[rows, cols] = (u[rows, cols] * mixed).astype(o_ref.dtype)


def _spatial_gating(p, g_norm, w_s, b_s):
    u_blk = (Q_LORA + KV_LORA) // SG_CH
    return pl.pallas_call(
        _sg_kernel,
        out_shape=jax.ShapeDtypeStruct((N_TOK, SG_CH), BF16),
        grid=(N_TOK // SG_TM,),
        in_specs=[
            pl.BlockSpec((SG_TM, SG_CH), lambda i: (i, u_blk)),
            pl.BlockSpec((SG_TM, SG_CH), lambda i: (i, u_blk + 1)),
            pl.BlockSpec((1, SG_CH), lambda i: (0, 0)),
            pl.BlockSpec((SG_GROUPS, CHUNK, CHUNK), lambda i: (0, 0, 0)),
            pl.BlockSpec((CHUNK, SG_GROUPS), lambda i: (0, 0)),
        ],
        out_specs=pl.BlockSpec((SG_TM, SG_CH), lambda i: (i, 0)),
        compiler_params=_params("parallel"),
        name="spatial_gating",
    )(p, p, g_norm.reshape(1, SG_CH), w_s.astype(BF16), b_s.T)


ROUTER_TM = 512


def _router_kernel(x_ref, g_ref, w_ref, b_ref, o_ref):
    xn = _rms(x_ref[...], g_ref[...])
    logits = jnp.dot(xn, w_ref[...], preferred_element_type=F32,
                     precision=lax.Precision.HIGHEST) + b_ref[...]
    lane = lax.broadcasted_iota(jnp.int32, logits.shape, 1).astype(F32)
    low = jnp.float32(-3.0e38)

    def first_max(vals):
        top = jnp.max(vals, axis=-1, keepdims=True)
        idx = jnp.min(jnp.where(vals == top, lane, float(LANES)), axis=-1, keepdims=True)
        return top, idx

    is_group = lane < N_GROUPS
    g_top, g_idx = first_max(jnp.where(is_group, logits, low))
    g_prob = 1.0 / jnp.sum(jnp.where(is_group, jnp.exp(logits - g_top), 0.0), axis=-1, keepdims=True)
    lo = N_GROUPS + g_idx * EXPERTS_PER_GROUP
    e_vals = jnp.where(lane >= lo, jnp.where(lane < lo + EXPERTS_PER_GROUP, logits, low), low)
    v1, i1 = first_max(e_vals)
    v2, i2 = first_max(jnp.where(lane == i1, low, e_vals))
    e21 = jnp.exp(v2 - v1)
    w1 = g_prob / (1.0 + e21)
    w2 = g_prob * e21 / (1.0 + e21)
    out = jnp.where(lane == 0, i1 - N_GROUPS,
                    jnp.where(lane == 1, i2 - N_GROUPS,
                              jnp.where(lane == 2, w1, jnp.where(lane == 3, w2, 0.0))))
    o_ref[...] = out


def _router(x, g, w, b):
    tm = ROUTER_TM
    return pl.pallas_call(
        _router_kernel,
        out_shape=jax.ShapeDtypeStruct((N_TOK, LANES), F32),
        grid=(N_TOK // tm,),
        in_specs=[
            pl.BlockSpec((tm, D_MODEL), lambda i: (i, 0)),
            pl.BlockSpec((1, D_MODEL), lambda i: (0, 0)),
            pl.BlockSpec((D_MODEL, LANES), lambda i: (0, 0)),
            pl.BlockSpec((1, LANES), lambda i: (0, 0)),
        ],
        out_specs=pl.BlockSpec((tm, LANES), lambda i: (i, 0)),
        compiler_params=_params("parallel"),
        name="router",
    )(x, g.reshape(1, D_MODEL), w, b)


def _dispatch_plan(eid):
    tok = jnp.arange(N_TOK, dtype=jnp.int32)
    keys = jnp.concatenate([eid[:, k] * N_ASSIGN + (k * N_TOK + tok) for k in range(TOP_K)])
    row_asg = jnp.sort(keys) & (N_ASSIGN - 1)
    row_asg = jnp.concatenate([row_asg, jnp.zeros((SUBLANES,), jnp.int32)])
    counts = jnp.sum(eid.reshape(-1)[:, None] == jnp.arange(N_EXPERTS, dtype=jnp.int32)[None, :], axis=0,
                     dtype=jnp.int32)
    starts = jnp.cumsum(counts) - counts
    nblk_e = (counts + MOE_BLOCK - 1) // MOE_BLOCK
    blk_end = jnp.cumsum(nblk_e)
    blk_start = blk_end - nblk_e
    n_active = blk_end[-1]
    bidx = jnp.arange(MOE_NBLK, dtype=jnp.int32)
    blk_x = jnp.minimum(bidx, n_active - 1)
    blk_e = jnp.minimum(jnp.sum(blk_end[None, :] <= blk_x[:, None], axis=1, dtype=jnp.int32), N_EXPERTS - 1)
    first = (blk_x - blk_start[blk_e]) * MOE_BLOCK
    blk_rows = jnp.where(bidx < n_active, jnp.clip(counts[blk_e] - first, 0, MOE_BLOCK), 0).astype(jnp.int32)
    blk_src = (starts[blk_e] + first).astype(jnp.int32)
    return row_asg, blk_e, blk_rows, blk_src


_ROW_BITS = tuple(1 << k for k in range(MOE_BLOCK.bit_length() - 1, -1, -1))


def _expert_kernel(row_asg, blk_e, blk_rows, blk_src, x_hbm, g_ref, w1_ref, w3_ref, w2_ref, ys_hbm,
                   xbuf, acc, xn_ref, gsem, ssem):
    b = pl.program_id(0)
    j = pl.program_id(1)
    rows = blk_rows[b]
    slot = b % 2

    def n_fetch(blk):
        return (blk_rows[blk] + SUBLANES - 1) & -SUBLANES

    def issue_gather(blk, s):
        base = blk_src[blk]

        def group(q, c):
            for u in range(SUBLANES):
                t = row_asg[base + q * SUBLANES + u] & (N_TOK - 1)
                pltpu.make_async_copy(x_hbm.at[t >> SUBLANE_SHIFT, pl.ds(t & (SUBLANES - 1), 1), :],
                                      xbuf.at[s, q, pl.ds(u, 1), :], gsem.at[s]).start()
            return c

        lax.fori_loop(0, n_fetch(blk) >> SUBLANE_SHIFT, group, 0)

    def wait_rows(n, copy_of):
        for bit in _ROW_BITS:
            @pl.when((n & bit) != 0)
            def _():
                copy_of(bit).wait()

    def gather_block(s):
        return lambda k: pltpu.make_async_copy(x_hbm.at[pl.ds(0, k // SUBLANES)],
                                               xbuf.at[s, pl.ds(0, k // SUBLANES)], gsem.at[s])

    def scatter_block(s):
        def copy_of(k):
            if k >= SUBLANES:
                return pltpu.make_async_copy(acc.at[s, pl.ds(0, k // SUBLANES)],
                                             ys_hbm.at[pl.ds(0, k // SUBLANES)], ssem.at[s])
            return pltpu.make_async_copy(acc.at[s, 0, pl.ds(0, k), :], ys_hbm.at[0, pl.ds(0, k), :], ssem.at[s])
        return copy_of

    def issue_scatter(blk, s, n):
        base = blk_src[blk]

        def one(q, u, a):
            pltpu.make_async_copy(acc.at[s, q, pl.ds(u, 1), :],
                                  ys_hbm.at[a >> SUBLANE_SHIFT, pl.ds(a & (SUBLANES - 1), 1), :],
                                  ssem.at[s]).start()

        def group(q, c):
            for u in range(SUBLANES):
                one(q, u, row_asg[base + q * SUBLANES + u])
            return c

        def tail(i, c):
            one(i >> SUBLANE_SHIFT, i & (SUBLANES - 1), row_asg[base + i])
            return c

        full = n >> SUBLANE_SHIFT
        lax.fori_loop(0, full, group, 0)
        lax.fori_loop(full * SUBLANES, n, tail, 0)

    def for_row_tile(fn):
        lo = 0
        for m in MOE_ROW_TILES:
            @pl.when(jnp.logical_and(rows > lo, rows <= m))
            def _():
                fn(m)
            lo = m

    nxt = jnp.minimum(b + 1, MOE_NBLK - 1)
    has_next = jnp.logical_and(b + 1 < MOE_NBLK, blk_rows[nxt] > 0)

    @pl.when(rows > 0)
    def _():
        @pl.when(j == 0)
        def _():
            @pl.when(b == 0)
            def _():
                xbuf[...] = jnp.zeros(xbuf.shape, F32)
                acc[...] = jnp.zeros(acc.shape, F32)
                issue_gather(0, 0)

            wait_rows(n_fetch(b), gather_block(slot))

            @pl.when(has_next)
            def _():
                issue_gather(nxt, 1 - slot)

            def norm_tile(m):
                xt = xbuf[slot, :m // SUBLANES].reshape(m, D_MODEL)
                xn_ref[:m, :] = _rms(xt, g_ref[...]).astype(BF16)

            for_row_tile(norm_tile)

        def ffn_tile(m):
            xs = xn_ref[:m, :]
            h1 = jnp.dot(xs, w1_ref[...].astype(BF16), preferred_element_type=F32)
            h3 = jnp.dot(xs, w3_ref[...].astype(BF16), preferred_element_type=F32)
            h = (jax.nn.silu(h1) * h3).astype(BF16)
            y = jnp.dot(h, w2_ref[...].astype(BF16), preferred_element_type=F32)
            mt = m // SUBLANES
            acc[slot, :mt] = jnp.where(j > 0, acc[slot, :mt], 0.0) + y.reshape(mt, SUBLANES, D_MODEL)

        for_row_tile(ffn_tile)

        @pl.when(j == N_FT - 1)
        def _():
            @pl.when(b > 0)
            def _():
                wait_rows(blk_rows[jnp.maximum(b - 1, 0)], scatter_block(1 - slot))

            issue_scatter(b, slot, rows)

            @pl.when(jnp.logical_not(has_next))
            def _():
                wait_rows(rows, scatter_block(slot))


def _moe_experts(x, g, w1, w3, w2, layer, row_asg, blk_e, blk_rows, blk_src):
    def w_col(b, j, ra, be, br, bs):
        return (layer, be[b], 0, jnp.where(br[b] > 0, j, N_FT - 1))

    def w_row(b, j, ra, be, br, bs):
        return (layer, be[b], jnp.where(br[b] > 0, j, N_FT - 1), 0)

    return pl.pallas_call(
        _expert_kernel,
        out_shape=jax.ShapeDtypeStruct((N_ASSIGN // SUBLANES, SUBLANES, D_MODEL), F32),
        grid_spec=pltpu.PrefetchScalarGridSpec(
            num_scalar_prefetch=4,
            grid=(MOE_NBLK, N_FT),
            in_specs=[
                pl.BlockSpec(memory_space=pl.ANY),
                pl.BlockSpec((1, D_MODEL), lambda b, j, ra, be, br, bs: (0, 0)),
                pl.BlockSpec((None, None, D_MODEL, MOE_FT), w_col),
                pl.BlockSpec((None, None, D_MODEL, MOE_FT), w_col),
                pl.BlockSpec((None, None, MOE_FT, D_MODEL), w_row),
            ],
            out_specs=pl.BlockSpec(memory_space=pl.ANY),
            scratch_shapes=[
                pltpu.VMEM((2, MOE_BLOCK // SUBLANES, SUBLANES, D_MODEL), F32),
                pltpu.VMEM((2, MOE_BLOCK // SUBLANES, SUBLANES, D_MODEL), F32),
                pltpu.VMEM((MOE_BLOCK, D_MODEL), BF16),
                pltpu.SemaphoreType.DMA((2,)),
                pltpu.SemaphoreType.DMA((2,)),
            ],
        ),
        compiler_params=_params("arbitrary", "arbitrary"),
        name="moe_experts",
    )(row_asg, blk_e, blk_rows, blk_src, x.reshape(N_TOK // SUBLANES, SUBLANES, D_MODEL), g.reshape(1, D_MODEL),
      w1, w3, w2).reshape(N_ASSIGN, D_MODEL)


def _combine_kernel(final_norm, x_ref, y0_ref, y1_ref, gate_ref, gf_ref, o_ref):
    gates = gate_ref[...]
    out = x_ref[...] + (gates[:, 2:3] * y0_ref[...] + gates[:, 3:4] * y1_ref[...])
    if final_norm:
        out = _rms(out, gf_ref[...])
    o_ref[...] = out


def _moe_combine(x, ys, route, g_final, final_norm):
    tm = COMBINE_TOK
    return pl.pallas_call(
        functools.partial(_combine_kernel, final_norm),
        out_shape=jax.ShapeDtypeStruct((N_TOK, D_MODEL), F32),
        grid=(N_TOK // tm,),
        in_specs=[
            pl.BlockSpec((tm, D_MODEL), lambda i: (i, 0)),
            pl.BlockSpec((tm, D_MODEL), lambda i: (i, 0)),
            pl.BlockSpec((tm, D_MODEL), lambda i: (N_TOK // tm + i, 0)),
            pl.BlockSpec((tm, LANES), lambda i: (i, 0)),
            pl.BlockSpec((1, D_MODEL), lambda i: (0, 0)),
        ],
        out_specs=pl.BlockSpec((tm, D_MODEL), lambda i: (i, 0)),
        compiler_params=_params("parallel"),
        name="moe_combine",
    )(x, ys, ys, route, g_final.reshape(1, D_MODEL))


def _hier_moe(x, g, wg, bg, we, be, w1, w3, w2, layer, g_final, final_norm):
    pad = LANES - N_GROUPS - N_EXPERTS
    w_r = jnp.concatenate([wg, we, jnp.zeros((D_MODEL, pad), F32)], axis=1)
    b_r = jnp.concatenate([bg, be, jnp.zeros((pad,), F32)]).reshape(1, LANES)
    route = _router(x, g, w_r, b_r)
    eid = route[:, :TOP_K].astype(jnp.int32)
    row_asg, blk_e, blk_rows, blk_src = _dispatch_plan(eid)
    ys = _moe_experts(x, g, w1, w3, w2, layer, row_asg, blk_e, blk_rows, blk_src)
    return _moe_combine(x, ys, route, g_final, final_norm)


def kernel(x, norm_mix, norm_ffn, norm_final, w_in_ab, na_rpb, conv_w, conv_b, w_out_ab, w_in_cd,
           q_norm, kv_norm, w_uq, w_ukv, sg_norm, sg_w, sg_b, w_out_cd, router_group_w,
           router_group_b, router_expert_w, router_expert_b, w1, w3, w2):
    xt = x.reshape(N_TOK, D_MODEL)
    for layer in range(DEPTH):
        i = layer // 2
        if layer % 2 == 0:
            p = _norm_matmul(xt, norm_mix[layer], w_in_ab[i].astype(BF16), tm=1024, tn=512)
            a_out = _natten(p, _natten_bias_table(na_rpb[i]))
            b_out = _gated_conv(p, conv_w[i], conv_b[i])
            xt = _out_proj(a_out, b_out, w_out_ab[i].astype(BF16), xt)
        else:
            w = w_in_cd[i]
            c0, c1, c2 = Q_LORA + KV_LORA, Q_LORA + KV_LORA + QK_ROPE, CD_IN_PAD - KR_COL - QK_ROPE
            w = jnp.concatenate([w[:, :c0], w[:, c1:], w[:, c0:c1], jnp.zeros((D_MODEL, c2), F32)], axis=1)
            p = _norm_matmul(xt, norm_mix[layer], w.astype(BF16), tm=1024, tn=640)
            wq = w_uq[i].reshape(Q_LORA, MLA_HEADS, QK_DIM)
            wq = jnp.concatenate([wq[:, :, :QK_NOPE].reshape(Q_LORA, NOPE_ALL),
                                  wq[:, :, QK_NOPE:].reshape(Q_LORA, ROPE_ALL)], axis=1).astype(BF16)
            q, k, v = _mla_prep(p, q_norm[i], kv_norm[i], wq, w_ukv[i].astype(BF16))
            c_out = _mla_attn(q, k, v)
            d_out = _spatial_gating(p, sg_norm[i], sg_w[i], sg_b[i])
            xt = _out_proj(c_out, d_out, w_out_cd[i].astype(BF16), xt)
        xt = _hier_moe(xt, norm_ffn[layer], router_group_w[layer], router_group_b[layer],
                       router_expert_w[layer], router_expert_b[layer], w1, w3, w2, layer,
                       norm_final, final_norm=(layer == DEPTH - 1))
    return xt.reshape(BATCH, SEQ, D_MODEL)
```

```python
import functools

import numpy as np
import jax
import jax.numpy as jnp
from jax import lax
from jax.experimental import pallas as pl
from jax.experimental.pallas import tpu as pltpu

F32 = jnp.float32
BF16 = jnp.bfloat16

D_MODEL = 2048
BATCH = 2
SEQ = 4096
N_TOK = BATCH * SEQ
DEPTH = 2
GRID_W = 64
GRID_ROWS = SEQ // GRID_W
WIN_ROWS = 8
WIN_COLS = 16
NA_HEADS = 8
NA_HEAD_DIM = 128
NA_WIDTH = NA_HEADS * NA_HEAD_DIM
NA_KEYS = WIN_ROWS * GRID_W
CONV_CH = D_MODEL // 2
MLA_HEADS = 8
Q_LORA = 512
KV_LORA = 512
QK_NOPE = 128
QK_ROPE = 64
QK_DIM = QK_NOPE + QK_ROPE
V_DIM = 128
ROPE_THETA = 10000.0
SG_CH = D_MODEL // 2
SG_GROUPS = 8
CHUNK = 128
N_GROUPS = 8
EXPERTS_PER_GROUP = 8
N_EXPERTS = N_GROUPS * EXPERTS_PER_GROUP
TOP_K = 2
D_EXPERT = 768
EPS = 1e-6
NEG_INF = -1e30
LOG2_E = 1.4426950408889634
AB_IN = 3 * NA_WIDTH + 3 * CONV_CH
CD_IN_PAD = 3200
KR_COL = Q_LORA + KV_LORA + 2 * SG_CH

LANES = 128
SUBLANES = 8
SUBLANE_SHIFT = 3
MOE_BLOCK = 512
MOE_ROW_TILES = (256, 320, 512)
MOE_FT = 256
N_FT = D_EXPERT // MOE_FT
N_ASSIGN = N_TOK * TOP_K
MOE_NBLK = N_ASSIGN // MOE_BLOCK + N_EXPERTS
COMBINE_TOK = 256
VMEM_LIMIT = 52 * 1024 * 1024


def _params(*sem):
    return pltpu.CompilerParams(dimension_semantics=sem, vmem_limit_bytes=VMEM_LIMIT)


def _rms(x, g):
    return x * lax.rsqrt(jnp.mean(x * x, axis=-1, keepdims=True) + EPS) * g


def _norm_mm_kernel(x_ref, g_ref, w_ref, o_ref, xn_ref):
    @pl.when(pl.program_id(1) == 0)
    def _():
        xn_ref[...] = _rms(x_ref[...].astype(F32), g_ref[...]).astype(BF16)

    o_ref[...] = jnp.dot(xn_ref[...], w_ref[...], preferred_element_type=F32).astype(o_ref.dtype)


def _norm_matmul(x, g, w, tm, tn):
    m, k = x.shape
    nout = w.shape[1]
    return pl.pallas_call(
        _norm_mm_kernel,
        out_shape=jax.ShapeDtypeStruct((m, nout), BF16),
        grid=(m // tm, nout // tn),
        in_specs=[
            pl.BlockSpec((tm, k), lambda i, j: (i, 0)),
            pl.BlockSpec((1, k), lambda i, j: (0, 0)),
            pl.BlockSpec((k, tn), lambda i, j: (0, j)),
        ],
        out_specs=pl.BlockSpec((tm, tn), lambda i, j: (i, j)),
        scratch_shapes=[pltpu.VMEM((tm, k), BF16)],
        compiler_params=_params("parallel", "arbitrary"),
        name="norm_matmul",
    )(x, g.reshape(1, k), w)


def _natten_bias_table(rpb):
    c = np.arange(GRID_W)
    col_start = np.clip(c - WIN_COLS // 2, 0, GRID_W - WIN_COLS)
    valid = (c[None, :] >= col_start[:, None]) & (c[None, :] < col_start[:, None] + WIN_COLS)
    dc = np.clip(c[None, :] - c[:, None] + WIN_COLS - 1, 0, 2 * WIN_COLS - 2)
    pick = (dc[:, :, None] == np.arange(2 * WIN_COLS - 1)).astype(np.float32)
    m = jnp.einsum('hrd,ckd->hcrk', rpb, jnp.asarray(pick), precision=lax.Precision.HIGHEST)
    m = jnp.where(valid[None, :, None, :], m, NEG_INF)
    t = jnp.stack([m[:, :, o:o + WIN_ROWS] for o in range(WIN_ROWS)], axis=1)
    return t.reshape(rpb.shape[0], WIN_ROWS, GRID_W, NA_KEYS).astype(F32)


NA_ROWS_PER_STEP = 4


def _natten_kernel(q_ref, k_ref, v_ref, t_ref, o_ref):
    scale = NA_HEAD_DIM ** -0.5

    def rows(i, carry):
        rs = [i * NA_ROWS_PER_STEP + u for u in range(NA_ROWS_PER_STEP)]
        kr0s = [jnp.clip(r - WIN_ROWS // 2, 0, GRID_ROWS - WIN_ROWS) for r in rs]
        q0s = [pl.multiple_of(r * GRID_W, GRID_W) for r in rs]
        k0s = [pl.multiple_of(kr0 * GRID_W, GRID_W) for kr0 in kr0s]
        ss = [lax.dot_general(q_ref[pl.ds(q0, GRID_W), :], k_ref[pl.ds(k0, NA_KEYS), :],
                              (((1,), (1,)), ((), ())), preferred_element_type=F32)
              for q0, k0 in zip(q0s, k0s)]
        ps, ls = [], []
        for s, r, kr0 in zip(ss, rs, kr0s):
            s = s * scale + t_ref[kr0 - r + WIN_ROWS - 1]
            p = jnp.exp(s - jnp.max(s, axis=-1, keepdims=True))
            ls.append(jnp.sum(p, axis=-1, keepdims=True))
            ps.append(p.astype(BF16))
        for p, l, q0, k0 in zip(ps, ls, q0s, k0s):
            o = jnp.dot(p, v_ref[pl.ds(k0, NA_KEYS), :], preferred_element_type=F32) / l
            o_ref[pl.ds(q0, GRID_W), :] = o.astype(o_ref.dtype)
        return carry

    lax.fori_loop(0, GRID_ROWS // NA_ROWS_PER_STEP, rows, 0)


def _natten(p, table):
    blk = (SEQ, NA_HEAD_DIM)
    return pl.pallas_call(
        _natten_kernel,
        out_shape=jax.ShapeDtypeStruct((N_TOK, NA_WIDTH), BF16),
        grid=(BATCH, NA_HEADS),
        in_specs=[
            pl.BlockSpec(blk, lambda b, h: (b, h)),
            pl.BlockSpec(blk, lambda b, h: (b, NA_HEADS + h)),
            pl.BlockSpec(blk, lambda b, h: (b, 2 * NA_HEADS + h)),
            pl.BlockSpec((None, WIN_ROWS, GRID_W, NA_KEYS), lambda b, h: (h, 0, 0, 0)),
        ],
        out_specs=pl.BlockSpec(blk, lambda b, h: (b, h)),
        compiler_params=_params("parallel", "parallel"),
        name="natten",
    )(p, p, p, table)


CONV_CB = 128


def _conv_kernel(gb_ref, gc_ref, hc_ref, w_ref, b_ref, o_ref):
    z = gc_ref[...].astype(F32) * hc_ref[...].astype(F32)
    pos = lax.broadcasted_iota(jnp.int32, z.shape, 0)
    z_prev = jnp.where(pos == 0, 0.0, pltpu.roll(z, 1, 0))
    z_next = jnp.where(pos == SEQ - 1, 0.0, pltpu.roll(z, SEQ - 1, 0))
    y = b_ref[...] + z_prev * w_ref[0:1, :] + z * w_ref[1:2, :] + z_next * w_ref[2:3, :]
    o_ref[...] = (gb_ref[...].astype(F32) * y).astype(o_ref.dtype)


def _gated_conv(p, w, b):
    base = 3 * NA_WIDTH // CONV_CB
    step = CONV_CH // CONV_CB
    blk = (SEQ, CONV_CB)
    return pl.pallas_call(
        _conv_kernel,
        out_shape=jax.ShapeDtypeStruct((N_TOK, CONV_CH), BF16),
        grid=(BATCH, step),
        in_specs=[
            pl.BlockSpec(blk, lambda bi, c: (bi, base + c)),
            pl.BlockSpec(blk, lambda bi, c: (bi, base + step + c)),
            pl.BlockSpec(blk, lambda bi, c: (bi, base + 2 * step + c)),
            pl.BlockSpec((3, CONV_CB), lambda bi, c: (0, c)),
            pl.BlockSpec((1, CONV_CB), lambda bi, c: (0, c)),
        ],
        out_specs=pl.BlockSpec(blk, lambda bi, c: (bi, c)),
        compiler_params=_params("parallel", "parallel"),
        name="gated_conv",
    )(p, p, p, w, b.reshape(1, CONV_CH))


def _out_proj_kernel(a_ref, b_ref, wa_ref, wb_ref, r_ref, o_ref):
    acc = jnp.dot(a_ref[...], wa_ref[...], preferred_element_type=F32)
    acc = acc + jnp.dot(b_ref[...], wb_ref[...], preferred_element_type=F32)
    o_ref[...] = r_ref[...] + acc


def _out_proj(a, b, w, res, tm=512):
    ka, kb = a.shape[1], b.shape[1]
    assert ka == kb
    return pl.pallas_call(
        _out_proj_kernel,
        out_shape=jax.ShapeDtypeStruct((N_TOK, D_MODEL), F32),
        grid=(N_TOK // tm,),
        in_specs=[
            pl.BlockSpec((tm, ka), lambda i: (i, 0)),
            pl.BlockSpec((tm, kb), lambda i: (i, 0)),
            pl.BlockSpec((ka, D_MODEL), lambda i: (0, 0)),
            pl.BlockSpec((kb, D_MODEL), lambda i: (1, 0)),
            pl.BlockSpec((tm, D_MODEL), lambda i: (i, 0)),
        ],
        out_specs=pl.BlockSpec((tm, D_MODEL), lambda i: (i, 0)),
        compiler_params=_params("parallel"),
        name="out_proj",
    )(a, b, w, w, res)


MLA_TM = 512


def _rope_tables():
    pos = jnp.arange(SEQ, dtype=jnp.int32)
    row = (pos // GRID_W).astype(F32)
    col = (pos % GRID_W).astype(F32)
    half = QK_ROPE // 2
    inv = ROPE_THETA ** (-jnp.arange(0, half, 2, dtype=F32) / half)
    ar, ac = row[:, None] * inv, col[:, None] * inv
    cos_t = jnp.concatenate([jnp.cos(ar), jnp.cos(ar), jnp.cos(ac), jnp.cos(ac)], axis=-1)
    sin_t = jnp.concatenate([-jnp.sin(ar), jnp.sin(ar), -jnp.sin(ac), jnp.sin(ac)], axis=-1)
    quarter = half // 2
    src = np.arange(QK_ROPE) + np.where((np.arange(QK_ROPE) // quarter) % 2 == 0, quarter, -quarter)
    swap = np.zeros((QK_ROPE, QK_ROPE), np.float32)
    swap[src, np.arange(QK_ROPE)] = 1.0
    swap_all = np.kron(np.eye(MLA_HEADS, dtype=np.float32), swap)
    return cos_t, sin_t, jnp.asarray(swap_all, BF16)


def _rope(x, cos_t, sin_t, swap):
    hi = x.astype(BF16)
    lo = (x - hi.astype(F32)).astype(BF16)
    xs = jnp.dot(hi, swap, preferred_element_type=F32) + jnp.dot(lo, swap, preferred_element_type=F32)
    return x * cos_t + xs * sin_t


ROPE_ALL = MLA_HEADS * QK_ROPE
NOPE_ALL = MLA_HEADS * QK_NOPE


def _mla_prep_kernel(cq_ref, ckv_ref, kr_ref, gq_ref, gkv_ref, wq_ref, wkv_ref, cos_ref, sin_ref,
                     swap_ref, q_ref, k_ref, v_ref):
    scale = QK_DIM ** -0.5 * LOG2_E
    cos_t, sin_t, swap = cos_ref[...], sin_ref[...], swap_ref[...]
    cqn = _rms(cq_ref[...].astype(F32), gq_ref[...]).astype(BF16)
    ckvn = _rms(ckv_ref[...].astype(F32), gkv_ref[...]).astype(BF16)
    kr = kr_ref[:, :QK_ROPE].astype(F32)
    krot = _rope(kr, cos_t, sin_t, swap[:QK_ROPE, :QK_ROPE]).astype(BF16)
    q = jnp.dot(cqn, wq_ref[...], preferred_element_type=F32)
    qn = (q[:, :NOPE_ALL] * scale).astype(BF16)
    cos_all = jnp.concatenate([cos_t] * MLA_HEADS, axis=1)
    sin_all = jnp.concatenate([sin_t] * MLA_HEADS, axis=1)
    qr = (_rope(q[:, NOPE_ALL:], cos_all, sin_all, swap) * scale).astype(BF16)
    kv = jnp.dot(ckvn, wkv_ref[...], preferred_element_type=F32)
    hw = QK_NOPE + V_DIM
    for h in range(MLA_HEADS):
        q_ref[h, :, :QK_NOPE] = qn[:, h * QK_NOPE:(h + 1) * QK_NOPE]
        q_ref[h, :, QK_NOPE:] = qr[:, h * QK_ROPE:(h + 1) * QK_ROPE]
        k_ref[h, :, :QK_NOPE] = kv[:, h * hw:h * hw + QK_NOPE].astype(BF16)
        k_ref[h, :, QK_NOPE:] = krot
        v_ref[h, 0] = kv[:, h * hw + QK_NOPE:(h + 1) * hw].T.astype(BF16)


def _mla_prep(p, q_norm, kv_norm, wq, wkv):
    cos_t, sin_t, swap = _rope_tables()
    tm = MLA_TM
    seq_blocks = SEQ // tm
    return pl.pallas_call(
        _mla_prep_kernel,
        out_shape=(
            jax.ShapeDtypeStruct((MLA_HEADS, N_TOK, QK_DIM), BF16),
            jax.ShapeDtypeStruct((MLA_HEADS, N_TOK, QK_DIM), BF16),
            jax.ShapeDtypeStruct((MLA_HEADS, N_TOK // tm, V_DIM, tm), BF16),
        ),
        grid=(N_TOK // tm,),
        in_specs=[
            pl.BlockSpec((tm, Q_LORA), lambda i: (i, 0)),
            pl.BlockSpec((tm, KV_LORA), lambda i: (i, 1)),
            pl.BlockSpec((tm, LANES), lambda i: (i, KR_COL // LANES)),
            pl.BlockSpec((1, Q_LORA), lambda i: (0, 0)),
            pl.BlockSpec((1, KV_LORA), lambda i: (0, 0)),
            pl.BlockSpec((Q_LORA, NOPE_ALL + ROPE_ALL), lambda i: (0, 0)),
            pl.BlockSpec((KV_LORA, MLA_HEADS * (QK_NOPE + V_DIM)), lambda i: (0, 0)),
            pl.BlockSpec((tm, QK_ROPE), lambda i: (i % seq_blocks, 0)),
            pl.BlockSpec((tm, QK_ROPE), lambda i: (i % seq_blocks, 0)),
            pl.BlockSpec((ROPE_ALL, ROPE_ALL), lambda i: (0, 0)),
        ],
        out_specs=(
            pl.BlockSpec((MLA_HEADS, tm, QK_DIM), lambda i: (0, i, 0)),
            pl.BlockSpec((MLA_HEADS, tm, QK_DIM), lambda i: (0, i, 0)),
            pl.BlockSpec((MLA_HEADS, 1, V_DIM, tm), lambda i: (0, i, 0, 0)),
        ),
        compiler_params=_params("parallel"),
        name="mla_prep",
    )(p, p, p, q_norm.reshape(1, Q_LORA), kv_norm.reshape(1, KV_LORA), wq, wkv, cos_t, sin_t, swap)


MLA_TQ = 4096
MLA_TK = 512
MLA_QH = 16
assert MLA_TK == MLA_TM


def _mla_attn_kernel(q_ref, k_ref, vt_ref, o_ref, s_ref, m_ref, l_ref, acc_ref):
    n_chunks = SEQ // MLA_TK
    cols = MLA_TQ // MLA_QH

    def scores(h, c):
        k0 = pl.multiple_of(c * MLA_TK, MLA_TK)
        return lax.dot_general(k_ref[pl.ds(k0, MLA_TK), :], q_ref[h * cols:(h + 1) * cols, :],
                               (((1,), (1,)), ((), ())), preferred_element_type=F32)

    m_ref[...] = jnp.full(m_ref.shape, NEG_INF, F32)
    l_ref[...] = jnp.zeros(l_ref.shape, F32)
    acc_ref[...] = jnp.zeros(acc_ref.shape, F32)
    for h in range(MLA_QH):
        s_ref[0, h] = scores(h, 0)

    def step(c, slot):
        c_next = jnp.where(c + 1 < n_chunks, c + 1, 0)
        for h in range(MLA_QH):
            s_ref[1 - slot, h] = scores(h, c_next)
            s = s_ref[slot, h]
            m_old = m_ref[h]
            m_new = jnp.maximum(m_old, jnp.max(s, axis=0, keepdims=True))
            p = jnp.exp2(s - m_new)
            alpha = jnp.exp2(m_old - m_new)
            l_ref[h] = alpha * l_ref[h] + jnp.sum(p, axis=0, keepdims=True)
            pv = jnp.dot(vt_ref[c], p.astype(BF16), preferred_element_type=F32)
            acc_ref[h] = alpha * acc_ref[h] + pv
            m_ref[h] = m_new

    def pair(i, carry):
        step(2 * i, 0)
        step(2 * i + 1, 1)
        return carry

    lax.fori_loop(0, n_chunks // 2, pair, 0)
    for h in range(MLA_QH):
        o_ref[h * cols:(h + 1) * cols, :] = (acc_ref[h] / l_ref[h]).T.astype(o_ref.dtype)


def _mla_attn(q, k, v):
    nq = SEQ // MLA_TQ
    return pl.pallas_call(
        _mla_attn_kernel,
        out_shape=jax.ShapeDtypeStruct((N_TOK, MLA_HEADS * V_DIM), BF16),
        grid=(BATCH, MLA_HEADS, nq),
        in_specs=[
            pl.BlockSpec((None, MLA_TQ, QK_DIM), lambda b, h, i: (h, b * nq + i, 0)),
            pl.BlockSpec((None, SEQ, QK_DIM), lambda b, h, i: (h, b, 0)),
            pl.BlockSpec((None, SEQ // MLA_TK, V_DIM, MLA_TK), lambda b, h, i: (h, b, 0, 0)),
        ],
        out_specs=pl.BlockSpec((MLA_TQ, V_DIM), lambda b, h, i: (b * nq + i, h)),
        scratch_shapes=[
            pltpu.VMEM((2, MLA_QH, MLA_TK, MLA_TQ // MLA_QH), F32),
            pltpu.VMEM((MLA_QH, 1, MLA_TQ // MLA_QH), F32),
            pltpu.VMEM((MLA_QH, 1, MLA_TQ // MLA_QH), F32),
            pltpu.VMEM((MLA_QH, V_DIM, MLA_TQ // MLA_QH), F32),
        ],
        compiler_params=_params("parallel", "parallel", "parallel"),
        name="mla_attn",
    )(q, k, v)


SG_TM = 256


def _sg_kernel(u_ref, v_ref, g_ref, w_ref, bt_ref, o_ref):
    v = jax.nn.gelu(v_ref[...].astype(F32))
    vn = _rms(v, g_ref[...]).astype(BF16)
    u = jax.nn.gelu(u_ref[...].astype(F32))
    gc = SG_CH // SG_GROUPS
    for n in range(SG_TM // CHUNK):
        rows = slice(n * CHUNK, (n + 1) * CHUNK)
        for g in range(SG_GROUPS):
            cols = slice(g * gc, (g + 1) * gc)
            mixed = jnp.dot(w_ref[g], vn[rows, cols], preferred_element_type=F32) + bt_ref[:, g:g + 1]
            o_ref[rows, cols] = (u[rows, cols] * mixed).astype(o_ref.dtype)


def _spatial_gating(p, g_norm, w_s, b_s):
    u_blk = (Q_LORA + KV_LORA) // SG_CH
    return pl.pallas_call(
        _sg_kernel,
        out_shape=jax.ShapeDtypeStruct((N_TOK, SG_CH), BF16),
        grid=(N_TOK // SG_TM,),
        in_specs=[
            pl.BlockSpec((SG_TM, SG_CH), lambda i: (i, u_blk)),
            pl.BlockSpec((SG_TM, SG_CH), lambda i: (i, u_blk + 1)),
            pl.BlockSpec((1, SG_CH), lambda i: (0, 0)),
            pl.BlockSpec((SG_GROUPS, CHUNK, CHUNK), lambda i: (0, 0, 0)),
            pl.BlockSpec((CHUNK, SG_GROUPS), lambda i: (0, 0)),
        ],
        out_specs=pl.BlockSpec((SG_TM, SG_CH), lambda i: (i, 0)),
        compiler_params=_params("parallel"),
        name="spatial_gating",
    )(p, p, g_norm.reshape(1, SG_CH), w_s.astype(BF16), b_s.T)


ROUTER_TM = 512


def _router_kernel(x_ref, g_ref, w_ref, b_ref, o_ref):
    xn = _rms(x_ref[...], g_ref[...])
    w = w_ref[...]
    xh, wh = xn.astype(BF16), w.astype(BF16)
    xl, wl = (xn - xh.astype(F32)).astype(BF16), (w - wh.astype(F32)).astype(BF16)
    logits = (jnp.dot(xh, wh, preferred_element_type=F32)
              + (jnp.dot(xh, wl, preferred_element_type=F32) + jnp.dot(xl, wh, preferred_element_type=F32))
              + b_ref[...])
    lane = lax.broadcasted_iota(jnp.int32, logits.shape, 1).astype(F32)
    low = jnp.float32(-3.0e38)

    def first_max(vals):
        top = jnp.max(vals, axis=-1, keepdims=True)
        idx = jnp.min(jnp.where(vals == top, lane, float(LANES)), axis=-1, keepdims=True)
        return top, idx

    is_group = lane < N_GROUPS
    g_top, g_idx = first_max(jnp.where(is_group, logits, low))
    g_prob = 1.0 / jnp.sum(jnp.where(is_group, jnp.exp(logits - g_top), 0.0), axis=-1, keepdims=True)
    lo = N_GROUPS + g_idx * EXPERTS_PER_GROUP
    e_vals = jnp.where(lane >= lo, jnp.where(lane < lo + EXPERTS_PER_GROUP, logits, low), low)
    v1, i1 = first_max(e_vals)
    v2, i2 = first_max(jnp.where(lane == i1, low, e_vals))
    e21 = jnp.exp(v2 - v1)
    w1 = g_prob / (1.0 + e21)
    w2 = g_prob * e21 / (1.0 + e21)
    out = jnp.where(lane == 0, i1 - N_GROUPS,
                    jnp.where(lane == 1, i2 - N_GROUPS,
                              jnp.where(lane == 2, w1, jnp.where(lane == 3, w2, 0.0))))
    o_ref[...] = out


def _router(x, g, w, b):
    tm = ROUTER_TM
    return pl.pallas_call(
        _router_kernel,
        out_shape=jax.ShapeDtypeStruct((N_TOK, LANES), F32),
        grid=(N_TOK // tm,),
        in_specs=[
            pl.BlockSpec((tm, D_MODEL), lambda i: (i, 0)),
            pl.BlockSpec((1, D_MODEL), lambda i: (0, 0)),
            pl.BlockSpec((D_MODEL, LANES), lambda i: (0, 0)),
            pl.BlockSpec((1, LANES), lambda i: (0, 0)),
        ],
        out_specs=pl.BlockSpec((tm, LANES), lambda i: (i, 0)),
        compiler_params=_params("parallel"),
        name="router",
    )(x, g.reshape(1, D_MODEL), w, b)


def _dispatch_plan(eid):
    tok = jnp.arange(N_TOK, dtype=jnp.int32)
    keys = jnp.concatenate([eid[:, k] * N_ASSIGN + (k * N_TOK + tok) for k in range(TOP_K)])
    row_asg = jnp.sort(keys) & (N_ASSIGN - 1)
    row_asg = jnp.concatenate([row_asg, jnp.zeros((SUBLANES,), jnp.int32)])
    counts = jnp.sum(eid.reshape(-1)[:, None] == jnp.arange(N_EXPERTS, dtype=jnp.int32)[None, :], axis=0,
                     dtype=jnp.int32)
    starts = jnp.cumsum(counts) - counts
    nblk_e = (counts + MOE_BLOCK - 1) // MOE_BLOCK
    blk_end = jnp.cumsum(nblk_e)
    blk_start = blk_end - nblk_e
    n_active = blk_end[-1]
    bidx = jnp.arange(MOE_NBLK, dtype=jnp.int32)
    blk_x = jnp.minimum(bidx, n_active - 1)
    blk_e = jnp.minimum(jnp.sum(blk_end[None, :] <= blk_x[:, None], axis=1, dtype=jnp.int32), N_EXPERTS - 1)
    first = (blk_x - blk_start[blk_e]) * MOE_BLOCK
    blk_rows = jnp.where(bidx < n_active, jnp.clip(counts[blk_e] - first, 0, MOE_BLOCK), 0).astype(jnp.int32)
    blk_src = (starts[blk_e] + first).astype(jnp.int32)
    return row_asg, blk_e, blk_rows, blk_src


_ROW_BITS = tuple(1 << k for k in range(MOE_BLOCK.bit_length() - 1, -1, -1))


def _expert_kernel(row_asg, blk_e, blk_rows, blk_src, x_hbm, g_ref, w1_ref, w3_ref, w2_ref, ys_hbm,
                   xbuf, acc, xn_ref, gsem, ssem):
    b = pl.program_id(0)
    j = pl.program_id(1)
    rows = blk_rows[b]
    slot = b % 2

    def n_fetch(blk):
        return (blk_rows[blk] + SUBLANES - 1) & -SUBLANES

    def issue_gather(blk, s):
        base = blk_src[blk]

        def group(q, c):
            for u in range(SUBLANES):
                t = row_asg[base + q * SUBLANES + u] & (N_TOK - 1)
                pltpu.make_async_copy(x_hbm.at[pl.ds(t, 1), :], xbuf.at[s, q, pl.ds(u, 1), :], gsem.at[s]).start()
            return c

        lax.fori_loop(0, n_fetch(blk) >> SUBLANE_SHIFT, group, 0)

    def wait_rows(n, copy_of):
        for bit in _ROW_BITS:
            @pl.when((n & bit) != 0)
            def _():
                copy_of(bit).wait()

    def gather_block(s):
        def copy_of(k):
            tiles = xbuf.at[s, pl.ds(0, k // SUBLANES)]
            return pltpu.make_async_copy(tiles, tiles, gsem.at[s])
        return copy_of

    def scatter_block(s):
        def copy_of(k):
            part = acc.at[s, pl.ds(0, k // SUBLANES)] if k >= SUBLANES else acc.at[s, 0, pl.ds(0, k), :]
            return pltpu.make_async_copy(part, part, ssem.at[s])
        return copy_of

    def issue_scatter(blk, s, n):
        base = blk_src[blk]

        def one(q, u, a):
            pltpu.make_async_copy(acc.at[s, q, pl.ds(u, 1), :], ys_hbm.at[pl.ds(a, 1), :], ssem.at[s]).start()

        def group(q, c):
            for u in range(SUBLANES):
                one(q, u, row_asg[base + q * SUBLANES + u])
            return c

        def tail(i, c):
            one(i >> SUBLANE_SHIFT, i & (SUBLANES - 1), row_asg[base + i])
            return c

        full = n >> SUBLANE_SHIFT
        lax.fori_loop(0, full, group, 0)
        lax.fori_loop(full * SUBLANES, n, tail, 0)

    def for_row_tile(fn):
        lo = 0
        for m in MOE_ROW_TILES:
            @pl.when(jnp.logical_and(rows > lo, rows <= m))
            def _():
                fn(m)
            lo = m

    nxt = jnp.minimum(b + 1, MOE_NBLK - 1)
    has_next = jnp.logical_and(b + 1 < MOE_NBLK, blk_rows[nxt] > 0)

    @pl.when(rows > 0)
    def _():
        @pl.when(j == 0)
        def _():
            @pl.when(b == 0)
            def _():
                xbuf[...] = jnp.zeros(xbuf.shape, F32)
                acc[...] = jnp.zeros(acc.shape, F32)
                issue_gather(0, 0)

            wait_rows(n_fetch(b), gather_block(slot))

            @pl.when(has_next)
            def _():
                issue_gather(nxt, 1 - slot)

            def norm_tile(m):
                xt = xbuf[slot, :m // SUBLANES].reshape(m, D_MODEL)
                xn_ref[:m, :] = _rms(xt, g_ref[...]).astype(BF16)

            for_row_tile(norm_tile)

        def ffn_tile(m):
            xs = xn_ref[:m, :]
            h1 = jnp.dot(xs, w1_ref[...].astype(BF16), preferred_element_type=F32)
            h3 = jnp.dot(xs, w3_ref[...].astype(BF16), preferred_element_type=F32)
            h = (jax.nn.silu(h1) * h3).astype(BF16)
            y = jnp.dot(h, w2_ref[...].astype(BF16), preferred_element_type=F32)
            mt = m // SUBLANES
            acc[slot, :mt] = jnp.where(j > 0, acc[slot, :mt], 0.0) + y.reshape(mt, SUBLANES, D_MODEL)

        for_row_tile(ffn_tile)

        @pl.when(j == N_FT - 1)
        def _():
            @pl.when(b > 0)
            def _():
                wait_rows(blk_rows[jnp.maximum(b - 1, 0)], scatter_block(1 - slot))

            issue_scatter(b, slot, rows)

            @pl.when(jnp.logical_not(has_next))
            def _():
                wait_rows(rows, scatter_block(slot))


def _moe_experts(x, g, w1, w3, w2, layer, row_asg, blk_e, blk_rows, blk_src):
    def w_col(b, j, ra, be, br, bs):
        return (layer, be[b], 0, jnp.where(br[b] > 0, j, N_FT - 1))

    def w_row(b, j, ra, be, br, bs):
        return (layer, be[b], jnp.where(br[b] > 0, j, N_FT - 1), 0)

    return pl.pallas_call(
        _expert_kernel,
        out_shape=jax.ShapeDtypeStruct((N_ASSIGN, D_MODEL), F32),
        grid_spec=pltpu.PrefetchScalarGridSpec(
            num_scalar_prefetch=4,
            grid=(MOE_NBLK, N_FT),
            in_specs=[
                pl.BlockSpec(memory_space=pl.ANY),
                pl.BlockSpec((1, D_MODEL), lambda b, j, ra, be, br, bs: (0, 0)),
                pl.BlockSpec((None, None, D_MODEL, MOE_FT), w_col),
                pl.BlockSpec((None, None, D_MODEL, MOE_FT), w_col),
                pl.BlockSpec((None, None, MOE_FT, D_MODEL), w_row),
            ],
            out_specs=pl.BlockSpec(memory_space=pl.ANY),
            scratch_shapes=[
                pltpu.VMEM((2, MOE_BLOCK // SUBLANES, SUBLANES, D_MODEL), F32),
                pltpu.VMEM((2, MOE_BLOCK // SUBLANES, SUBLANES, D_MODEL), F32),
                pltpu.VMEM((MOE_BLOCK, D_MODEL), BF16),
                pltpu.SemaphoreType.DMA((2,)),
                pltpu.SemaphoreType.DMA((2,)),
            ],
        ),
        compiler_params=_params("arbitrary", "arbitrary"),
        name="moe_experts",
    )(row_asg, blk_e, blk_rows, blk_src, x, g.reshape(1, D_MODEL), w1, w3, w2)


def _combine_kernel(final_norm, x_ref, y0_ref, y1_ref, gate_ref, gf_ref, o_ref):
    gates = gate_ref[...]
    out = x_ref[...] + (gates[:, 2:3] * y0_ref[...] + gates[:, 3:4] * y1_ref[...])
    if final_norm:
        out = _rms(out, gf_ref[...])
    o_ref[...] = out


def _moe_combine(x, ys, route, g_final, final_norm):
    tm = COMBINE_TOK
    return pl.pallas_call(
        functools.partial(_combine_kernel, final_norm),
        out_shape=jax.ShapeDtypeStruct((N_TOK, D_MODEL), F32),
        grid=(N_TOK // tm,),
        in_specs=[
            pl.BlockSpec((tm, D_MODEL), lambda i: (i, 0)),
            pl.BlockSpec((tm, D_MODEL), lambda i: (i, 0)),
            pl.BlockSpec((tm, D_MODEL), lambda i: (N_TOK // tm + i, 0)),
            pl.BlockSpec((tm, LANES), lambda i: (i, 0)),
            pl.BlockSpec((1, D_MODEL), lambda i: (0, 0)),
        ],
        out_specs=pl.BlockSpec((tm, D_MODEL), lambda i: (i, 0)),
        compiler_params=_params("parallel"),
        name="moe_combine",
    )(x, ys, ys, route, g_final.reshape(1, D_MODEL))


def _hier_moe(x, g, wg, bg, we, be, w1, w3, w2, layer, g_final, final_norm):
    pad = LANES - N_GROUPS - N_EXPERTS
    w_r = jnp.concatenate([wg, we, jnp.zeros((D_MODEL, pad), F32)], axis=1)
    b_r = jnp.concatenate([bg, be, jnp.zeros((pad,), F32)]).reshape(1, LANES)
    route = _router(x, g, w_r, b_r)
    eid = route[:, :TOP_K].astype(jnp.int32)
    row_asg, blk_e, blk_rows, blk_src = _dispatch_plan(eid)
    ys = _moe_experts(x, g, w1, w3, w2, layer, row_asg, blk_e, blk_rows, blk_src)
    return _moe_combine(x, ys, route, g_final, final_norm)


def kernel(x, norm_mix, norm_ffn, norm_final, w_in_ab, na_rpb, conv_w, conv_b, w_out_ab, w_in_cd,
           q_norm, kv_norm, w_uq, w_ukv, sg_norm, sg_w, sg_b, w_out_cd, router_group_w,
           router_group_b, router_expert_w, router_expert_b, w1, w3, w2):
    xt = x.reshape(N_TOK, D_MODEL)
    for layer in range(DEPTH):
        i = layer // 2
        if layer % 2 == 0:
            p = _norm_matmul(xt, norm_mix[layer], w_in_ab[i].astype(BF16), tm=1024, tn=512)
            a_out = _natten(p, _natten_bias_table(na_rpb[i]))
            b_out = _gated_conv(p, conv_w[i], conv_b[i])
            xt = _out_proj(a_out, b_out, w_out_ab[i].astype(BF16), xt)
        else:
            w = w_in_cd[i]
            c0, c1, c2 = Q_LORA + KV_LORA, Q_LORA + KV_LORA + QK_ROPE, CD_IN_PAD - KR_COL - QK_ROPE
            w = jnp.concatenate([w[:, :c0], w[:, c1:], w[:, c0:c1], jnp.zeros((D_MODEL, c2), F32)], axis=1)
            p = _norm_matmul(xt, norm_mix[layer], w.astype(BF16), tm=1024, tn=640)
            wq = w_uq[i].reshape(Q_LORA, MLA_HEADS, QK_DIM)
            wq = jnp.concatenate([wq[:, :, :QK_NOPE].reshape(Q_LORA, NOPE_ALL),
                                  wq[:, :, QK_NOPE:].reshape(Q_LORA, ROPE_ALL)], axis=1).astype(BF16)
            q, k, v = _mla_prep(p, q_norm[i], kv_norm[i], wq, w_ukv[i].astype(BF16))
            c_out = _mla_attn(q, k, v)
            d_out = _spatial_gating(p, sg_norm[i], sg_w[i], sg_b[i])
            xt = _out_proj(c_out, d_out, w_out_cd[i].astype(BF16), xt)
        xt = _hier_moe(xt, norm_ffn[layer], router_group_w[layer], router_group_b[layer],
                       router_expert_w[layer], router_expert_b[layer], w1, w3, w2, layer,
                       norm_final, final_norm=(layer == DEPTH - 1))
    return xt.reshape(BATCH, SEQ, D_MODEL)
```

```python
import functools

import numpy as np
import jax
import jax.numpy as jnp
from jax import lax
from jax.experimental import pallas as pl
from jax.experimental.pallas import tpu as pltpu

F32 = jnp.float32
BF16 = jnp.bfloat16

D_MODEL = 2048
BATCH = 2
SEQ = 4096
N_TOK = BATCH * SEQ
DEPTH = 2
GRID_W = 64
GRID_ROWS = SEQ // GRID_W
WIN_ROWS = 8
WIN_COLS = 16
NA_HEADS = 8
NA_HEAD_DIM = 128
NA_WIDTH = NA_HEADS * NA_HEAD_DIM
NA_KEYS = WIN_ROWS * GRID_W
CONV_CH = D_MODEL // 2
MLA_HEADS = 8
Q_LORA = 512
KV_LORA = 512
QK_NOPE = 128
QK_ROPE = 64
QK_DIM = QK_NOPE + QK_ROPE
V_DIM = 128
ROPE_THETA = 10000.0
SG_CH = D_MODEL // 2
SG_GROUPS = 8
CHUNK = 128
N_GROUPS = 8
EXPERTS_PER_GROUP = 8
N_EXPERTS = N_GROUPS * EXPERTS_PER_GROUP
TOP_K = 2
D_EXPERT = 768
EPS = 1e-6
NEG_INF = -1e30
LOG2_E = 1.4426950408889634
AB_IN = 3 * NA_WIDTH + 3 * CONV_CH

LANES = 128
SUBLANES = 8
SUBLANE_SHIFT = 3
MOE_BLOCK = 512
MOE_ROW_TILES = (256, 320, 512)
MOE_FT = 256
N_FT = D_EXPERT // MOE_FT
N_ASSIGN = N_TOK * TOP_K
MOE_NBLK = N_ASSIGN // MOE_BLOCK + N_EXPERTS
COMBINE_TOK = 256
VMEM_LIMIT = 52 * 1024 * 1024


def _params(*sem):
    return pltpu.CompilerParams(dimension_semantics=sem, vmem_limit_bytes=VMEM_LIMIT)


def _rms(x, g):
    return x * lax.rsqrt(jnp.mean(x * x, axis=-1, keepdims=True) + EPS) * g


def _norm_mm_kernel(x_ref, g_ref, w_ref, o_ref, xn_ref):
    @pl.when(pl.program_id(1) == 0)
    def _():
        xn_ref[...] = _rms(x_ref[...].astype(F32), g_ref[...]).astype(BF16)

    o_ref[...] = jnp.dot(xn_ref[...], w_ref[...], preferred_element_type=F32).astype(o_ref.dtype)


def _norm_matmul(x, g, w, tm, tn):
    m, k = x.shape
    nout = w.shape[1]
    return pl.pallas_call(
        _norm_mm_kernel,
        out_shape=jax.ShapeDtypeStruct((m, nout), BF16),
        grid=(m // tm, nout // tn),
        in_specs=[
            pl.BlockSpec((tm, k), lambda i, j: (i, 0)),
            pl.BlockSpec((1, k), lambda i, j: (0, 0)),
            pl.BlockSpec((k, tn), lambda i, j: (0, j)),
        ],
        out_specs=pl.BlockSpec((tm, tn), lambda i, j: (i, j)),
        scratch_shapes=[pltpu.VMEM((tm, k), BF16)],
        compiler_params=_params("parallel", "arbitrary"),
        name="norm_matmul",
    )(x, g.reshape(1, k), w)


CD_TN = 512
CD_MAIN = Q_LORA + KV_LORA + 2 * SG_CH
CD_A_TILES = (Q_LORA + KV_LORA) // CD_TN
CD_MAIN_TILES = CD_MAIN // CD_TN


def _cd_proj_kernel(x_ref, g_ref, wa_ref, wb_ref, wc_ref, main_ref, kr_ref, xn_ref):
    j = pl.program_id(1)

    @pl.when(j == 0)
    def _():
        xn_ref[...] = _rms(x_ref[...], g_ref[...]).astype(BF16)

    @pl.when(j < CD_A_TILES)
    def _():
        main_ref[...] = jnp.dot(xn_ref[...], wa_ref[...], preferred_element_type=F32).astype(BF16)

    @pl.when(jnp.logical_and(j >= CD_A_TILES, j < CD_MAIN_TILES))
    def _():
        main_ref[...] = jnp.dot(xn_ref[...], wb_ref[...], preferred_element_type=F32).astype(BF16)

    @pl.when(j == CD_MAIN_TILES)
    def _():
        kr_ref[...] = jnp.dot(xn_ref[...], wc_ref[...], preferred_element_type=F32).astype(BF16)


def _cd_in_proj(x, g, w, tm=1024):
    c0, c1 = Q_LORA + KV_LORA, Q_LORA + KV_LORA + QK_ROPE
    wa = w[:, :c0].astype(BF16)
    wb = w[:, c1:].astype(BF16)
    wc = jnp.pad(w[:, c0:c1], ((0, 0), (0, LANES - QK_ROPE))).astype(BF16)
    return pl.pallas_call(
        _cd_proj_kernel,
        out_shape=(jax.ShapeDtypeStruct((N_TOK, CD_MAIN), BF16), jax.ShapeDtypeStruct((N_TOK, LANES), BF16)),
        grid=(N_TOK // tm, CD_MAIN_TILES + 1),
        in_specs=[
            pl.BlockSpec((tm, D_MODEL), lambda i, j: (i, 0)),
            pl.BlockSpec((1, D_MODEL), lambda i, j: (0, 0)),
            pl.BlockSpec((D_MODEL, CD_TN), lambda i, j: (0, jnp.minimum(j, CD_A_TILES - 1))),
            pl.BlockSpec((D_MODEL, CD_TN),
                         lambda i, j: (0, jnp.clip(j - CD_A_TILES, 0, CD_MAIN_TILES - CD_A_TILES - 1))),
            pl.BlockSpec((D_MODEL, LANES), lambda i, j: (0, 0)),
        ],
        out_specs=(
            pl.BlockSpec((tm, CD_TN), lambda i, j: (i, jnp.minimum(j, CD_MAIN_TILES - 1))),
            pl.BlockSpec((tm, LANES), lambda i, j: (i, 0)),
        ),
        scratch_shapes=[pltpu.VMEM((tm, D_MODEL), BF16)],
        compiler_params=_params("parallel", "arbitrary"),
        name="cd_in_proj",
    )(x, g.reshape(1, D_MODEL), wa, wb, wc)


def _natten_bias_table(rpb):
    c = np.arange(GRID_W)
    col_start = np.clip(c - WIN_COLS // 2, 0, GRID_W - WIN_COLS)
    valid = (c[None, :] >= col_start[:, None]) & (c[None, :] < col_start[:, None] + WIN_COLS)
    dc = np.clip(c[None, :] - c[:, None] + WIN_COLS - 1, 0, 2 * WIN_COLS - 2)
    pick = (dc[:, :, None] == np.arange(2 * WIN_COLS - 1)).astype(np.float32)
    m = jnp.einsum('hrd,ckd->hcrk', rpb, jnp.asarray(pick), precision=lax.Precision.HIGHEST)
    m = jnp.where(valid[None, :, None, :], m, NEG_INF)
    t = jnp.stack([m[:, :, o:o + WIN_ROWS] for o in range(WIN_ROWS)], axis=1)
    return t.reshape(rpb.shape[0], WIN_ROWS, GRID_W, NA_KEYS).astype(F32)


NA_ROWS_PER_STEP = 4


def _natten_kernel(q_ref, k_ref, v_ref, t_ref, o_ref):
    scale = NA_HEAD_DIM ** -0.5

    def rows(i, carry):
        rs = [i * NA_ROWS_PER_STEP + u for u in range(NA_ROWS_PER_STEP)]
        kr0s = [jnp.clip(r - WIN_ROWS // 2, 0, GRID_ROWS - WIN_ROWS) for r in rs]
        q0s = [pl.multiple_of(r * GRID_W, GRID_W) for r in rs]
        k0s = [pl.multiple_of(kr0 * GRID_W, GRID_W) for kr0 in kr0s]
        ss = [lax.dot_general(q_ref[pl.ds(q0, GRID_W), :], k_ref[pl.ds(k0, NA_KEYS), :],
                              (((1,), (1,)), ((), ())), preferred_element_type=F32)
              for q0, k0 in zip(q0s, k0s)]
        ps, ls = [], []
        for s, r, kr0 in zip(ss, rs, kr0s):
            s = s * scale + t_ref[kr0 - r + WIN_ROWS - 1]
            p = jnp.exp(s - jnp.max(s, axis=-1, keepdims=True))
            ls.append(jnp.sum(p, axis=-1, keepdims=True))
            ps.append(p.astype(BF16))
        for p, l, q0, k0 in zip(ps, ls, q0s, k0s):
            o = jnp.dot(p, v_ref[pl.ds(k0, NA_KEYS), :], preferred_element_type=F32) / l
            o_ref[pl.ds(q0, GRID_W), :] = o.astype(o_ref.dtype)
        return carry

    lax.fori_loop(0, GRID_ROWS // NA_ROWS_PER_STEP, rows, 0)


def _natten(p, table):
    blk = (SEQ, NA_HEAD_DIM)
    return pl.pallas_call(
        _natten_kernel,
        out_shape=jax.ShapeDtypeStruct((N_TOK, NA_WIDTH), BF16),
        grid=(BATCH, NA_HEADS),
        in_specs=[
            pl.BlockSpec(blk, lambda b, h: (b, h)),
            pl.BlockSpec(blk, lambda b, h: (b, NA_HEADS + h)),
            pl.BlockSpec(blk, lambda b, h: (b, 2 * NA_HEADS + h)),
            pl.BlockSpec((None, WIN_ROWS, GRID_W, NA_KEYS), lambda b, h: (h, 0, 0, 0)),
        ],
        out_specs=pl.BlockSpec(blk, lambda b, h: (b, h)),
        compiler_params=_params("parallel", "parallel"),
        name="natten",
    )(p, p, p, table)


CONV_CB = 128


def _conv_kernel(gb_ref, gc_ref, hc_ref, w_ref, b_ref, o_ref):
    z = gc_ref[...].astype(F32) * hc_ref[...].astype(F32)
    pos = lax.broadcasted_iota(jnp.int32, z.shape, 0)
    z_prev = jnp.where(pos == 0, 0.0, pltpu.roll(z, 1, 0))
    z_next = jnp.where(pos == SEQ - 1, 0.0, pltpu.roll(z, SEQ - 1, 0))
    y = b_ref[...] + z_prev * w_ref[0:1, :] + z * w_ref[1:2, :] + z_next * w_ref[2:3, :]
    o_ref[...] = (gb_ref[...].astype(F32) * y).astype(o_ref.dtype)


def _gated_conv(p, w, b):
    base = 3 * NA_WIDTH // CONV_CB
    step = CONV_CH // CONV_CB
    blk = (SEQ, CONV_CB)
    return pl.pallas_call(
        _conv_kernel,
        out_shape=jax.ShapeDtypeStruct((N_TOK, CONV_CH), BF16),
        grid=(BATCH, step),
        in_specs=[
            pl.BlockSpec(blk, lambda bi, c: (bi, base + c)),
            pl.BlockSpec(blk, lambda bi, c: (bi, base + step + c)),
            pl.BlockSpec(blk, lambda bi, c: (bi, base + 2 * step + c)),
            pl.BlockSpec((3, CONV_CB), lambda bi, c: (0, c)),
            pl.BlockSpec((1, CONV_CB), lambda bi, c: (0, c)),
        ],
        out_specs=pl.BlockSpec(blk, lambda bi, c: (bi, c)),
        compiler_params=_params("parallel", "parallel"),
        name="gated_conv",
    )(p, p, p, w, b.reshape(1, CONV_CH))


def _out_proj_kernel(a_ref, b_ref, wa_ref, wb_ref, r_ref, o_ref):
    acc = jnp.dot(a_ref[...], wa_ref[...], preferred_element_type=F32)
    acc = acc + jnp.dot(b_ref[...], wb_ref[...], preferred_element_type=F32)
    o_ref[...] = r_ref[...] + acc


def _out_proj(a, b, w, res, tm=512):
    ka, kb = a.shape[1], b.shape[1]
    assert ka == kb
    return pl.pallas_call(
        _out_proj_kernel,
        out_shape=jax.ShapeDtypeStruct((N_TOK, D_MODEL), F32),
        grid=(N_TOK // tm,),
        in_specs=[
            pl.BlockSpec((tm, ka), lambda i: (i, 0)),
            pl.BlockSpec((tm, kb), lambda i: (i, 0)),
            pl.BlockSpec((ka, D_MODEL), lambda i: (0, 0)),
            pl.BlockSpec((kb, D_MODEL), lambda i: (1, 0)),
            pl.BlockSpec((tm, D_MODEL), lambda i: (i, 0)),
        ],
        out_specs=pl.BlockSpec((tm, D_MODEL), lambda i: (i, 0)),
        compiler_params=_params("parallel"),
        name="out_proj",
    )(a, b, w, w, res)


MLA_TM = 512


def _rope_tables():
    pos = jnp.arange(SEQ, dtype=jnp.int32)
    row = (pos // GRID_W).astype(F32)
    col = (pos % GRID_W).astype(F32)
    half = QK_ROPE // 2
    inv = ROPE_THETA ** (-jnp.arange(0, half, 2, dtype=F32) / half)
    ar, ac = row[:, None] * inv, col[:, None] * inv
    cos_t = jnp.concatenate([jnp.cos(ar), jnp.cos(ar), jnp.cos(ac), jnp.cos(ac)], axis=-1)
    sin_t = jnp.concatenate([-jnp.sin(ar), jnp.sin(ar), -jnp.sin(ac), jnp.sin(ac)], axis=-1)
    quarter = half // 2
    src = np.arange(QK_ROPE) + np.where((np.arange(QK_ROPE) // quarter) % 2 == 0, quarter, -quarter)
    swap = np.zeros((QK_ROPE, QK_ROPE), np.float32)
    swap[src, np.arange(QK_ROPE)] = 1.0
    swap_all = np.kron(np.eye(MLA_HEADS, dtype=np.float32), swap)
    return cos_t, sin_t, jnp.asarray(swap_all, BF16)


def _rope(x, cos_t, sin_t, swap):
    hi = x.astype(BF16)
    lo = (x - hi.astype(F32)).astype(BF16)
    xs = jnp.dot(hi, swap, preferred_element_type=F32) + jnp.dot(lo, swap, preferred_element_type=F32)
    return x * cos_t + xs * sin_t


ROPE_ALL = MLA_HEADS * QK_ROPE
NOPE_ALL = MLA_HEADS * QK_NOPE


def _mla_prep_kernel(cq_ref, ckv_ref, kr_ref, gq_ref, gkv_ref, wq_ref, wkv_ref, cos_ref, sin_ref,
                     swap_ref, q_ref, k_ref, v_ref):
    scale = QK_DIM ** -0.5 * LOG2_E
    cos_t, sin_t, swap = cos_ref[...], sin_ref[...], swap_ref[...]
    cqn = _rms(cq_ref[...].astype(F32), gq_ref[...]).astype(BF16)
    ckvn = _rms(ckv_ref[...].astype(F32), gkv_ref[...]).astype(BF16)
    kr = kr_ref[:, :QK_ROPE].astype(F32)
    krot = _rope(kr, cos_t, sin_t, swap[:QK_ROPE, :QK_ROPE]).astype(BF16)
    q = jnp.dot(cqn, wq_ref[...], preferred_element_type=F32)
    qn = (q[:, :NOPE_ALL] * scale).astype(BF16)
    cos_all = jnp.concatenate([cos_t] * MLA_HEADS, axis=1)
    sin_all = jnp.concatenate([sin_t] * MLA_HEADS, axis=1)
    qr = (_rope(q[:, NOPE_ALL:], cos_all, sin_all, swap) * scale).astype(BF16)
    kv = jnp.dot(ckvn, wkv_ref[...], preferred_element_type=F32)
    hw = QK_NOPE + V_DIM
    for h in range(MLA_HEADS):
        q_ref[h, :, :QK_NOPE] = qn[:, h * QK_NOPE:(h + 1) * QK_NOPE]
        q_ref[h, :, QK_NOPE:] = qr[:, h * QK_ROPE:(h + 1) * QK_ROPE]
        k_ref[h, :, :QK_NOPE] = kv[:, h * hw:h * hw + QK_NOPE].astype(BF16)
        k_ref[h, :, QK_NOPE:] = krot
        v_ref[h, 0] = kv[:, h * hw + QK_NOPE:(h + 1) * hw].T.astype(BF16)


def _mla_prep(p, k_rope, q_norm, kv_norm, wq, wkv):
    cos_t, sin_t, swap = _rope_tables()
    tm = MLA_TM
    seq_blocks = SEQ // tm
    return pl.pallas_call(
        _mla_prep_kernel,
        out_shape=(
            jax.ShapeDtypeStruct((MLA_HEADS, N_TOK, QK_DIM), BF16),
            jax.ShapeDtypeStruct((MLA_HEADS, N_TOK, QK_DIM), BF16),
            jax.ShapeDtypeStruct((MLA_HEADS, N_TOK // tm, V_DIM, tm), BF16),
        ),
        grid=(N_TOK // tm,),
        in_specs=[
            pl.BlockSpec((tm, Q_LORA), lambda i: (i, 0)),
            pl.BlockSpec((tm, KV_LORA), lambda i: (i, 1)),
            pl.BlockSpec((tm, LANES), lambda i: (i, 0)),
            pl.BlockSpec((1, Q_LORA), lambda i: (0, 0)),
            pl.BlockSpec((1, KV_LORA), lambda i: (0, 0)),
            pl.BlockSpec((Q_LORA, NOPE_ALL + ROPE_ALL), lambda i: (0, 0)),
            pl.BlockSpec((KV_LORA, MLA_HEADS * (QK_NOPE + V_DIM)), lambda i: (0, 0)),
            pl.BlockSpec((tm, QK_ROPE), lambda i: (i % seq_blocks, 0)),
            pl.BlockSpec((tm, QK_ROPE), lambda i: (i % seq_blocks, 0)),
            pl.BlockSpec((ROPE_ALL, ROPE_ALL), lambda i: (0, 0)),
        ],
        out_specs=(
            pl.BlockSpec((MLA_HEADS, tm, QK_DIM), lambda i: (0, i, 0)),
            pl.BlockSpec((MLA_HEADS, tm, QK_DIM), lambda i: (0, i, 0)),
            pl.BlockSpec((MLA_HEADS, 1, V_DIM, tm), lambda i: (0, i, 0, 0)),
        ),
        compiler_params=_params("parallel"),
        name="mla_prep",
    )(p, p, k_rope, q_norm.reshape(1, Q_LORA), kv_norm.reshape(1, KV_LORA), wq, wkv, cos_t, sin_t, swap)


MLA_TQ = 4096
MLA_TK = 512
MLA_QH = 16
assert MLA_TK == MLA_TM


def _mla_attn_kernel(q_ref, k_ref, vt_ref, o_ref, s_ref, m_ref, l_ref, acc_ref):
    n_chunks = SEQ // MLA_TK
    cols = MLA_TQ // MLA_QH

    def scores(h, c):
        k0 = pl.multiple_of(c * MLA_TK, MLA_TK)
        return lax.dot_general(k_ref[pl.ds(k0, MLA_TK), :], q_ref[h * cols:(h + 1) * cols, :],
                               (((1,), (1,)), ((), ())), preferred_element_type=F32)

    m_ref[...] = jnp.full(m_ref.shape, NEG_INF, F32)
    l_ref[...] = jnp.zeros(l_ref.shape, F32)
    acc_ref[...] = jnp.zeros(acc_ref.shape, F32)
    for h in range(MLA_QH):
        s_ref[0, h] = scores(h, 0)

    def step(c, slot):
        c_next = jnp.where(c + 1 < n_chunks, c + 1, 0)
        for h in range(MLA_QH):
            s_ref[1 - slot, h] = scores(h, c_next)
            s = s_ref[slot, h]
            m_old = m_ref[h]
            m_new = jnp.maximum(m_old, jnp.max(s, axis=0, keepdims=True))
            p = jnp.exp2(s - m_new)
            alpha = jnp.exp2(m_old - m_new)
            l_ref[h] = alpha * l_ref[h] + jnp.sum(p, axis=0, keepdims=True)
            pv = jnp.dot(vt_ref[c], p.astype(BF16), preferred_element_type=F32)
            acc_ref[h] = alpha * acc_ref[h] + pv
            m_ref[h] = m_new

    def pair(i, carry):
        step(2 * i, 0)
        step(2 * i + 1, 1)
        return carry

    lax.fori_loop(0, n_chunks // 2, pair, 0)
    for h in range(MLA_QH):
        o_ref[h * cols:(h + 1) * cols, :] = (acc_ref[h] / l_ref[h]).T.astype(o_ref.dtype)


def _mla_attn(q, k, v):
    nq = SEQ // MLA_TQ
    return pl.pallas_call(
        _mla_attn_kernel,
        out_shape=jax.ShapeDtypeStruct((N_TOK, MLA_HEADS * V_DIM), BF16),
        grid=(BATCH, MLA_HEADS, nq),
        in_specs=[
            pl.BlockSpec((None, MLA_TQ, QK_DIM), lambda b, h, i: (h, b * nq + i, 0)),
            pl.BlockSpec((None, SEQ, QK_DIM), lambda b, h, i: (h, b, 0)),
            pl.BlockSpec((None, SEQ // MLA_TK, V_DIM, MLA_TK), lambda b, h, i: (h, b, 0, 0)),
        ],
        out_specs=pl.BlockSpec((MLA_TQ, V_DIM), lambda b, h, i: (b * nq + i, h)),
        scratch_shapes=[
            pltpu.VMEM((2, MLA_QH, MLA_TK, MLA_TQ // MLA_QH), F32),
            pltpu.VMEM((MLA_QH, 1, MLA_TQ // MLA_QH), F32),
            pltpu.VMEM((MLA_QH, 1, MLA_TQ // MLA_QH), F32),
            pltpu.VMEM((MLA_QH, V_DIM, MLA_TQ // MLA_QH), F32),
        ],
        compiler_params=_params("parallel", "parallel", "parallel"),
        name="mla_attn",
    )(q, k, v)


SG_TM = 256


def _sg_kernel(u_ref, v_ref, g_ref, w_ref, bt_ref, o_ref):
    v = jax.nn.gelu(v_ref[...].astype(F32))
    vn = _rms(v, g_ref[...]).astype(BF16)
    u = jax.nn.gelu(u_ref[...].astype(F32))
    gc = SG_CH // SG_GROUPS
    for n in range(SG_TM // CHUNK):
        rows = slice(n * CHUNK, (n + 1) * CHUNK)
        for g in range(SG_GROUPS):
            cols = slice(g * gc, (g + 1) * gc)
            mixed = jnp.dot(w_ref[g], vn[rows, cols], preferred_element_type=F32) + bt_ref[:, g:g + 1]
            o_ref[rows, cols] = (u[rows, cols] * mixed).astype(o_ref.dtype)


def _spatial_gating(p, g_norm, w_s, b_s):
    u_blk = (Q_LORA + KV_LORA) // SG_CH
    return pl.pallas_call(
        _sg_kernel,
        out_shape=jax.ShapeDtypeStruct((N_TOK, SG_CH), BF16),
        grid=(N_TOK // SG_TM,),
        in_specs=[
            pl.BlockSpec((SG_TM, SG_CH), lambda i: (i, u_blk)),
            pl.BlockSpec((SG_TM, SG_CH), lambda i: (i, u_blk + 1)),
            pl.BlockSpec((1, SG_CH), lambda i: (0, 0)),
            pl.BlockSpec((SG_GROUPS, CHUNK, CHUNK), lambda i: (0, 0, 0)),
            pl.BlockSpec((CHUNK, SG_GROUPS), lambda i: (0, 0)),
        ],
        out_specs=pl.BlockSpec((SG_TM, SG_CH), lambda i: (i, 0)),
        compiler_params=_params("parallel"),
        name="spatial_gating",
    )(p, p, g_norm.reshape(1, SG_CH), w_s.astype(BF16), b_s.T)


ROUTER_TM = 512


def _router_kernel(x_ref, g_ref, w_ref, b_ref, o_ref, xn_ref):
    xn = _rms(x_ref[...], g_ref[...])
    xn_ref[...] = xn
    w = w_ref[...]
    xh, wh = xn.astype(BF16), w.astype(BF16)
    xl, wl = (xn - xh.astype(F32)).astype(BF16), (w - wh.astype(F32)).astype(BF16)
    logits = (jnp.dot(xh, wh, preferred_element_type=F32)
              + (jnp.dot(xh, wl, preferred_element_type=F32) + jnp.dot(xl, wh, preferred_element_type=F32))
              + b_ref[...])
    lane = lax.broadcasted_iota(jnp.int32, logits.shape, 1).astype(F32)
    low = jnp.float32(-3.0e38)

    def first_max(vals):
        top = jnp.max(vals, axis=-1, keepdims=True)
        idx = jnp.min(jnp.where(vals == top, lane, float(LANES)), axis=-1, keepdims=True)
        return top, idx

    is_group = lane < N_GROUPS
    g_top, g_idx = first_max(jnp.where(is_group, logits, low))
    g_prob = 1.0 / jnp.sum(jnp.where(is_group, jnp.exp(logits - g_top), 0.0), axis=-1, keepdims=True)
    lo = N_GROUPS + g_idx * EXPERTS_PER_GROUP
    e_vals = jnp.where(lane >= lo, jnp.where(lane < lo + EXPERTS_PER_GROUP, logits, low), low)
    v1, i1 = first_max(e_vals)
    v2, i2 = first_max(jnp.where(lane == i1, low, e_vals))
    e21 = jnp.exp(v2 - v1)
    w1 = g_prob / (1.0 + e21)
    w2 = g_prob * e21 / (1.0 + e21)
    out = jnp.where(lane == 0, i1 - N_GROUPS,
                    jnp.where(lane == 1, i2 - N_GROUPS,
                              jnp.where(lane == 2, w1, jnp.where(lane == 3, w2, 0.0))))
    o_ref[...] = out


def _router(x, g, w, b):
    tm = ROUTER_TM
    return pl.pallas_call(
        _router_kernel,
        out_shape=(jax.ShapeDtypeStruct((N_TOK, LANES), F32), jax.ShapeDtypeStruct((N_TOK, D_MODEL), F32)),
        grid=(N_TOK // tm,),
        in_specs=[
            pl.BlockSpec((tm, D_MODEL), lambda i: (i, 0)),
            pl.BlockSpec((1, D_MODEL), lambda i: (0, 0)),
            pl.BlockSpec((D_MODEL, LANES), lambda i: (0, 0)),
            pl.BlockSpec((1, LANES), lambda i: (0, 0)),
        ],
        out_specs=(pl.BlockSpec((tm, LANES), lambda i: (i, 0)), pl.BlockSpec((tm, D_MODEL), lambda i: (i, 0))),
        compiler_params=_params("parallel"),
        name="router",
    )(x, g.reshape(1, D_MODEL), w, b)


def _dispatch_plan(eid):
    tok = jnp.arange(N_TOK, dtype=jnp.int32)
    keys = jnp.concatenate([eid[:, k] * N_ASSIGN + (k * N_TOK + tok) for k in range(TOP_K)])
    row_asg = jnp.sort(keys) & (N_ASSIGN - 1)
    row_asg = jnp.concatenate([row_asg, jnp.zeros((SUBLANES,), jnp.int32)])
    counts = jnp.sum(eid.reshape(-1)[:, None] == jnp.arange(N_EXPERTS, dtype=jnp.int32)[None, :], axis=0,
                     dtype=jnp.int32)
    starts = jnp.cumsum(counts) - counts
    nblk_e = (counts + MOE_BLOCK - 1) // MOE_BLOCK
    blk_end = jnp.cumsum(nblk_e)
    blk_start = blk_end - nblk_e
    n_active = blk_end[-1]
    bidx = jnp.arange(MOE_NBLK, dtype=jnp.int32)
    blk_x = jnp.minimum(bidx, n_active - 1)
    blk_e = jnp.minimum(jnp.sum(blk_end[None, :] <= blk_x[:, None], axis=1, dtype=jnp.int32), N_EXPERTS - 1)
    first = (blk_x - blk_start[blk_e]) * MOE_BLOCK
    blk_rows = jnp.where(bidx < n_active, jnp.clip(counts[blk_e] - first, 0, MOE_BLOCK), 0).astype(jnp.int32)
    blk_src = (starts[blk_e] + first).astype(jnp.int32)
    return row_asg, blk_e, blk_rows, blk_src


GATHER_STEP = min(1, N_FT - 1)
_ROW_BITS = tuple(1 << k for k in range(MOE_BLOCK.bit_length() - 1, -1, -1))


def _expert_kernel(row_asg, blk_e, blk_rows, blk_src, x_hbm, w1_ref, w3_ref, w2_ref, ys_hbm,
                   xbuf, acc, xn_ref, gsem, ssem):
    b = pl.program_id(0)
    j = pl.program_id(1)
    rows = blk_rows[b]
    slot = b % 2

    def n_fetch(blk):
        return (blk_rows[blk] + SUBLANES - 1) & -SUBLANES

    def issue_gather(blk, s):
        base = blk_src[blk]

        def group(q, c):
            for u in range(SUBLANES):
                t = row_asg[base + q * SUBLANES + u] & (N_TOK - 1)
                pltpu.make_async_copy(x_hbm.at[pl.ds(t, 1), :], xbuf.at[s, q, pl.ds(u, 1), :], gsem.at[s]).start()
            return c

        lax.fori_loop(0, n_fetch(blk) >> SUBLANE_SHIFT, group, 0)

    def wait_rows(n, copy_of):
        for bit in _ROW_BITS:
            @pl.when((n & bit) != 0)
            def _():
                copy_of(bit).wait()

    def gather_block(s):
        def copy_of(k):
            tiles = xbuf.at[s, pl.ds(0, k // SUBLANES)]
            return pltpu.make_async_copy(tiles, tiles, gsem.at[s])
        return copy_of

    def scatter_block(s):
        def copy_of(k):
            part = acc.at[s, pl.ds(0, k // SUBLANES)] if k >= SUBLANES else acc.at[s, 0, pl.ds(0, k), :]
            return pltpu.make_async_copy(part, part, ssem.at[s])
        return copy_of

    def issue_scatter(blk, s, n):
        base = blk_src[blk]

        def one(q, u, a):
            pltpu.make_async_copy(acc.at[s, q, pl.ds(u, 1), :], ys_hbm.at[pl.ds(a, 1), :], ssem.at[s]).start()

        def group(q, c):
            for u in range(SUBLANES):
                one(q, u, row_asg[base + q * SUBLANES + u])
            return c

        def tail(i, c):
            one(i >> SUBLANE_SHIFT, i & (SUBLANES - 1), row_asg[base + i])
            return c

        full = n >> SUBLANE_SHIFT
        lax.fori_loop(0, full, group, 0)
        lax.fori_loop(full * SUBLANES, n, tail, 0)

    def for_row_tile(fn):
        lo = 0
        for m in MOE_ROW_TILES:
            @pl.when(jnp.logical_and(rows > lo, rows <= m))
            def _():
                fn(m)
            lo = m

    nxt = jnp.minimum(b + 1, MOE_NBLK - 1)
    has_next = jnp.logical_and(b + 1 < MOE_NBLK, blk_rows[nxt] > 0)

    @pl.when(rows > 0)
    def _():
        @pl.when(j == 0)
        def _():
            @pl.when(b == 0)
            def _():
                xbuf[...] = jnp.zeros(xbuf.shape, F32)
                acc[...] = jnp.zeros(acc.shape, F32)
                issue_gather(0, 0)

            wait_rows(n_fetch(b), gather_block(slot))

            def cast_tile(m):
                xn_ref[:m, :] = xbuf[slot, :m // SUBLANES].reshape(m, D_MODEL).astype(BF16)

            for_row_tile(cast_tile)

        @pl.when(jnp.logical_and(j == GATHER_STEP, has_next))
        def _():
            issue_gather(nxt, 1 - slot)

        def ffn_tile(m):
            xs = xn_ref[:m, :]
            h1 = jnp.dot(xs, w1_ref[...].astype(BF16), preferred_element_type=F32)
            h3 = jnp.dot(xs, w3_ref[...].astype(BF16), preferred_element_type=F32)
            h = (jax.nn.silu(h1) * h3).astype(BF16)
            y = jnp.dot(h, w2_ref[...].astype(BF16), preferred_element_type=F32)
            mt = m // SUBLANES
            acc[slot, :mt] = jnp.where(j > 0, acc[slot, :mt], 0.0) + y.reshape(mt, SUBLANES, D_MODEL)

        for_row_tile(ffn_tile)

        @pl.when(j == N_FT - 1)
        def _():
            @pl.when(b > 0)
            def _():
                wait_rows(blk_rows[jnp.maximum(b - 1, 0)], scatter_block(1 - slot))

            issue_scatter(b, slot, rows)

            @pl.when(jnp.logical_not(has_next))
            def _():
                wait_rows(rows, scatter_block(slot))


def _moe_experts(xn, w1, w3, w2, layer, row_asg, blk_e, blk_rows, blk_src):
    def w_col(b, j, ra, be, br, bs):
        return (layer, be[b], 0, jnp.where(br[b] > 0, j, N_FT - 1))

    def w_row(b, j, ra, be, br, bs):
        return (layer, be[b], jnp.where(br[b] > 0, j, N_FT - 1), 0)

    return pl.pallas_call(
        _expert_kernel,
        out_shape=jax.ShapeDtypeStruct((N_ASSIGN, D_MODEL), F32),
        grid_spec=pltpu.PrefetchScalarGridSpec(
            num_scalar_prefetch=4,
            grid=(MOE_NBLK, N_FT),
            in_specs=[
                pl.BlockSpec(memory_space=pl.ANY),
                pl.BlockSpec((None, None, D_MODEL, MOE_FT), w_col),
                pl.BlockSpec((None, None, D_MODEL, MOE_FT), w_col),
                pl.BlockSpec((None, None, MOE_FT, D_MODEL), w_row),
            ],
            out_specs=pl.BlockSpec(memory_space=pl.ANY),
            scratch_shapes=[
                pltpu.VMEM((2, MOE_BLOCK // SUBLANES, SUBLANES, D_MODEL), F32),
                pltpu.VMEM((2, MOE_BLOCK // SUBLANES, SUBLANES, D_MODEL), F32),
                pltpu.VMEM((MOE_BLOCK, D_MODEL), BF16),
                pltpu.SemaphoreType.DMA((2,)),
                pltpu.SemaphoreType.DMA((2,)),
            ],
        ),
        compiler_params=_params("arbitrary", "arbitrary"),
        name="moe_experts",
    )(row_asg, blk_e, blk_rows, blk_src, xn, w1, w3, w2)


def _combine_kernel(final_norm, x_ref, y0_ref, y1_ref, gate_ref, gf_ref, o_ref):
    gates = gate_ref[...]
    out = x_ref[...] + (gates[:, 2:3] * y0_ref[...] + gates[:, 3:4] * y1_ref[...])
    if final_norm:
        out = _rms(out, gf_ref[...])
    o_ref[...] = out


def _moe_combine(x, ys, route, g_final, final_norm):
    tm = COMBINE_TOK
    return pl.pallas_call(
        functools.partial(_combine_kernel, final_norm),
        out_shape=jax.ShapeDtypeStruct((N_TOK, D_MODEL), F32),
        grid=(N_TOK // tm,),
        in_specs=[
            pl.BlockSpec((tm, D_MODEL), lambda i: (i, 0)),
            pl.BlockSpec((tm, D_MODEL), lambda i: (i, 0)),
            pl.BlockSpec((tm, D_MODEL), lambda i: (N_TOK // tm + i, 0)),
            pl.BlockSpec((tm, LANES), lambda i: (i, 0)),
            pl.BlockSpec((1, D_MODEL), lambda i: (0, 0)),
        ],
        out_specs=pl.BlockSpec((tm, D_MODEL), lambda i: (i, 0)),
        compiler_params=_params("parallel"),
        name="moe_combine",
    )(x, ys, ys, route, g_final.reshape(1, D_MODEL))


def _hier_moe(x, g, wg, bg, we, be, w1, w3, w2, layer, g_final, final_norm):
    pad = LANES - N_GROUPS - N_EXPERTS
    w_r = jnp.concatenate([wg, we, jnp.zeros((D_MODEL, pad), F32)], axis=1)
    b_r = jnp.concatenate([bg, be, jnp.zeros((pad,), F32)]).reshape(1, LANES)
    route, xn = _router(x, g, w_r, b_r)
    eid = route[:, :TOP_K].astype(jnp.int32)
    row_asg, blk_e, blk_rows, blk_src = _dispatch_plan(eid)
    ys = _moe_experts(xn, w1, w3, w2, layer, row_asg, blk_e, blk_rows, blk_src)
    return _moe_combine(x, ys, route, g_final, final_norm)


def kernel(x, norm_mix, norm_ffn, norm_final, w_in_ab, na_rpb, conv_w, conv_b, w_out_ab, w_in_cd,
           q_norm, kv_norm, w_uq, w_ukv, sg_norm, sg_w, sg_b, w_out_cd, router_group_w,
           router_group_b, router_expert_w, router_expert_b, w1, w3, w2):
    xt = x.reshape(N_TOK, D_MODEL)
    for layer in range(DEPTH):
        i = layer // 2
        if layer % 2 == 0:
            p = _norm_matmul(xt, norm_mix[layer], w_in_ab[i].astype(BF16), tm=1024, tn=512)
            a_out = _natten(p, _natten_bias_table(na_rpb[i]))
            b_out = _gated_conv(p, conv_w[i], conv_b[i])
            xt = _out_proj(a_out, b_out, w_out_ab[i].astype(BF16), xt)
        else:
            p, k_rope = _cd_in_proj(xt, norm_mix[layer], w_in_cd[i])
            wq = w_uq[i].reshape(Q_LORA, MLA_HEADS, QK_DIM)
            wq = jnp.concatenate([wq[:, :, :QK_NOPE].reshape(Q_LORA, NOPE_ALL),
                                  wq[:, :, QK_NOPE:].reshape(Q_LORA, ROPE_ALL)], axis=1).astype(BF16)
            q, k, v = _mla_prep(p, k_rope, q_norm[i], kv_norm[i], wq, w_ukv[i].astype(BF16))
            c_out = _mla_attn(q, k, v)
            d_out = _spatial_gating(p, sg_norm[i], sg_w[i], sg_b[i])
            xt = _out_proj(c_out, d_out, w_out_cd[i].astype(BF16), xt)
        xt = _hier_moe(xt, norm_ffn[layer], router_group_w[layer], router_group_b[layer],
                       router_expert_w[layer], router_expert_b[layer], w1, w3, w2, layer,
                       norm_final, final_norm=(layer == DEPTH - 1))
    return xt.reshape(BATCH, SEQ, D_MODEL)
```

```python
import functools

import numpy as np
import jax
import jax.numpy as jnp
from jax import lax
from jax.experimental import pallas as pl
from jax.experimental.pallas import tpu as pltpu

F32 = jnp.float32
BF16 = jnp.bfloat16

D_MODEL = 2048
BATCH = 2
SEQ = 4096
N_TOK = BATCH * SEQ
DEPTH = 2
GRID_W = 64
GRID_ROWS = SEQ // GRID_W
WIN_ROWS = 8
WIN_COLS = 16
NA_HEADS = 8
NA_HEAD_DIM = 128
NA_WIDTH = NA_HEADS * NA_HEAD_DIM
NA_KEYS = WIN_ROWS * GRID_W
CONV_CH = D_MODEL // 2
MLA_HEADS = 8
Q_LORA = 512
KV_LORA = 512
QK_NOPE = 128
QK_ROPE = 64
QK_DIM = QK_NOPE + QK_ROPE
V_DIM = 128
ROPE_THETA = 10000.0
SG_CH = D_MODEL // 2
SG_GROUPS = 8
CHUNK = 128
N_GROUPS = 8
EXPERTS_PER_GROUP = 8
N_EXPERTS = N_GROUPS * EXPERTS_PER_GROUP
TOP_K = 2
D_EXPERT = 768
EPS = 1e-6
NEG_INF = -1e30
LOG2_E = 1.4426950408889634
AB_IN = 3 * NA_WIDTH + 3 * CONV_CH

LANES = 128
SUBLANES = 8
SUBLANE_SHIFT = 3
MOE_BLOCK = 512
MOE_ROW_TILES = (256, 320, 512)
MOE_FT = 256
N_FT = D_EXPERT // MOE_FT
N_ASSIGN = N_TOK * TOP_K
MOE_NBLK = N_ASSIGN // MOE_BLOCK + N_EXPERTS
COMBINE_TOK = 256
VMEM_LIMIT = 52 * 1024 * 1024


def _params(*sem):
    return pltpu.CompilerParams(dimension_semantics=sem, vmem_limit_bytes=VMEM_LIMIT)


def _rms(x, g):
    return x * lax.rsqrt(jnp.mean(x * x, axis=-1, keepdims=True) + EPS) * g


def _norm_mm_kernel(x_ref, g_ref, w_ref, o_ref, xn_ref):
    @pl.when(pl.program_id(1) == 0)
    def _():
        xn_ref[...] = _rms(x_ref[...].astype(F32), g_ref[...]).astype(BF16)

    o_ref[...] = jnp.dot(xn_ref[...], w_ref[...], preferred_element_type=F32).astype(o_ref.dtype)


def _norm_matmul(x, g, w, tm, tn):
    m, k = x.shape
    nout = w.shape[1]
    return pl.pallas_call(
        _norm_mm_kernel,
        out_shape=jax.ShapeDtypeStruct((m, nout), BF16),
        grid=(m // tm, nout // tn),
        in_specs=[
            pl.BlockSpec((tm, k), lambda i, j: (i, 0)),
            pl.BlockSpec((1, k), lambda i, j: (0, 0)),
            pl.BlockSpec((k, tn), lambda i, j: (0, j)),
        ],
        out_specs=pl.BlockSpec((tm, tn), lambda i, j: (i, j)),
        scratch_shapes=[pltpu.VMEM((tm, k), BF16)],
        compiler_params=_params("parallel", "arbitrary"),
        name="norm_matmul",
    )(x, g.reshape(1, k), w)


CD_TN = 512
CD_MAIN = Q_LORA + KV_LORA + 2 * SG_CH
CD_A_TILES = (Q_LORA + KV_LORA) // CD_TN
CD_MAIN_TILES = CD_MAIN // CD_TN


def _cd_proj_kernel(x_ref, g_ref, wa_ref, wb_ref, wc_ref, main_ref, kr_ref, xn_ref):
    j = pl.program_id(1)

    @pl.when(j == 0)
    def _():
        xn_ref[...] = _rms(x_ref[...], g_ref[...]).astype(BF16)

    @pl.when(j < CD_A_TILES)
    def _():
        main_ref[...] = jnp.dot(xn_ref[...], wa_ref[...], preferred_element_type=F32).astype(BF16)

    @pl.when(jnp.logical_and(j >= CD_A_TILES, j < CD_MAIN_TILES))
    def _():
        main_ref[...] = jnp.dot(xn_ref[...], wb_ref[...], preferred_element_type=F32).astype(BF16)

    @pl.when(j == CD_MAIN_TILES)
    def _():
        kr_ref[...] = jnp.dot(xn_ref[...], wc_ref[...], preferred_element_type=F32).astype(BF16)


def _cd_in_proj(x, g, w, tm=1024):
    c0, c1 = Q_LORA + KV_LORA, Q_LORA + KV_LORA + QK_ROPE
    wa = w[:, :c0].astype(BF16)
    wb = w[:, c1:].astype(BF16)
    wc = jnp.pad(w[:, c0:c1], ((0, 0), (0, LANES - QK_ROPE))).astype(BF16)
    return pl.pallas_call(
        _cd_proj_kernel,
        out_shape=(jax.ShapeDtypeStruct((N_TOK, CD_MAIN), BF16), jax.ShapeDtypeStruct((N_TOK, LANES), BF16)),
        grid=(N_TOK // tm, CD_MAIN_TILES + 1),
        in_specs=[
            pl.BlockSpec((tm, D_MODEL), lambda i, j: (i, 0)),
            pl.BlockSpec((1, D_MODEL), lambda i, j: (0, 0)),
            pl.BlockSpec((D_MODEL, CD_TN), lambda i, j: (0, jnp.minimum(j, CD_A_TILES - 1))),
            pl.BlockSpec((D_MODEL, CD_TN),
                         lambda i, j: (0, jnp.clip(j - CD_A_TILES, 0, CD_MAIN_TILES - CD_A_TILES - 1))),
            pl.BlockSpec((D_MODEL, LANES), lambda i, j: (0, 0)),
        ],
        out_specs=(
            pl.BlockSpec((tm, CD_TN), lambda i, j: (i, jnp.minimum(j, CD_MAIN_TILES - 1))),
            pl.BlockSpec((tm, LANES), lambda i, j: (i, 0)),
        ),
        scratch_shapes=[pltpu.VMEM((tm, D_MODEL), BF16)],
        compiler_params=_params("parallel", "arbitrary"),
        name="cd_in_proj",
    )(x, g.reshape(1, D_MODEL), wa, wb, wc)


def _natten_bias_table(rpb):
    c = np.arange(GRID_W)
    col_start = np.clip(c - WIN_COLS // 2, 0, GRID_W - WIN_COLS)
    valid = (c[None, :] >= col_start[:, None]) & (c[None, :] < col_start[:, None] + WIN_COLS)
    edge = GRID_W - WIN_COLS
    rp = jnp.pad(rpb, ((0, 0), (0, 0), (edge, edge)))
    m = jnp.stack([rp[:, :, GRID_W - 1 - q:2 * GRID_W - 1 - q] for q in range(GRID_W)], axis=1)
    m = jnp.where(valid[None, :, None, :], m, NEG_INF)
    t = jnp.stack([m[:, :, o:o + WIN_ROWS] for o in range(WIN_ROWS)], axis=1)
    return (t.reshape(rpb.shape[0], WIN_ROWS, GRID_W, NA_KEYS) * LOG2_E).astype(F32)


NA_ROWS_PER_STEP = 16
NA_Q_SCALE = NA_HEAD_DIM ** -0.5 * LOG2_E


def _natten_kernel(q_ref, k_ref, v_ref, t_ref, o_ref):
    def rows(i, carry):
        rs = [i * NA_ROWS_PER_STEP + u for u in range(NA_ROWS_PER_STEP)]
        kr0s = [jnp.clip(r - WIN_ROWS // 2, 0, GRID_ROWS - WIN_ROWS) for r in rs]
        q0s = [pl.multiple_of(r * GRID_W, GRID_W) for r in rs]
        k0s = [pl.multiple_of(kr0 * GRID_W, GRID_W) for kr0 in kr0s]
        ss = [lax.dot_general(q_ref[pl.ds(q0, GRID_W), :], k_ref[pl.ds(k0, NA_KEYS), :],
                              (((1,), (1,)), ((), ())), preferred_element_type=F32)
              for q0, k0 in zip(q0s, k0s)]
        ps, ls = [], []
        for s, r, kr0 in zip(ss, rs, kr0s):
            s = s + t_ref[kr0 - r + WIN_ROWS - 1]
            p = jnp.exp2(s - jnp.max(s, axis=-1, keepdims=True))
            ls.append(jnp.sum(p, axis=-1, keepdims=True))
            ps.append(p.astype(BF16))
        for p, l, q0, k0 in zip(ps, ls, q0s, k0s):
            o = jnp.dot(p, v_ref[pl.ds(k0, NA_KEYS), :], preferred_element_type=F32) / l
            o_ref[pl.ds(q0, GRID_W), :] = o.astype(o_ref.dtype)
        return carry

    lax.fori_loop(0, GRID_ROWS // NA_ROWS_PER_STEP, rows, 0)


def _natten(p, table):
    blk = (SEQ, NA_HEAD_DIM)
    return pl.pallas_call(
        _natten_kernel,
        out_shape=jax.ShapeDtypeStruct((N_TOK, NA_WIDTH), BF16),
        grid=(BATCH, NA_HEADS),
        in_specs=[
            pl.BlockSpec(blk, lambda b, h: (b, h)),
            pl.BlockSpec(blk, lambda b, h: (b, NA_HEADS + h)),
            pl.BlockSpec(blk, lambda b, h: (b, 2 * NA_HEADS + h)),
            pl.BlockSpec((None, WIN_ROWS, GRID_W, NA_KEYS), lambda b, h: (h, 0, 0, 0)),
        ],
        out_specs=pl.BlockSpec(blk, lambda b, h: (b, h)),
        compiler_params=_params("parallel", "parallel"),
        name="natten",
    )(p, p, p, table)


CONV_CB = 128


def _conv_kernel(gb_ref, gc_ref, hc_ref, w_ref, b_ref, o_ref):
    z = gc_ref[...].astype(F32) * hc_ref[...].astype(F32)
    pos = lax.broadcasted_iota(jnp.int32, z.shape, 0)
    z_prev = jnp.where(pos == 0, 0.0, pltpu.roll(z, 1, 0))
    z_next = jnp.where(pos == SEQ - 1, 0.0, pltpu.roll(z, SEQ - 1, 0))
    y = b_ref[...] + z_prev * w_ref[0:1, :] + z * w_ref[1:2, :] + z_next * w_ref[2:3, :]
    o_ref[...] = (gb_ref[...].astype(F32) * y).astype(o_ref.dtype)


def _gated_conv(p, w, b):
    base = 3 * NA_WIDTH // CONV_CB
    step = CONV_CH // CONV_CB
    blk = (SEQ, CONV_CB)
    return pl.pallas_call(
        _conv_kernel,
        out_shape=jax.ShapeDtypeStruct((N_TOK, CONV_CH), BF16),
        grid=(BATCH, step),
        in_specs=[
            pl.BlockSpec(blk, lambda bi, c: (bi, base + c)),
            pl.BlockSpec(blk, lambda bi, c: (bi, base + step + c)),
            pl.BlockSpec(blk, lambda bi, c: (bi, base + 2 * step + c)),
            pl.BlockSpec((3, CONV_CB), lambda bi, c: (0, c)),
            pl.BlockSpec((1, CONV_CB), lambda bi, c: (0, c)),
        ],
        out_specs=pl.BlockSpec(blk, lambda bi, c: (bi, c)),
        compiler_params=_params("parallel", "parallel"),
        name="gated_conv",
    )(p, p, p, w, b.reshape(1, CONV_CH))


def _out_proj_kernel(a_ref, b_ref, wa_ref, wb_ref, r_ref, o_ref):
    acc = jnp.dot(a_ref[...], wa_ref[...], preferred_element_type=F32)
    acc = acc + jnp.dot(b_ref[...], wb_ref[...], preferred_element_type=F32)
    o_ref[...] = r_ref[...] + acc


def _out_proj(a, b, w, res, tm=512):
    ka, kb = a.shape[1], b.shape[1]
    assert ka == kb
    return pl.pallas_call(
        _out_proj_kernel,
        out_shape=jax.ShapeDtypeStruct((N_TOK, D_MODEL), F32),
        grid=(N_TOK // tm,),
        in_specs=[
            pl.BlockSpec((tm, ka), lambda i: (i, 0)),
            pl.BlockSpec((tm, kb), lambda i: (i, 0)),
            pl.BlockSpec((ka, D_MODEL), lambda i: (0, 0)),
            pl.BlockSpec((kb, D_MODEL), lambda i: (1, 0)),
            pl.BlockSpec((tm, D_MODEL), lambda i: (i, 0)),
        ],
        out_specs=pl.BlockSpec((tm, D_MODEL), lambda i: (i, 0)),
        compiler_params=_params("parallel"),
        name="out_proj",
    )(a, b, w, w, res)


MLA_TM = 512


def _rope_tables():
    pos = jnp.arange(SEQ, dtype=jnp.int32)
    row = (pos // GRID_W).astype(F32)
    col = (pos % GRID_W).astype(F32)
    half = QK_ROPE // 2
    inv = ROPE_THETA ** (-jnp.arange(0, half, 2, dtype=F32) / half)
    ar, ac = row[:, None] * inv, col[:, None] * inv
    cos_t = jnp.concatenate([jnp.cos(ar), jnp.cos(ar), jnp.cos(ac), jnp.cos(ac)], axis=-1)
    sin_t = jnp.concatenate([-jnp.sin(ar), jnp.sin(ar), -jnp.sin(ac), jnp.sin(ac)], axis=-1)
    quarter = half // 2
    src = np.arange(QK_ROPE) + np.where((np.arange(QK_ROPE) // quarter) % 2 == 0, quarter, -quarter)
    swap = np.zeros((QK_ROPE, QK_ROPE), np.float32)
    swap[src, np.arange(QK_ROPE)] = 1.0
    swap_all = np.kron(np.eye(MLA_HEADS, dtype=np.float32), swap)
    return cos_t, sin_t, jnp.asarray(swap_all, BF16)


def _rope(x, cos_t, sin_t, swap):
    hi = x.astype(BF16)
    lo = (x - hi.astype(F32)).astype(BF16)
    xs = jnp.dot(hi, swap, preferred_element_type=F32) + jnp.dot(lo, swap, preferred_element_type=F32)
    return x * cos_t + xs * sin_t


ROPE_ALL = MLA_HEADS * QK_ROPE
NOPE_ALL = MLA_HEADS * QK_NOPE


def _mla_prep_kernel(cq_ref, ckv_ref, kr_ref, gq_ref, gkv_ref, wq_ref, wkv_ref, cos_ref, sin_ref,
                     swap_ref, q_ref, k_ref, v_ref):
    scale = QK_DIM ** -0.5 * LOG2_E
    cos_t, sin_t, swap = cos_ref[...], sin_ref[...], swap_ref[...]
    cqn = _rms(cq_ref[...].astype(F32), gq_ref[...]).astype(BF16)
    ckvn = _rms(ckv_ref[...].astype(F32), gkv_ref[...]).astype(BF16)
    kr = kr_ref[:, :QK_ROPE].astype(F32)
    krot = _rope(kr, cos_t, sin_t, swap[:QK_ROPE, :QK_ROPE]).astype(BF16)
    q = jnp.dot(cqn, wq_ref[...], preferred_element_type=F32)
    qn = (q[:, :NOPE_ALL] * scale).astype(BF16)
    cos_all = jnp.concatenate([cos_t] * MLA_HEADS, axis=1)
    sin_all = jnp.concatenate([sin_t] * MLA_HEADS, axis=1)
    qr = (_rope(q[:, NOPE_ALL:], cos_all, sin_all, swap) * scale).astype(BF16)
    kv = jnp.dot(ckvn, wkv_ref[...], preferred_element_type=F32)
    hw = QK_NOPE + V_DIM
    for h in range(MLA_HEADS):
        q_ref[h, :, :QK_NOPE] = qn[:, h * QK_NOPE:(h + 1) * QK_NOPE]
        q_ref[h, :, QK_NOPE:] = qr[:, h * QK_ROPE:(h + 1) * QK_ROPE]
        k_ref[h, :, :QK_NOPE] = kv[:, h * hw:h * hw + QK_NOPE].astype(BF16)
        k_ref[h, :, QK_NOPE:] = krot
        v_ref[h, 0] = kv[:, h * hw + QK_NOPE:(h + 1) * hw].T.astype(BF16)


def _mla_prep(p, k_rope, q_norm, kv_norm, wq, wkv):
    cos_t, sin_t, swap = _rope_tables()
    tm = MLA_TM
    seq_blocks = SEQ // tm
    return pl.pallas_call(
        _mla_prep_kernel,
        out_shape=(
            jax.ShapeDtypeStruct((MLA_HEADS, N_TOK, QK_DIM), BF16),
            jax.ShapeDtypeStruct((MLA_HEADS, N_TOK, QK_DIM), BF16),
            jax.ShapeDtypeStruct((MLA_HEADS, N_TOK // tm, V_DIM, tm), BF16),
        ),
        grid=(N_TOK // tm,),
        in_specs=[
            pl.BlockSpec((tm, Q_LORA), lambda i: (i, 0)),
            pl.BlockSpec((tm, KV_LORA), lambda i: (i, 1)),
            pl.BlockSpec((tm, LANES), lambda i: (i, 0)),
            pl.BlockSpec((1, Q_LORA), lambda i: (0, 0)),
            pl.BlockSpec((1, KV_LORA), lambda i: (0, 0)),
            pl.BlockSpec((Q_LORA, NOPE_ALL + ROPE_ALL), lambda i: (0, 0)),
            pl.BlockSpec((KV_LORA, MLA_HEADS * (QK_NOPE + V_DIM)), lambda i: (0, 0)),
            pl.BlockSpec((tm, QK_ROPE), lambda i: (i % seq_blocks, 0)),
            pl.BlockSpec((tm, QK_ROPE), lambda i: (i % seq_blocks, 0)),
            pl.BlockSpec((ROPE_ALL, ROPE_ALL), lambda i: (0, 0)),
        ],
        out_specs=(
            pl.BlockSpec((MLA_HEADS, tm, QK_DIM), lambda i: (0, i, 0)),
            pl.BlockSpec((MLA_HEADS, tm, QK_DIM), lambda i: (0, i, 0)),
            pl.BlockSpec((MLA_HEADS, 1, V_DIM, tm), lambda i: (0, i, 0, 0)),
        ),
        compiler_params=_params("parallel"),
        name="mla_prep",
    )(p, p, k_rope, q_norm.reshape(1, Q_LORA), kv_norm.reshape(1, KV_LORA), wq, wkv, cos_t, sin_t, swap)


MLA_TQ = 4096
MLA_TK = 512
MLA_QH = 16
assert MLA_TK == MLA_TM


def _mla_attn_kernel(q_ref, k_ref, vt_ref, o_ref, s_ref, m_ref, l_ref, acc_ref):
    n_chunks = SEQ // MLA_TK
    cols = MLA_TQ // MLA_QH

    def scores(h, c):
        k0 = pl.multiple_of(c * MLA_TK, MLA_TK)
        return lax.dot_general(k_ref[pl.ds(k0, MLA_TK), :], q_ref[h * cols:(h + 1) * cols, :],
                               (((1,), (1,)), ((), ())), preferred_element_type=F32)

    m_ref[...] = jnp.full(m_ref.shape, NEG_INF, F32)
    l_ref[...] = jnp.zeros(l_ref.shape, F32)
    acc_ref[...] = jnp.zeros(acc_ref.shape, F32)
    for h in range(MLA_QH):
        s_ref[0, h] = scores(h, 0)

    def step(c, slot):
        c_next = jnp.where(c + 1 < n_chunks, c + 1, 0)
        for h in range(MLA_QH):
            s_ref[1 - slot, h] = scores(h, c_next)
            s = s_ref[slot, h]
            m_old = m_ref[h]
            m_new = jnp.maximum(m_old, jnp.max(s, axis=0, keepdims=True))
            p = jnp.exp2(s - m_new)
            alpha = jnp.exp2(m_old - m_new)
            l_ref[h] = alpha * l_ref[h] + jnp.sum(p, axis=0, keepdims=True)
            pv = jnp.dot(vt_ref[c], p.astype(BF16), preferred_element_type=F32)
            acc_ref[h] = alpha * acc_ref[h] + pv
            m_ref[h] = m_new

    def pair(i, carry):
        step(2 * i, 0)
        step(2 * i + 1, 1)
        return carry

    lax.fori_loop(0, n_chunks // 2, pair, 0)
    for h in range(MLA_QH):
        o_ref[h * cols:(h + 1) * cols, :] = (acc_ref[h] / l_ref[h]).T.astype(o_ref.dtype)


def _mla_attn(q, k, v):
    nq = SEQ // MLA_TQ
    return pl.pallas_call(
        _mla_attn_kernel,
        out_shape=jax.ShapeDtypeStruct((N_TOK, MLA_HEADS * V_DIM), BF16),
        grid=(BATCH, MLA_HEADS, nq),
        in_specs=[
            pl.BlockSpec((None, MLA_TQ, QK_DIM), lambda b, h, i: (h, b * nq + i, 0)),
            pl.BlockSpec((None, SEQ, QK_DIM), lambda b, h, i: (h, b, 0)),
            pl.BlockSpec((None, SEQ // MLA_TK, V_DIM, MLA_TK), lambda b, h, i: (h, b, 0, 0)),
        ],
        out_specs=pl.BlockSpec((MLA_TQ, V_DIM), lambda b, h, i: (b * nq + i, h)),
        scratch_shapes=[
            pltpu.VMEM((2, MLA_QH, MLA_TK, MLA_TQ // MLA_QH), F32),
            pltpu.VMEM((MLA_QH, 1, MLA_TQ // MLA_QH), F32),
            pltpu.VMEM((MLA_QH, 1, MLA_TQ // MLA_QH), F32),
            pltpu.VMEM((MLA_QH, V_DIM, MLA_TQ // MLA_QH), F32),
        ],
        compiler_params=_params("parallel", "parallel", "parallel"),
        name="mla_attn",
    )(q, k, v)


SG_TM = 256


def _sg_kernel(u_ref, v_ref, g_ref, w_ref, bt_ref, o_ref):
    v = jax.nn.gelu(v_ref[...].astype(F32))
    vn = _rms(v, g_ref[...]).astype(BF16)
    u = jax.nn.gelu(u_ref[...].astype(F32))
    gc = SG_CH // SG_GROUPS
    for n in range(SG_TM // CHUNK):
        rows = slice(n * CHUNK, (n + 1) * CHUNK)
        for g in range(SG_GROUPS):
            cols = slice(g * gc, (g + 1) * gc)
            mixed = jnp.dot(w_ref[g], vn[rows, cols], preferred_element_type=F32) + bt_ref[:, g:g + 1]
            o_ref[rows, cols] = (u[rows, cols] * mixed).astype(o_ref.dtype)


def _spatial_gating(p, g_norm, w_s, b_s):
    u_blk = (Q_LORA + KV_LORA) // SG_CH
    return pl.pallas_call(
        _sg_kernel,
        out_shape=jax.ShapeDtypeStruct((N_TOK, SG_CH), BF16),
        grid=(N_TOK // SG_TM,),
        in_specs=[
            pl.BlockSpec((SG_TM, SG_CH), lambda i: (i, u_blk)),
            pl.BlockSpec((SG_TM, SG_CH), lambda i: (i, u_blk + 1)),
            pl.BlockSpec((1, SG_CH), lambda i: (0, 0)),
            pl.BlockSpec((SG_GROUPS, CHUNK, CHUNK), lambda i: (0, 0, 0)),
            pl.BlockSpec((CHUNK, SG_GROUPS), lambda i: (0, 0)),
        ],
        out_specs=pl.BlockSpec((SG_TM, SG_CH), lambda i: (i, 0)),
        compiler_params=_params("parallel"),
        name="spatial_gating",
    )(p, p, g_norm.reshape(1, SG_CH), w_s.astype(BF16), b_s.T)


ROUTER_TM = 512


def _router_kernel(x_ref, g_ref, w_ref, b_ref, o_ref, xn_ref):
    xn = _rms(x_ref[...], g_ref[...])
    xn_ref[...] = xn
    w = w_ref[...]
    xh, wh = xn.astype(BF16), w.astype(BF16)
    xl, wl = (xn - xh.astype(F32)).astype(BF16), (w - wh.astype(F32)).astype(BF16)
    logits = (jnp.dot(xh, wh, preferred_element_type=F32)
              + (jnp.dot(xh, wl, preferred_element_type=F32) + jnp.dot(xl, wh, preferred_element_type=F32))
              + b_ref[...])
    lane = lax.broadcasted_iota(jnp.int32, logits.shape, 1).astype(F32)
    low = jnp.float32(-3.0e38)

    def first_max(vals):
        top = jnp.max(vals, axis=-1, keepdims=True)
        idx = jnp.min(jnp.where(vals == top, lane, float(LANES)), axis=-1, keepdims=True)
        return top, idx

    is_group = lane < N_GROUPS
    g_top, g_idx = first_max(jnp.where(is_group, logits, low))
    g_prob = 1.0 / jnp.sum(jnp.where(is_group, jnp.exp(logits - g_top), 0.0), axis=-1, keepdims=True)
    lo = N_GROUPS + g_idx * EXPERTS_PER_GROUP
    e_vals = jnp.where(lane >= lo, jnp.where(lane < lo + EXPERTS_PER_GROUP, logits, low), low)
    v1, i1 = first_max(e_vals)
    v2, i2 = first_max(jnp.where(lane == i1, low, e_vals))
    e21 = jnp.exp(v2 - v1)
    w1 = g_prob / (1.0 + e21)
    w2 = g_prob * e21 / (1.0 + e21)
    out = jnp.where(lane == 0, i1 - N_GROUPS,
                    jnp.where(lane == 1, i2 - N_GROUPS,
                              jnp.where(lane == 2, w1, jnp.where(lane == 3, w2, 0.0))))
    o_ref[...] = out


def _router(x, g, w, b):
    tm = ROUTER_TM
    return pl.pallas_call(
        _router_kernel,
        out_shape=(jax.ShapeDtypeStruct((N_TOK, LANES), F32), jax.ShapeDtypeStruct((N_TOK, D_MODEL), F32)),
        grid=(N_TOK // tm,),
        in_specs=[
            pl.BlockSpec((tm, D_MODEL), lambda i: (i, 0)),
            pl.BlockSpec((1, D_MODEL), lambda i: (0, 0)),
            pl.BlockSpec((D_MODEL, LANES), lambda i: (0, 0)),
            pl.BlockSpec((1, LANES), lambda i: (0, 0)),
        ],
        out_specs=(pl.BlockSpec((tm, LANES), lambda i: (i, 0)), pl.BlockSpec((tm, D_MODEL), lambda i: (i, 0))),
        compiler_params=_params("parallel"),
        name="router",
    )(x, g.reshape(1, D_MODEL), w, b)


def _dispatch_plan(eid):
    tok = jnp.arange(N_TOK, dtype=jnp.int32)
    keys = jnp.concatenate([eid[:, k] * N_ASSIGN + (k * N_TOK + tok) for k in range(TOP_K)])
    row_asg = jnp.sort(keys) & (N_ASSIGN - 1)
    row_asg = jnp.concatenate([row_asg, jnp.zeros((SUBLANES,), jnp.int32)])
    counts = jnp.sum(eid.reshape(-1)[:, None] == jnp.arange(N_EXPERTS, dtype=jnp.int32)[None, :], axis=0,
                     dtype=jnp.int32)
    starts = jnp.cumsum(counts) - counts
    nblk_e = (counts + MOE_BLOCK - 1) // MOE_BLOCK
    blk_end = jnp.cumsum(nblk_e)
    blk_start = blk_end - nblk_e
    n_active = blk_end[-1]
    bidx = jnp.arange(MOE_NBLK, dtype=jnp.int32)
    blk_x = jnp.minimum(bidx, n_active - 1)
    blk_e = jnp.minimum(jnp.sum(blk_end[None, :] <= blk_x[:, None], axis=1, dtype=jnp.int32), N_EXPERTS - 1)
    first = (blk_x - blk_start[blk_e]) * MOE_BLOCK
    blk_rows = jnp.where(bidx < n_active, jnp.clip(counts[blk_e] - first, 0, MOE_BLOCK), 0).astype(jnp.int32)
    blk_src = (starts[blk_e] + first).astype(jnp.int32)
    return row_asg, blk_e, blk_rows, blk_src


GATHER_STEP = min(1, N_FT - 1)
_ROW_BITS = tuple(1 << k for k in range(MOE_BLOCK.bit_length() - 1, -1, -1))


def _expert_kernel(row_asg, blk_e, blk_rows, blk_src, x_hbm, w1_ref, w3_ref, w2_ref, ys_hbm,
                   xbuf, acc, xn_ref, gsem, ssem):
    b = pl.program_id(0)
    j = pl.program_id(1)
    rows = blk_rows[b]
    slot = b % 2

    def n_fetch(blk):
        return (blk_rows[blk] + SUBLANES - 1) & -SUBLANES

    def issue_gather(blk, s):
        base = blk_src[blk]

        def group(q, c):
            for u in range(SUBLANES):
                t = row_asg[base + q * SUBLANES + u] & (N_TOK - 1)
                pltpu.make_async_copy(x_hbm.at[pl.ds(t, 1), :], xbuf.at[s, q, pl.ds(u, 1), :], gsem.at[s]).start()
            return c

        lax.fori_loop(0, n_fetch(blk) >> SUBLANE_SHIFT, group, 0)

    def wait_rows(n, copy_of):
        for bit in _ROW_BITS:
            @pl.when((n & bit) != 0)
            def _():
                copy_of(bit).wait()

    def gather_block(s):
        def copy_of(k):
            tiles = xbuf.at[s, pl.ds(0, k // SUBLANES)]
            return pltpu.make_async_copy(tiles, tiles, gsem.at[s])
        return copy_of

    def scatter_block(s):
        def copy_of(k):
            part = acc.at[s, pl.ds(0, k // SUBLANES)] if k >= SUBLANES else acc.at[s, 0, pl.ds(0, k), :]
            return pltpu.make_async_copy(part, part, ssem.at[s])
        return copy_of

    def issue_scatter(blk, s, n):
        base = blk_src[blk]

        def one(q, u, a):
            pltpu.make_async_copy(acc.at[s, q, pl.ds(u, 1), :], ys_hbm.at[pl.ds(a, 1), :], ssem.at[s]).start()

        def group(q, c):
            for u in range(SUBLANES):
                one(q, u, row_asg[base + q * SUBLANES + u])
            return c

        def tail(i, c):
            one(i >> SUBLANE_SHIFT, i & (SUBLANES - 1), row_asg[base + i])
            return c

        full = n >> SUBLANE_SHIFT
        lax.fori_loop(0, full, group, 0)
        lax.fori_loop(full * SUBLANES, n, tail, 0)

    def for_row_tile(fn):
        lo = 0
        for m in MOE_ROW_TILES:
            @pl.when(jnp.logical_and(rows > lo, rows <= m))
            def _():
                fn(m)
            lo = m

    nxt = jnp.minimum(b + 1, MOE_NBLK - 1)
    has_next = jnp.logical_and(b + 1 < MOE_NBLK, blk_rows[nxt] > 0)

    @pl.when(rows > 0)
    def _():
        @pl.when(j == 0)
        def _():
            @pl.when(b == 0)
            def _():
                xbuf[...] = jnp.zeros(xbuf.shape, F32)
                acc[...] = jnp.zeros(acc.shape, F32)
                issue_gather(0, 0)

            wait_rows(n_fetch(b), gather_block(slot))

            def cast_tile(m):
                xn_ref[:m, :] = xbuf[slot, :m // SUBLANES].reshape(m, D_MODEL).astype(BF16)

            for_row_tile(cast_tile)

        @pl.when(jnp.logical_and(j == GATHER_STEP, has_next))
        def _():
            issue_gather(nxt, 1 - slot)

        def ffn_tile(m):
            xs = xn_ref[:m, :]
            h1 = jnp.dot(xs, w1_ref[...].astype(BF16), preferred_element_type=F32)
            h3 = jnp.dot(xs, w3_ref[...].astype(BF16), preferred_element_type=F32)
            h = (jax.nn.silu(h1) * h3).astype(BF16)
            y = jnp.dot(h, w2_ref[...].astype(BF16), preferred_element_type=F32)
            mt = m // SUBLANES
            acc[slot, :mt] = jnp.where(j > 0, acc[slot, :mt], 0.0) + y.reshape(mt, SUBLANES, D_MODEL)

        for_row_tile(ffn_tile)

        @pl.when(j == N_FT - 1)
        def _():
            @pl.when(b > 0)
            def _():
                wait_rows(blk_rows[jnp.maximum(b - 1, 0)], scatter_block(1 - slot))

            issue_scatter(b, slot, rows)

            @pl.when(jnp.logical_not(has_next))
            def _():
                wait_rows(rows, scatter_block(slot))


def _moe_experts(xn, w1, w3, w2, layer, row_asg, blk_e, blk_rows, blk_src):
    def w_col(b, j, ra, be, br, bs):
        return (layer, be[b], 0, jnp.where(br[b] > 0, j, N_FT - 1))

    def w_row(b, j, ra, be, br, bs):
        return (layer, be[b], jnp.where(br[b] > 0, j, N_FT - 1), 0)

    return pl.pallas_call(
        _expert_kernel,
        out_shape=jax.ShapeDtypeStruct((N_ASSIGN, D_MODEL), F32),
        grid_spec=pltpu.PrefetchScalarGridSpec(
            num_scalar_prefetch=4,
            grid=(MOE_NBLK, N_FT),
            in_specs=[
                pl.BlockSpec(memory_space=pl.ANY),
                pl.BlockSpec((None, None, D_MODEL, MOE_FT), w_col),
                pl.BlockSpec((None, None, D_MODEL, MOE_FT), w_col),
                pl.BlockSpec((None, None, MOE_FT, D_MODEL), w_row),
            ],
            out_specs=pl.BlockSpec(memory_space=pl.ANY),
            scratch_shapes=[
                pltpu.VMEM((2, MOE_BLOCK // SUBLANES, SUBLANES, D_MODEL), F32),
                pltpu.VMEM((2, MOE_BLOCK // SUBLANES, SUBLANES, D_MODEL), F32),
                pltpu.VMEM((MOE_BLOCK, D_MODEL), BF16),
                pltpu.SemaphoreType.DMA((2,)),
                pltpu.SemaphoreType.DMA((2,)),
            ],
        ),
        compiler_params=_params("arbitrary", "arbitrary"),
        name="moe_experts",
    )(row_asg, blk_e, blk_rows, blk_src, xn, w1, w3, w2)


def _combine_kernel(final_norm, x_ref, y0_ref, y1_ref, gate_ref, gf_ref, o_ref):
    gates = gate_ref[...]
    out = x_ref[...] + (gates[:, 2:3] * y0_ref[...] + gates[:, 3:4] * y1_ref[...])
    if final_norm:
        out = _rms(out, gf_ref[...])
    o_ref[...] = out


def _moe_combine(x, ys, route, g_final, final_norm):
    tm = COMBINE_TOK
    return pl.pallas_call(
        functools.partial(_combine_kernel, final_norm),
        out_shape=jax.ShapeDtypeStruct((N_TOK, D_MODEL), F32),
        grid=(N_TOK // tm,),
        in_specs=[
            pl.BlockSpec((tm, D_MODEL), lambda i: (i, 0)),
            pl.BlockSpec((tm, D_MODEL), lambda i: (i, 0)),
            pl.BlockSpec((tm, D_MODEL), lambda i: (N_TOK // tm + i, 0)),
            pl.BlockSpec((tm, LANES), lambda i: (i, 0)),
            pl.BlockSpec((1, D_MODEL), lambda i: (0, 0)),
        ],
        out_specs=pl.BlockSpec((tm, D_MODEL), lambda i: (i, 0)),
        compiler_params=_params("parallel"),
        name="moe_combine",
    )(x, ys, ys, route, g_final.reshape(1, D_MODEL))


def _hier_moe(x, g, wg, bg, we, be, w1, w3, w2, layer, g_final, final_norm):
    pad = LANES - N_GROUPS - N_EXPERTS
    w_r = jnp.concatenate([wg, we, jnp.zeros((D_MODEL, pad), F32)], axis=1)
    b_r = jnp.concatenate([bg, be, jnp.zeros((pad,), F32)]).reshape(1, LANES)
    route, xn = _router(x, g, w_r, b_r)
    eid = route[:, :TOP_K].astype(jnp.int32)
    row_asg, blk_e, blk_rows, blk_src = _dispatch_plan(eid)
    ys = _moe_experts(xn, w1, w3, w2, layer, row_asg, blk_e, blk_rows, blk_src)
    return _moe_combine(x, ys, route, g_final, final_norm)


def kernel(x, norm_mix, norm_ffn, norm_final, w_in_ab, na_rpb, conv_w, conv_b, w_out_ab, w_in_cd,
           q_norm, kv_norm, w_uq, w_ukv, sg_norm, sg_w, sg_b, w_out_cd, router_group_w,
           router_group_b, router_expert_w, router_expert_b, w1, w3, w2):
    xt = x.reshape(N_TOK, D_MODEL)
    for layer in range(DEPTH):
        i = layer // 2
        if layer % 2 == 0:
            q_scale = np.where(np.arange(AB_IN) < NA_WIDTH, NA_Q_SCALE, 1.0).astype(np.float32)
            p = _norm_matmul(xt, norm_mix[layer], (w_in_ab[i] * q_scale).astype(BF16), tm=1024, tn=512)
            a_out = _natten(p, _natten_bias_table(na_rpb[i]))
            b_out = _gated_conv(p, conv_w[i], conv_b[i])
            xt = _out_proj(a_out, b_out, w_out_ab[i].astype(BF16), xt)
        else:
            p, k_rope = _cd_in_proj(xt, norm_mix[layer], w_in_cd[i])
            wq = w_uq[i].reshape(Q_LORA, MLA_HEADS, QK_DIM)
            wq = jnp.concatenate([wq[:, :, :QK_NOPE].reshape(Q_LORA, NOPE_ALL),
                                  wq[:, :, QK_NOPE:].reshape(Q_LORA, ROPE_ALL)], axis=1).astype(BF16)
            q, k, v = _mla_prep(p, k_rope, q_norm[i], kv_norm[i], wq, w_ukv[i].astype(BF16))
            c_out = _mla_attn(q, k, v)
            d_out = _spatial_gating(p, sg_norm[i], sg_w[i], sg_b[i])
            xt = _out_proj(c_out, d_out, w_out_cd[i].astype(BF16), xt)
        xt = _hier_moe(xt, norm_ffn[layer], router_group_w[layer], router_group_b[layer],
                       router_expert_w[layer], router_expert_b[layer], w1, w3, w2, layer,
                       norm_final, final_norm=(layer == DEPTH - 1))
    return xt.reshape(BATCH, SEQ, D_MODEL)
```

```python
import functools

import numpy as np
import jax
import jax.numpy as jnp
from jax import lax
from jax.experimental import pallas as pl
from jax.experimental.pallas import tpu as pltpu

F32 = jnp.float32
BF16 = jnp.bfloat16

D_MODEL = 2048
BATCH = 2
SEQ = 4096
N_TOK = BATCH * SEQ
DEPTH = 2
GRID_W = 64
GRID_ROWS = SEQ // GRID_W
WIN_ROWS = 8
WIN_COLS = 16
NA_HEADS = 8
NA_HEAD_DIM = 128
NA_WIDTH = NA_HEADS * NA_HEAD_DIM
NA_KEYS = WIN_ROWS * GRID_W
CONV_CH = D_MODEL // 2
MLA_HEADS = 8
Q_LORA = 512
KV_LORA = 512
QK_NOPE = 128
QK_ROPE = 64
QK_DIM = QK_NOPE + QK_ROPE
V_DIM = 128
ROPE_THETA = 10000.0
SG_CH = D_MODEL // 2
SG_GROUPS = 8
CHUNK = 128
N_GROUPS = 8
EXPERTS_PER_GROUP = 8
N_EXPERTS = N_GROUPS * EXPERTS_PER_GROUP
TOP_K = 2
D_EXPERT = 768
EPS = 1e-6
NEG_INF = -1e30
LOG2_E = 1.4426950408889634
AB_IN = 3 * NA_WIDTH + 3 * CONV_CH

LANES = 128
SUBLANES = 8
SUBLANE_SHIFT = 3
MOE_BLOCK = 512
MOE_ROW_TILES = (256, 320, 512)
MOE_FT = 256
N_FT = D_EXPERT // MOE_FT
N_ASSIGN = N_TOK * TOP_K
MOE_NBLK = N_ASSIGN // MOE_BLOCK + N_EXPERTS
COMBINE_TOK = 256
VMEM_LIMIT = 52 * 1024 * 1024


def _params(*sem):
    return pltpu.CompilerParams(dimension_semantics=sem, vmem_limit_bytes=VMEM_LIMIT)


def _rms(x, g):
    return x * lax.rsqrt(jnp.mean(x * x, axis=-1, keepdims=True) + EPS) * g


def _norm_mm_kernel(x_ref, g_ref, w_ref, o_ref, xn_ref):
    @pl.when(pl.program_id(1) == 0)
    def _():
        xn_ref[...] = _rms(x_ref[...].astype(F32), g_ref[...]).astype(BF16)

    o_ref[...] = jnp.dot(xn_ref[...], w_ref[...], preferred_element_type=F32).astype(o_ref.dtype)


def _norm_matmul(x, g, w, tm, tn):
    m, k = x.shape
    nout = w.shape[1]
    return pl.pallas_call(
        _norm_mm_kernel,
        out_shape=jax.ShapeDtypeStruct((m, nout), BF16),
        grid=(m // tm, nout // tn),
        in_specs=[
            pl.BlockSpec((tm, k), lambda i, j: (i, 0)),
            pl.BlockSpec((1, k), lambda i, j: (0, 0)),
            pl.BlockSpec((k, tn), lambda i, j: (0, j)),
        ],
        out_specs=pl.BlockSpec((tm, tn), lambda i, j: (i, j)),
        scratch_shapes=[pltpu.VMEM((tm, k), BF16)],
        compiler_params=_params("parallel", "arbitrary"),
        name="norm_matmul",
    )(x, g.reshape(1, k), w)


CD_TN = 512
CD_MAIN = Q_LORA + KV_LORA + 2 * SG_CH
CD_A_TILES = (Q_LORA + KV_LORA) // CD_TN
CD_MAIN_TILES = CD_MAIN // CD_TN


def _cd_proj_kernel(fused_combine, *refs):
    if fused_combine:
        x_ref, y0_ref, y1_ref, gate_ref, g_ref, wa_ref, wb_ref, wc_ref, main_ref, kr_ref, xout_ref, xn_ref = refs
    else:
        x_ref, g_ref, wa_ref, wb_ref, wc_ref, main_ref, kr_ref, xn_ref = refs
    j = pl.program_id(1)

    @pl.when(j == 0)
    def _():
        x = x_ref[...]
        if fused_combine:
            gates = gate_ref[...]
            x = x + (gates[:, 2:3] * y0_ref[...] + gates[:, 3:4] * y1_ref[...])
            xout_ref[...] = x
        xn_ref[...] = _rms(x, g_ref[...]).astype(BF16)

    @pl.when(j < CD_A_TILES)
    def _():
        main_ref[...] = jnp.dot(xn_ref[...], wa_ref[...], preferred_element_type=F32).astype(BF16)

    @pl.when(j >= CD_A_TILES)
    def _():
        main_ref[...] = jnp.dot(xn_ref[...], wb_ref[...], preferred_element_type=F32).astype(BF16)

    @pl.when(j == 0)
    def _():
        kr_ref[...] = jnp.dot(xn_ref[...], wc_ref[...], preferred_element_type=F32).astype(BF16)


def _cd_in_proj(x, g, w, pending_moe=None):
    c0, c1 = Q_LORA + KV_LORA, Q_LORA + KV_LORA + QK_ROPE
    wa = w[:, :c0].astype(BF16)
    wb = w[:, c1:].astype(BF16)
    wc = jnp.pad(w[:, c0:c1], ((0, 0), (0, LANES - QK_ROPE))).astype(BF16)
    fused = pending_moe is not None
    tm = 512 if fused else 1024
    row = pl.BlockSpec((tm, D_MODEL), lambda i, j: (i, 0))
    w_specs = [
        pl.BlockSpec((1, D_MODEL), lambda i, j: (0, 0)),
        pl.BlockSpec((D_MODEL, CD_TN), lambda i, j: (0, jnp.minimum(j, CD_A_TILES - 1))),
        pl.BlockSpec((D_MODEL, CD_TN), lambda i, j: (0, jnp.maximum(j - CD_A_TILES, 0))),
        pl.BlockSpec((D_MODEL, LANES), lambda i, j: (0, 0)),
    ]
    out_shape = [jax.ShapeDtypeStruct((N_TOK, CD_MAIN), BF16), jax.ShapeDtypeStruct((N_TOK, LANES), BF16)]
    out_specs = [pl.BlockSpec((tm, CD_TN), lambda i, j: (i, j)), pl.BlockSpec((tm, LANES), lambda i, j: (i, 0))]
    if fused:
        ys, route = pending_moe
        in_specs = [row, row, pl.BlockSpec((tm, D_MODEL), lambda i, j: (N_TOK // tm + i, 0)),
                    pl.BlockSpec((tm, LANES), lambda i, j: (i, 0))] + w_specs
        args = (x, ys, ys, route)
        out_shape.append(jax.ShapeDtypeStruct((N_TOK, D_MODEL), F32))
        out_specs.append(row)
    else:
        in_specs = [row] + w_specs
        args = (x,)
    outs = pl.pallas_call(
        functools.partial(_cd_proj_kernel, fused),
        out_shape=tuple(out_shape),
        grid=(N_TOK // tm, CD_MAIN_TILES),
        in_specs=in_specs,
        out_specs=tuple(out_specs),
        scratch_shapes=[pltpu.VMEM((tm, D_MODEL), BF16)],
        compiler_params=_params("parallel", "arbitrary"),
        name="cd_in_proj",
    )(*args, g.reshape(1, D_MODEL), wa, wb, wc)
    return (outs[0], outs[1], outs[2]) if fused else (outs[0], outs[1], x)


def _natten_bias_table(rpb):
    c = np.arange(GRID_W)
    col_start = np.clip(c - WIN_COLS // 2, 0, GRID_W - WIN_COLS)
    valid = (c[None, :] >= col_start[:, None]) & (c[None, :] < col_start[:, None] + WIN_COLS)
    dc = np.clip(c[None, :] - c[:, None] + WIN_COLS - 1, 0, 2 * WIN_COLS - 2)
    pick = (dc[:, :, None] == np.arange(2 * WIN_COLS - 1)).astype(np.float32)
    m = jnp.einsum('hrd,ckd->hcrk', rpb, jnp.asarray(pick), precision=lax.Precision.HIGHEST)
    m = jnp.where(valid[None, :, None, :], m, NEG_INF)
    t = jnp.stack([m[:, :, o:o + WIN_ROWS] for o in range(WIN_ROWS)], axis=1)
    return (t.reshape(rpb.shape[0], WIN_ROWS, GRID_W, NA_KEYS) * LOG2_E).astype(F32)


NA_ROWS_PER_STEP = 16
NA_Q_SCALE = NA_HEAD_DIM ** -0.5 * LOG2_E


def _natten_kernel(q_ref, k_ref, v_ref, t_ref, o_ref):
    def rows(i, carry):
        rs = [i * NA_ROWS_PER_STEP + u for u in range(NA_ROWS_PER_STEP)]
        kr0s = [jnp.clip(r - WIN_ROWS // 2, 0, GRID_ROWS - WIN_ROWS) for r in rs]
        q0s = [pl.multiple_of(r * GRID_W, GRID_W) for r in rs]
        k0s = [pl.multiple_of(kr0 * GRID_W, GRID_W) for kr0 in kr0s]
        ss = [lax.dot_general(q_ref[pl.ds(q0, GRID_W), :], k_ref[pl.ds(k0, NA_KEYS), :],
                              (((1,), (1,)), ((), ())), preferred_element_type=F32)
              for q0, k0 in zip(q0s, k0s)]
        ps, ls = [], []
        for s, r, kr0 in zip(ss, rs, kr0s):
            s = s + t_ref[kr0 - r + WIN_ROWS - 1]
            p = jnp.exp2(s - jnp.max(s, axis=-1, keepdims=True))
            ls.append(jnp.sum(p, axis=-1, keepdims=True))
            ps.append(p.astype(BF16))
        for p, l, q0, k0 in zip(ps, ls, q0s, k0s):
            o = jnp.dot(p, v_ref[pl.ds(k0, NA_KEYS), :], preferred_element_type=F32) / l
            o_ref[pl.ds(q0, GRID_W), :] = o.astype(o_ref.dtype)
        return carry

    lax.fori_loop(0, GRID_ROWS // NA_ROWS_PER_STEP, rows, 0)


def _natten(p, table):
    blk = (SEQ, NA_HEAD_DIM)
    return pl.pallas_call(
        _natten_kernel,
        out_shape=jax.ShapeDtypeStruct((N_TOK, NA_WIDTH), BF16),
        grid=(BATCH, NA_HEADS),
        in_specs=[
            pl.BlockSpec(blk, lambda b, h: (b, h)),
            pl.BlockSpec(blk, lambda b, h: (b, NA_HEADS + h)),
            pl.BlockSpec(blk, lambda b, h: (b, 2 * NA_HEADS + h)),
            pl.BlockSpec((None, WIN_ROWS, GRID_W, NA_KEYS), lambda b, h: (h, 0, 0, 0)),
        ],
        out_specs=pl.BlockSpec(blk, lambda b, h: (b, h)),
        compiler_params=_params("parallel", "parallel"),
        name="natten",
    )(p, p, p, table)


CONV_CB = 128


def _conv_kernel(gb_ref, gc_ref, hc_ref, w_ref, b_ref, o_ref):
    z = gc_ref[...].astype(F32) * hc_ref[...].astype(F32)
    pos = lax.broadcasted_iota(jnp.int32, z.shape, 0)
    z_prev = jnp.where(pos == 0, 0.0, pltpu.roll(z, 1, 0))
    z_next = jnp.where(pos == SEQ - 1, 0.0, pltpu.roll(z, SEQ - 1, 0))
    y = b_ref[...] + z_prev * w_ref[0:1, :] + z * w_ref[1:2, :] + z_next * w_ref[2:3, :]
    o_ref[...] = (gb_ref[...].astype(F32) * y).astype(o_ref.dtype)


def _gated_conv(p, w, b):
    base = 3 * NA_WIDTH // CONV_CB
    step = CONV_CH // CONV_CB
    blk = (SEQ, CONV_CB)
    return pl.pallas_call(
        _conv_kernel,
        out_shape=jax.ShapeDtypeStruct((N_TOK, CONV_CH), BF16),
        grid=(BATCH, step),
        in_specs=[
            pl.BlockSpec(blk, lambda bi, c: (bi, base + c)),
            pl.BlockSpec(blk, lambda bi, c: (bi, base + step + c)),
            pl.BlockSpec(blk, lambda bi, c: (bi, base + 2 * step + c)),
            pl.BlockSpec((3, CONV_CB), lambda bi, c: (0, c)),
            pl.BlockSpec((1, CONV_CB), lambda bi, c: (0, c)),
        ],
        out_specs=pl.BlockSpec(blk, lambda bi, c: (bi, c)),
        compiler_params=_params("parallel", "parallel"),
        name="gated_conv",
    )(p, p, p, w, b.reshape(1, CONV_CH))


def _out_proj_kernel(a_ref, b_ref, wa_ref, wb_ref, r_ref, o_ref):
    acc = jnp.dot(a_ref[...], wa_ref[...], preferred_element_type=F32)
    acc = acc + jnp.dot(b_ref[...], wb_ref[...], preferred_element_type=F32)
    o_ref[...] = r_ref[...] + acc


def _out_proj(a, b, w, res, tm=512):
    ka, kb = a.shape[1], b.shape[1]
    assert ka == kb
    return pl.pallas_call(
        _out_proj_kernel,
        out_shape=jax.ShapeDtypeStruct((N_TOK, D_MODEL), F32),
        grid=(N_TOK // tm,),
        in_specs=[
            pl.BlockSpec((tm, ka), lambda i: (i, 0)),
            pl.BlockSpec((tm, kb), lambda i: (i, 0)),
            pl.BlockSpec((ka, D_MODEL), lambda i: (0, 0)),
            pl.BlockSpec((kb, D_MODEL), lambda i: (1, 0)),
            pl.BlockSpec((tm, D_MODEL), lambda i: (i, 0)),
        ],
        out_specs=pl.BlockSpec((tm, D_MODEL), lambda i: (i, 0)),
        compiler_params=_params("parallel"),
        name="out_proj",
    )(a, b, w, w, res)


MLA_TM = 512


def _rope_tables():
    f32 = np.float32
    pos = np.arange(SEQ)
    row = (pos // GRID_W).astype(f32)
    col = (pos % GRID_W).astype(f32)
    half = QK_ROPE // 2
    inv = np.power(f32(ROPE_THETA), -np.arange(0, half, 2, dtype=f32) / f32(half)).astype(f32)
    ar, ac = row[:, None] * inv, col[:, None] * inv
    cos_t = jnp.asarray(np.concatenate([np.cos(ar), np.cos(ar), np.cos(ac), np.cos(ac)], axis=-1).astype(f32))
    sin_t = jnp.asarray(np.concatenate([-np.sin(ar), np.sin(ar), -np.sin(ac), np.sin(ac)], axis=-1).astype(f32))
    quarter = half // 2
    src = np.arange(QK_ROPE) + np.where((np.arange(QK_ROPE) // quarter) % 2 == 0, quarter, -quarter)
    swap = np.zeros((QK_ROPE, QK_ROPE), np.float32)
    swap[src, np.arange(QK_ROPE)] = 1.0
    swap_all = np.kron(np.eye(MLA_HEADS, dtype=np.float32), swap)
    return cos_t, sin_t, jnp.asarray(swap_all, BF16)


def _rope(x, cos_t, sin_t, swap):
    hi = x.astype(BF16)
    lo = (x - hi.astype(F32)).astype(BF16)
    xs = jnp.dot(hi, swap, preferred_element_type=F32) + jnp.dot(lo, swap, preferred_element_type=F32)
    return x * cos_t + xs * sin_t


ROPE_ALL = MLA_HEADS * QK_ROPE
NOPE_ALL = MLA_HEADS * QK_NOPE


def _mla_prep_kernel(cq_ref, ckv_ref, kr_ref, gq_ref, gkv_ref, wq_ref, wkv_ref, cos_ref, sin_ref,
                     swap_ref, q_ref, k_ref, v_ref):
    scale = QK_DIM ** -0.5 * LOG2_E
    cos_t, sin_t, swap = cos_ref[...], sin_ref[...], swap_ref[...]
    cqn = _rms(cq_ref[...].astype(F32), gq_ref[...]).astype(BF16)
    ckvn = _rms(ckv_ref[...].astype(F32), gkv_ref[...]).astype(BF16)
    kr = kr_ref[:, :QK_ROPE].astype(F32)
    krot = _rope(kr, cos_t, sin_t, swap[:QK_ROPE, :QK_ROPE]).astype(BF16)
    q = jnp.dot(cqn, wq_ref[...], preferred_element_type=F32)
    qn = (q[:, :NOPE_ALL] * scale).astype(BF16)
    cos_all = jnp.concatenate([cos_t] * MLA_HEADS, axis=1)
    sin_all = jnp.concatenate([sin_t] * MLA_HEADS, axis=1)
    qr = (_rope(q[:, NOPE_ALL:], cos_all, sin_all, swap) * scale).astype(BF16)
    kv = jnp.dot(ckvn, wkv_ref[...], preferred_element_type=F32)
    hw = QK_NOPE + V_DIM
    for h in range(MLA_HEADS):
        q_ref[h, :, :QK_NOPE] = qn[:, h * QK_NOPE:(h + 1) * QK_NOPE]
        q_ref[h, :, QK_NOPE:] = qr[:, h * QK_ROPE:(h + 1) * QK_ROPE]
        k_ref[h, :, :QK_NOPE] = kv[:, h * hw:h * hw + QK_NOPE].astype(BF16)
        k_ref[h, :, QK_NOPE:] = krot
        v_ref[h, 0] = kv[:, h * hw + QK_NOPE:(h + 1) * hw].T.astype(BF16)


def _mla_prep(p, k_rope, q_norm, kv_norm, wq, wkv):
    cos_t, sin_t, swap = _rope_tables()
    tm = MLA_TM
    seq_blocks = SEQ // tm
    return pl.pallas_call(
        _mla_prep_kernel,
        out_shape=(
            jax.ShapeDtypeStruct((MLA_HEADS, N_TOK, QK_DIM), BF16),
            jax.ShapeDtypeStruct((MLA_HEADS, N_TOK, QK_DIM), BF16),
            jax.ShapeDtypeStruct((MLA_HEADS, N_TOK // tm, V_DIM, tm), BF16),
        ),
        grid=(N_TOK // tm,),
        in_specs=[
            pl.BlockSpec((tm, Q_LORA), lambda i: (i, 0)),
            pl.BlockSpec((tm, KV_LORA), lambda i: (i, 1)),
            pl.BlockSpec((tm, LANES), lambda i: (i, 0)),
            pl.BlockSpec((1, Q_LORA), lambda i: (0, 0)),
            pl.BlockSpec((1, KV_LORA), lambda i: (0, 0)),
            pl.BlockSpec((Q_LORA, NOPE_ALL + ROPE_ALL), lambda i: (0, 0)),
            pl.BlockSpec((KV_LORA, MLA_HEADS * (QK_NOPE + V_DIM)), lambda i: (0, 0)),
            pl.BlockSpec((tm, QK_ROPE), lambda i: (i % seq_blocks, 0)),
            pl.BlockSpec((tm, QK_ROPE), lambda i: (i % seq_blocks, 0)),
            pl.BlockSpec((ROPE_ALL, ROPE_ALL), lambda i: (0, 0)),
        ],
        out_specs=(
            pl.BlockSpec((MLA_HEADS, tm, QK_DIM), lambda i: (0, i, 0)),
            pl.BlockSpec((MLA_HEADS, tm, QK_DIM), lambda i: (0, i, 0)),
            pl.BlockSpec((MLA_HEADS, 1, V_DIM, tm), lambda i: (0, i, 0, 0)),
        ),
        compiler_params=_params("parallel"),
        name="mla_prep",
    )(p, p, k_rope, q_norm.reshape(1, Q_LORA), kv_norm.reshape(1, KV_LORA), wq, wkv, cos_t, sin_t, swap)


MLA_TQ = 4096
MLA_TK = 512
MLA_QH = 16
assert MLA_TK == MLA_TM


def _mla_attn_kernel(q_ref, k_ref, vt_ref, o_ref, s_ref, m_ref, l_ref, acc_ref):
    n_chunks = SEQ // MLA_TK
    cols = MLA_TQ // MLA_QH

    def scores(h, c):
        k0 = pl.multiple_of(c * MLA_TK, MLA_TK)
        return lax.dot_general(k_ref[pl.ds(k0, MLA_TK), :], q_ref[h * cols:(h + 1) * cols, :],
                               (((1,), (1,)), ((), ())), preferred_element_type=F32)

    m_ref[...] = jnp.full(m_ref.shape, NEG_INF, F32)
    l_ref[...] = jnp.zeros(l_ref.shape, F32)
    acc_ref[...] = jnp.zeros(acc_ref.shape, F32)
    for h in range(MLA_QH):
        s_ref[0, h] = scores(h, 0)

    def step(c, slot):
        c_next = jnp.where(c + 1 < n_chunks, c + 1, 0)
        for h in range(MLA_QH):
            s_ref[1 - slot, h] = scores(h, c_next)
            s = s_ref[slot, h]
            m_old = m_ref[h]
            m_new = jnp.maximum(m_old, jnp.max(s, axis=0, keepdims=True))
            p = jnp.exp2(s - m_new)
            alpha = jnp.exp2(m_old - m_new)
            l_ref[h] = alpha * l_ref[h] + jnp.sum(p, axis=0, keepdims=True)
            pv = jnp.dot(vt_ref[c], p.astype(BF16), preferred_element_type=F32)
            acc_ref[h] = alpha * acc_ref[h] + pv
            m_ref[h] = m_new

    def pair(i, carry):
        step(2 * i, 0)
        step(2 * i + 1, 1)
        return carry

    lax.fori_loop(0, n_chunks // 2, pair, 0)
    for h in range(MLA_QH):
        o_ref[h * cols:(h + 1) * cols, :] = (acc_ref[h] / l_ref[h]).T.astype(o_ref.dtype)


def _mla_attn(q, k, v):
    nq = SEQ // MLA_TQ
    return pl.pallas_call(
        _mla_attn_kernel,
        out_shape=jax.ShapeDtypeStruct((N_TOK, MLA_HEADS * V_DIM), BF16),
        grid=(BATCH, MLA_HEADS, nq),
        in_specs=[
            pl.BlockSpec((None, MLA_TQ, QK_DIM), lambda b, h, i: (h, b * nq + i, 0)),
            pl.BlockSpec((None, SEQ, QK_DIM), lambda b, h, i: (h, b, 0)),
            pl.BlockSpec((None, SEQ // MLA_TK, V_DIM, MLA_TK), lambda b, h, i: (h, b, 0, 0)),
        ],
        out_specs=pl.BlockSpec((MLA_TQ, V_DIM), lambda b, h, i: (b * nq + i, h)),
        scratch_shapes=[
            pltpu.VMEM((2, MLA_QH, MLA_TK, MLA_TQ // MLA_QH), F32),
            pltpu.VMEM((MLA_QH, 1, MLA_TQ // MLA_QH), F32),
            pltpu.VMEM((MLA_QH, 1, MLA_TQ // MLA_QH), F32),
            pltpu.VMEM((MLA_QH, V_DIM, MLA_TQ // MLA_QH), F32),
        ],
        compiler_params=_params("parallel", "parallel", "parallel"),
        name="mla_attn",
    )(q, k, v)


SG_TM = 256


def _sg_kernel(u_ref, v_ref, g_ref, w_ref, bt_ref, o_ref):
    v = jax.nn.gelu(v_ref[...].astype(F32))
    vn = _rms(v, g_ref[...]).astype(BF16)
    u = jax.nn.gelu(u_ref[...].astype(F32))
    gc = SG_CH // SG_GROUPS
    for n in range(SG_TM // CHUNK):
        rows = slice(n * CHUNK, (n + 1) * CHUNK)
        for g in range(SG_GROUPS):
            cols = slice(g * gc, (g + 1) * gc)
            mixed = jnp.dot(w_ref[g], vn[rows, cols], preferred_element_type=F32) + bt_ref[:, g:g + 1]
            o_ref[rows, cols] = (u[rows, cols] * mixed).astype(o_ref.dtype)


def _spatial_gating(p, g_norm, w_s, b_s):
    u_blk = (Q_LORA + KV_LORA) // SG_CH
    return pl.pallas_call(
        _sg_kernel,
        out_shape=jax.ShapeDtypeStruct((N_TOK, SG_CH), BF16),
        grid=(N_TOK // SG_TM,),
        in_specs=[
            pl.BlockSpec((SG_TM, SG_CH), lambda i: (i, u_blk)),
            pl.BlockSpec((SG_TM, SG_CH), lambda i: (i, u_blk + 1)),
            pl.BlockSpec((1, SG_CH), lambda i: (0, 0)),
            pl.BlockSpec((SG_GROUPS, CHUNK, CHUNK), lambda i: (0, 0, 0)),
            pl.BlockSpec((CHUNK, SG_GROUPS), lambda i: (0, 0)),
        ],
        out_specs=pl.BlockSpec((SG_TM, SG_CH), lambda i: (i, 0)),
        compiler_params=_params("parallel"),
        name="spatial_gating",
    )(p, p, g_norm.reshape(1, SG_CH), w_s.astype(BF16), b_s.T)


ROUTER_TM = 512


def _router_kernel(x_ref, g_ref, w_ref, b_ref, o_ref, xn_ref):
    xn = _rms(x_ref[...], g_ref[...])
    xn_ref[...] = xn
    w = w_ref[...]
    xh, wh = xn.astype(BF16), w.astype(BF16)
    xl, wl = (xn - xh.astype(F32)).astype(BF16), (w - wh.astype(F32)).astype(BF16)
    logits = (jnp.dot(xh, wh, preferred_element_type=F32)
              + (jnp.dot(xh, wl, preferred_element_type=F32) + jnp.dot(xl, wh, preferred_element_type=F32))
              + b_ref[...])
    lane = lax.broadcasted_iota(jnp.int32, logits.shape, 1).astype(F32)
    low = jnp.float32(-3.0e38)

    def first_max(vals):
        top = jnp.max(vals, axis=-1, keepdims=True)
        idx = jnp.min(jnp.where(vals == top, lane, float(LANES)), axis=-1, keepdims=True)
        return top, idx

    is_group = lane < N_GROUPS
    g_top, g_idx = first_max(jnp.where(is_group, logits, low))
    g_prob = 1.0 / jnp.sum(jnp.where(is_group, jnp.exp(logits - g_top), 0.0), axis=-1, keepdims=True)
    lo = N_GROUPS + g_idx * EXPERTS_PER_GROUP
    e_vals = jnp.where(lane >= lo, jnp.where(lane < lo + EXPERTS_PER_GROUP, logits, low), low)
    v1, i1 = first_max(e_vals)
    v2, i2 = first_max(jnp.where(lane == i1, low, e_vals))
    e21 = jnp.exp(v2 - v1)
    w1 = g_prob / (1.0 + e21)
    w2 = g_prob * e21 / (1.0 + e21)
    out = jnp.where(lane == 0, i1 - N_GROUPS,
                    jnp.where(lane == 1, i2 - N_GROUPS,
                              jnp.where(lane == 2, w1, jnp.where(lane == 3, w2, 0.0))))
    o_ref[...] = out


def _router(x, g, w, b):
    tm = ROUTER_TM
    return pl.pallas_call(
        _router_kernel,
        out_shape=(jax.ShapeDtypeStruct((N_TOK, LANES), F32), jax.ShapeDtypeStruct((N_TOK, D_MODEL), F32)),
        grid=(N_TOK // tm,),
        in_specs=[
            pl.BlockSpec((tm, D_MODEL), lambda i: (i, 0)),
            pl.BlockSpec((1, D_MODEL), lambda i: (0, 0)),
            pl.BlockSpec((D_MODEL, LANES), lambda i: (0, 0)),
            pl.BlockSpec((1, LANES), lambda i: (0, 0)),
        ],
        out_specs=(pl.BlockSpec((tm, LANES), lambda i: (i, 0)), pl.BlockSpec((tm, D_MODEL), lambda i: (i, 0))),
        compiler_params=_params("parallel"),
        name="router",
    )(x, g.reshape(1, D_MODEL), w, b)


def _dispatch_plan(eid):
    tok = jnp.arange(N_TOK, dtype=jnp.int32)
    keys = jnp.concatenate([eid[:, k] * N_ASSIGN + (k * N_TOK + tok) for k in range(TOP_K)])
    row_asg = jnp.sort(keys) & (N_ASSIGN - 1)
    row_asg = jnp.concatenate([row_asg, jnp.zeros((SUBLANES,), jnp.int32)])
    counts = jnp.sum(eid.reshape(-1)[:, None] == jnp.arange(N_EXPERTS, dtype=jnp.int32)[None, :], axis=0,
                     dtype=jnp.int32)
    starts = jnp.cumsum(counts) - counts
    nblk_e = (counts + MOE_BLOCK - 1) // MOE_BLOCK
    blk_end = jnp.cumsum(nblk_e)
    blk_start = blk_end - nblk_e
    n_active = blk_end[-1]
    bidx = jnp.arange(MOE_NBLK, dtype=jnp.int32)
    blk_x = jnp.minimum(bidx, n_active - 1)
    blk_e = jnp.minimum(jnp.sum(blk_end[None, :] <= blk_x[:, None], axis=1, dtype=jnp.int32), N_EXPERTS - 1)
    first = (blk_x - blk_start[blk_e]) * MOE_BLOCK
    blk_rows = jnp.where(bidx < n_active, jnp.clip(counts[blk_e] - first, 0, MOE_BLOCK), 0).astype(jnp.int32)
    blk_src = (starts[blk_e] + first).astype(jnp.int32)
    return row_asg, blk_e, blk_rows, blk_src


GATHER_STEP = min(1, N_FT - 1)
_ROW_BITS = tuple(1 << k for k in range(MOE_BLOCK.bit_length() - 1, -1, -1))


def _expert_kernel(row_asg, blk_e, blk_rows, blk_src, x_hbm, w1_ref, w3_ref, w2_ref, ys_hbm,
                   xbuf, acc, xn_ref, gsem, ssem):
    b = pl.program_id(0)
    j = pl.program_id(1)
    rows = blk_rows[b]
    slot = b % 2

    def n_fetch(blk):
        return (blk_rows[blk] + SUBLANES - 1) & -SUBLANES

    def issue_gather(blk, s):
        base = blk_src[blk]

        def group(q, c):
            for u in range(SUBLANES):
                t = row_asg[base + q * SUBLANES + u] & (N_TOK - 1)
                pltpu.make_async_copy(x_hbm.at[pl.ds(t, 1), :], xbuf.at[s, q, pl.ds(u, 1), :], gsem.at[s]).start()
            return c

        lax.fori_loop(0, n_fetch(blk) >> SUBLANE_SHIFT, group, 0)

    def wait_rows(n, copy_of):
        for bit in _ROW_BITS:
            @pl.when((n & bit) != 0)
            def _():
                copy_of(bit).wait()

    def gather_block(s):
        def copy_of(k):
            tiles = xbuf.at[s, pl.ds(0, k // SUBLANES)]
            return pltpu.make_async_copy(tiles, tiles, gsem.at[s])
        return copy_of

    def scatter_block(s):
        def copy_of(k):
            part = acc.at[s, pl.ds(0, k // SUBLANES)] if k >= SUBLANES else acc.at[s, 0, pl.ds(0, k), :]
            return pltpu.make_async_copy(part, part, ssem.at[s])
        return copy_of

    def issue_scatter(blk, s, n):
        base = blk_src[blk]

        def one(q, u, a):
            pltpu.make_async_copy(acc.at[s, q, pl.ds(u, 1), :], ys_hbm.at[pl.ds(a, 1), :], ssem.at[s]).start()

        def group(q, c):
            for u in range(SUBLANES):
                one(q, u, row_asg[base + q * SUBLANES + u])
            return c

        def tail(i, c):
            one(i >> SUBLANE_SHIFT, i & (SUBLANES - 1), row_asg[base + i])
            return c

        full = n >> SUBLANE_SHIFT
        lax.fori_loop(0, full, group, 0)
        lax.fori_loop(full * SUBLANES, n, tail, 0)

    def for_row_tile(fn):
        lo = 0
        for m in MOE_ROW_TILES:
            @pl.when(jnp.logical_and(rows > lo, rows <= m))
            def _():
                fn(m)
            lo = m

    nxt = jnp.minimum(b + 1, MOE_NBLK - 1)
    has_next = jnp.logical_and(b + 1 < MOE_NBLK, blk_rows[nxt] > 0)

    @pl.when(rows > 0)
    def _():
        @pl.when(j == 0)
        def _():
            @pl.when(b == 0)
            def _():
                xbuf[...] = jnp.zeros(xbuf.shape, F32)
                acc[...] = jnp.zeros(acc.shape, F32)
                issue_gather(0, 0)

            wait_rows(n_fetch(b), gather_block(slot))

            def cast_tile(m):
                xn_ref[:m, :] = xbuf[slot, :m // SUBLANES].reshape(m, D_MODEL).astype(BF16)

            for_row_tile(cast_tile)

        @pl.when(jnp.logical_and(j == GATHER_STEP, has_next))
        def _():
            issue_gather(nxt, 1 - slot)

        def ffn_tile(m):
            xs = xn_ref[:m, :]
            h1 = jnp.dot(xs, w1_ref[...].astype(BF16), preferred_element_type=F32)
            h3 = jnp.dot(xs, w3_ref[...].astype(BF16), preferred_element_type=F32)
            h = (jax.nn.silu(h1) * h3).astype(BF16)
            y = jnp.dot(h, w2_ref[...].astype(BF16), preferred_element_type=F32)
            mt = m // SUBLANES
            acc[slot, :mt] = jnp.where(j > 0, acc[slot, :mt], 0.0) + y.reshape(mt, SUBLANES, D_MODEL)

        for_row_tile(ffn_tile)

        @pl.when(j == N_FT - 1)
        def _():
            @pl.when(b > 0)
            def _():
                wait_rows(blk_rows[jnp.maximum(b - 1, 0)], scatter_block(1 - slot))

            issue_scatter(b, slot, rows)

            @pl.when(jnp.logical_not(has_next))
            def _():
                wait_rows(rows, scatter_block(slot))


def _moe_experts(xn, w1, w3, w2, layer, row_asg, blk_e, blk_rows, blk_src):
    def w_col(b, j, ra, be, br, bs):
        return (layer, be[b], 0, jnp.where(br[b] > 0, j, N_FT - 1))

    def w_row(b, j, ra, be, br, bs):
        return (layer, be[b], jnp.where(br[b] > 0, j, N_FT - 1), 0)

    return pl.pallas_call(
        _expert_kernel,
        out_shape=jax.ShapeDtypeStruct((N_ASSIGN, D_MODEL), F32),
        grid_spec=pltpu.PrefetchScalarGridSpec(
            num_scalar_prefetch=4,
            grid=(MOE_NBLK, N_FT),
            in_specs=[
                pl.BlockSpec(memory_space=pl.ANY),
                pl.BlockSpec((None, None, D_MODEL, MOE_FT), w_col),
                pl.BlockSpec((None, None, D_MODEL, MOE_FT), w_col),
                pl.BlockSpec((None, None, MOE_FT, D_MODEL), w_row),
            ],
            out_specs=pl.BlockSpec(memory_space=pl.ANY),
            scratch_shapes=[
                pltpu.VMEM((2, MOE_BLOCK // SUBLANES, SUBLANES, D_MODEL), F32),
                pltpu.VMEM((2, MOE_BLOCK // SUBLANES, SUBLANES, D_MODEL), F32),
                pltpu.VMEM((MOE_BLOCK, D_MODEL), BF16),
                pltpu.SemaphoreType.DMA((2,)),
                pltpu.SemaphoreType.DMA((2,)),
            ],
        ),
        compiler_params=_params("arbitrary", "arbitrary"),
        name="moe_experts",
    )(row_asg, blk_e, blk_rows, blk_src, xn, w1, w3, w2)


def _combine_kernel(final_norm, x_ref, y0_ref, y1_ref, gate_ref, gf_ref, o_ref):
    gates = gate_ref[...]
    out = x_ref[...] + (gates[:, 2:3] * y0_ref[...] + gates[:, 3:4] * y1_ref[...])
    if final_norm:
        out = _rms(out, gf_ref[...])
    o_ref[...] = out


def _moe_combine(x, ys, route, g_final, final_norm):
    tm = COMBINE_TOK
    return pl.pallas_call(
        functools.partial(_combine_kernel, final_norm),
        out_shape=jax.ShapeDtypeStruct((N_TOK, D_MODEL), F32),
        grid=(N_TOK // tm,),
        in_specs=[
            pl.BlockSpec((tm, D_MODEL), lambda i: (i, 0)),
            pl.BlockSpec((tm, D_MODEL), lambda i: (i, 0)),
            pl.BlockSpec((tm, D_MODEL), lambda i: (N_TOK // tm + i, 0)),
            pl.BlockSpec((tm, LANES), lambda i: (i, 0)),
            pl.BlockSpec((1, D_MODEL), lambda i: (0, 0)),
        ],
        out_specs=pl.BlockSpec((tm, D_MODEL), lambda i: (i, 0)),
        compiler_params=_params("parallel"),
        name="moe_combine",
    )(x, ys, ys, route, g_final.reshape(1, D_MODEL))


def _hier_moe(x, g, wg, bg, we, be, w1, w3, w2, layer, g_final, final_norm, defer_combine):
    pad = LANES - N_GROUPS - N_EXPERTS
    w_r = jnp.concatenate([wg, we, jnp.zeros((D_MODEL, pad), F32)], axis=1)
    b_r = jnp.concatenate([bg, be, jnp.zeros((pad,), F32)]).reshape(1, LANES)
    route, xn = _router(x, g, w_r, b_r)
    eid = route[:, :TOP_K].astype(jnp.int32)
    row_asg, blk_e, blk_rows, blk_src = _dispatch_plan(eid)
    ys = _moe_experts(xn, w1, w3, w2, layer, row_asg, blk_e, blk_rows, blk_src)
    if defer_combine:
        return x, (ys, route)
    return _moe_combine(x, ys, route, g_final, final_norm), None


def kernel(x, norm_mix, norm_ffn, norm_final, w_in_ab, na_rpb, conv_w, conv_b, w_out_ab, w_in_cd,
           q_norm, kv_norm, w_uq, w_ukv, sg_norm, sg_w, sg_b, w_out_cd, router_group_w,
           router_group_b, router_expert_w, router_expert_b, w1, w3, w2):
    xt = x.reshape(N_TOK, D_MODEL)
    pending = None
    for layer in range(DEPTH):
        i = layer // 2
        if layer % 2 == 0:
            assert pending is None
            q_scale = np.where(np.arange(AB_IN) < NA_WIDTH, NA_Q_SCALE, 1.0).astype(np.float32)
            p = _norm_matmul(xt, norm_mix[layer], (w_in_ab[i] * q_scale).astype(BF16), tm=1024, tn=512)
            a_out = _natten(p, _natten_bias_table(na_rpb[i]))
            b_out = _gated_conv(p, conv_w[i], conv_b[i])
            xt = _out_proj(a_out, b_out, w_out_ab[i].astype(BF16), xt)
        else:
            p, k_rope, xt = _cd_in_proj(xt, norm_mix[layer], w_in_cd[i], pending)
            wq = w_uq[i].reshape(Q_LORA, MLA_HEADS, QK_DIM)
            wq = jnp.concatenate([wq[:, :, :QK_NOPE].reshape(Q_LORA, NOPE_ALL),
                                  wq[:, :, QK_NOPE:].reshape(Q_LORA, ROPE_ALL)], axis=1).astype(BF16)
            q, k, v = _mla_prep(p, k_rope, q_norm[i], kv_norm[i], wq, w_ukv[i].astype(BF16))
            c_out = _mla_attn(q, k, v)
            d_out = _spatial_gating(p, sg_norm[i], sg_w[i], sg_b[i])
            xt = _out_proj(c_out, d_out, w_out_cd[i].astype(BF16), xt)
        last = layer == DEPTH - 1
        xt, pending = _hier_moe(xt, norm_ffn[layer], router_group_w[layer], router_group_b[layer],
                                router_expert_w[layer], router_expert_b[layer], w1, w3, w2, layer,
                                norm_final, final_norm=last, defer_combine=(not last and layer % 2 == 0))
    return xt.reshape(BATCH, SEQ, D_MODEL)
```

```python
import functools

import numpy as np
import jax
import jax.numpy as jnp
from jax import lax
from jax.experimental import pallas as pl
from jax.experimental.pallas import tpu as pltpu

F32 = jnp.float32
BF16 = jnp.bfloat16

D_MODEL = 2048
BATCH = 2
SEQ = 4096
N_TOK = BATCH * SEQ
DEPTH = 2
GRID_W = 64
GRID_ROWS = SEQ // GRID_W
WIN_ROWS = 8
WIN_COLS = 16
NA_HEADS = 8
NA_HEAD_DIM = 128
NA_WIDTH = NA_HEADS * NA_HEAD_DIM
NA_KEYS = WIN_ROWS * GRID_W
CONV_CH = D_MODEL // 2
MLA_HEADS = 8
Q_LORA = 512
KV_LORA = 512
QK_NOPE = 128
QK_ROPE = 64
QK_DIM = QK_NOPE + QK_ROPE
V_DIM = 128
ROPE_THETA = 10000.0
SG_CH = D_MODEL // 2
SG_GROUPS = 8
CHUNK = 128
N_GROUPS = 8
EXPERTS_PER_GROUP = 8
N_EXPERTS = N_GROUPS * EXPERTS_PER_GROUP
TOP_K = 2
D_EXPERT = 768
EPS = 1e-6
NEG_INF = -1e30
LOG2_E = 1.4426950408889634
AB_IN = 3 * NA_WIDTH + 3 * CONV_CH

LANES = 128
SUBLANES = 8
SUBLANE_SHIFT = 3
MOE_BLOCK = 512
MOE_ROW_TILES = (256, 320, 512)
MOE_FT = 256
N_FT = D_EXPERT // MOE_FT
N_ASSIGN = N_TOK * TOP_K
MOE_NBLK = N_ASSIGN // MOE_BLOCK + N_EXPERTS
COMBINE_TOK = 256
VMEM_LIMIT = 52 * 1024 * 1024


def _params(*sem):
    return pltpu.CompilerParams(dimension_semantics=sem, vmem_limit_bytes=VMEM_LIMIT)


def _rms(x, g):
    return x * lax.rsqrt(jnp.mean(x * x, axis=-1, keepdims=True) + EPS) * g


def _norm_mm_kernel(x_ref, g_ref, w_ref, o_ref, xn_ref):
    @pl.when(pl.program_id(1) == 0)
    def _():
        xn_ref[...] = _rms(x_ref[...].astype(F32), g_ref[...]).astype(BF16)

    o_ref[...] = jnp.dot(xn_ref[...], w_ref[...], preferred_element_type=F32).astype(o_ref.dtype)


def _norm_matmul(x, g, w, tm, tn):
    m, k = x.shape
    nout = w.shape[1]
    return pl.pallas_call(
        _norm_mm_kernel,
        out_shape=jax.ShapeDtypeStruct((m, nout), BF16),
        grid=(m // tm, nout // tn),
        in_specs=[
            pl.BlockSpec((tm, k), lambda i, j: (i, 0)),
            pl.BlockSpec((1, k), lambda i, j: (0, 0)),
            pl.BlockSpec((k, tn), lambda i, j: (0, j)),
        ],
        out_specs=pl.BlockSpec((tm, tn), lambda i, j: (i, j)),
        scratch_shapes=[pltpu.VMEM((tm, k), BF16)],
        compiler_params=_params("parallel", "arbitrary"),
        name="norm_matmul",
    )(x, g.reshape(1, k), w)


CD_TN = 512
CD_MAIN = Q_LORA + KV_LORA + 2 * SG_CH
CD_A_TILES = (Q_LORA + KV_LORA) // CD_TN
CD_MAIN_TILES = CD_MAIN // CD_TN


def _cd_proj_kernel(x_ref, g_ref, wa_ref, wb_ref, wc_ref, main_ref, kr_ref, xn_ref):
    j = pl.program_id(1)

    @pl.when(j == 0)
    def _():
        xn_ref[...] = _rms(x_ref[...], g_ref[...]).astype(BF16)

    @pl.when(j < CD_A_TILES)
    def _():
        main_ref[...] = jnp.dot(xn_ref[...], wa_ref[...], preferred_element_type=F32).astype(BF16)

    @pl.when(j >= CD_A_TILES)
    def _():
        main_ref[...] = jnp.dot(xn_ref[...], wb_ref[...], preferred_element_type=F32).astype(BF16)

    @pl.when(j == 0)
    def _():
        kr_ref[...] = jnp.dot(xn_ref[...], wc_ref[...], preferred_element_type=F32).astype(BF16)


def _cd_in_proj(x, g, w, tm=1024):
    c0, c1 = Q_LORA + KV_LORA, Q_LORA + KV_LORA + QK_ROPE
    wa = w[:, :c0].astype(BF16)
    wb = w[:, c1:].astype(BF16)
    wc = jnp.pad(w[:, c0:c1], ((0, 0), (0, LANES - QK_ROPE))).astype(BF16)
    return pl.pallas_call(
        _cd_proj_kernel,
        out_shape=(jax.ShapeDtypeStruct((N_TOK, CD_MAIN), BF16), jax.ShapeDtypeStruct((N_TOK, LANES), BF16)),
        grid=(N_TOK // tm, CD_MAIN_TILES),
        in_specs=[
            pl.BlockSpec((tm, D_MODEL), lambda i, j: (i, 0)),
            pl.BlockSpec((1, D_MODEL), lambda i, j: (0, 0)),
            pl.BlockSpec((D_MODEL, CD_TN), lambda i, j: (0, jnp.minimum(j, CD_A_TILES - 1))),
            pl.BlockSpec((D_MODEL, CD_TN), lambda i, j: (0, jnp.maximum(j - CD_A_TILES, 0))),
            pl.BlockSpec((D_MODEL, LANES), lambda i, j: (0, 0)),
        ],
        out_specs=(
            pl.BlockSpec((tm, CD_TN), lambda i, j: (i, j)),
            pl.BlockSpec((tm, LANES), lambda i, j: (i, 0)),
        ),
        scratch_shapes=[pltpu.VMEM((tm, D_MODEL), BF16)],
        compiler_params=_params("parallel", "arbitrary"),
        name="cd_in_proj",
    )(x, g.reshape(1, D_MODEL), wa, wb, wc)


def _natten_bias_table(rpb):
    c = np.arange(GRID_W)
    col_start = np.clip(c - WIN_COLS // 2, 0, GRID_W - WIN_COLS)
    valid = (c[None, :] >= col_start[:, None]) & (c[None, :] < col_start[:, None] + WIN_COLS)
    dc = np.clip(c[None, :] - c[:, None] + WIN_COLS - 1, 0, 2 * WIN_COLS - 2)
    pick = (dc[:, :, None] == np.arange(2 * WIN_COLS - 1)).astype(np.float32)
    m = jnp.einsum('hrd,ckd->hcrk', rpb, jnp.asarray(pick), precision=lax.Precision.HIGHEST)
    m = jnp.where(valid[None, :, None, :], m, NEG_INF)
    t = jnp.stack([m[:, :, o:o + WIN_ROWS] for o in range(WIN_ROWS)], axis=1)
    return (t.reshape(rpb.shape[0], WIN_ROWS, GRID_W, NA_KEYS) * LOG2_E).astype(F32)


NA_ROWS_PER_STEP = 16
NA_Q_SCALE = NA_HEAD_DIM ** -0.5 * LOG2_E


def _natten_kernel(q_ref, k_ref, v_ref, t_ref, o_ref):
    def rows(i, carry):
        rs = [i * NA_ROWS_PER_STEP + u for u in range(NA_ROWS_PER_STEP)]
        kr0s = [jnp.clip(r - WIN_ROWS // 2, 0, GRID_ROWS - WIN_ROWS) for r in rs]
        q0s = [pl.multiple_of(r * GRID_W, GRID_W) for r in rs]
        k0s = [pl.multiple_of(kr0 * GRID_W, GRID_W) for kr0 in kr0s]
        ss = [lax.dot_general(q_ref[pl.ds(q0, GRID_W), :], k_ref[pl.ds(k0, NA_KEYS), :],
                              (((1,), (1,)), ((), ())), preferred_element_type=F32)
              for q0, k0 in zip(q0s, k0s)]
        ps, ls = [], []
        for s, r, kr0 in zip(ss, rs, kr0s):
            s = s + t_ref[kr0 - r + WIN_ROWS - 1]
            p = jnp.exp2(s - jnp.max(s, axis=-1, keepdims=True))
            ls.append(jnp.sum(p, axis=-1, keepdims=True))
            ps.append(p.astype(BF16))
        for p, l, q0, k0 in zip(ps, ls, q0s, k0s):
            o = jnp.dot(p, v_ref[pl.ds(k0, NA_KEYS), :], preferred_element_type=F32) / l
            o_ref[pl.ds(q0, GRID_W), :] = o.astype(o_ref.dtype)
        return carry

    lax.fori_loop(0, GRID_ROWS // NA_ROWS_PER_STEP, rows, 0)


def _natten(p, table):
    blk = (SEQ, NA_HEAD_DIM)
    return pl.pallas_call(
        _natten_kernel,
        out_shape=jax.ShapeDtypeStruct((N_TOK, NA_WIDTH), BF16),
        grid=(BATCH, NA_HEADS),
        in_specs=[
            pl.BlockSpec(blk, lambda b, h: (b, h)),
            pl.BlockSpec(blk, lambda b, h: (b, NA_HEADS + h)),
            pl.BlockSpec(blk, lambda b, h: (b, 2 * NA_HEADS + h)),
            pl.BlockSpec((None, WIN_ROWS, GRID_W, NA_KEYS), lambda b, h: (h, 0, 0, 0)),
        ],
        out_specs=pl.BlockSpec(blk, lambda b, h: (b, h)),
        compiler_params=_params("parallel", "parallel"),
        name="natten",
    )(p, p, p, table)


CONV_CB = 128


def _conv_kernel(gb_ref, gc_ref, hc_ref, w_ref, b_ref, o_ref):
    z = gc_ref[...].astype(F32) * hc_ref[...].astype(F32)
    pos = lax.broadcasted_iota(jnp.int32, z.shape, 0)
    z_prev = jnp.where(pos == 0, 0.0, pltpu.roll(z, 1, 0))
    z_next = jnp.where(pos == SEQ - 1, 0.0, pltpu.roll(z, SEQ - 1, 0))
    y = b_ref[...] + z_prev * w_ref[0:1, :] + z * w_ref[1:2, :] + z_next * w_ref[2:3, :]
    o_ref[...] = (gb_ref[...].astype(F32) * y).astype(o_ref.dtype)


def _gated_conv(p, w, b):
    base = 3 * NA_WIDTH // CONV_CB
    step = CONV_CH // CONV_CB
    blk = (SEQ, CONV_CB)
    return pl.pallas_call(
        _conv_kernel,
        out_shape=jax.ShapeDtypeStruct((N_TOK, CONV_CH), BF16),
        grid=(BATCH, step),
        in_specs=[
            pl.BlockSpec(blk, lambda bi, c: (bi, base + c)),
            pl.BlockSpec(blk, lambda bi, c: (bi, base + step + c)),
            pl.BlockSpec(blk, lambda bi, c: (bi, base + 2 * step + c)),
            pl.BlockSpec((3, CONV_CB), lambda bi, c: (0, c)),
            pl.BlockSpec((1, CONV_CB), lambda bi, c: (0, c)),
        ],
        out_specs=pl.BlockSpec(blk, lambda bi, c: (bi, c)),
        compiler_params=_params("parallel", "parallel"),
        name="gated_conv",
    )(p, p, p, w, b.reshape(1, CONV_CH))


def _out_proj_kernel(a_ref, b_ref, wa_ref, wb_ref, r_ref, o_ref):
    acc = jnp.dot(a_ref[...], wa_ref[...], preferred_element_type=F32)
    acc = acc + jnp.dot(b_ref[...], wb_ref[...], preferred_element_type=F32)
    o_ref[...] = r_ref[...] + acc


def _out_proj(a, b, w, res, tm=512):
    ka, kb = a.shape[1], b.shape[1]
    assert ka == kb
    return pl.pallas_call(
        _out_proj_kernel,
        out_shape=jax.ShapeDtypeStruct((N_TOK, D_MODEL), F32),
        grid=(N_TOK // tm,),
        in_specs=[
            pl.BlockSpec((tm, ka), lambda i: (i, 0)),
            pl.BlockSpec((tm, kb), lambda i: (i, 0)),
            pl.BlockSpec((ka, D_MODEL), lambda i: (0, 0)),
            pl.BlockSpec((kb, D_MODEL), lambda i: (1, 0)),
            pl.BlockSpec((tm, D_MODEL), lambda i: (i, 0)),
        ],
        out_specs=pl.BlockSpec((tm, D_MODEL), lambda i: (i, 0)),
        compiler_params=_params("parallel"),
        name="out_proj",
    )(a, b, w, w, res)


MLA_TM = 512


def _rope_tables():
    f32 = np.float32
    pos = np.arange(SEQ)
    row = (pos // GRID_W).astype(f32)
    col = (pos % GRID_W).astype(f32)
    half = QK_ROPE // 2
    inv = np.power(f32(ROPE_THETA), -np.arange(0, half, 2, dtype=f32) / f32(half)).astype(f32)
    ar, ac = row[:, None] * inv, col[:, None] * inv
    cos_t = jnp.asarray(np.concatenate([np.cos(ar), np.cos(ar), np.cos(ac), np.cos(ac)], axis=-1).astype(f32))
    sin_t = jnp.asarray(np.concatenate([-np.sin(ar), np.sin(ar), -np.sin(ac), np.sin(ac)], axis=-1).astype(f32))
    quarter = half // 2
    src = np.arange(QK_ROPE) + np.where((np.arange(QK_ROPE) // quarter) % 2 == 0, quarter, -quarter)
    swap = np.zeros((QK_ROPE, QK_ROPE), np.float32)
    swap[src, np.arange(QK_ROPE)] = 1.0
    swap_all = np.kron(np.eye(MLA_HEADS, dtype=np.float32), swap)
    return cos_t, sin_t, jnp.asarray(swap_all, BF16)


def _rope(x, cos_t, sin_t, swap):
    hi = x.astype(BF16)
    lo = (x - hi.astype(F32)).astype(BF16)
    xs = jnp.dot(hi, swap, preferred_element_type=F32) + jnp.dot(lo, swap, preferred_element_type=F32)
    return x * cos_t + xs * sin_t


ROPE_ALL = MLA_HEADS * QK_ROPE
NOPE_ALL = MLA_HEADS * QK_NOPE


def _mla_prep_kernel(cq_ref, ckv_ref, kr_ref, gq_ref, gkv_ref, wq_ref, wkv_ref, cos_ref, sin_ref,
                     swap_ref, q_ref, k_ref, v_ref):
    scale = QK_DIM ** -0.5 * LOG2_E
    cos_t, sin_t, swap = cos_ref[...], sin_ref[...], swap_ref[...]
    cqn = _rms(cq_ref[...].astype(F32), gq_ref[...]).astype(BF16)
    ckvn = _rms(ckv_ref[...].astype(F32), gkv_ref[...]).astype(BF16)
    kr = kr_ref[:, :QK_ROPE].astype(F32)
    krot = _rope(kr, cos_t, sin_t, swap[:QK_ROPE, :QK_ROPE]).astype(BF16)
    q = jnp.dot(cqn, wq_ref[...], preferred_element_type=F32)
    qn = (q[:, :NOPE_ALL] * scale).astype(BF16)
    cos_all = jnp.concatenate([cos_t] * MLA_HEADS, axis=1)
    sin_all = jnp.concatenate([sin_t] * MLA_HEADS, axis=1)
    qr = (_rope(q[:, NOPE_ALL:], cos_all, sin_all, swap) * scale).astype(BF16)
    kv = jnp.dot(ckvn, wkv_ref[...], preferred_element_type=F32)
    hw = QK_NOPE + V_DIM
    for h in range(MLA_HEADS):
        q_ref[h, :, :QK_NOPE] = qn[:, h * QK_NOPE:(h + 1) * QK_NOPE]
        q_ref[h, :, QK_NOPE:] = qr[:, h * QK_ROPE:(h + 1) * QK_ROPE]
        k_ref[h, :, :QK_NOPE] = kv[:, h * hw:h * hw + QK_NOPE].astype(BF16)
        k_ref[h, :, QK_NOPE:] = krot
        v_ref[h, 0] = kv[:, h * hw + QK_NOPE:(h + 1) * hw].T.astype(BF16)


def _mla_prep(p, k_rope, q_norm, kv_norm, wq, wkv):
    cos_t, sin_t, swap = _rope_tables()
    tm = MLA_TM
    seq_blocks = SEQ // tm
    return pl.pallas_call(
        _mla_prep_kernel,
        out_shape=(
            jax.ShapeDtypeStruct((MLA_HEADS, N_TOK, QK_DIM), BF16),
            jax.ShapeDtypeStruct((MLA_HEADS, N_TOK, QK_DIM), BF16),
            jax.ShapeDtypeStruct((MLA_HEADS, N_TOK // tm, V_DIM, tm), BF16),
        ),
        grid=(N_TOK // tm,),
        in_specs=[
            pl.BlockSpec((tm, Q_LORA), lambda i: (i, 0)),
            pl.BlockSpec((tm, KV_LORA), lambda i: (i, 1)),
            pl.BlockSpec((tm, LANES), lambda i: (i, 0)),
            pl.BlockSpec((1, Q_LORA), lambda i: (0, 0)),
            pl.BlockSpec((1, KV_LORA), lambda i: (0, 0)),
            pl.BlockSpec((Q_LORA, NOPE_ALL + ROPE_ALL), lambda i: (0, 0)),
            pl.BlockSpec((KV_LORA, MLA_HEADS * (QK_NOPE + V_DIM)), lambda i: (0, 0)),
            pl.BlockSpec((tm, QK_ROPE), lambda i: (i % seq_blocks, 0)),
            pl.BlockSpec((tm, QK_ROPE), lambda i: (i % seq_blocks, 0)),
            pl.BlockSpec((ROPE_ALL, ROPE_ALL), lambda i: (0, 0)),
        ],
        out_specs=(
            pl.BlockSpec((MLA_HEADS, tm, QK_DIM), lambda i: (0, i, 0)),
            pl.BlockSpec((MLA_HEADS, tm, QK_DIM), lambda i: (0, i, 0)),
            pl.BlockSpec((MLA_HEADS, 1, V_DIM, tm), lambda i: (0, i, 0, 0)),
        ),
        compiler_params=_params("parallel"),
        name="mla_prep",
    )(p, p, k_rope, q_norm.reshape(1, Q_LORA), kv_norm.reshape(1, KV_LORA), wq, wkv, cos_t, sin_t, swap)


MLA_TQ = 4096
MLA_TK = 512
MLA_QH = 16
assert MLA_TK == MLA_TM


def _mla_attn_kernel(q_ref, k_ref, vt_ref, o_ref, s_ref, m_ref, l_ref, acc_ref):
    n_chunks = SEQ // MLA_TK
    cols = MLA_TQ // MLA_QH

    def scores(h, c):
        k0 = pl.multiple_of(c * MLA_TK, MLA_TK)
        return lax.dot_general(k_ref[pl.ds(k0, MLA_TK), :], q_ref[h * cols:(h + 1) * cols, :],
                               (((1,), (1,)), ((), ())), preferred_element_type=F32)

    m_ref[...] = jnp.full(m_ref.shape, NEG_INF, F32)
    l_ref[...] = jnp.zeros(l_ref.shape, F32)
    acc_ref[...] = jnp.zeros(acc_ref.shape, F32)
    for h in range(MLA_QH):
        s_ref[0, h] = scores(h, 0)

    def step(c, slot):
        c_next = jnp.where(c + 1 < n_chunks, c + 1, 0)
        for h in range(MLA_QH):
            s_ref[1 - slot, h] = scores(h, c_next)
            s = s_ref[slot, h]
            m_old = m_ref[h]
            m_new = jnp.maximum(m_old, jnp.max(s, axis=0, keepdims=True))
            p = jnp.exp2(s - m_new)
            alpha = jnp.exp2(m_old - m_new)
            l_ref[h] = alpha * l_ref[h] + jnp.sum(p, axis=0, keepdims=True)
            pv = jnp.dot(vt_ref[c], p.astype(BF16), preferred_element_type=F32)
            acc_ref[h] = alpha * acc_ref[h] + pv
            m_ref[h] = m_new

    def pair(i, carry):
        step(2 * i, 0)
        step(2 * i + 1, 1)
        return carry

    lax.fori_loop(0, n_chunks // 2, pair, 0)
    for h in range(MLA_QH):
        o_ref[h * cols:(h + 1) * cols, :] = (acc_ref[h] / l_ref[h]).T.astype(o_ref.dtype)


def _mla_attn(q, k, v):
    nq = SEQ // MLA_TQ
    return pl.pallas_call(
        _mla_attn_kernel,
        out_shape=jax.ShapeDtypeStruct((N_TOK, MLA_HEADS * V_DIM), BF16),
        grid=(BATCH, MLA_HEADS, nq),
        in_specs=[
            pl.BlockSpec((None, MLA_TQ, QK_DIM), lambda b, h, i: (h, b * nq + i, 0)),
            pl.BlockSpec((None, SEQ, QK_DIM), lambda b, h, i: (h, b, 0)),
            pl.BlockSpec((None, SEQ // MLA_TK, V_DIM, MLA_TK), lambda b, h, i: (h, b, 0, 0)),
        ],
        out_specs=pl.BlockSpec((MLA_TQ, V_DIM), lambda b, h, i: (b * nq + i, h)),
        scratch_shapes=[
            pltpu.VMEM((2, MLA_QH, MLA_TK, MLA_TQ // MLA_QH), F32),
            pltpu.VMEM((MLA_QH, 1, MLA_TQ // MLA_QH), F32),
            pltpu.VMEM((MLA_QH, 1, MLA_TQ // MLA_QH), F32),
            pltpu.VMEM((MLA_QH, V_DIM, MLA_TQ // MLA_QH), F32),
        ],
        compiler_params=_params("parallel", "parallel", "parallel"),
        name="mla_attn",
    )(q, k, v)


SG_TM = 256


def _sg_kernel(u_ref, v_ref, g_ref, w_ref, bt_ref, o_ref):
    v = jax.nn.gelu(v_ref[...].astype(F32))
    vn = _rms(v, g_ref[...]).astype(BF16)
    u = jax.nn.gelu(u_ref[...].astype(F32))
    gc = SG_CH // SG_GROUPS
    for n in range(SG_TM // CHUNK):
        rows = slice(n * CHUNK, (n + 1) * CHUNK)
        for g in range(SG_GROUPS):
            cols = slice(g * gc, (g + 1) * gc)
            mixed = jnp.dot(w_ref[g], vn[rows, cols], preferred_element_type=F32) + bt_ref[:, g:g + 1]
            o_ref[rows, cols] = (u[rows, cols] * mixed).astype(o_ref.dtype)


def _spatial_gating(p, g_norm, w_s, b_s):
    u_blk = (Q_LORA + KV_LORA) // SG_CH
    return pl.pallas_call(
        _sg_kernel,
        out_shape=jax.ShapeDtypeStruct((N_TOK, SG_CH), BF16),
        grid=(N_TOK // SG_TM,),
        in_specs=[
            pl.BlockSpec((SG_TM, SG_CH), lambda i: (i, u_blk)),
            pl.BlockSpec((SG_TM, SG_CH), lambda i: (i, u_blk + 1)),
            pl.BlockSpec((1, SG_CH), lambda i: (0, 0)),
            pl.BlockSpec((SG_GROUPS, CHUNK, CHUNK), lambda i: (0, 0, 0)),
            pl.BlockSpec((CHUNK, SG_GROUPS), lambda i: (0, 0)),
        ],
        out_specs=pl.BlockSpec((SG_TM, SG_CH), lambda i: (i, 0)),
        compiler_params=_params("parallel"),
        name="spatial_gating",
    )(p, p, g_norm.reshape(1, SG_CH), w_s.astype(BF16), b_s.T)


ROUTER_TM = 512


def _router_kernel(x_ref, g_ref, w_ref, b_ref, o_ref, xn_ref):
    xn = _rms(x_ref[...], g_ref[...])
    xn_ref[...] = xn
    w = w_ref[...]
    xh, wh = xn.astype(BF16), w.astype(BF16)
    xl, wl = (xn - xh.astype(F32)).astype(BF16), (w - wh.astype(F32)).astype(BF16)
    logits = (jnp.dot(xh, wh, preferred_element_type=F32)
              + (jnp.dot(xh, wl, preferred_element_type=F32) + jnp.dot(xl, wh, preferred_element_type=F32))
              + b_ref[...])
    lane = lax.broadcasted_iota(jnp.int32, logits.shape, 1).astype(F32)
    low = jnp.float32(-3.0e38)

    def first_max(vals):
        top = jnp.max(vals, axis=-1, keepdims=True)
        idx = jnp.min(jnp.where(vals == top, lane, float(LANES)), axis=-1, keepdims=True)
        return top, idx

    is_group = lane < N_GROUPS
    g_top, g_idx = first_max(jnp.where(is_group, logits, low))
    g_prob = 1.0 / jnp.sum(jnp.where(is_group, jnp.exp(logits - g_top), 0.0), axis=-1, keepdims=True)
    lo = N_GROUPS + g_idx * EXPERTS_PER_GROUP
    e_vals = jnp.where(lane >= lo, jnp.where(lane < lo + EXPERTS_PER_GROUP, logits, low), low)
    v1, i1 = first_max(e_vals)
    v2, i2 = first_max(jnp.where(lane == i1, low, e_vals))
    e21 = jnp.exp(v2 - v1)
    w1 = g_prob / (1.0 + e21)
    w2 = g_prob * e21 / (1.0 + e21)
    out = jnp.where(lane == 0, i1 - N_GROUPS,
                    jnp.where(lane == 1, i2 - N_GROUPS,
                              jnp.where(lane == 2, w1, jnp.where(lane == 3, w2, 0.0))))
    o_ref[...] = out


def _router(x, g, w, b):
    tm = ROUTER_TM
    return pl.pallas_call(
        _router_kernel,
        out_shape=(jax.ShapeDtypeStruct((N_TOK, LANES), F32), jax.ShapeDtypeStruct((N_TOK, D_MODEL), F32)),
        grid=(N_TOK // tm,),
        in_specs=[
            pl.BlockSpec((tm, D_MODEL), lambda i: (i, 0)),
            pl.BlockSpec((1, D_MODEL), lambda i: (0, 0)),
            pl.BlockSpec((D_MODEL, LANES), lambda i: (0, 0)),
            pl.BlockSpec((1, LANES), lambda i: (0, 0)),
        ],
        out_specs=(pl.BlockSpec((tm, LANES), lambda i: (i, 0)), pl.BlockSpec((tm, D_MODEL), lambda i: (i, 0))),
        compiler_params=_params("parallel"),
        name="router",
    )(x, g.reshape(1, D_MODEL), w, b)


def _dispatch_plan(eid):
    tok = jnp.arange(N_TOK, dtype=jnp.int32)
    keys = jnp.sort(jnp.concatenate([eid[:, k] * N_ASSIGN + (k * N_TOK + tok) for k in range(TOP_K)]))
    asg_sorted = keys & (N_ASSIGN - 1)
    row_asg = jnp.concatenate([asg_sorted, jnp.zeros((SUBLANES,), jnp.int32)])
    counts = jnp.sum(eid.reshape(-1)[:, None] == jnp.arange(N_EXPERTS, dtype=jnp.int32)[None, :], axis=0,
                     dtype=jnp.int32)
    starts = jnp.cumsum(counts) - counts
    padded = (counts + SUBLANES - 1) // SUBLANES * SUBLANES
    ys_start = jnp.cumsum(padded) - padded
    ys_pos = jnp.arange(N_ASSIGN, dtype=jnp.int32) + (ys_start - starts)[keys // N_ASSIGN]
    _, ys_row = lax.sort((asg_sorted, ys_pos), num_keys=1)
    nblk_e = (counts + MOE_BLOCK - 1) // MOE_BLOCK
    blk_end = jnp.cumsum(nblk_e)
    blk_start = blk_end - nblk_e
    n_active = blk_end[-1]
    bidx = jnp.arange(MOE_NBLK, dtype=jnp.int32)
    blk_x = jnp.minimum(bidx, n_active - 1)
    blk_e = jnp.minimum(jnp.sum(blk_end[None, :] <= blk_x[:, None], axis=1, dtype=jnp.int32), N_EXPERTS - 1)
    first = (blk_x - blk_start[blk_e]) * MOE_BLOCK
    blk_rows = jnp.where(bidx < n_active, jnp.clip(counts[blk_e] - first, 0, MOE_BLOCK), 0).astype(jnp.int32)
    blk_src = (starts[blk_e] + first).astype(jnp.int32)
    blk_dst = (ys_start[blk_e] + first).astype(jnp.int32)
    return row_asg, blk_e, blk_rows, blk_src, blk_dst, ys_row.astype(jnp.int32)


GATHER_STEP = min(1, N_FT - 1)
BLOCK_TILES = MOE_BLOCK // SUBLANES
YS_TILES = (N_ASSIGN + N_EXPERTS * SUBLANES) // SUBLANES
_TILE_BITS = tuple(1 << k for k in range(BLOCK_TILES.bit_length() - 1, -1, -1))


def _expert_kernel(row_asg, blk_e, blk_rows, blk_src, blk_dst, x_hbm, w1_ref, w3_ref, w2_ref, ys_hbm,
                   xbuf, acc, xn_ref, gsem, ssem):
    b = pl.program_id(0)
    j = pl.program_id(1)
    rows = blk_rows[b]
    slot = b % 2

    def n_fetch(blk):
        return (blk_rows[blk] + SUBLANES - 1) & -SUBLANES

    def for_tile_runs(n_tiles, fn):
        off = 0
        for bit in _TILE_BITS:
            @pl.when((n_tiles & bit) != 0)
            def _():
                fn(off, bit)
            off = off + (n_tiles & bit)

    def ys_write(blk, s):
        dst = blk_dst[blk] >> SUBLANE_SHIFT

        def copy_of(off, size):
            return pltpu.make_async_copy(acc.at[s, pl.ds(off, size)], ys_hbm.at[pl.ds(dst + off, size)], ssem.at[s])
        return n_fetch(blk) >> SUBLANE_SHIFT, copy_of

    def issue_gather(blk, s):
        base = blk_src[blk]

        def group(q, c):
            for u in range(SUBLANES):
                t = row_asg[base + q * SUBLANES + u] & (N_TOK - 1)
                pltpu.make_async_copy(x_hbm.at[pl.ds(t, 1), :], xbuf.at[s, q, pl.ds(u, 1), :], gsem.at[s]).start()
            return c

        lax.fori_loop(0, n_fetch(blk) >> SUBLANE_SHIFT, group, 0)

    def wait_gather(blk, s):
        def wait(off, size):
            tiles = xbuf.at[s, pl.ds(0, size)]
            pltpu.make_async_copy(tiles, tiles, gsem.at[s]).wait()
        for_tile_runs(n_fetch(blk) >> SUBLANE_SHIFT, wait)

    def for_row_tile(fn):
        lo = 0
        for m in MOE_ROW_TILES:
            @pl.when(jnp.logical_and(rows > lo, rows <= m))
            def _():
                fn(m)
            lo = m

    nxt = jnp.minimum(b + 1, MOE_NBLK - 1)
    has_next = jnp.logical_and(b + 1 < MOE_NBLK, blk_rows[nxt] > 0)

    @pl.when(rows > 0)
    def _():
        @pl.when(j == 0)
        def _():
            @pl.when(b == 0)
            def _():
                xbuf[...] = jnp.zeros(xbuf.shape, F32)
                acc[...] = jnp.zeros(acc.shape, F32)
                issue_gather(0, 0)

            wait_gather(b, slot)

            def cast_tile(m):
                xn_ref[:m, :] = xbuf[slot, :m // SUBLANES].reshape(m, D_MODEL).astype(BF16)

            for_row_tile(cast_tile)

        @pl.when(jnp.logical_and(j == GATHER_STEP, has_next))
        def _():
            issue_gather(nxt, 1 - slot)

        def ffn_tile(m):
            xs = xn_ref[:m, :]
            h1 = jnp.dot(xs, w1_ref[...].astype(BF16), preferred_element_type=F32)
            h3 = jnp.dot(xs, w3_ref[...].astype(BF16), preferred_element_type=F32)
            h = (jax.nn.silu(h1) * h3).astype(BF16)
            y = jnp.dot(h, w2_ref[...].astype(BF16), preferred_element_type=F32)
            mt = m // SUBLANES
            acc[slot, :mt] = jnp.where(j > 0, acc[slot, :mt], 0.0) + y.reshape(mt, SUBLANES, D_MODEL)

        for_row_tile(ffn_tile)

        @pl.when(j == N_FT - 1)
        def _():
            @pl.when(b > 0)
            def _():
                n_prev, prev_copy = ys_write(jnp.maximum(b - 1, 0), 1 - slot)
                for_tile_runs(n_prev, lambda off, size: prev_copy(off, size).wait())

            n_own, own_copy = ys_write(b, slot)
            for_tile_runs(n_own, lambda off, size: own_copy(off, size).start())

            @pl.when(jnp.logical_not(has_next))
            def _():
                end = (blk_dst[b] >> SUBLANE_SHIFT) + n_own

                def tail_copy(off, size):
                    return pltpu.make_async_copy(acc.at[1 - slot, pl.ds(0, size)],
                                                 ys_hbm.at[pl.ds(end + off, size)], ssem.at[1 - slot])

                for_tile_runs(YS_TILES - end, lambda off, size: tail_copy(off, size).start())
                for_tile_runs(YS_TILES - end, lambda off, size: tail_copy(off, size).wait())
                for_tile_runs(n_own, lambda off, size: own_copy(off, size).wait())


def _moe_experts(xn, w1, w3, w2, layer, row_asg, blk_e, blk_rows, blk_src, blk_dst):
    def w_col(b, j, ra, be, br, bs, bd):
        return (layer, be[b], 0, jnp.where(br[b] > 0, j, N_FT - 1))

    def w_row(b, j, ra, be, br, bs, bd):
        return (layer, be[b], jnp.where(br[b] > 0, j, N_FT - 1), 0)

    return pl.pallas_call(
        _expert_kernel,
        out_shape=jax.ShapeDtypeStruct((YS_TILES, SUBLANES, D_MODEL), F32),
        grid_spec=pltpu.PrefetchScalarGridSpec(
            num_scalar_prefetch=5,
            grid=(MOE_NBLK, N_FT),
            in_specs=[
                pl.BlockSpec(memory_space=pl.ANY),
                pl.BlockSpec((None, None, D_MODEL, MOE_FT), w_col),
                pl.BlockSpec((None, None, D_MODEL, MOE_FT), w_col),
                pl.BlockSpec((None, None, MOE_FT, D_MODEL), w_row),
            ],
            out_specs=pl.BlockSpec(memory_space=pl.ANY),
            scratch_shapes=[
                pltpu.VMEM((2, MOE_BLOCK // SUBLANES, SUBLANES, D_MODEL), F32),
                pltpu.VMEM((2, MOE_BLOCK // SUBLANES, SUBLANES, D_MODEL), F32),
                pltpu.VMEM((MOE_BLOCK, D_MODEL), BF16),
                pltpu.SemaphoreType.DMA((2,)),
                pltpu.SemaphoreType.DMA((2,)),
            ],
        ),
        compiler_params=_params("arbitrary", "arbitrary"),
        name="moe_experts",
    )(row_asg, blk_e, blk_rows, blk_src, blk_dst, xn, w1, w3, w2).reshape(YS_TILES * SUBLANES, D_MODEL)


COMBINE_TILES = COMBINE_TOK // SUBLANES


def _combine_kernel(final_norm, ys_row, ys_hbm, x_ref, gate_ref, gf_ref, o_ref, ybuf, sem):
    i = pl.program_id(0)
    slot = i % 2

    def issue(tile, s):
        def group(q, c):
            for k in range(TOP_K):
                for u in range(SUBLANES):
                    r = ys_row[k * N_TOK + tile * COMBINE_TOK + q * SUBLANES + u]
                    pltpu.make_async_copy(ys_hbm.at[pl.ds(r, 1), :], ybuf.at[s, k, q, pl.ds(u, 1), :],
                                          sem.at[s]).start()
            return c

        lax.fori_loop(0, COMBINE_TILES, group, 0)

    @pl.when(i == 0)
    def _():
        issue(0, 0)

    @pl.when(i + 1 < pl.num_programs(0))
    def _():
        issue(i + 1, 1 - slot)

    pltpu.make_async_copy(ybuf.at[slot], ybuf.at[slot], sem.at[slot]).wait()
    gates = gate_ref[...]
    y0 = ybuf[slot, 0].reshape(COMBINE_TOK, D_MODEL)
    y1 = ybuf[slot, 1].reshape(COMBINE_TOK, D_MODEL)
    out = x_ref[...] + (gates[:, 2:3] * y0 + gates[:, 3:4] * y1)
    if final_norm:
        out = _rms(out, gf_ref[...])
    o_ref[...] = out


def _moe_combine(x, ys, ys_row, route, g_final, final_norm):
    tm = COMBINE_TOK
    return pl.pallas_call(
        functools.partial(_combine_kernel, final_norm),
        out_shape=jax.ShapeDtypeStruct((N_TOK, D_MODEL), F32),
        grid_spec=pltpu.PrefetchScalarGridSpec(
            num_scalar_prefetch=1,
            grid=(N_TOK // tm,),
            in_specs=[
                pl.BlockSpec(memory_space=pl.ANY),
                pl.BlockSpec((tm, D_MODEL), lambda i, yr: (i, 0)),
                pl.BlockSpec((tm, LANES), lambda i, yr: (i, 0)),
                pl.BlockSpec((1, D_MODEL), lambda i, yr: (0, 0)),
            ],
            out_specs=pl.BlockSpec((tm, D_MODEL), lambda i, yr: (i, 0)),
            scratch_shapes=[
                pltpu.VMEM((2, TOP_K, COMBINE_TILES, SUBLANES, D_MODEL), F32),
                pltpu.SemaphoreType.DMA((2,)),
            ],
        ),
        compiler_params=_params("arbitrary"),
        name="moe_combine",
    )(ys_row, ys, x, route, g_final.reshape(1, D_MODEL))


def _hier_moe(x, g, wg, bg, we, be, w1, w3, w2, layer, g_final, final_norm):
    pad = LANES - N_GROUPS - N_EXPERTS
    w_r = jnp.concatenate([wg, we, jnp.zeros((D_MODEL, pad), F32)], axis=1)
    b_r = jnp.concatenate([bg, be, jnp.zeros((pad,), F32)]).reshape(1, LANES)
    route, xn = _router(x, g, w_r, b_r)
    eid = route[:, :TOP_K].astype(jnp.int32)
    row_asg, blk_e, blk_rows, blk_src, blk_dst, ys_row = _dispatch_plan(eid)
    ys = _moe_experts(xn, w1, w3, w2, layer, row_asg, blk_e, blk_rows, blk_src, blk_dst)
    return _moe_combine(x, ys, ys_row, route, g_final, final_norm)


def kernel(x, norm_mix, norm_ffn, norm_final, w_in_ab, na_rpb, conv_w, conv_b, w_out_ab, w_in_cd,
           q_norm, kv_norm, w_uq, w_ukv, sg_norm, sg_w, sg_b, w_out_cd, router_group_w,
           router_group_b, router_expert_w, router_expert_b, w1, w3, w2):
    xt = x.reshape(N_TOK, D_MODEL)
    for layer in range(DEPTH):
        i = layer // 2
        if layer % 2 == 0:
            q_scale = np.where(np.arange(AB_IN) < NA_WIDTH, NA_Q_SCALE, 1.0).astype(np.float32)
            p = _norm_matmul(xt, norm_mix[layer], (w_in_ab[i] * q_scale).astype(BF16), tm=1024, tn=512)
            a_out = _natten(p, _natten_bias_table(na_rpb[i]))
            b_out = _gated_conv(p, conv_w[i], conv_b[i])
            xt = _out_proj(a_out, b_out, w_out_ab[i].astype(BF16), xt)
        else:
            p, k_rope = _cd_in_proj(xt, norm_mix[layer], w_in_cd[i])
            wq = w_uq[i].reshape(Q_LORA, MLA_HEADS, QK_DIM)
            wq = jnp.concatenate([wq[:, :, :QK_NOPE].reshape(Q_LORA, NOPE_ALL),
                                  wq[:, :, QK_NOPE:].reshape(Q_LORA, ROPE_ALL)], axis=1).astype(BF16)
            q, k, v = _mla_prep(p, k_rope, q_norm[i], kv_norm[i], wq, w_ukv[i].astype(BF16))
            c_out = _mla_attn(q, k, v)
            d_out = _spatial_gating(p, sg_norm[i], sg_w[i], sg_b[i])
            xt = _out_proj(c_out, d_out, w_out_cd[i].astype(BF16), xt)
        xt = _hier_moe(xt, norm_ffn[layer], router_group_w[layer], router_group_b[layer],
                       router_expert_w[layer], router_expert_b[layer], w1, w3, w2, layer,
                       norm_final, final_norm=(layer == DEPTH - 1))
    return xt.reshape(BATCH, SEQ, D_MODEL)
```

```python
import functools

import numpy as np
import jax
import jax.numpy as jnp
from jax import lax
from jax.experimental import pallas as pl
from jax.experimental.pallas import tpu as pltpu

F32 = jnp.float32
BF16 = jnp.bfloat16

D_MODEL = 2048
BATCH = 2
SEQ = 4096
N_TOK = BATCH * SEQ
DEPTH = 2
GRID_W = 64
GRID_ROWS = SEQ // GRID_W
WIN_ROWS = 8
WIN_COLS = 16
NA_HEADS = 8
NA_HEAD_DIM = 128
NA_WIDTH = NA_HEADS * NA_HEAD_DIM
NA_KEYS = WIN_ROWS * GRID_W
CONV_CH = D_MODEL // 2
MLA_HEADS = 8
Q_LORA = 512
KV_LORA = 512
QK_NOPE = 128
QK_ROPE = 64
QK_DIM = QK_NOPE + QK_ROPE
V_DIM = 128
ROPE_THETA = 10000.0
SG_CH = D_MODEL // 2
SG_GROUPS = 8
CHUNK = 128
N_GROUPS = 8
EXPERTS_PER_GROUP = 8
N_EXPERTS = N_GROUPS * EXPERTS_PER_GROUP
TOP_K = 2
D_EXPERT = 768
EPS = 1e-6
NEG_INF = -1e30
LOG2_E = 1.4426950408889634
AB_IN = 3 * NA_WIDTH + 3 * CONV_CH

LANES = 128
SUBLANES = 8
SUBLANE_SHIFT = 3
MOE_BLOCK = 512
MOE_ROW_TILES = (256, 320, 512)
MOE_FT = 256
N_FT = D_EXPERT // MOE_FT
N_ASSIGN = N_TOK * TOP_K
MOE_NBLK = N_ASSIGN // MOE_BLOCK + N_EXPERTS
COMBINE_TOK = 256
VMEM_LIMIT = 52 * 1024 * 1024


def _params(*sem):
    return pltpu.CompilerParams(dimension_semantics=sem, vmem_limit_bytes=VMEM_LIMIT)


def _rms(x, g):
    return x * lax.rsqrt(jnp.mean(x * x, axis=-1, keepdims=True) + EPS) * g


HALF = D_MODEL // 2
HIGH_HALF = 0xFFFF0000


def _pack_halves(x):
    lo = pltpu.bitcast(x[:, :HALF].astype(BF16).astype(F32), jnp.uint32)
    hi = pltpu.bitcast(x[:, HALF:].astype(BF16).astype(F32), jnp.uint32)
    return (lo >> 16) | (hi & jnp.uint32(HIGH_HALF))


def _unpack_halves(w):
    return pltpu.bitcast(w << 16, F32), pltpu.bitcast(w & jnp.uint32(HIGH_HALF), F32)


def _norm_mm_kernel(x_ref, g_ref, w_ref, o_ref, xn_ref):
    @pl.when(pl.program_id(1) == 0)
    def _():
        xn_ref[...] = _rms(x_ref[...].astype(F32), g_ref[...]).astype(BF16)

    o_ref[...] = jnp.dot(xn_ref[...], w_ref[...], preferred_element_type=F32).astype(o_ref.dtype)


def _norm_matmul(x, g, w, tm, tn):
    m, k = x.shape
    nout = w.shape[1]
    return pl.pallas_call(
        _norm_mm_kernel,
        out_shape=jax.ShapeDtypeStruct((m, nout), BF16),
        grid=(m // tm, nout // tn),
        in_specs=[
            pl.BlockSpec((tm, k), lambda i, j: (i, 0)),
            pl.BlockSpec((1, k), lambda i, j: (0, 0)),
            pl.BlockSpec((k, tn), lambda i, j: (0, j)),
        ],
        out_specs=pl.BlockSpec((tm, tn), lambda i, j: (i, j)),
        scratch_shapes=[pltpu.VMEM((tm, k), BF16)],
        compiler_params=_params("parallel", "arbitrary"),
        name="norm_matmul",
    )(x, g.reshape(1, k), w)


CD_TN = 512
CD_MAIN = Q_LORA + KV_LORA + 2 * SG_CH
CD_A_TILES = (Q_LORA + KV_LORA) // CD_TN
CD_MAIN_TILES = CD_MAIN // CD_TN


def _cd_proj_kernel(x_ref, g_ref, wa_ref, wb_ref, wc_ref, main_ref, kr_ref, xn_ref):
    j = pl.program_id(1)

    @pl.when(j == 0)
    def _():
        xn_ref[...] = _rms(x_ref[...], g_ref[...]).astype(BF16)

    @pl.when(j < CD_A_TILES)
    def _():
        main_ref[...] = jnp.dot(xn_ref[...], wa_ref[...], preferred_element_type=F32).astype(BF16)

    @pl.when(j >= CD_A_TILES)
    def _():
        main_ref[...] = jnp.dot(xn_ref[...], wb_ref[...], preferred_element_type=F32).astype(BF16)

    @pl.when(j == 0)
    def _():
        kr_ref[...] = jnp.dot(xn_ref[...], wc_ref[...], preferred_element_type=F32).astype(BF16)


def _cd_in_proj(x, g, w, tm=1024):
    c0, c1 = Q_LORA + KV_LORA, Q_LORA + KV_LORA + QK_ROPE
    wa = w[:, :c0].astype(BF16)
    wb = w[:, c1:].astype(BF16)
    wc = jnp.pad(w[:, c0:c1], ((0, 0), (0, LANES - QK_ROPE))).astype(BF16)
    return pl.pallas_call(
        _cd_proj_kernel,
        out_shape=(jax.ShapeDtypeStruct((N_TOK, CD_MAIN), BF16), jax.ShapeDtypeStruct((N_TOK, LANES), BF16)),
        grid=(N_TOK // tm, CD_MAIN_TILES),
        in_specs=[
            pl.BlockSpec((tm, D_MODEL), lambda i, j: (i, 0)),
            pl.BlockSpec((1, D_MODEL), lambda i, j: (0, 0)),
            pl.BlockSpec((D_MODEL, CD_TN), lambda i, j: (0, jnp.minimum(j, CD_A_TILES - 1))),
            pl.BlockSpec((D_MODEL, CD_TN), lambda i, j: (0, jnp.maximum(j - CD_A_TILES, 0))),
            pl.BlockSpec((D_MODEL, LANES), lambda i, j: (0, 0)),
        ],
        out_specs=(
            pl.BlockSpec((tm, CD_TN), lambda i, j: (i, j)),
            pl.BlockSpec((tm, LANES), lambda i, j: (i, 0)),
        ),
        scratch_shapes=[pltpu.VMEM((tm, D_MODEL), BF16)],
        compiler_params=_params("parallel", "arbitrary"),
        name="cd_in_proj",
    )(x, g.reshape(1, D_MODEL), wa, wb, wc)


def _natten_bias_table(rpb):
    c = np.arange(GRID_W)
    col_start = np.clip(c - WIN_COLS // 2, 0, GRID_W - WIN_COLS)
    valid = (c[None, :] >= col_start[:, None]) & (c[None, :] < col_start[:, None] + WIN_COLS)
    dc = np.clip(c[None, :] - c[:, None] + WIN_COLS - 1, 0, 2 * WIN_COLS - 2)
    pick = (dc[:, :, None] == np.arange(2 * WIN_COLS - 1)).astype(np.float32)
    m = jnp.einsum('hrd,ckd->hcrk', rpb, jnp.asarray(pick), precision=lax.Precision.HIGHEST)
    m = jnp.where(valid[None, :, None, :], m, NEG_INF)
    t = jnp.stack([m[:, :, o:o + WIN_ROWS] for o in range(WIN_ROWS)], axis=1)
    return (t.reshape(rpb.shape[0], WIN_ROWS, GRID_W, NA_KEYS) * LOG2_E).astype(F32)


NA_ROWS_PER_STEP = 16
NA_Q_SCALE = NA_HEAD_DIM ** -0.5 * LOG2_E


def _natten_kernel(q_ref, k_ref, v_ref, t_ref, o_ref):
    def rows(i, carry):
        rs = [i * NA_ROWS_PER_STEP + u for u in range(NA_ROWS_PER_STEP)]
        kr0s = [jnp.clip(r - WIN_ROWS // 2, 0, GRID_ROWS - WIN_ROWS) for r in rs]
        q0s = [pl.multiple_of(r * GRID_W, GRID_W) for r in rs]
        k0s = [pl.multiple_of(kr0 * GRID_W, GRID_W) for kr0 in kr0s]
        ss = [lax.dot_general(q_ref[pl.ds(q0, GRID_W), :], k_ref[pl.ds(k0, NA_KEYS), :],
                              (((1,), (1,)), ((), ())), preferred_element_type=F32)
              for q0, k0 in zip(q0s, k0s)]
        ps, ls = [], []
        for s, r, kr0 in zip(ss, rs, kr0s):
            s = s + t_ref[kr0 - r + WIN_ROWS - 1]
            p = jnp.exp2(s - jnp.max(s, axis=-1, keepdims=True))
            ls.append(jnp.sum(p, axis=-1, keepdims=True))
            ps.append(p.astype(BF16))
        for p, l, q0, k0 in zip(ps, ls, q0s, k0s):
            o = jnp.dot(p, v_ref[pl.ds(k0, NA_KEYS), :], preferred_element_type=F32) / l
            o_ref[pl.ds(q0, GRID_W), :] = o.astype(o_ref.dtype)
        return carry

    lax.fori_loop(0, GRID_ROWS // NA_ROWS_PER_STEP, rows, 0)


def _natten(p, table):
    blk = (SEQ, NA_HEAD_DIM)
    return pl.pallas_call(
        _natten_kernel,
        out_shape=jax.ShapeDtypeStruct((N_TOK, NA_WIDTH), BF16),
        grid=(BATCH, NA_HEADS),
        in_specs=[
            pl.BlockSpec(blk, lambda b, h: (b, h)),
            pl.BlockSpec(blk, lambda b, h: (b, NA_HEADS + h)),
            pl.BlockSpec(blk, lambda b, h: (b, 2 * NA_HEADS + h)),
            pl.BlockSpec((None, WIN_ROWS, GRID_W, NA_KEYS), lambda b, h: (h, 0, 0, 0)),
        ],
        out_specs=pl.BlockSpec(blk, lambda b, h: (b, h)),
        compiler_params=_params("parallel", "parallel"),
        name="natten",
    )(p, p, p, table)


CONV_CB = 128


def _conv_kernel(gb_ref, gc_ref, hc_ref, w_ref, b_ref, o_ref):
    z = gc_ref[...].astype(F32) * hc_ref[...].astype(F32)
    pos = lax.broadcasted_iota(jnp.int32, z.shape, 0)
    z_prev = jnp.where(pos == 0, 0.0, pltpu.roll(z, 1, 0))
    z_next = jnp.where(pos == SEQ - 1, 0.0, pltpu.roll(z, SEQ - 1, 0))
    y = b_ref[...] + z_prev * w_ref[0:1, :] + z * w_ref[1:2, :] + z_next * w_ref[2:3, :]
    o_ref[...] = (gb_ref[...].astype(F32) * y).astype(o_ref.dtype)


def _gated_conv(p, w, b):
    base = 3 * NA_WIDTH // CONV_CB
    step = CONV_CH // CONV_CB
    blk = (SEQ, CONV_CB)
    return pl.pallas_call(
        _conv_kernel,
        out_shape=jax.ShapeDtypeStruct((N_TOK, CONV_CH), BF16),
        grid=(BATCH, step),
        in_specs=[
            pl.BlockSpec(blk, lambda bi, c: (bi, base + c)),
            pl.BlockSpec(blk, lambda bi, c: (bi, base + step + c)),
            pl.BlockSpec(blk, lambda bi, c: (bi, base + 2 * step + c)),
            pl.BlockSpec((3, CONV_CB), lambda bi, c: (0, c)),
            pl.BlockSpec((1, CONV_CB), lambda bi, c: (0, c)),
        ],
        out_specs=pl.BlockSpec(blk, lambda bi, c: (bi, c)),
        compiler_params=_params("parallel", "parallel"),
        name="gated_conv",
    )(p, p, p, w, b.reshape(1, CONV_CH))


def _out_proj_kernel(a_ref, b_ref, wa_ref, wb_ref, r_ref, o_ref):
    acc = jnp.dot(a_ref[...], wa_ref[...], preferred_element_type=F32)
    acc = acc + jnp.dot(b_ref[...], wb_ref[...], preferred_element_type=F32)
    o_ref[...] = r_ref[...] + acc


def _out_proj(a, b, w, res, tm=512):
    ka, kb = a.shape[1], b.shape[1]
    assert ka == kb
    return pl.pallas_call(
        _out_proj_kernel,
        out_shape=jax.ShapeDtypeStruct((N_TOK, D_MODEL), F32),
        grid=(N_TOK // tm,),
        in_specs=[
            pl.BlockSpec((tm, ka), lambda i: (i, 0)),
            pl.BlockSpec((tm, kb), lambda i: (i, 0)),
            pl.BlockSpec((ka, D_MODEL), lambda i: (0, 0)),
            pl.BlockSpec((kb, D_MODEL), lambda i: (1, 0)),
            pl.BlockSpec((tm, D_MODEL), lambda i: (i, 0)),
        ],
        out_specs=pl.BlockSpec((tm, D_MODEL), lambda i: (i, 0)),
        compiler_params=_params("parallel"),
        name="out_proj",
    )(a, b, w, w, res)


MLA_TM = 512


def _rope_tables():
    f32 = np.float32
    pos = np.arange(SEQ)
    row = (pos // GRID_W).astype(f32)
    col = (pos % GRID_W).astype(f32)
    half = QK_ROPE // 2
    inv = np.power(f32(ROPE_THETA), -np.arange(0, half, 2, dtype=f32) / f32(half)).astype(f32)
    ar, ac = row[:, None] * inv, col[:, None] * inv
    cos_t = jnp.asarray(np.concatenate([np.cos(ar), np.cos(ar), np.cos(ac), np.cos(ac)], axis=-1).astype(f32))
    sin_t = jnp.asarray(np.concatenate([-np.sin(ar), np.sin(ar), -np.sin(ac), np.sin(ac)], axis=-1).astype(f32))
    quarter = half // 2
    src = np.arange(QK_ROPE) + np.where((np.arange(QK_ROPE) // quarter) % 2 == 0, quarter, -quarter)
    swap = np.zeros((QK_ROPE, QK_ROPE), np.float32)
    swap[src, np.arange(QK_ROPE)] = 1.0
    swap_all = np.kron(np.eye(MLA_HEADS, dtype=np.float32), swap)
    return cos_t, sin_t, jnp.asarray(swap_all, BF16)


def _rope(x, cos_t, sin_t, swap):
    hi = x.astype(BF16)
    lo = (x - hi.astype(F32)).astype(BF16)
    xs = jnp.dot(hi, swap, preferred_element_type=F32) + jnp.dot(lo, swap, preferred_element_type=F32)
    return x * cos_t + xs * sin_t


ROPE_ALL = MLA_HEADS * QK_ROPE
NOPE_ALL = MLA_HEADS * QK_NOPE


def _mla_prep_kernel(cq_ref, ckv_ref, kr_ref, gq_ref, gkv_ref, wq_ref, wkv_ref, cos_ref, sin_ref,
                     swap_ref, q_ref, k_ref, v_ref):
    scale = QK_DIM ** -0.5 * LOG2_E
    cos_t, sin_t, swap = cos_ref[...], sin_ref[...], swap_ref[...]
    cqn = _rms(cq_ref[...].astype(F32), gq_ref[...]).astype(BF16)
    ckvn = _rms(ckv_ref[...].astype(F32), gkv_ref[...]).astype(BF16)
    kr = kr_ref[:, :QK_ROPE].astype(F32)
    krot = _rope(kr, cos_t, sin_t, swap[:QK_ROPE, :QK_ROPE]).astype(BF16)
    q = jnp.dot(cqn, wq_ref[...], preferred_element_type=F32)
    qn = (q[:, :NOPE_ALL] * scale).astype(BF16)
    cos_all = jnp.concatenate([cos_t] * MLA_HEADS, axis=1)
    sin_all = jnp.concatenate([sin_t] * MLA_HEADS, axis=1)
    qr = (_rope(q[:, NOPE_ALL:], cos_all, sin_all, swap) * scale).astype(BF16)
    kv = jnp.dot(ckvn, wkv_ref[...], preferred_element_type=F32)
    hw = QK_NOPE + V_DIM
    for h in range(MLA_HEADS):
        q_ref[h, :, :QK_NOPE] = qn[:, h * QK_NOPE:(h + 1) * QK_NOPE]
        q_ref[h, :, QK_NOPE:] = qr[:, h * QK_ROPE:(h + 1) * QK_ROPE]
        k_ref[h, :, :QK_NOPE] = kv[:, h * hw:h * hw + QK_NOPE].astype(BF16)
        k_ref[h, :, QK_NOPE:] = krot
        v_ref[h, 0] = kv[:, h * hw + QK_NOPE:(h + 1) * hw].T.astype(BF16)


def _mla_prep(p, k_rope, q_norm, kv_norm, wq, wkv):
    cos_t, sin_t, swap = _rope_tables()
    tm = MLA_TM
    seq_blocks = SEQ // tm
    return pl.pallas_call(
        _mla_prep_kernel,
        out_shape=(
            jax.ShapeDtypeStruct((MLA_HEADS, N_TOK, QK_DIM), BF16),
            jax.ShapeDtypeStruct((MLA_HEADS, N_TOK, QK_DIM), BF16),
            jax.ShapeDtypeStruct((MLA_HEADS, N_TOK // tm, V_DIM, tm), BF16),
        ),
        grid=(N_TOK // tm,),
        in_specs=[
            pl.BlockSpec((tm, Q_LORA), lambda i: (i, 0)),
            pl.BlockSpec((tm, KV_LORA), lambda i: (i, 1)),
            pl.BlockSpec((tm, LANES), lambda i: (i, 0)),
            pl.BlockSpec((1, Q_LORA), lambda i: (0, 0)),
            pl.BlockSpec((1, KV_LORA), lambda i: (0, 0)),
            pl.BlockSpec((Q_LORA, NOPE_ALL + ROPE_ALL), lambda i: (0, 0)),
            pl.BlockSpec((KV_LORA, MLA_HEADS * (QK_NOPE + V_DIM)), lambda i: (0, 0)),
            pl.BlockSpec((tm, QK_ROPE), lambda i: (i % seq_blocks, 0)),
            pl.BlockSpec((tm, QK_ROPE), lambda i: (i % seq_blocks, 0)),
            pl.BlockSpec((ROPE_ALL, ROPE_ALL), lambda i: (0, 0)),
        ],
        out_specs=(
            pl.BlockSpec((MLA_HEADS, tm, QK_DIM), lambda i: (0, i, 0)),
            pl.BlockSpec((MLA_HEADS, tm, QK_DIM), lambda i: (0, i, 0)),
            pl.BlockSpec((MLA_HEADS, 1, V_DIM, tm), lambda i: (0, i, 0, 0)),
        ),
        compiler_params=_params("parallel"),
        name="mla_prep",
    )(p, p, k_rope, q_norm.reshape(1, Q_LORA), kv_norm.reshape(1, KV_LORA), wq, wkv, cos_t, sin_t, swap)


MLA_TQ = 4096
MLA_TK = 512
MLA_QH = 16
assert MLA_TK == MLA_TM


def _mla_attn_kernel(q_ref, k_ref, vt_ref, o_ref, s_ref, m_ref, l_ref, acc_ref):
    n_chunks = SEQ // MLA_TK
    cols = MLA_TQ // MLA_QH

    def scores(h, c):
        k0 = pl.multiple_of(c * MLA_TK, MLA_TK)
        return lax.dot_general(k_ref[pl.ds(k0, MLA_TK), :], q_ref[h * cols:(h + 1) * cols, :],
                               (((1,), (1,)), ((), ())), preferred_element_type=F32)

    m_ref[...] = jnp.full(m_ref.shape, NEG_INF, F32)
    l_ref[...] = jnp.zeros(l_ref.shape, F32)
    acc_ref[...] = jnp.zeros(acc_ref.shape, F32)
    for h in range(MLA_QH):
        s_ref[0, h] = scores(h, 0)

    def step(c, slot):
        c_next = jnp.where(c + 1 < n_chunks, c + 1, 0)
        for h in range(MLA_QH):
            s_ref[1 - slot, h] = scores(h, c_next)
            s = s_ref[slot, h]
            m_old = m_ref[h]
            m_new = jnp.maximum(m_old, jnp.max(s, axis=0, keepdims=True))
            p = jnp.exp2(s - m_new)
            alpha = jnp.exp2(m_old - m_new)
            l_ref[h] = alpha * l_ref[h] + jnp.sum(p, axis=0, keepdims=True)
            pv = jnp.dot(vt_ref[c], p.astype(BF16), preferred_element_type=F32)
            acc_ref[h] = alpha * acc_ref[h] + pv
            m_ref[h] = m_new

    def pair(i, carry):
        step(2 * i, 0)
        step(2 * i + 1, 1)
        return carry

    lax.fori_loop(0, n_chunks // 2, pair, 0)
    for h in range(MLA_QH):
        o_ref[h * cols:(h + 1) * cols, :] = (acc_ref[h] / l_ref[h]).T.astype(o_ref.dtype)


def _mla_attn(q, k, v):
    nq = SEQ // MLA_TQ
    return pl.pallas_call(
        _mla_attn_kernel,
        out_shape=jax.ShapeDtypeStruct((N_TOK, MLA_HEADS * V_DIM), BF16),
        grid=(BATCH, MLA_HEADS, nq),
        in_specs=[
            pl.BlockSpec((None, MLA_TQ, QK_DIM), lambda b, h, i: (h, b * nq + i, 0)),
            pl.BlockSpec((None, SEQ, QK_DIM), lambda b, h, i: (h, b, 0)),
            pl.BlockSpec((None, SEQ // MLA_TK, V_DIM, MLA_TK), lambda b, h, i: (h, b, 0, 0)),
        ],
        out_specs=pl.BlockSpec((MLA_TQ, V_DIM), lambda b, h, i: (b * nq + i, h)),
        scratch_shapes=[
            pltpu.VMEM((2, MLA_QH, MLA_TK, MLA_TQ // MLA_QH), F32),
            pltpu.VMEM((MLA_QH, 1, MLA_TQ // MLA_QH), F32),
            pltpu.VMEM((MLA_QH, 1, MLA_TQ // MLA_QH), F32),
            pltpu.VMEM((MLA_QH, V_DIM, MLA_TQ // MLA_QH), F32),
        ],
        compiler_params=_params("parallel", "parallel", "parallel"),
        name="mla_attn",
    )(q, k, v)


SG_TM = 256


def _sg_kernel(u_ref, v_ref, g_ref, w_ref, bt_ref, o_ref):
    v = jax.nn.gelu(v_ref[...].astype(F32))
    vn = _rms(v, g_ref[...]).astype(BF16)
    u = jax.nn.gelu(u_ref[...].astype(F32))
    gc = SG_CH // SG_GROUPS
    for n in range(SG_TM // CHUNK):
        rows = slice(n * CHUNK, (n + 1) * CHUNK)
        for g in range(SG_GROUPS):
            cols = slice(g * gc, (g + 1) * gc)
            mixed = jnp.dot(w_ref[g], vn[rows, cols], preferred_element_type=F32) + bt_ref[:, g:g + 1]
            o_ref[rows, cols] = (u[rows, cols] * mixed).astype(o_ref.dtype)


def _spatial_gating(p, g_norm, w_s, b_s):
    u_blk = (Q_LORA + KV_LORA) // SG_CH
    return pl.pallas_call(
        _sg_kernel,
        out_shape=jax.ShapeDtypeStruct((N_TOK, SG_CH), BF16),
        grid=(N_TOK // SG_TM,),
        in_specs=[
            pl.BlockSpec((SG_TM, SG_CH), lambda i: (i, u_blk)),
            pl.BlockSpec((SG_TM, SG_CH), lambda i: (i, u_blk + 1)),
            pl.BlockSpec((1, SG_CH), lambda i: (0, 0)),
            pl.BlockSpec((SG_GROUPS, CHUNK, CHUNK), lambda i: (0, 0, 0)),
            pl.BlockSpec((CHUNK, SG_GROUPS), lambda i: (0, 0)),
        ],
        out_specs=pl.BlockSpec((SG_TM, SG_CH), lambda i: (i, 0)),
        compiler_params=_params("parallel"),
        name="spatial_gating",
    )(p, p, g_norm.reshape(1, SG_CH), w_s.astype(BF16), b_s.T)


ROUTER_TM = 512


def _router_kernel(x_ref, g_ref, w_ref, b_ref, o_ref, xn_ref):
    xn = _rms(x_ref[...], g_ref[...])
    xn_ref[...] = _pack_halves(xn)
    w = w_ref[...]
    xh, wh = xn.astype(BF16), w.astype(BF16)
    xl, wl = (xn - xh.astype(F32)).astype(BF16), (w - wh.astype(F32)).astype(BF16)
    logits = (jnp.dot(xh, wh, preferred_element_type=F32)
              + (jnp.dot(xh, wl, preferred_element_type=F32) + jnp.dot(xl, wh, preferred_element_type=F32))
              + b_ref[...])
    lane = lax.broadcasted_iota(jnp.int32, logits.shape, 1).astype(F32)
    low = jnp.float32(-3.0e38)

    def first_max(vals):
        top = jnp.max(vals, axis=-1, keepdims=True)
        idx = jnp.min(jnp.where(vals == top, lane, float(LANES)), axis=-1, keepdims=True)
        return top, idx

    is_group = lane < N_GROUPS
    g_top, g_idx = first_max(jnp.where(is_group, logits, low))
    g_prob = 1.0 / jnp.sum(jnp.where(is_group, jnp.exp(logits - g_top), 0.0), axis=-1, keepdims=True)
    lo = N_GROUPS + g_idx * EXPERTS_PER_GROUP
    e_vals = jnp.where(lane >= lo, jnp.where(lane < lo + EXPERTS_PER_GROUP, logits, low), low)
    v1, i1 = first_max(e_vals)
    v2, i2 = first_max(jnp.where(lane == i1, low, e_vals))
    e21 = jnp.exp(v2 - v1)
    w1 = g_prob / (1.0 + e21)
    w2 = g_prob * e21 / (1.0 + e21)
    out = jnp.where(lane == 0, i1 - N_GROUPS,
                    jnp.where(lane == 1, i2 - N_GROUPS,
                              jnp.where(lane == 2, w1, jnp.where(lane == 3, w2, 0.0))))
    o_ref[...] = out


def _router(x, g, w, b):
    tm = ROUTER_TM
    return pl.pallas_call(
        _router_kernel,
        out_shape=(jax.ShapeDtypeStruct((N_TOK, LANES), F32), jax.ShapeDtypeStruct((N_TOK, HALF), jnp.uint32)),
        grid=(N_TOK // tm,),
        in_specs=[
            pl.BlockSpec((tm, D_MODEL), lambda i: (i, 0)),
            pl.BlockSpec((1, D_MODEL), lambda i: (0, 0)),
            pl.BlockSpec((D_MODEL, LANES), lambda i: (0, 0)),
            pl.BlockSpec((1, LANES), lambda i: (0, 0)),
        ],
        out_specs=(pl.BlockSpec((tm, LANES), lambda i: (i, 0)), pl.BlockSpec((tm, HALF), lambda i: (i, 0))),
        compiler_params=_params("parallel"),
        name="router",
    )(x, g.reshape(1, D_MODEL), w, b)


def _dispatch_plan(eid):
    tok = jnp.arange(N_TOK, dtype=jnp.int32)
    keys = jnp.concatenate([eid[:, k] * N_ASSIGN + (k * N_TOK + tok) for k in range(TOP_K)])
    row_asg = jnp.sort(keys) & (N_ASSIGN - 1)
    row_asg = jnp.concatenate([row_asg, jnp.zeros((SUBLANES,), jnp.int32)])
    counts = jnp.sum(eid.reshape(-1)[:, None] == jnp.arange(N_EXPERTS, dtype=jnp.int32)[None, :], axis=0,
                     dtype=jnp.int32)
    starts = jnp.cumsum(counts) - counts
    nblk_e = (counts + MOE_BLOCK - 1) // MOE_BLOCK
    blk_end = jnp.cumsum(nblk_e)
    blk_start = blk_end - nblk_e
    n_active = blk_end[-1]
    bidx = jnp.arange(MOE_NBLK, dtype=jnp.int32)
    blk_x = jnp.minimum(bidx, n_active - 1)
    blk_e = jnp.minimum(jnp.sum(blk_end[None, :] <= blk_x[:, None], axis=1, dtype=jnp.int32), N_EXPERTS - 1)
    first = (blk_x - blk_start[blk_e]) * MOE_BLOCK
    blk_rows = jnp.where(bidx < n_active, jnp.clip(counts[blk_e] - first, 0, MOE_BLOCK), 0).astype(jnp.int32)
    blk_src = (starts[blk_e] + first).astype(jnp.int32)
    return row_asg, blk_e, blk_rows, blk_src


GATHER_STEP = min(1, N_FT - 1)
_ROW_BITS = tuple(1 << k for k in range(MOE_BLOCK.bit_length() - 1, -1, -1))


def _expert_kernel(row_asg, blk_e, blk_rows, blk_src, x_hbm, w1_ref, w3_ref, w2_ref, ys_hbm,
                   xbuf, acc, ypk, xn_ref, gsem, ssem):
    b = pl.program_id(0)
    j = pl.program_id(1)
    rows = blk_rows[b]
    slot = b % 2

    def n_fetch(blk):
        return (blk_rows[blk] + SUBLANES - 1) & -SUBLANES

    def issue_gather(blk, s):
        base = blk_src[blk]

        def group(q, c):
            for u in range(SUBLANES):
                t = row_asg[base + q * SUBLANES + u] & (N_TOK - 1)
                pltpu.make_async_copy(x_hbm.at[pl.ds(t, 1), :], xbuf.at[s, q, pl.ds(u, 1), :], gsem.at[s]).start()
            return c

        lax.fori_loop(0, n_fetch(blk) >> SUBLANE_SHIFT, group, 0)

    def wait_rows(n, copy_of):
        for bit in _ROW_BITS:
            @pl.when((n & bit) != 0)
            def _():
                copy_of(bit).wait()

    def gather_block(s):
        def copy_of(k):
            tiles = xbuf.at[s, pl.ds(0, k // SUBLANES)]
            return pltpu.make_async_copy(tiles, tiles, gsem.at[s])
        return copy_of

    def scatter_block(s):
        def copy_of(k):
            part = ypk.at[s, pl.ds(0, k // SUBLANES)] if k >= SUBLANES else ypk.at[s, 0, pl.ds(0, k), :]
            return pltpu.make_async_copy(part, part, ssem.at[s])
        return copy_of

    def issue_scatter(blk, s, n):
        base = blk_src[blk]

        def one(q, u, a):
            pltpu.make_async_copy(ypk.at[s, q, pl.ds(u, 1), :], ys_hbm.at[pl.ds(a, 1), :], ssem.at[s]).start()

        def group(q, c):
            for u in range(SUBLANES):
                one(q, u, row_asg[base + q * SUBLANES + u])
            return c

        def tail(i, c):
            one(i >> SUBLANE_SHIFT, i & (SUBLANES - 1), row_asg[base + i])
            return c

        full = n >> SUBLANE_SHIFT
        lax.fori_loop(0, full, group, 0)
        lax.fori_loop(full * SUBLANES, n, tail, 0)

    def for_row_tile(fn):
        lo = 0
        for m in MOE_ROW_TILES:
            @pl.when(jnp.logical_and(rows > lo, rows <= m))
            def _():
                fn(m)
            lo = m

    nxt = jnp.minimum(b + 1, MOE_NBLK - 1)
    has_next = jnp.logical_and(b + 1 < MOE_NBLK, blk_rows[nxt] > 0)

    @pl.when(rows > 0)
    def _():
        @pl.when(j == 0)
        def _():
            @pl.when(b == 0)
            def _():
                xbuf[...] = jnp.zeros(xbuf.shape, jnp.uint32)
                acc[...] = jnp.zeros(acc.shape, F32)
                issue_gather(0, 0)

            wait_rows(n_fetch(b), gather_block(slot))

            def unpack_tile(m):
                lo, hi = _unpack_halves(xbuf[slot, :m // SUBLANES].reshape(m, HALF))
                xn_ref[:m, :HALF] = lo.astype(BF16)
                xn_ref[:m, HALF:] = hi.astype(BF16)

            for_row_tile(unpack_tile)

        @pl.when(jnp.logical_and(j == GATHER_STEP, has_next))
        def _():
            issue_gather(nxt, 1 - slot)

        def ffn_tile(m):
            xs = xn_ref[:m, :]
            h1 = jnp.dot(xs, w1_ref[...].astype(BF16), preferred_element_type=F32)
            h3 = jnp.dot(xs, w3_ref[...].astype(BF16), preferred_element_type=F32)
            h = (jax.nn.silu(h1) * h3).astype(BF16)
            y = jnp.dot(h, w2_ref[...].astype(BF16), preferred_element_type=F32)
            acc[:m, :] = jnp.where(j > 0, acc[:m, :], 0.0) + y

        for_row_tile(ffn_tile)

        @pl.when(j == N_FT - 1)
        def _():
            @pl.when(b > 0)
            def _():
                wait_rows(blk_rows[jnp.maximum(b - 1, 0)], scatter_block(1 - slot))

            def pack_tile(m):
                ypk[slot, :m // SUBLANES] = _pack_halves(acc[:m, :]).reshape(m // SUBLANES, SUBLANES, HALF)

            for_row_tile(pack_tile)

            issue_scatter(b, slot, rows)

            @pl.when(jnp.logical_not(has_next))
            def _():
                wait_rows(rows, scatter_block(slot))


def _moe_experts(xn, w1, w3, w2, layer, row_asg, blk_e, blk_rows, blk_src):
    def w_col(b, j, ra, be, br, bs):
        return (layer, be[b], 0, jnp.where(br[b] > 0, j, N_FT - 1))

    def w_row(b, j, ra, be, br, bs):
        return (layer, be[b], jnp.where(br[b] > 0, j, N_FT - 1), 0)

    return pl.pallas_call(
        _expert_kernel,
        out_shape=jax.ShapeDtypeStruct((N_ASSIGN, HALF), jnp.uint32),
        grid_spec=pltpu.PrefetchScalarGridSpec(
            num_scalar_prefetch=4,
            grid=(MOE_NBLK, N_FT),
            in_specs=[
                pl.BlockSpec(memory_space=pl.ANY),
                pl.BlockSpec((None, None, D_MODEL, MOE_FT), w_col),
                pl.BlockSpec((None, None, D_MODEL, MOE_FT), w_col),
                pl.BlockSpec((None, None, MOE_FT, D_MODEL), w_row),
            ],
            out_specs=pl.BlockSpec(memory_space=pl.ANY),
            scratch_shapes=[
                pltpu.VMEM((2, MOE_BLOCK // SUBLANES, SUBLANES, HALF), jnp.uint32),
                pltpu.VMEM((MOE_BLOCK, D_MODEL), F32),
                pltpu.VMEM((2, MOE_BLOCK // SUBLANES, SUBLANES, HALF), jnp.uint32),
                pltpu.VMEM((MOE_BLOCK, D_MODEL), BF16),
                pltpu.SemaphoreType.DMA((2,)),
                pltpu.SemaphoreType.DMA((2,)),
            ],
        ),
        compiler_params=_params("arbitrary", "arbitrary"),
        name="moe_experts",
    )(row_asg, blk_e, blk_rows, blk_src, xn, w1, w3, w2)


def _combine_kernel(final_norm, x_ref, y0_ref, y1_ref, gate_ref, gf_ref, o_ref):
    gates = gate_ref[...]
    g0, g1 = gates[:, 2:3], gates[:, 3:4]
    lo0, hi0 = _unpack_halves(y0_ref[...])
    lo1, hi1 = _unpack_halves(y1_ref[...])
    lo = x_ref[:, :HALF] + (g0 * lo0 + g1 * lo1)
    hi = x_ref[:, HALF:] + (g0 * hi0 + g1 * hi1)
    if final_norm:
        ms = (jnp.sum(lo * lo, axis=-1, keepdims=True) + jnp.sum(hi * hi, axis=-1, keepdims=True)) / D_MODEL
        r = lax.rsqrt(ms + EPS)
        lo, hi = lo * r * gf_ref[:, :HALF], hi * r * gf_ref[:, HALF:]
    o_ref[:, :HALF] = lo
    o_ref[:, HALF:] = hi


def _moe_combine(x, ys, route, g_final, final_norm):
    tm = COMBINE_TOK
    return pl.pallas_call(
        functools.partial(_combine_kernel, final_norm),
        out_shape=jax.ShapeDtypeStruct((N_TOK, D_MODEL), F32),
        grid=(N_TOK // tm,),
        in_specs=[
            pl.BlockSpec((tm, D_MODEL), lambda i: (i, 0)),
            pl.BlockSpec((tm, HALF), lambda i: (i, 0)),
            pl.BlockSpec((tm, HALF), lambda i: (N_TOK // tm + i, 0)),
            pl.BlockSpec((tm, LANES), lambda i: (i, 0)),
            pl.BlockSpec((1, D_MODEL), lambda i: (0, 0)),
        ],
        out_specs=pl.BlockSpec((tm, D_MODEL), lambda i: (i, 0)),
        compiler_params=_params("parallel"),
        name="moe_combine",
    )(x, ys, ys, route, g_final.reshape(1, D_MODEL))


def _hier_moe(x, g, wg, bg, we, be, w1, w3, w2, layer, g_final, final_norm):
    pad = LANES - N_GROUPS - N_EXPERTS
    w_r = jnp.concatenate([wg, we, jnp.zeros((D_MODEL, pad), F32)], axis=1)
    b_r = jnp.concatenate([bg, be, jnp.zeros((pad,), F32)]).reshape(1, LANES)
    route, xn = _router(x, g, w_r, b_r)
    eid = route[:, :TOP_K].astype(jnp.int32)
    row_asg, blk_e, blk_rows, blk_src = _dispatch_plan(eid)
    ys = _moe_experts(xn, w1, w3, w2, layer, row_asg, blk_e, blk_rows, blk_src)
    return _moe_combine(x, ys, route, g_final, final_norm)


def kernel(x, norm_mix, norm_ffn, norm_final, w_in_ab, na_rpb, conv_w, conv_b, w_out_ab, w_in_cd,
           q_norm, kv_norm, w_uq, w_ukv, sg_norm, sg_w, sg_b, w_out_cd, router_group_w,
           router_group_b, router_expert_w, router_expert_b, w1, w3, w2):
    xt = x.reshape(N_TOK, D_MODEL)
    for layer in range(DEPTH):
        i = layer // 2
        if layer % 2 == 0:
            q_scale = np.where(np.arange(AB_IN) < NA_WIDTH, NA_Q_SCALE, 1.0).astype(np.float32)
            p = _norm_matmul(xt, norm_mix[layer], (w_in_ab[i] * q_scale).astype(BF16), tm=1024, tn=512)
            a_out = _natten(p, _natten_bias_table(na_rpb[i]))
            b_out = _gated_conv(p, conv_w[i], conv_b[i])
            xt = _out_proj(a_out, b_out, w_out_ab[i].astype(BF16), xt)
        else:
            p, k_rope = _cd_in_proj(xt, norm_mix[layer], w_in_cd[i])
            wq = w_uq[i].reshape(Q_LORA, MLA_HEADS, QK_DIM)
            wq = jnp.concatenate([wq[:, :, :QK_NOPE].reshape(Q_LORA, NOPE_ALL),
                                  wq[:, :, QK_NOPE:].reshape(Q_LORA, ROPE_ALL)], axis=1).astype(BF16)
            q, k, v = _mla_prep(p, k_rope, q_norm[i], kv_norm[i], wq, w_ukv[i].astype(BF16))
            c_out = _mla_attn(q, k, v)
            d_out = _spatial_gating(p, sg_norm[i], sg_w[i], sg_b[i])
            xt = _out_proj(c_out, d_out, w_out_cd[i].astype(BF16), xt)
        xt = _hier_moe(xt, norm_ffn[layer], router_group_w[layer], router_group_b[layer],
                       router_expert_w[layer], router_expert_b[layer], w1, w3, w2, layer,
                       norm_final, final_norm=(layer == DEPTH - 1))
    return xt.reshape(BATCH, SEQ, D_MODEL)
```

```python
import functools

import numpy as np
import jax
import jax.numpy as jnp
from jax import lax
from jax.experimental import pallas as pl
from jax.experimental.pallas import tpu as pltpu

F32 = jnp.float32
BF16 = jnp.bfloat16

D_MODEL = 2048
BATCH = 2
SEQ = 4096
N_TOK = BATCH * SEQ
DEPTH = 2
GRID_W = 64
GRID_ROWS = SEQ // GRID_W
WIN_ROWS = 8
WIN_COLS = 16
NA_HEADS = 8
NA_HEAD_DIM = 128
NA_WIDTH = NA_HEADS * NA_HEAD_DIM
NA_KEYS = WIN_ROWS * GRID_W
CONV_CH = D_MODEL // 2
MLA_HEADS = 8
Q_LORA = 512
KV_LORA = 512
QK_NOPE = 128
QK_ROPE = 64
QK_DIM = QK_NOPE + QK_ROPE
V_DIM = 128
ROPE_THETA = 10000.0
SG_CH = D_MODEL // 2
SG_GROUPS = 8
CHUNK = 128
N_GROUPS = 8
EXPERTS_PER_GROUP = 8
N_EXPERTS = N_GROUPS * EXPERTS_PER_GROUP
TOP_K = 2
D_EXPERT = 768
EPS = 1e-6
NEG_INF = -1e30
LOG2_E = 1.4426950408889634
AB_IN = 3 * NA_WIDTH + 3 * CONV_CH

LANES = 128
SUBLANES = 8
SUBLANE_SHIFT = 3
MOE_BLOCK = 512
MOE_ROW_TILES = (256, 320, 512)
MOE_FT = 256
N_FT = D_EXPERT // MOE_FT
N_ASSIGN = N_TOK * TOP_K
MOE_NBLK = N_ASSIGN // MOE_BLOCK + N_EXPERTS
COMBINE_TOK = 256
VMEM_LIMIT = 52 * 1024 * 1024


def _params(*sem):
    return pltpu.CompilerParams(dimension_semantics=sem, vmem_limit_bytes=VMEM_LIMIT)


def _rms(x, g):
    return x * lax.rsqrt(jnp.mean(x * x, axis=-1, keepdims=True) + EPS) * g


HALF = D_MODEL // 2
HIGH_HALF = 0xFFFF0000


def _pack_halves(x):
    lo = pltpu.bitcast(x[:, :HALF].astype(BF16).astype(F32), jnp.uint32)
    hi = pltpu.bitcast(x[:, HALF:].astype(BF16).astype(F32), jnp.uint32)
    return (lo >> 16) | (hi & jnp.uint32(HIGH_HALF))


def _unpack_halves(w):
    return pltpu.bitcast(w << 16, F32), pltpu.bitcast(w & jnp.uint32(HIGH_HALF), F32)


def _norm_mm_kernel(x_ref, g_ref, w_ref, o_ref, xn_ref):
    @pl.when(pl.program_id(1) == 0)
    def _():
        xn_ref[...] = _rms(x_ref[...].astype(F32), g_ref[...]).astype(BF16)

    o_ref[...] = jnp.dot(xn_ref[...], w_ref[...], preferred_element_type=F32).astype(o_ref.dtype)


def _norm_matmul(x, g, w, tm, tn):
    m, k = x.shape
    nout = w.shape[1]
    return pl.pallas_call(
        _norm_mm_kernel,
        out_shape=jax.ShapeDtypeStruct((m, nout), BF16),
        grid=(m // tm, nout // tn),
        in_specs=[
            pl.BlockSpec((tm, k), lambda i, j: (i, 0)),
            pl.BlockSpec((1, k), lambda i, j: (0, 0)),
            pl.BlockSpec((k, tn), lambda i, j: (0, j)),
        ],
        out_specs=pl.BlockSpec((tm, tn), lambda i, j: (i, j)),
        scratch_shapes=[pltpu.VMEM((tm, k), BF16)],
        compiler_params=_params("parallel", "arbitrary"),
        name="norm_matmul",
    )(x, g.reshape(1, k), w)


CD_TN = 1024
CD_MAIN = Q_LORA + KV_LORA + 2 * SG_CH
CD_A_TILES = (Q_LORA + KV_LORA) // CD_TN
CD_MAIN_TILES = CD_MAIN // CD_TN


def _cd_proj_kernel(x_ref, g_ref, wa_ref, wb_ref, wc_ref, main_ref, kr_ref, xn_ref):
    j = pl.program_id(1)

    @pl.when(j == 0)
    def _():
        xn_ref[...] = _rms(x_ref[...], g_ref[...]).astype(BF16)

    @pl.when(j < CD_A_TILES)
    def _():
        main_ref[...] = jnp.dot(xn_ref[...], wa_ref[...], preferred_element_type=F32).astype(BF16)

    @pl.when(j >= CD_A_TILES)
    def _():
        main_ref[...] = jnp.dot(xn_ref[...], wb_ref[...], preferred_element_type=F32).astype(BF16)

    @pl.when(j == 0)
    def _():
        kr_ref[...] = jnp.dot(xn_ref[...], wc_ref[...], preferred_element_type=F32).astype(BF16)


def _cd_in_proj(x, g, w, tm=1024):
    c0, c1 = Q_LORA + KV_LORA, Q_LORA + KV_LORA + QK_ROPE
    wa = w[:, :c0].astype(BF16)
    wb = w[:, c1:].astype(BF16)
    wc = jnp.pad(w[:, c0:c1], ((0, 0), (0, LANES - QK_ROPE))).astype(BF16)
    return pl.pallas_call(
        _cd_proj_kernel,
        out_shape=(jax.ShapeDtypeStruct((N_TOK, CD_MAIN), BF16), jax.ShapeDtypeStruct((N_TOK, LANES), BF16)),
        grid=(N_TOK // tm, CD_MAIN_TILES),
        in_specs=[
            pl.BlockSpec((tm, D_MODEL), lambda i, j: (i, 0)),
            pl.BlockSpec((1, D_MODEL), lambda i, j: (0, 0)),
            pl.BlockSpec((D_MODEL, CD_TN), lambda i, j: (0, jnp.minimum(j, CD_A_TILES - 1))),
            pl.BlockSpec((D_MODEL, CD_TN), lambda i, j: (0, jnp.maximum(j - CD_A_TILES, 0))),
            pl.BlockSpec((D_MODEL, LANES), lambda i, j: (0, 0)),
        ],
        out_specs=(
            pl.BlockSpec((tm, CD_TN), lambda i, j: (i, j)),
            pl.BlockSpec((tm, LANES), lambda i, j: (i, 0)),
        ),
        scratch_shapes=[pltpu.VMEM((tm, D_MODEL), BF16)],
        compiler_params=_params("parallel", "arbitrary"),
        name="cd_in_proj",
    )(x, g.reshape(1, D_MODEL), wa, wb, wc)


def _natten_bias_table(rpb):
    c = np.arange(GRID_W)
    col_start = np.clip(c - WIN_COLS // 2, 0, GRID_W - WIN_COLS)
    valid = (c[None, :] >= col_start[:, None]) & (c[None, :] < col_start[:, None] + WIN_COLS)
    dc = np.clip(c[None, :] - c[:, None] + WIN_COLS - 1, 0, 2 * WIN_COLS - 2)
    pick = (dc[:, :, None] == np.arange(2 * WIN_COLS - 1)).astype(np.float32)
    m = jnp.einsum('hrd,ckd->hcrk', rpb * LOG2_E, jnp.asarray(pick), precision=lax.Precision.HIGHEST)
    m = jnp.where(valid[None, :, None, :], m, NEG_INF)
    t = jnp.stack([m[:, :, o:o + WIN_ROWS] for o in range(WIN_ROWS)], axis=1)
    return t.reshape(rpb.shape[0], WIN_ROWS, GRID_W, NA_KEYS).astype(F32)


NA_ROWS_PER_STEP = 32
NA_Q_SCALE = NA_HEAD_DIM ** -0.5 * LOG2_E


def _natten_kernel(q_ref, k_ref, v_ref, t_ref, o_ref):
    def rows(i, carry):
        rs = [i * NA_ROWS_PER_STEP + u for u in range(NA_ROWS_PER_STEP)]
        kr0s = [jnp.clip(r - WIN_ROWS // 2, 0, GRID_ROWS - WIN_ROWS) for r in rs]
        q0s = [pl.multiple_of(r * GRID_W, GRID_W) for r in rs]
        k0s = [pl.multiple_of(kr0 * GRID_W, GRID_W) for kr0 in kr0s]
        ss = [lax.dot_general(q_ref[pl.ds(q0, GRID_W), :], k_ref[pl.ds(k0, NA_KEYS), :],
                              (((1,), (1,)), ((), ())), preferred_element_type=F32)
              for q0, k0 in zip(q0s, k0s)]
        ps, ls = [], []
        for s, r, kr0 in zip(ss, rs, kr0s):
            s = s + t_ref[kr0 - r + WIN_ROWS - 1]
            p = jnp.exp2(s - jnp.max(s, axis=-1, keepdims=True))
            ls.append(jnp.sum(p, axis=-1, keepdims=True))
            ps.append(p.astype(BF16))
        for p, l, q0, k0 in zip(ps, ls, q0s, k0s):
            o = jnp.dot(p, v_ref[pl.ds(k0, NA_KEYS), :], preferred_element_type=F32) / l
            o_ref[pl.ds(q0, GRID_W), :] = o.astype(o_ref.dtype)
        return carry

    lax.fori_loop(0, GRID_ROWS // NA_ROWS_PER_STEP, rows, 0)


def _natten(p, table):
    blk = (SEQ, NA_HEAD_DIM)
    return pl.pallas_call(
        _natten_kernel,
        out_shape=jax.ShapeDtypeStruct((N_TOK, NA_WIDTH), BF16),
        grid=(BATCH, NA_HEADS),
        in_specs=[
            pl.BlockSpec(blk, lambda b, h: (b, h)),
            pl.BlockSpec(blk, lambda b, h: (b, NA_HEADS + h)),
            pl.BlockSpec(blk, lambda b, h: (b, 2 * NA_HEADS + h)),
            pl.BlockSpec((None, WIN_ROWS, GRID_W, NA_KEYS), lambda b, h: (h, 0, 0, 0)),
        ],
        out_specs=pl.BlockSpec(blk, lambda b, h: (b, h)),
        compiler_params=_params("parallel", "parallel"),
        name="natten",
    )(p, p, p, table)


CONV_CB = 128


def _conv_kernel(gb_ref, gc_ref, hc_ref, w_ref, b_ref, o_ref):
    z = gc_ref[...].astype(F32) * hc_ref[...].astype(F32)
    pos = lax.broadcasted_iota(jnp.int32, z.shape, 0)
    z_prev = jnp.where(pos == 0, 0.0, pltpu.roll(z, 1, 0))
    z_next = jnp.where(pos == SEQ - 1, 0.0, pltpu.roll(z, SEQ - 1, 0))
    y = b_ref[...] + z_prev * w_ref[0:1, :] + z * w_ref[1:2, :] + z_next * w_ref[2:3, :]
    o_ref[...] = (gb_ref[...].astype(F32) * y).astype(o_ref.dtype)


def _gated_conv(p, w, b):
    base = 3 * NA_WIDTH // CONV_CB
    step = CONV_CH // CONV_CB
    blk = (SEQ, CONV_CB)
    return pl.pallas_call(
        _conv_kernel,
        out_shape=jax.ShapeDtypeStruct((N_TOK, CONV_CH), BF16),
        grid=(BATCH, step),
        in_specs=[
            pl.BlockSpec(blk, lambda bi, c: (bi, base + c)),
            pl.BlockSpec(blk, lambda bi, c: (bi, base + step + c)),
            pl.BlockSpec(blk, lambda bi, c: (bi, base + 2 * step + c)),
            pl.BlockSpec((3, CONV_CB), lambda bi, c: (0, c)),
            pl.BlockSpec((1, CONV_CB), lambda bi, c: (0, c)),
        ],
        out_specs=pl.BlockSpec(blk, lambda bi, c: (bi, c)),
        compiler_params=_params("parallel", "parallel"),
        name="gated_conv",
    )(p, p, p, w, b.reshape(1, CONV_CH))


def _out_proj_kernel(a_ref, b_ref, wa_ref, wb_ref, r_ref, o_ref):
    acc = jnp.dot(a_ref[...], wa_ref[...], preferred_element_type=F32)
    acc = acc + jnp.dot(b_ref[...], wb_ref[...], preferred_element_type=F32)
    o_ref[...] = r_ref[...] + acc


def _out_proj(a, b, w, res, tm=512):
    ka, kb = a.shape[1], b.shape[1]
    assert ka == kb
    return pl.pallas_call(
        _out_proj_kernel,
        out_shape=jax.ShapeDtypeStruct((N_TOK, D_MODEL), F32),
        grid=(N_TOK // tm,),
        in_specs=[
            pl.BlockSpec((tm, ka), lambda i: (i, 0)),
            pl.BlockSpec((tm, kb), lambda i: (i, 0)),
            pl.BlockSpec((ka, D_MODEL), lambda i: (0, 0)),
            pl.BlockSpec((kb, D_MODEL), lambda i: (1, 0)),
            pl.BlockSpec((tm, D_MODEL), lambda i: (i, 0)),
        ],
        out_specs=pl.BlockSpec((tm, D_MODEL), lambda i: (i, 0)),
        compiler_params=_params("parallel"),
        name="out_proj",
    )(a, b, w, w, res)


MLA_TM = 512


def _rope_tables():
    f32 = np.float32
    pos = np.arange(SEQ)
    row = (pos // GRID_W).astype(f32)
    col = (pos % GRID_W).astype(f32)
    half = QK_ROPE // 2
    inv = np.power(f32(ROPE_THETA), -np.arange(0, half, 2, dtype=f32) / f32(half)).astype(f32)
    ar, ac = row[:, None] * inv, col[:, None] * inv
    cos_t = jnp.asarray(np.concatenate([np.cos(ar), np.cos(ar), np.cos(ac), np.cos(ac)], axis=-1).astype(f32))
    sin_t = jnp.asarray(np.concatenate([-np.sin(ar), np.sin(ar), -np.sin(ac), np.sin(ac)], axis=-1).astype(f32))
    quarter = half // 2
    src = np.arange(QK_ROPE) + np.where((np.arange(QK_ROPE) // quarter) % 2 == 0, quarter, -quarter)
    swap = np.zeros((QK_ROPE, QK_ROPE), np.float32)
    swap[src, np.arange(QK_ROPE)] = 1.0
    swap_all = np.kron(np.eye(MLA_HEADS, dtype=np.float32), swap)
    return cos_t, sin_t, jnp.asarray(swap_all, BF16)


def _rope(x, cos_t, sin_t, swap):
    hi = x.astype(BF16)
    lo = (x - hi.astype(F32)).astype(BF16)
    xs = jnp.dot(hi, swap, preferred_element_type=F32) + jnp.dot(lo, swap, preferred_element_type=F32)
    return x * cos_t + xs * sin_t


ROPE_ALL = MLA_HEADS * QK_ROPE
NOPE_ALL = MLA_HEADS * QK_NOPE


def _mla_prep_kernel(cq_ref, ckv_ref, kr_ref, gq_ref, gkv_ref, wq_ref, wkv_ref, cos_ref, sin_ref,
                     swap_ref, q_ref, k_ref, v_ref):
    scale = QK_DIM ** -0.5 * LOG2_E
    cos_t, sin_t, swap = cos_ref[...], sin_ref[...], swap_ref[...]
    cqn = _rms(cq_ref[...].astype(F32), gq_ref[...]).astype(BF16)
    ckvn = _rms(ckv_ref[...].astype(F32), gkv_ref[...]).astype(BF16)
    kr = kr_ref[:, :QK_ROPE].astype(F32)
    krot = _rope(kr, cos_t, sin_t, swap[:QK_ROPE, :QK_ROPE]).astype(BF16)
    q = jnp.dot(cqn, wq_ref[...], preferred_element_type=F32)
    qn = (q[:, :NOPE_ALL] * scale).astype(BF16)
    cos_all = jnp.concatenate([cos_t] * MLA_HEADS, axis=1)
    sin_all = jnp.concatenate([sin_t] * MLA_HEADS, axis=1)
    qr = (_rope(q[:, NOPE_ALL:], cos_all, sin_all, swap) * scale).astype(BF16)
    kv = jnp.dot(ckvn, wkv_ref[...], preferred_element_type=F32)
    hw = QK_NOPE + V_DIM
    for h in range(MLA_HEADS):
        q_ref[h, :, :QK_NOPE] = qn[:, h * QK_NOPE:(h + 1) * QK_NOPE]
        q_ref[h, :, QK_NOPE:] = qr[:, h * QK_ROPE:(h + 1) * QK_ROPE]
        k_ref[h, :, :QK_NOPE] = kv[:, h * hw:h * hw + QK_NOPE].astype(BF16)
        k_ref[h, :, QK_NOPE:] = krot
        v_ref[h, 0] = kv[:, h * hw + QK_NOPE:(h + 1) * hw].T.astype(BF16)


def _mla_prep(p, k_rope, q_norm, kv_norm, wq, wkv):
    cos_t, sin_t, swap = _rope_tables()
    tm = MLA_TM
    seq_blocks = SEQ // tm
    return pl.pallas_call(
        _mla_prep_kernel,
        out_shape=(
            jax.ShapeDtypeStruct((MLA_HEADS, N_TOK, QK_DIM), BF16),
            jax.ShapeDtypeStruct((MLA_HEADS, N_TOK, QK_DIM), BF16),
            jax.ShapeDtypeStruct((MLA_HEADS, N_TOK // tm, V_DIM, tm), BF16),
        ),
        grid=(N_TOK // tm,),
        in_specs=[
            pl.BlockSpec((tm, Q_LORA), lambda i: (i, 0)),
            pl.BlockSpec((tm, KV_LORA), lambda i: (i, 1)),
            pl.BlockSpec((tm, LANES), lambda i: (i, 0)),
            pl.BlockSpec((1, Q_LORA), lambda i: (0, 0)),
            pl.BlockSpec((1, KV_LORA), lambda i: (0, 0)),
            pl.BlockSpec((Q_LORA, NOPE_ALL + ROPE_ALL), lambda i: (0, 0)),
            pl.BlockSpec((KV_LORA, MLA_HEADS * (QK_NOPE + V_DIM)), lambda i: (0, 0)),
            pl.BlockSpec((tm, QK_ROPE), lambda i: (i % seq_blocks, 0)),
            pl.BlockSpec((tm, QK_ROPE), lambda i: (i % seq_blocks, 0)),
            pl.BlockSpec((ROPE_ALL, ROPE_ALL), lambda i: (0, 0)),
        ],
        out_specs=(
            pl.BlockSpec((MLA_HEADS, tm, QK_DIM), lambda i: (0, i, 0)),
            pl.BlockSpec((MLA_HEADS, tm, QK_DIM), lambda i: (0, i, 0)),
            pl.BlockSpec((MLA_HEADS, 1, V_DIM, tm), lambda i: (0, i, 0, 0)),
        ),
        compiler_params=_params("parallel"),
        name="mla_prep",
    )(p, p, k_rope, q_norm.reshape(1, Q_LORA), kv_norm.reshape(1, KV_LORA), wq, wkv, cos_t, sin_t, swap)


MLA_TQ = 4096
MLA_TK = 512
MLA_QH = 16
assert MLA_TK == MLA_TM


def _mla_attn_kernel(q_ref, k_ref, vt_ref, o_ref, s_ref, m_ref, l_ref, acc_ref):
    n_chunks = SEQ // MLA_TK
    cols = MLA_TQ // MLA_QH

    def scores(h, c):
        k0 = c * MLA_TK if isinstance(c, int) else pl.multiple_of(c * MLA_TK, MLA_TK)
        return lax.dot_general(k_ref[pl.ds(k0, MLA_TK), :], q_ref[h * cols:(h + 1) * cols, :],
                               (((1,), (1,)), ((), ())), preferred_element_type=F32)

    m_ref[...] = jnp.full(m_ref.shape, NEG_INF, F32)
    l_ref[...] = jnp.zeros(l_ref.shape, F32)
    acc_ref[...] = jnp.zeros(acc_ref.shape, F32)
    for h in range(MLA_QH):
        s_ref[0, h] = scores(h, 0)

    def step(c, slot, prefetch=True):
        for h in range(MLA_QH):
            if prefetch:
                s_ref[1 - slot, h] = scores(h, c + 1)
            s = s_ref[slot, h]
            m_old = m_ref[h]
            m_new = jnp.maximum(m_old, jnp.max(s, axis=0, keepdims=True))
            p = jnp.exp2(s - m_new)
            alpha = jnp.exp2(m_old - m_new)
            l_ref[h] = alpha * l_ref[h] + jnp.sum(p, axis=0, keepdims=True)
            pv = jnp.dot(vt_ref[c], p.astype(BF16), preferred_element_type=F32)
            acc_ref[h] = alpha * acc_ref[h] + pv
            m_ref[h] = m_new

    def pair(i, carry):
        step(2 * i, 0)
        step(2 * i + 1, 1)
        return carry

    lax.fori_loop(0, n_chunks // 2 - 1, pair, 0)
    step(n_chunks - 2, 0)
    step(n_chunks - 1, 1, prefetch=False)
    for h in range(MLA_QH):
        o_ref[h * cols:(h + 1) * cols, :] = (acc_ref[h] / l_ref[h]).T.astype(o_ref.dtype)


def _mla_attn(q, k, v):
    nq = SEQ // MLA_TQ
    return pl.pallas_call(
        _mla_attn_kernel,
        out_shape=jax.ShapeDtypeStruct((N_TOK, MLA_HEADS * V_DIM), BF16),
        grid=(BATCH, MLA_HEADS, nq),
        in_specs=[
            pl.BlockSpec((None, MLA_TQ, QK_DIM), lambda b, h, i: (h, b * nq + i, 0)),
            pl.BlockSpec((None, SEQ, QK_DIM), lambda b, h, i: (h, b, 0)),
            pl.BlockSpec((None, SEQ // MLA_TK, V_DIM, MLA_TK), lambda b, h, i: (h, b, 0, 0)),
        ],
        out_specs=pl.BlockSpec((MLA_TQ, V_DIM), lambda b, h, i: (b * nq + i, h)),
        scratch_shapes=[
            pltpu.VMEM((2, MLA_QH, MLA_TK, MLA_TQ // MLA_QH), F32),
            pltpu.VMEM((MLA_QH, 1, MLA_TQ // MLA_QH), F32),
            pltpu.VMEM((MLA_QH, 1, MLA_TQ // MLA_QH), F32),
            pltpu.VMEM((MLA_QH, V_DIM, MLA_TQ // MLA_QH), F32),
        ],
        compiler_params=_params("parallel", "parallel", "parallel"),
        name="mla_attn",
    )(q, k, v)


SG_TM = 256


def _sg_kernel(u_ref, v_ref, g_ref, w_ref, bt_ref, o_ref):
    v = jax.nn.gelu(v_ref[...].astype(F32))
    vn = _rms(v, g_ref[...]).astype(BF16)
    u = jax.nn.gelu(u_ref[...].astype(F32))
    gc = SG_CH // SG_GROUPS
    for n in range(SG_TM // CHUNK):
        rows = slice(n * CHUNK, (n + 1) * CHUNK)
        for g in range(SG_GROUPS):
            cols = slice(g * gc, (g + 1) * gc)
            mixed = jnp.dot(w_ref[g], vn[rows, cols], preferred_element_type=F32) + bt_ref[:, g:g + 1]
            o_ref[rows, cols] = (u[rows, cols] * mixed).astype(o_ref.dtype)


def _spatial_gating(p, g_norm, w_s, b_s):
    u_blk = (Q_LORA + KV_LORA) // SG_CH
    return pl.pallas_call(
        _sg_kernel,
        out_shape=jax.ShapeDtypeStruct((N_TOK, SG_CH), BF16),
        grid=(N_TOK // SG_TM,),
        in_specs=[
            pl.BlockSpec((SG_TM, SG_CH), lambda i: (i, u_blk)),
            pl.BlockSpec((SG_TM, SG_CH), lambda i: (i, u_blk + 1)),
            pl.BlockSpec((1, SG_CH), lambda i: (0, 0)),
            pl.BlockSpec((SG_GROUPS, CHUNK, CHUNK), lambda i: (0, 0, 0)),
            pl.BlockSpec((CHUNK, SG_GROUPS), lambda i: (0, 0)),
        ],
        out_specs=pl.BlockSpec((SG_TM, SG_CH), lambda i: (i, 0)),
        compiler_params=_params("parallel"),
        name="spatial_gating",
    )(p, p, g_norm.reshape(1, SG_CH), w_s.astype(BF16), b_s.T)


ROUTER_TM = 512


def _router_kernel(x_ref, g_ref, w_ref, b_ref, o_ref, xn_ref):
    xn = _rms(x_ref[...], g_ref[...])
    xn_ref[...] = _pack_halves(xn)
    w = w_ref[...]
    xh, wh = xn.astype(BF16), w.astype(BF16)
    xl, wl = (xn - xh.astype(F32)).astype(BF16), (w - wh.astype(F32)).astype(BF16)
    logits = (jnp.dot(xh, wh, preferred_element_type=F32)
              + (jnp.dot(xh, wl, preferred_element_type=F32) + jnp.dot(xl, wh, preferred_element_type=F32))
              + b_ref[...])
    lane = lax.broadcasted_iota(jnp.int32, logits.shape, 1).astype(F32)
    low = jnp.float32(-3.0e38)

    def first_max(vals):
        top = jnp.max(vals, axis=-1, keepdims=True)
        idx = jnp.min(jnp.where(vals == top, lane, float(LANES)), axis=-1, keepdims=True)
        return top, idx

    is_group = lane < N_GROUPS
    g_top, g_idx = first_max(jnp.where(is_group, logits, low))
    g_prob = 1.0 / jnp.sum(jnp.where(is_group, jnp.exp(logits - g_top), 0.0), axis=-1, keepdims=True)
    lo = N_GROUPS + g_idx * EXPERTS_PER_GROUP
    e_vals = jnp.where(lane >= lo, jnp.where(lane < lo + EXPERTS_PER_GROUP, logits, low), low)
    v1, i1 = first_max(e_vals)
    v2, i2 = first_max(jnp.where(lane == i1, low, e_vals))
    e21 = jnp.exp(v2 - v1)
    w1 = g_prob / (1.0 + e21)
    w2 = g_prob * e21 / (1.0 + e21)
    out = jnp.where(lane == 0, i1 - N_GROUPS,
                    jnp.where(lane == 1, i2 - N_GROUPS,
                              jnp.where(lane == 2, w1, jnp.where(lane == 3, w2, 0.0))))
    o_ref[...] = out


def _router(x, g, w, b):
    tm = ROUTER_TM
    return pl.pallas_call(
        _router_kernel,
        out_shape=(jax.ShapeDtypeStruct((N_TOK, LANES), F32), jax.ShapeDtypeStruct((N_TOK, HALF), jnp.uint32)),
        grid=(N_TOK // tm,),
        in_specs=[
            pl.BlockSpec((tm, D_MODEL), lambda i: (i, 0)),
            pl.BlockSpec((1, D_MODEL), lambda i: (0, 0)),
            pl.BlockSpec((D_MODEL, LANES), lambda i: (0, 0)),
            pl.BlockSpec((1, LANES), lambda i: (0, 0)),
        ],
        out_specs=(pl.BlockSpec((tm, LANES), lambda i: (i, 0)), pl.BlockSpec((tm, HALF), lambda i: (i, 0))),
        compiler_params=_params("parallel"),
        name="router",
    )(x, g.reshape(1, D_MODEL), w, b)


def _dispatch_plan(eid):
    tok = jnp.arange(N_TOK, dtype=jnp.int32)
    keys = jnp.concatenate([eid[:, k] * N_ASSIGN + (k * N_TOK + tok) for k in range(TOP_K)])
    row_asg = jnp.sort(keys) & (N_ASSIGN - 1)
    row_asg = jnp.concatenate([row_asg, jnp.zeros((SUBLANES,), jnp.int32)])
    counts = jnp.sum(eid.reshape(-1)[:, None] == jnp.arange(N_EXPERTS, dtype=jnp.int32)[None, :], axis=0,
                     dtype=jnp.int32)
    starts = jnp.cumsum(counts) - counts
    nblk_e = (counts + MOE_BLOCK - 1) // MOE_BLOCK
    blk_end = jnp.cumsum(nblk_e)
    blk_start = blk_end - nblk_e
    n_active = blk_end[-1]
    bidx = jnp.arange(MOE_NBLK, dtype=jnp.int32)
    blk_x = jnp.minimum(bidx, n_active - 1)
    blk_e = jnp.minimum(jnp.sum(blk_end[None, :] <= blk_x[:, None], axis=1, dtype=jnp.int32), N_EXPERTS - 1)
    first = (blk_x - blk_start[blk_e]) * MOE_BLOCK
    blk_rows = jnp.where(bidx < n_active, jnp.clip(counts[blk_e] - first, 0, MOE_BLOCK), 0).astype(jnp.int32)
    blk_src = (starts[blk_e] + first).astype(jnp.int32)
    return row_asg, blk_e, blk_rows, blk_src


GATHER_STEP = min(1, N_FT - 1)
_ROW_BITS = tuple(1 << k for k in range(MOE_BLOCK.bit_length() - 1, -1, -1))


def _expert_kernel(row_asg, blk_e, blk_rows, blk_src, x_hbm, w1_ref, w3_ref, w2_ref, ys_hbm,
                   xbuf, acc, ypk, xn_ref, gsem, ssem):
    b = pl.program_id(0)
    j = pl.program_id(1)
    rows = blk_rows[b]
    slot = b % 2

    def n_fetch(blk):
        return (blk_rows[blk] + SUBLANES - 1) & -SUBLANES

    def issue_gather(blk, s):
        base = blk_src[blk]

        def group(q, c):
            for u in range(SUBLANES):
                t = row_asg[base + q * SUBLANES + u] & (N_TOK - 1)
                pltpu.make_async_copy(x_hbm.at[pl.ds(t, 1), :], xbuf.at[s, q, pl.ds(u, 1), :], gsem.at[s]).start()
            return c

        lax.fori_loop(0, n_fetch(blk) >> SUBLANE_SHIFT, group, 0)

    def wait_rows(n, copy_of):
        for bit in _ROW_BITS:
            @pl.when((n & bit) != 0)
            def _():
                copy_of(bit).wait()

    def gather_block(s):
        def copy_of(k):
            tiles = xbuf.at[s, pl.ds(0, k // SUBLANES)]
            return pltpu.make_async_copy(tiles, tiles, gsem.at[s])
        return copy_of

    def scatter_block(s):
        def copy_of(k):
            part = ypk.at[s, pl.ds(0, k // SUBLANES)] if k >= SUBLANES else ypk.at[s, 0, pl.ds(0, k), :]
            return pltpu.make_async_copy(part, part, ssem.at[s])
        return copy_of

    def issue_scatter(blk, s, n):
        base = blk_src[blk]

        def one(q, u, a):
            pltpu.make_async_copy(ypk.at[s, q, pl.ds(u, 1), :], ys_hbm.at[pl.ds(a, 1), :], ssem.at[s]).start()

        def group(q, c):
            for u in range(SUBLANES):
                one(q, u, row_asg[base + q * SUBLANES + u])
            return c

        def tail(i, c):
            one(i >> SUBLANE_SHIFT, i & (SUBLANES - 1), row_asg[base + i])
            return c

        full = n >> SUBLANE_SHIFT
        lax.fori_loop(0, full, group, 0)
        lax.fori_loop(full * SUBLANES, n, tail, 0)

    def for_row_tile(fn):
        lo = 0
        for m in MOE_ROW_TILES:
            @pl.when(jnp.logical_and(rows > lo, rows <= m))
            def _():
                fn(m)
            lo = m

    nxt = jnp.minimum(b + 1, MOE_NBLK - 1)
    has_next = jnp.logical_and(b + 1 < MOE_NBLK, blk_rows[nxt] > 0)

    @pl.when(rows > 0)
    def _():
        @pl.when(j == 0)
        def _():
            @pl.when(b == 0)
            def _():
                xbuf[...] = jnp.zeros(xbuf.shape, jnp.uint32)
                acc[...] = jnp.zeros(acc.shape, F32)
                issue_gather(0, 0)

            wait_rows(n_fetch(b), gather_block(slot))

            def unpack_tile(m):
                lo, hi = _unpack_halves(xbuf[slot, :m // SUBLANES].reshape(m, HALF))
                xn_ref[:m, :HALF] = lo.astype(BF16)
                xn_ref[:m, HALF:] = hi.astype(BF16)

            for_row_tile(unpack_tile)

        @pl.when(jnp.logical_and(j == GATHER_STEP, has_next))
        def _():
            issue_gather(nxt, 1 - slot)

        def ffn_tile(m):
            xs = xn_ref[:m, :]
            h1 = jnp.dot(xs, w1_ref[...].astype(BF16), preferred_element_type=F32)
            h3 = jnp.dot(xs, w3_ref[...].astype(BF16), preferred_element_type=F32)
            h = (jax.nn.silu(h1) * h3).astype(BF16)
            y = jnp.dot(h, w2_ref[...].astype(BF16), preferred_element_type=F32)
            acc[:m, :] = jnp.where(j > 0, acc[:m, :], 0.0) + y

        for_row_tile(ffn_tile)

        @pl.when(j == N_FT - 1)
        def _():
            @pl.when(b > 0)
            def _():
                wait_rows(blk_rows[jnp.maximum(b - 1, 0)], scatter_block(1 - slot))

            def pack_tile(m):
                ypk[slot, :m // SUBLANES] = _pack_halves(acc[:m, :]).reshape(m // SUBLANES, SUBLANES, HALF)

            for_row_tile(pack_tile)

            issue_scatter(b, slot, rows)

            @pl.when(jnp.logical_not(has_next))
            def _():
                wait_rows(rows, scatter_block(slot))


def _moe_experts(xn, w1, w3, w2, layer, row_asg, blk_e, blk_rows, blk_src):
    def w_col(b, j, ra, be, br, bs):
        return (layer, be[b], 0, jnp.where(br[b] > 0, j, N_FT - 1))

    def w_row(b, j, ra, be, br, bs):
        return (layer, be[b], jnp.where(br[b] > 0, j, N_FT - 1), 0)

    return pl.pallas_call(
        _expert_kernel,
        out_shape=jax.ShapeDtypeStruct((N_ASSIGN, HALF), jnp.uint32),
        grid_spec=pltpu.PrefetchScalarGridSpec(
            num_scalar_prefetch=4,
            grid=(MOE_NBLK, N_FT),
            in_specs=[
                pl.BlockSpec(memory_space=pl.ANY),
                pl.BlockSpec((None, None, D_MODEL, MOE_FT), w_col),
                pl.BlockSpec((None, None, D_MODEL, MOE_FT), w_col),
                pl.BlockSpec((None, None, MOE_FT, D_MODEL), w_row),
            ],
            out_specs=pl.BlockSpec(memory_space=pl.ANY),
            scratch_shapes=[
                pltpu.VMEM((2, MOE_BLOCK // SUBLANES, SUBLANES, HALF), jnp.uint32),
                pltpu.VMEM((MOE_BLOCK, D_MODEL), F32),
                pltpu.VMEM((2, MOE_BLOCK // SUBLANES, SUBLANES, HALF), jnp.uint32),
                pltpu.VMEM((MOE_BLOCK, D_MODEL), BF16),
                pltpu.SemaphoreType.DMA((2,)),
                pltpu.SemaphoreType.DMA((2,)),
            ],
        ),
        compiler_params=_params("arbitrary", "arbitrary"),
        name="moe_experts",
    )(row_asg, blk_e, blk_rows, blk_src, xn, w1, w3, w2)


def _combine_kernel(final_norm, x_ref, y0_ref, y1_ref, gate_ref, gf_ref, o_ref):
    gates = gate_ref[...]
    g0, g1 = gates[:, 2:3], gates[:, 3:4]
    lo0, hi0 = _unpack_halves(y0_ref[...])
    lo1, hi1 = _unpack_halves(y1_ref[...])
    lo = x_ref[:, :HALF] + (g0 * lo0 + g1 * lo1)
    hi = x_ref[:, HALF:] + (g0 * hi0 + g1 * hi1)
    if final_norm:
        ms = (jnp.sum(lo * lo, axis=-1, keepdims=True) + jnp.sum(hi * hi, axis=-1, keepdims=True)) / D_MODEL
        r = lax.rsqrt(ms + EPS)
        lo, hi = lo * r * gf_ref[:, :HALF], hi * r * gf_ref[:, HALF:]
    o_ref[:, :HALF] = lo
    o_ref[:, HALF:] = hi


def _moe_combine(x, ys, route, g_final, final_norm):
    tm = COMBINE_TOK
    return pl.pallas_call(
        functools.partial(_combine_kernel, final_norm),
        out_shape=jax.ShapeDtypeStruct((N_TOK, D_MODEL), F32),
        grid=(N_TOK // tm,),
        in_specs=[
            pl.BlockSpec((tm, D_MODEL), lambda i: (i, 0)),
            pl.BlockSpec((tm, HALF), lambda i: (i, 0)),
            pl.BlockSpec((tm, HALF), lambda i: (N_TOK // tm + i, 0)),
            pl.BlockSpec((tm, LANES), lambda i: (i, 0)),
            pl.BlockSpec((1, D_MODEL), lambda i: (0, 0)),
        ],
        out_specs=pl.BlockSpec((tm, D_MODEL), lambda i: (i, 0)),
        compiler_params=_params("parallel"),
        name="moe_combine",
    )(x, ys, ys, route, g_final.reshape(1, D_MODEL))


def _hier_moe(x, g, wg, bg, we, be, w1, w3, w2, layer, g_final, final_norm):
    pad = LANES - N_GROUPS - N_EXPERTS
    w_r = jnp.concatenate([wg, we, jnp.zeros((D_MODEL, pad), F32)], axis=1)
    b_r = jnp.concatenate([bg, be, jnp.zeros((pad,), F32)]).reshape(1, LANES)
    route, xn = _router(x, g, w_r, b_r)
    eid = route[:, :TOP_K].astype(jnp.int32)
    row_asg, blk_e, blk_rows, blk_src = _dispatch_plan(eid)
    ys = _moe_experts(xn, w1, w3, w2, layer, row_asg, blk_e, blk_rows, blk_src)
    return _moe_combine(x, ys, route, g_final, final_norm)


def kernel(x, norm_mix, norm_ffn, norm_final, w_in_ab, na_rpb, conv_w, conv_b, w_out_ab, w_in_cd,
           q_norm, kv_norm, w_uq, w_ukv, sg_norm, sg_w, sg_b, w_out_cd, router_group_w,
           router_group_b, router_expert_w, router_expert_b, w1, w3, w2):
    xt = x.reshape(N_TOK, D_MODEL)
    for layer in range(DEPTH):
        i = layer // 2
        if layer % 2 == 0:
            q_scale = np.where(np.arange(AB_IN) < NA_WIDTH, NA_Q_SCALE, 1.0).astype(np.float32)
            p = _norm_matmul(xt, norm_mix[layer], (w_in_ab[i] * q_scale).astype(BF16), tm=1024, tn=1024)
            a_out = _natten(p, _natten_bias_table(na_rpb[i]))
            b_out = _gated_conv(p, conv_w[i], conv_b[i])
            xt = _out_proj(a_out, b_out, w_out_ab[i].astype(BF16), xt)
        else:
            p, k_rope = _cd_in_proj(xt, norm_mix[layer], w_in_cd[i])
            wq = w_uq[i].reshape(Q_LORA, MLA_HEADS, QK_DIM)
            wq = jnp.concatenate([wq[:, :, :QK_NOPE].reshape(Q_LORA, NOPE_ALL),
                                  wq[:, :, QK_NOPE:].reshape(Q_LORA, ROPE_ALL)], axis=1).astype(BF16)
            q, k, v = _mla_prep(p, k_rope, q_norm[i], kv_norm[i], wq, w_ukv[i].astype(BF16))
            c_out = _mla_attn(q, k, v)
            d_out = _spatial_gating(p, sg_norm[i], sg_w[i], sg_b[i])
            xt = _out_proj(c_out, d_out, w_out_cd[i].astype(BF16), xt)
        xt = _hier_moe(xt, norm_ffn[layer], router_group_w[layer], router_group_b[layer],
                       router_expert_w[layer], router_expert_b[layer], w1, w3, w2, layer,
                       norm_final, final_norm=(layer == DEPTH - 1))
    return xt.reshape(BATCH, SEQ, D_MODEL)
```

```python
import functools

import numpy as np
import jax
import jax.numpy as jnp
from jax import lax
from jax.experimental import pallas as pl
from jax.experimental.pallas import tpu as pltpu

F32 = jnp.float32
BF16 = jnp.bfloat16

D_MODEL = 2048
BATCH = 2
SEQ = 4096
N_TOK = BATCH * SEQ
DEPTH = 2
GRID_W = 64
GRID_ROWS = SEQ // GRID_W
WIN_ROWS = 8
WIN_COLS = 16
NA_HEADS = 8
NA_HEAD_DIM = 128
NA_WIDTH = NA_HEADS * NA_HEAD_DIM
NA_KEYS = WIN_ROWS * GRID_W
CONV_CH = D_MODEL // 2
MLA_HEADS = 8
Q_LORA = 512
KV_LORA = 512
QK_NOPE = 128
QK_ROPE = 64
QK_DIM = QK_NOPE + QK_ROPE
V_DIM = 128
ROPE_THETA = 10000.0
SG_CH = D_MODEL // 2
SG_GROUPS = 8
CHUNK = 128
N_GROUPS = 8
EXPERTS_PER_GROUP = 8
N_EXPERTS = N_GROUPS * EXPERTS_PER_GROUP
TOP_K = 2
D_EXPERT = 768
EPS = 1e-6
NEG_INF = -1e30
LOG2_E = 1.4426950408889634
AB_IN = 3 * NA_WIDTH + 3 * CONV_CH

LANES = 128
SUBLANES = 8
SUBLANE_SHIFT = 3
MOE_BLOCK = 512
MOE_ROW_TILES = (256, 320, 512)
MOE_FT = 256
N_FT = D_EXPERT // MOE_FT
N_ASSIGN = N_TOK * TOP_K
MOE_NBLK = N_ASSIGN // MOE_BLOCK + N_EXPERTS
COMBINE_TOK = 512
VMEM_LIMIT = 52 * 1024 * 1024


def _params(*sem):
    return pltpu.CompilerParams(dimension_semantics=sem, vmem_limit_bytes=VMEM_LIMIT)


def _rms(x, g):
    return x * lax.rsqrt(jnp.mean(x * x, axis=-1, keepdims=True) + EPS) * g


HALF = D_MODEL // 2
HIGH_HALF = 0xFFFF0000


def _pack_halves(x):
    lo = pltpu.bitcast(x[:, :HALF].astype(BF16).astype(F32), jnp.uint32)
    hi = pltpu.bitcast(x[:, HALF:].astype(BF16).astype(F32), jnp.uint32)
    return (lo >> 16) | (hi & jnp.uint32(HIGH_HALF))


def _unpack_halves(w):
    return pltpu.bitcast(w << 16, F32), pltpu.bitcast(w & jnp.uint32(HIGH_HALF), F32)


def _norm_mm_kernel(x_ref, g_ref, w_ref, o_ref, xn_ref):
    @pl.when(pl.program_id(1) == 0)
    def _():
        xn_ref[...] = _rms(x_ref[...].astype(F32), g_ref[...]).astype(BF16)

    o_ref[...] = jnp.dot(xn_ref[...], w_ref[...], preferred_element_type=F32).astype(o_ref.dtype)


def _norm_matmul(x, g, w, tm, tn):
    m, k = x.shape
    nout = w.shape[1]
    return pl.pallas_call(
        _norm_mm_kernel,
        out_shape=jax.ShapeDtypeStruct((m, nout), BF16),
        grid=(m // tm, nout // tn),
        in_specs=[
            pl.BlockSpec((tm, k), lambda i, j: (i, 0)),
            pl.BlockSpec((1, k), lambda i, j: (0, 0)),
            pl.BlockSpec((k, tn), lambda i, j: (0, j)),
        ],
        out_specs=pl.BlockSpec((tm, tn), lambda i, j: (i, j)),
        scratch_shapes=[pltpu.VMEM((tm, k), BF16)],
        compiler_params=_params("parallel", "arbitrary"),
        name="norm_matmul",
    )(x, g.reshape(1, k), w)


CD_TN = 1024
CD_MAIN = Q_LORA + KV_LORA + 2 * SG_CH
CD_A_TILES = (Q_LORA + KV_LORA) // CD_TN
CD_MAIN_TILES = CD_MAIN // CD_TN


def _cd_proj_kernel(x_ref, g_ref, wa_ref, wb_ref, wc_ref, main_ref, kr_ref, xn_ref):
    j = pl.program_id(1)

    @pl.when(j == 0)
    def _():
        xn_ref[...] = _rms(x_ref[...], g_ref[...]).astype(BF16)

    @pl.when(j < CD_A_TILES)
    def _():
        main_ref[...] = jnp.dot(xn_ref[...], wa_ref[...], preferred_element_type=F32).astype(BF16)

    @pl.when(j >= CD_A_TILES)
    def _():
        main_ref[...] = jnp.dot(xn_ref[...], wb_ref[...], preferred_element_type=F32).astype(BF16)

    @pl.when(j == 0)
    def _():
        kr_ref[...] = jnp.dot(xn_ref[...], wc_ref[...], preferred_element_type=F32).astype(BF16)


def _cd_in_proj(x, g, w, tm=1024):
    c0, c1 = Q_LORA + KV_LORA, Q_LORA + KV_LORA + QK_ROPE
    wa = w[:, :c0].astype(BF16)
    wb = w[:, c1:].astype(BF16)
    wc = jnp.pad(w[:, c0:c1], ((0, 0), (0, LANES - QK_ROPE))).astype(BF16)
    return pl.pallas_call(
        _cd_proj_kernel,
        out_shape=(jax.ShapeDtypeStruct((N_TOK, CD_MAIN), BF16), jax.ShapeDtypeStruct((N_TOK, LANES), BF16)),
        grid=(N_TOK // tm, CD_MAIN_TILES),
        in_specs=[
            pl.BlockSpec((tm, D_MODEL), lambda i, j: (i, 0)),
            pl.BlockSpec((1, D_MODEL), lambda i, j: (0, 0)),
            pl.BlockSpec((D_MODEL, CD_TN), lambda i, j: (0, jnp.minimum(j, CD_A_TILES - 1))),
            pl.BlockSpec((D_MODEL, CD_TN), lambda i, j: (0, jnp.maximum(j - CD_A_TILES, 0))),
            pl.BlockSpec((D_MODEL, LANES), lambda i, j: (0, 0)),
        ],
        out_specs=(
            pl.BlockSpec((tm, CD_TN), lambda i, j: (i, j)),
            pl.BlockSpec((tm, LANES), lambda i, j: (i, 0)),
        ),
        scratch_shapes=[pltpu.VMEM((tm, D_MODEL), BF16)],
        compiler_params=_params("parallel", "arbitrary"),
        name="cd_in_proj",
    )(x, g.reshape(1, D_MODEL), wa, wb, wc)


def _natten_bias_table(rpb):
    c = np.arange(GRID_W)
    col_start = np.clip(c - WIN_COLS // 2, 0, GRID_W - WIN_COLS)
    valid = (c[None, :] >= col_start[:, None]) & (c[None, :] < col_start[:, None] + WIN_COLS)
    dc = np.clip(c[None, :] - c[:, None] + WIN_COLS - 1, 0, 2 * WIN_COLS - 2)
    pick = (dc[:, :, None] == np.arange(2 * WIN_COLS - 1)).astype(np.float32)
    m = jnp.einsum('hrd,ckd->hcrk', rpb * LOG2_E, jnp.asarray(pick), precision=lax.Precision.HIGHEST)
    m = jnp.where(valid[None, :, None, :], m, NEG_INF)
    t = jnp.stack([m[:, :, o:o + WIN_ROWS] for o in range(WIN_ROWS)], axis=1)
    return t.reshape(rpb.shape[0], WIN_ROWS, GRID_W, NA_KEYS).astype(F32)


NA_ROWS_PER_STEP = 32
NA_Q_SCALE = NA_HEAD_DIM ** -0.5 * LOG2_E


def _natten_kernel(q_ref, k_ref, v_ref, t_ref, o_ref):
    def rows(i, carry):
        rs = [i * NA_ROWS_PER_STEP + u for u in range(NA_ROWS_PER_STEP)]
        kr0s = [jnp.clip(r - WIN_ROWS // 2, 0, GRID_ROWS - WIN_ROWS) for r in rs]
        q0s = [pl.multiple_of(r * GRID_W, GRID_W) for r in rs]
        k0s = [pl.multiple_of(kr0 * GRID_W, GRID_W) for kr0 in kr0s]
        ss = [lax.dot_general(q_ref[pl.ds(q0, GRID_W), :], k_ref[pl.ds(k0, NA_KEYS), :],
                              (((1,), (1,)), ((), ())), preferred_element_type=F32)
              for q0, k0 in zip(q0s, k0s)]
        ps, ls = [], []
        for s, r, kr0 in zip(ss, rs, kr0s):
            s = s + t_ref[kr0 - r + WIN_ROWS - 1]
            p = jnp.exp2(s - jnp.max(s, axis=-1, keepdims=True))
            ls.append(jnp.sum(p, axis=-1, keepdims=True))
            ps.append(p.astype(BF16))
        for p, l, q0, k0 in zip(ps, ls, q0s, k0s):
            o = jnp.dot(p, v_ref[pl.ds(k0, NA_KEYS), :], preferred_element_type=F32) / l
            o_ref[pl.ds(q0, GRID_W), :] = o.astype(o_ref.dtype)
        return carry

    lax.fori_loop(0, GRID_ROWS // NA_ROWS_PER_STEP, rows, 0)


def _natten(p, table):
    blk = (SEQ, NA_HEAD_DIM)
    return pl.pallas_call(
        _natten_kernel,
        out_shape=jax.ShapeDtypeStruct((N_TOK, NA_WIDTH), BF16),
        grid=(BATCH, NA_HEADS),
        in_specs=[
            pl.BlockSpec(blk, lambda b, h: (b, h)),
            pl.BlockSpec(blk, lambda b, h: (b, NA_HEADS + h)),
            pl.BlockSpec(blk, lambda b, h: (b, 2 * NA_HEADS + h)),
            pl.BlockSpec((None, WIN_ROWS, GRID_W, NA_KEYS), lambda b, h: (h, 0, 0, 0)),
        ],
        out_specs=pl.BlockSpec(blk, lambda b, h: (b, h)),
        compiler_params=_params("parallel", "parallel"),
        name="natten",
    )(p, p, p, table)


CONV_CB = 128


def _conv_kernel(gb_ref, gc_ref, hc_ref, w_ref, b_ref, o_ref):
    z = gc_ref[...].astype(F32) * hc_ref[...].astype(F32)
    pos = lax.broadcasted_iota(jnp.int32, z.shape, 0)
    z_prev = jnp.where(pos == 0, 0.0, pltpu.roll(z, 1, 0))
    z_next = jnp.where(pos == SEQ - 1, 0.0, pltpu.roll(z, SEQ - 1, 0))
    y = b_ref[...] + z_prev * w_ref[0:1, :] + z * w_ref[1:2, :] + z_next * w_ref[2:3, :]
    o_ref[...] = (gb_ref[...].astype(F32) * y).astype(o_ref.dtype)


def _gated_conv(p, w, b):
    base = 3 * NA_WIDTH // CONV_CB
    step = CONV_CH // CONV_CB
    blk = (SEQ, CONV_CB)
    return pl.pallas_call(
        _conv_kernel,
        out_shape=jax.ShapeDtypeStruct((N_TOK, CONV_CH), BF16),
        grid=(BATCH, step),
        in_specs=[
            pl.BlockSpec(blk, lambda bi, c: (bi, base + c)),
            pl.BlockSpec(blk, lambda bi, c: (bi, base + step + c)),
            pl.BlockSpec(blk, lambda bi, c: (bi, base + 2 * step + c)),
            pl.BlockSpec((3, CONV_CB), lambda bi, c: (0, c)),
            pl.BlockSpec((1, CONV_CB), lambda bi, c: (0, c)),
        ],
        out_specs=pl.BlockSpec(blk, lambda bi, c: (bi, c)),
        compiler_params=_params("parallel", "parallel"),
        name="gated_conv",
    )(p, p, p, w, b.reshape(1, CONV_CH))


def _out_proj_kernel(a_ref, b_ref, wa_ref, wb_ref, r_ref, o_ref):
    acc = jnp.dot(a_ref[...], wa_ref[...], preferred_element_type=F32)
    acc = acc + jnp.dot(b_ref[...], wb_ref[...], preferred_element_type=F32)
    o_ref[...] = r_ref[...] + acc


def _out_proj(a, b, w, res, tm=512):
    ka, kb = a.shape[1], b.shape[1]
    assert ka == kb
    return pl.pallas_call(
        _out_proj_kernel,
        out_shape=jax.ShapeDtypeStruct((N_TOK, D_MODEL), F32),
        grid=(N_TOK // tm,),
        in_specs=[
            pl.BlockSpec((tm, ka), lambda i: (i, 0)),
            pl.BlockSpec((tm, kb), lambda i: (i, 0)),
            pl.BlockSpec((ka, D_MODEL), lambda i: (0, 0)),
            pl.BlockSpec((kb, D_MODEL), lambda i: (1, 0)),
            pl.BlockSpec((tm, D_MODEL), lambda i: (i, 0)),
        ],
        out_specs=pl.BlockSpec((tm, D_MODEL), lambda i: (i, 0)),
        compiler_params=_params("parallel"),
        name="out_proj",
    )(a, b, w, w, res)


MLA_TM = 512


def _rope_tables():
    f32 = np.float32
    pos = np.arange(SEQ)
    row = (pos // GRID_W).astype(f32)
    col = (pos % GRID_W).astype(f32)
    half = QK_ROPE // 2
    inv = np.power(f32(ROPE_THETA), -np.arange(0, half, 2, dtype=f32) / f32(half)).astype(f32)
    ar, ac = row[:, None] * inv, col[:, None] * inv
    cos_t = jnp.asarray(np.concatenate([np.cos(ar), np.cos(ar), np.cos(ac), np.cos(ac)], axis=-1).astype(f32))
    sin_t = jnp.asarray(np.concatenate([-np.sin(ar), np.sin(ar), -np.sin(ac), np.sin(ac)], axis=-1).astype(f32))
    quarter = half // 2
    src = np.arange(QK_ROPE) + np.where((np.arange(QK_ROPE) // quarter) % 2 == 0, quarter, -quarter)
    swap = np.zeros((QK_ROPE, QK_ROPE), np.float32)
    swap[src, np.arange(QK_ROPE)] = 1.0
    swap_all = np.kron(np.eye(MLA_HEADS, dtype=np.float32), swap)
    return cos_t, sin_t, jnp.asarray(swap_all, BF16)


def _rope(x, cos_t, sin_t, swap):
    hi = x.astype(BF16)
    lo = (x - hi.astype(F32)).astype(BF16)
    xs = jnp.dot(hi, swap, preferred_element_type=F32) + jnp.dot(lo, swap, preferred_element_type=F32)
    return x * cos_t + xs * sin_t


ROPE_ALL = MLA_HEADS * QK_ROPE
NOPE_ALL = MLA_HEADS * QK_NOPE


def _mla_prep_kernel(cq_ref, ckv_ref, kr_ref, gq_ref, gkv_ref, wq_ref, wkv_ref, cos_ref, sin_ref,
                     swap_ref, q_ref, k_ref, v_ref):
    scale = QK_DIM ** -0.5 * LOG2_E
    cos_t, sin_t, swap = cos_ref[...], sin_ref[...], swap_ref[...]
    cqn = _rms(cq_ref[...].astype(F32), gq_ref[...]).astype(BF16)
    ckvn = _rms(ckv_ref[...].astype(F32), gkv_ref[...]).astype(BF16)
    kr = kr_ref[:, :QK_ROPE].astype(F32)
    krot = _rope(kr, cos_t, sin_t, swap[:QK_ROPE, :QK_ROPE]).astype(BF16)
    q = jnp.dot(cqn, wq_ref[...], preferred_element_type=F32)
    qn = (q[:, :NOPE_ALL] * scale).astype(BF16)
    cos_all = jnp.concatenate([cos_t] * MLA_HEADS, axis=1)
    sin_all = jnp.concatenate([sin_t] * MLA_HEADS, axis=1)
    qr = (_rope(q[:, NOPE_ALL:], cos_all, sin_all, swap) * scale).astype(BF16)
    kv = jnp.dot(ckvn, wkv_ref[...], preferred_element_type=F32)
    hw = QK_NOPE + V_DIM
    for h in range(MLA_HEADS):
        q_ref[h, :, :QK_NOPE] = qn[:, h * QK_NOPE:(h + 1) * QK_NOPE]
        q_ref[h, :, QK_NOPE:] = qr[:, h * QK_ROPE:(h + 1) * QK_ROPE]
        k_ref[h, :, :QK_NOPE] = kv[:, h * hw:h * hw + QK_NOPE].astype(BF16)
        k_ref[h, :, QK_NOPE:] = krot
        v_ref[h, 0] = kv[:, h * hw + QK_NOPE:(h + 1) * hw].T.astype(BF16)


def _mla_prep(p, k_rope, q_norm, kv_norm, wq, wkv):
    cos_t, sin_t, swap = _rope_tables()
    tm = MLA_TM
    seq_blocks = SEQ // tm
    return pl.pallas_call(
        _mla_prep_kernel,
        out_shape=(
            jax.ShapeDtypeStruct((MLA_HEADS, N_TOK, QK_DIM), BF16),
            jax.ShapeDtypeStruct((MLA_HEADS, N_TOK, QK_DIM), BF16),
            jax.ShapeDtypeStruct((MLA_HEADS, N_TOK // tm, V_DIM, tm), BF16),
        ),
        grid=(N_TOK // tm,),
        in_specs=[
            pl.BlockSpec((tm, Q_LORA), lambda i: (i, 0)),
            pl.BlockSpec((tm, KV_LORA), lambda i: (i, 1)),
            pl.BlockSpec((tm, LANES), lambda i: (i, 0)),
            pl.BlockSpec((1, Q_LORA), lambda i: (0, 0)),
            pl.BlockSpec((1, KV_LORA), lambda i: (0, 0)),
            pl.BlockSpec((Q_LORA, NOPE_ALL + ROPE_ALL), lambda i: (0, 0)),
            pl.BlockSpec((KV_LORA, MLA_HEADS * (QK_NOPE + V_DIM)), lambda i: (0, 0)),
            pl.BlockSpec((tm, QK_ROPE), lambda i: (i % seq_blocks, 0)),
            pl.BlockSpec((tm, QK_ROPE), lambda i: (i % seq_blocks, 0)),
            pl.BlockSpec((ROPE_ALL, ROPE_ALL), lambda i: (0, 0)),
        ],
        out_specs=(
            pl.BlockSpec((MLA_HEADS, tm, QK_DIM), lambda i: (0, i, 0)),
            pl.BlockSpec((MLA_HEADS, tm, QK_DIM), lambda i: (0, i, 0)),
            pl.BlockSpec((MLA_HEADS, 1, V_DIM, tm), lambda i: (0, i, 0, 0)),
        ),
        compiler_params=_params("parallel"),
        name="mla_prep",
    )(p, p, k_rope, q_norm.reshape(1, Q_LORA), kv_norm.reshape(1, KV_LORA), wq, wkv, cos_t, sin_t, swap)


MLA_TQ = 4096
MLA_TK = 512
MLA_QH = 16
assert MLA_TK == MLA_TM


def _mla_attn_kernel(q_ref, k_ref, vt_ref, o_ref, s_ref, m_ref, l_ref, acc_ref):
    n_chunks = SEQ // MLA_TK
    cols = MLA_TQ // MLA_QH

    def scores(h, c):
        k0 = c * MLA_TK if isinstance(c, int) else pl.multiple_of(c * MLA_TK, MLA_TK)
        return lax.dot_general(k_ref[pl.ds(k0, MLA_TK), :], q_ref[h * cols:(h + 1) * cols, :],
                               (((1,), (1,)), ((), ())), preferred_element_type=F32)

    m_ref[...] = jnp.full(m_ref.shape, NEG_INF, F32)
    l_ref[...] = jnp.zeros(l_ref.shape, F32)
    acc_ref[...] = jnp.zeros(acc_ref.shape, F32)
    for h in range(MLA_QH):
        s_ref[0, h] = scores(h, 0)

    def step(c, slot, prefetch=True):
        for h in range(MLA_QH):
            if prefetch:
                s_ref[1 - slot, h] = scores(h, c + 1)
            s = s_ref[slot, h]
            m_old = m_ref[h]
            m_new = jnp.maximum(m_old, jnp.max(s, axis=0, keepdims=True))
            p = jnp.exp2(s - m_new)
            alpha = jnp.exp2(m_old - m_new)
            l_ref[h] = alpha * l_ref[h] + jnp.sum(p, axis=0, keepdims=True)
            pv = jnp.dot(vt_ref[c], p.astype(BF16), preferred_element_type=F32)
            acc_ref[h] = alpha * acc_ref[h] + pv
            m_ref[h] = m_new

    def pair(i, carry):
        step(2 * i, 0)
        step(2 * i + 1, 1)
        return carry

    lax.fori_loop(0, n_chunks // 2 - 1, pair, 0)
    step(n_chunks - 2, 0)
    step(n_chunks - 1, 1, prefetch=False)
    for h in range(MLA_QH):
        o_ref[h * cols:(h + 1) * cols, :] = (acc_ref[h] / l_ref[h]).T.astype(o_ref.dtype)


def _mla_attn(q, k, v):
    nq = SEQ // MLA_TQ
    return pl.pallas_call(
        _mla_attn_kernel,
        out_shape=jax.ShapeDtypeStruct((N_TOK, MLA_HEADS * V_DIM), BF16),
        grid=(BATCH, MLA_HEADS, nq),
        in_specs=[
            pl.BlockSpec((None, MLA_TQ, QK_DIM), lambda b, h, i: (h, b * nq + i, 0)),
            pl.BlockSpec((None, SEQ, QK_DIM), lambda b, h, i: (h, b, 0)),
            pl.BlockSpec((None, SEQ // MLA_TK, V_DIM, MLA_TK), lambda b, h, i: (h, b, 0, 0)),
        ],
        out_specs=pl.BlockSpec((MLA_TQ, V_DIM), lambda b, h, i: (b * nq + i, h)),
        scratch_shapes=[
            pltpu.VMEM((2, MLA_QH, MLA_TK, MLA_TQ // MLA_QH), F32),
            pltpu.VMEM((MLA_QH, 1, MLA_TQ // MLA_QH), F32),
            pltpu.VMEM((MLA_QH, 1, MLA_TQ // MLA_QH), F32),
            pltpu.VMEM((MLA_QH, V_DIM, MLA_TQ // MLA_QH), F32),
        ],
        compiler_params=_params("parallel", "parallel", "parallel"),
        name="mla_attn",
    )(q, k, v)


SG_TM = 256


def _sg_kernel(u_ref, v_ref, g_ref, w_ref, bt_ref, o_ref):
    v = jax.nn.gelu(v_ref[...].astype(F32))
    vn = _rms(v, g_ref[...]).astype(BF16)
    u = jax.nn.gelu(u_ref[...].astype(F32))
    gc = SG_CH // SG_GROUPS
    for n in range(SG_TM // CHUNK):
        rows = slice(n * CHUNK, (n + 1) * CHUNK)
        for g in range(SG_GROUPS):
            cols = slice(g * gc, (g + 1) * gc)
            mixed = jnp.dot(w_ref[g], vn[rows, cols], preferred_element_type=F32) + bt_ref[:, g:g + 1]
            o_ref[rows, cols] = (u[rows, cols] * mixed).astype(o_ref.dtype)


def _spatial_gating(p, g_norm, w_s, b_s):
    u_blk = (Q_LORA + KV_LORA) // SG_CH
    return pl.pallas_call(
        _sg_kernel,
        out_shape=jax.ShapeDtypeStruct((N_TOK, SG_CH), BF16),
        grid=(N_TOK // SG_TM,),
        in_specs=[
            pl.BlockSpec((SG_TM, SG_CH), lambda i: (i, u_blk)),
            pl.BlockSpec((SG_TM, SG_CH), lambda i: (i, u_blk + 1)),
            pl.BlockSpec((1, SG_CH), lambda i: (0, 0)),
            pl.BlockSpec((SG_GROUPS, CHUNK, CHUNK), lambda i: (0, 0, 0)),
            pl.BlockSpec((CHUNK, SG_GROUPS), lambda i: (0, 0)),
        ],
        out_specs=pl.BlockSpec((SG_TM, SG_CH), lambda i: (i, 0)),
        compiler_params=_params("parallel"),
        name="spatial_gating",
    )(p, p, g_norm.reshape(1, SG_CH), w_s.astype(BF16), b_s.T)


ROUTER_TM = 512


def _router_kernel(x_ref, g_ref, w_ref, b_ref, o_ref, xn_ref):
    xn = _rms(x_ref[...], g_ref[...])
    xn_ref[...] = _pack_halves(xn)
    w = w_ref[...]
    xh, wh = xn.astype(BF16), w.astype(BF16)
    xl, wl = (xn - xh.astype(F32)).astype(BF16), (w - wh.astype(F32)).astype(BF16)
    logits = (jnp.dot(xh, wh, preferred_element_type=F32)
              + (jnp.dot(xh, wl, preferred_element_type=F32) + jnp.dot(xl, wh, preferred_element_type=F32))
              + b_ref[...])
    lane = lax.broadcasted_iota(jnp.int32, logits.shape, 1).astype(F32)
    low = jnp.float32(-3.0e38)

    def first_max(vals):
        top = jnp.max(vals, axis=-1, keepdims=True)
        idx = jnp.min(jnp.where(vals == top, lane, float(LANES)), axis=-1, keepdims=True)
        return top, idx

    is_group = lane < N_GROUPS
    g_top, g_idx = first_max(jnp.where(is_group, logits, low))
    g_prob = 1.0 / jnp.sum(jnp.where(is_group, jnp.exp(logits - g_top), 0.0), axis=-1, keepdims=True)
    lo = N_GROUPS + g_idx * EXPERTS_PER_GROUP
    e_vals = jnp.where(lane >= lo, jnp.where(lane < lo + EXPERTS_PER_GROUP, logits, low), low)
    v1, i1 = first_max(e_vals)
    v2, i2 = first_max(jnp.where(lane == i1, low, e_vals))
    e21 = jnp.exp(v2 - v1)
    w1 = g_prob / (1.0 + e21)
    w2 = g_prob * e21 / (1.0 + e21)
    out = jnp.where(lane == 0, i1 - N_GROUPS,
                    jnp.where(lane == 1, i2 - N_GROUPS,
                              jnp.where(lane == 2, w1, jnp.where(lane == 3, w2, 0.0))))
    o_ref[...] = out


def _router(x, g, w, b):
    tm = ROUTER_TM
    return pl.pallas_call(
        _router_kernel,
        out_shape=(jax.ShapeDtypeStruct((N_TOK, LANES), F32), jax.ShapeDtypeStruct((N_TOK, HALF), jnp.uint32)),
        grid=(N_TOK // tm,),
        in_specs=[
            pl.BlockSpec((tm, D_MODEL), lambda i: (i, 0)),
            pl.BlockSpec((1, D_MODEL), lambda i: (0, 0)),
            pl.BlockSpec((D_MODEL, LANES), lambda i: (0, 0)),
            pl.BlockSpec((1, LANES), lambda i: (0, 0)),
        ],
        out_specs=(pl.BlockSpec((tm, LANES), lambda i: (i, 0)), pl.BlockSpec((tm, HALF), lambda i: (i, 0))),
        compiler_params=_params("parallel"),
        name="router",
    )(x, g.reshape(1, D_MODEL), w, b)


def _dispatch_plan(eid):
    tok = jnp.arange(N_TOK, dtype=jnp.int32)
    keys = jnp.concatenate([eid[:, k] * N_ASSIGN + (k * N_TOK + tok) for k in range(TOP_K)])
    row_asg = jnp.sort(keys) & (N_ASSIGN - 1)
    row_asg = jnp.concatenate([row_asg, jnp.zeros((SUBLANES,), jnp.int32)])
    counts = jnp.sum(eid.reshape(-1)[:, None] == jnp.arange(N_EXPERTS, dtype=jnp.int32)[None, :], axis=0,
                     dtype=jnp.int32)
    starts = jnp.cumsum(counts) - counts
    nblk_e = (counts + MOE_BLOCK - 1) // MOE_BLOCK
    blk_end = jnp.cumsum(nblk_e)
    blk_start = blk_end - nblk_e
    n_active = blk_end[-1]
    bidx = jnp.arange(MOE_NBLK, dtype=jnp.int32)
    blk_x = jnp.minimum(bidx, n_active - 1)
    blk_e = jnp.minimum(jnp.sum(blk_end[None, :] <= blk_x[:, None], axis=1, dtype=jnp.int32), N_EXPERTS - 1)
    first = (blk_x - blk_start[blk_e]) * MOE_BLOCK
    blk_rows = jnp.where(bidx < n_active, jnp.clip(counts[blk_e] - first, 0, MOE_BLOCK), 0).astype(jnp.int32)
    blk_src = (starts[blk_e] + first).astype(jnp.int32)
    return row_asg, blk_e, blk_rows, blk_src


GATHER_STEP = min(1, N_FT - 1)
_ROW_BITS = tuple(1 << k for k in range(MOE_BLOCK.bit_length() - 1, -1, -1))


def _expert_kernel(row_asg, blk_e, blk_rows, blk_src, x_hbm, w1_ref, w3_ref, w2_ref, ys_hbm,
                   xbuf, acc, ypk, xn_ref, gsem, ssem):
    b = pl.program_id(0)
    j = pl.program_id(1)
    rows = blk_rows[b]
    slot = b % 2

    def n_fetch(blk):
        return (blk_rows[blk] + SUBLANES - 1) & -SUBLANES

    def issue_gather(blk, s):
        base = blk_src[blk]

        def group(q, c):
            for u in range(SUBLANES):
                t = row_asg[base + q * SUBLANES + u] & (N_TOK - 1)
                pltpu.make_async_copy(x_hbm.at[pl.ds(t, 1), :], xbuf.at[s, q, pl.ds(u, 1), :], gsem.at[s]).start()
            return c

        lax.fori_loop(0, n_fetch(blk) >> SUBLANE_SHIFT, group, 0)

    def wait_rows(n, copy_of):
        for bit in _ROW_BITS:
            @pl.when((n & bit) != 0)
            def _():
                copy_of(bit).wait()

    def gather_block(s):
        def copy_of(k):
            tiles = xbuf.at[s, pl.ds(0, k // SUBLANES)]
            return pltpu.make_async_copy(tiles, tiles, gsem.at[s])
        return copy_of

    def scatter_block(s):
        def copy_of(k):
            part = ypk.at[s, pl.ds(0, k // SUBLANES)] if k >= SUBLANES else ypk.at[s, 0, pl.ds(0, k), :]
            return pltpu.make_async_copy(part, part, ssem.at[s])
        return copy_of

    def issue_scatter(blk, s, n):
        base = blk_src[blk]

        def one(q, u, a):
            pltpu.make_async_copy(ypk.at[s, q, pl.ds(u, 1), :], ys_hbm.at[pl.ds(a, 1), :], ssem.at[s]).start()

        def group(q, c):
            for u in range(SUBLANES):
                one(q, u, row_asg[base + q * SUBLANES + u])
            return c

        def tail(i, c):
            one(i >> SUBLANE_SHIFT, i & (SUBLANES - 1), row_asg[base + i])
            return c

        full = n >> SUBLANE_SHIFT
        lax.fori_loop(0, full, group, 0)
        lax.fori_loop(full * SUBLANES, n, tail, 0)

    def for_row_tile(fn):
        lo = 0
        for m in MOE_ROW_TILES:
            @pl.when(jnp.logical_and(rows > lo, rows <= m))
            def _():
                fn(m)
            lo = m

    nxt = jnp.minimum(b + 1, MOE_NBLK - 1)
    has_next = jnp.logical_and(b + 1 < MOE_NBLK, blk_rows[nxt] > 0)

    @pl.when(rows > 0)
    def _():
        @pl.when(j == 0)
        def _():
            @pl.when(b == 0)
            def _():
                xbuf[...] = jnp.zeros(xbuf.shape, jnp.uint32)
                acc[...] = jnp.zeros(acc.shape, F32)
                issue_gather(0, 0)

            wait_rows(n_fetch(b), gather_block(slot))

            def unpack_tile(m):
                lo, hi = _unpack_halves(xbuf[slot, :m // SUBLANES].reshape(m, HALF))
                xn_ref[:m, :HALF] = lo.astype(BF16)
                xn_ref[:m, HALF:] = hi.astype(BF16)

            for_row_tile(unpack_tile)

        @pl.when(jnp.logical_and(j == GATHER_STEP, has_next))
        def _():
            issue_gather(nxt, 1 - slot)

        def ffn_tile(m):
            xs = xn_ref[:m, :]
            h1 = jnp.dot(xs, w1_ref[...].astype(BF16), preferred_element_type=F32)
            h3 = jnp.dot(xs, w3_ref[...].astype(BF16), preferred_element_type=F32)
            h = (jax.nn.silu(h1) * h3).astype(BF16)
            y = jnp.dot(h, w2_ref[...].astype(BF16), preferred_element_type=F32)
            acc[:m, :] = jnp.where(j > 0, acc[:m, :], 0.0) + y

        for_row_tile(ffn_tile)

        @pl.when(j == N_FT - 1)
        def _():
            @pl.when(b > 0)
            def _():
                wait_rows(blk_rows[jnp.maximum(b - 1, 0)], scatter_block(1 - slot))

            def pack_tile(m):
                ypk[slot, :m // SUBLANES] = _pack_halves(acc[:m, :]).reshape(m // SUBLANES, SUBLANES, HALF)

            for_row_tile(pack_tile)

            issue_scatter(b, slot, rows)

            @pl.when(jnp.logical_not(has_next))
            def _():
                wait_rows(rows, scatter_block(slot))


def _moe_experts(xn, w1, w3, w2, layer, row_asg, blk_e, blk_rows, blk_src):
    def w_col(b, j, ra, be, br, bs):
        return (layer, be[b], 0, jnp.where(br[b] > 0, j, N_FT - 1))

    def w_row(b, j, ra, be, br, bs):
        return (layer, be[b], jnp.where(br[b] > 0, j, N_FT - 1), 0)

    return pl.pallas_call(
        _expert_kernel,
        out_shape=jax.ShapeDtypeStruct((N_ASSIGN, HALF), jnp.uint32),
        grid_spec=pltpu.PrefetchScalarGridSpec(
            num_scalar_prefetch=4,
            grid=(MOE_NBLK, N_FT),
            in_specs=[
                pl.BlockSpec(memory_space=pl.ANY),
                pl.BlockSpec((None, None, D_MODEL, MOE_FT), w_col),
                pl.BlockSpec((None, None, D_MODEL, MOE_FT), w_col),
                pl.BlockSpec((None, None, MOE_FT, D_MODEL), w_row),
            ],
            out_specs=pl.BlockSpec(memory_space=pl.ANY),
            scratch_shapes=[
                pltpu.VMEM((2, MOE_BLOCK // SUBLANES, SUBLANES, HALF), jnp.uint32),
                pltpu.VMEM((MOE_BLOCK, D_MODEL), F32),
                pltpu.VMEM((2, MOE_BLOCK // SUBLANES, SUBLANES, HALF), jnp.uint32),
                pltpu.VMEM((MOE_BLOCK, D_MODEL), BF16),
                pltpu.SemaphoreType.DMA((2,)),
                pltpu.SemaphoreType.DMA((2,)),
            ],
        ),
        compiler_params=_params("arbitrary", "arbitrary"),
        name="moe_experts",
    )(row_asg, blk_e, blk_rows, blk_src, xn, w1, w3, w2)


def _combine_kernel(final_norm, x_ref, y0_ref, y1_ref, gate_ref, gf_ref, o_ref):
    gates = gate_ref[...]
    g0, g1 = gates[:, 2:3], gates[:, 3:4]
    lo0, hi0 = _unpack_halves(y0_ref[...])
    lo1, hi1 = _unpack_halves(y1_ref[...])
    lo = x_ref[:, :HALF] + (g0 * lo0 + g1 * lo1)
    hi = x_ref[:, HALF:] + (g0 * hi0 + g1 * hi1)
    if final_norm:
        ms = (jnp.sum(lo * lo, axis=-1, keepdims=True) + jnp.sum(hi * hi, axis=-1, keepdims=True)) / D_MODEL
        r = lax.rsqrt(ms + EPS)
        lo, hi = lo * r * gf_ref[:, :HALF], hi * r * gf_ref[:, HALF:]
    o_ref[:, :HALF] = lo
    o_ref[:, HALF:] = hi


def _moe_combine(x, ys, route, g_final, final_norm):
    tm = COMBINE_TOK
    return pl.pallas_call(
        functools.partial(_combine_kernel, final_norm),
        out_shape=jax.ShapeDtypeStruct((N_TOK, D_MODEL), F32),
        grid=(N_TOK // tm,),
        in_specs=[
            pl.BlockSpec((tm, D_MODEL), lambda i: (i, 0)),
            pl.BlockSpec((tm, HALF), lambda i: (i, 0)),
            pl.BlockSpec((tm, HALF), lambda i: (N_TOK // tm + i, 0)),
            pl.BlockSpec((tm, LANES), lambda i: (i, 0)),
            pl.BlockSpec((1, D_MODEL), lambda i: (0, 0)),
        ],
        out_specs=pl.BlockSpec((tm, D_MODEL), lambda i: (i, 0)),
        compiler_params=_params("parallel"),
        name="moe_combine",
    )(x, ys, ys, route, g_final.reshape(1, D_MODEL))


def _hier_moe(x, g, wg, bg, we, be, w1, w3, w2, layer, g_final, final_norm):
    pad = LANES - N_GROUPS - N_EXPERTS
    w_r = jnp.concatenate([wg, we, jnp.zeros((D_MODEL, pad), F32)], axis=1)
    b_r = jnp.concatenate([bg, be, jnp.zeros((pad,), F32)]).reshape(1, LANES)
    route, xn = _router(x, g, w_r, b_r)
    eid = route[:, :TOP_K].astype(jnp.int32)
    row_asg, blk_e, blk_rows, blk_src = _dispatch_plan(eid)
    ys = _moe_experts(xn, w1, w3, w2, layer, row_asg, blk_e, blk_rows, blk_src)
    return _moe_combine(x, ys, route, g_final, final_norm)


def kernel(x, norm_mix, norm_ffn, norm_final, w_in_ab, na_rpb, conv_w, conv_b, w_out_ab, w_in_cd,
           q_norm, kv_norm, w_uq, w_ukv, sg_norm, sg_w, sg_b, w_out_cd, router_group_w,
           router_group_b, router_expert_w, router_expert_b, w1, w3, w2):
    xt = x.reshape(N_TOK, D_MODEL)
    for layer in range(DEPTH):
        i = layer // 2
        if layer % 2 == 0:
            q_scale = np.where(np.arange(AB_IN) < NA_WIDTH, NA_Q_SCALE, 1.0).astype(np.float32)
            p = _norm_matmul(xt, norm_mix[layer], (w_in_ab[i] * q_scale).astype(BF16), tm=1024, tn=1536)
            a_out = _natten(p, _natten_bias_table(na_rpb[i]))
            b_out = _gated_conv(p, conv_w[i], conv_b[i])
            xt = _out_proj(a_out, b_out, w_out_ab[i].astype(BF16), xt)
        else:
            p, k_rope = _cd_in_proj(xt, norm_mix[layer], w_in_cd[i])
            wq = w_uq[i].reshape(Q_LORA, MLA_HEADS, QK_DIM)
            wq = jnp.concatenate([wq[:, :, :QK_NOPE].reshape(Q_LORA, NOPE_ALL),
                                  wq[:, :, QK_NOPE:].reshape(Q_LORA, ROPE_ALL)], axis=1).astype(BF16)
            q, k, v = _mla_prep(p, k_rope, q_norm[i], kv_norm[i], wq, w_ukv[i].astype(BF16))
            c_out = _mla_attn(q, k, v)
            d_out = _spatial_gating(p, sg_norm[i], sg_w[i], sg_b[i])
            xt = _out_proj(c_out, d_out, w_out_cd[i].astype(BF16), xt)
        xt = _hier_moe(xt, norm_ffn[layer], router_group_w[layer], router_group_b[layer],
                       router_expert_w[layer], router_expert_b[layer], w1, w3, w2, layer,
                       norm_final, final_norm=(layer == DEPTH - 1))
    return xt.reshape(BATCH, SEQ, D_MODEL)
```

```python
import functools

import numpy as np
import jax
import jax.numpy as jnp
from jax import lax
from jax.experimental import pallas as pl
from jax.experimental.pallas import tpu as pltpu

F32 = jnp.float32
BF16 = jnp.bfloat16

D_MODEL = 2048
BATCH = 2
SEQ = 4096
N_TOK = BATCH * SEQ
DEPTH = 2
GRID_W = 64
GRID_ROWS = SEQ // GRID_W
WIN_ROWS = 8
WIN_COLS = 16
NA_HEADS = 8
NA_HEAD_DIM = 128
NA_WIDTH = NA_HEADS * NA_HEAD_DIM
NA_KEYS = WIN_ROWS * GRID_W
CONV_CH = D_MODEL // 2
MLA_HEADS = 8
Q_LORA = 512
KV_LORA = 512
QK_NOPE = 128
QK_ROPE = 64
QK_DIM = QK_NOPE + QK_ROPE
V_DIM = 128
ROPE_THETA = 10000.0
SG_CH = D_MODEL // 2
SG_GROUPS = 8
CHUNK = 128
N_GROUPS = 8
EXPERTS_PER_GROUP = 8
N_EXPERTS = N_GROUPS * EXPERTS_PER_GROUP
TOP_K = 2
D_EXPERT = 768
EPS = 1e-6
NEG_INF = -1e30
LOG2_E = 1.4426950408889634
AB_IN = 3 * NA_WIDTH + 3 * CONV_CH

LANES = 128
SUBLANES = 8
SUBLANE_SHIFT = 3
MOE_BLOCK = 512
MOE_ROW_TILES = (256, 288, 320, 384, 512)
MOE_FT = 256
N_FT = D_EXPERT // MOE_FT
N_ASSIGN = N_TOK * TOP_K
MOE_NBLK = N_ASSIGN // MOE_BLOCK + N_EXPERTS
COMBINE_TOK = 512
VMEM_LIMIT = 52 * 1024 * 1024


def _params(*sem):
    return pltpu.CompilerParams(dimension_semantics=sem, vmem_limit_bytes=VMEM_LIMIT)


def _rms(x, g):
    return x * lax.rsqrt(jnp.mean(x * x, axis=-1, keepdims=True) + EPS) * g


HALF = D_MODEL // 2
BF16_BITS = 16
HIGH_HALF = 0xFFFF0000


def _pack_halves(x):
    lo = pltpu.bitcast(x[:, :HALF].astype(BF16).astype(F32), jnp.uint32)
    hi = pltpu.bitcast(x[:, HALF:].astype(BF16).astype(F32), jnp.uint32)
    return (lo >> BF16_BITS) | (hi & jnp.uint32(HIGH_HALF))


def _unpack_halves(w):
    return pltpu.bitcast(w << BF16_BITS, F32), pltpu.bitcast(w & jnp.uint32(HIGH_HALF), F32)


def _norm_mm_kernel(x_ref, g_ref, w_ref, o_ref, xn_ref):
    @pl.when(pl.program_id(1) == 0)
    def _():
        xn_ref[...] = _rms(x_ref[...].astype(F32), g_ref[...]).astype(BF16)

    o_ref[...] = jnp.dot(xn_ref[...], w_ref[...], preferred_element_type=F32).astype(o_ref.dtype)


def _norm_matmul(x, g, w, tm, tn):
    m, k = x.shape
    nout = w.shape[1]
    return pl.pallas_call(
        _norm_mm_kernel,
        out_shape=jax.ShapeDtypeStruct((m, nout), BF16),
        grid=(m // tm, nout // tn),
        in_specs=[
            pl.BlockSpec((tm, k), lambda i, j: (i, 0)),
            pl.BlockSpec((1, k), lambda i, j: (0, 0)),
            pl.BlockSpec((k, tn), lambda i, j: (0, j)),
        ],
        out_specs=pl.BlockSpec((tm, tn), lambda i, j: (i, j)),
        scratch_shapes=[pltpu.VMEM((tm, k), BF16)],
        compiler_params=_params("parallel", "arbitrary"),
        name="norm_matmul",
    )(x, g.reshape(1, k), w)


CD_TN = 1024
CD_MAIN = Q_LORA + KV_LORA + 2 * SG_CH
CD_A_TILES = (Q_LORA + KV_LORA) // CD_TN
CD_MAIN_TILES = CD_MAIN // CD_TN


def _cd_proj_kernel(x_ref, g_ref, wa_ref, wb_ref, wc_ref, main_ref, kr_ref, xn_ref):
    j = pl.program_id(1)

    @pl.when(j == 0)
    def _():
        xn_ref[...] = _rms(x_ref[...], g_ref[...]).astype(BF16)

    @pl.when(j < CD_A_TILES)
    def _():
        main_ref[...] = jnp.dot(xn_ref[...], wa_ref[...], preferred_element_type=F32).astype(BF16)

    @pl.when(j >= CD_A_TILES)
    def _():
        main_ref[...] = jnp.dot(xn_ref[...], wb_ref[...], preferred_element_type=F32).astype(BF16)

    @pl.when(j == 0)
    def _():
        kr_ref[...] = jnp.dot(xn_ref[...], wc_ref[...], preferred_element_type=F32).astype(BF16)


def _cd_in_proj(x, g, w, tm=1024):
    c0, c1 = Q_LORA + KV_LORA, Q_LORA + KV_LORA + QK_ROPE
    wa = w[:, :c0].astype(BF16)
    wb = w[:, c1:].astype(BF16)
    wc = jnp.pad(w[:, c0:c1], ((0, 0), (0, LANES - QK_ROPE))).astype(BF16)
    return pl.pallas_call(
        _cd_proj_kernel,
        out_shape=(jax.ShapeDtypeStruct((N_TOK, CD_MAIN), BF16), jax.ShapeDtypeStruct((N_TOK, LANES), BF16)),
        grid=(N_TOK // tm, CD_MAIN_TILES),
        in_specs=[
            pl.BlockSpec((tm, D_MODEL), lambda i, j: (i, 0)),
            pl.BlockSpec((1, D_MODEL), lambda i, j: (0, 0)),
            pl.BlockSpec((D_MODEL, CD_TN), lambda i, j: (0, jnp.minimum(j, CD_A_TILES - 1))),
            pl.BlockSpec((D_MODEL, CD_TN), lambda i, j: (0, jnp.maximum(j - CD_A_TILES, 0))),
            pl.BlockSpec((D_MODEL, LANES), lambda i, j: (0, 0)),
        ],
        out_specs=(
            pl.BlockSpec((tm, CD_TN), lambda i, j: (i, j)),
            pl.BlockSpec((tm, LANES), lambda i, j: (i, 0)),
        ),
        scratch_shapes=[pltpu.VMEM((tm, D_MODEL), BF16)],
        compiler_params=_params("parallel", "arbitrary"),
        name="cd_in_proj",
    )(x, g.reshape(1, D_MODEL), wa, wb, wc)


def _natten_bias_table(rpb):
    c = np.arange(GRID_W)
    col_start = np.clip(c - WIN_COLS // 2, 0, GRID_W - WIN_COLS)
    valid = (c[None, :] >= col_start[:, None]) & (c[None, :] < col_start[:, None] + WIN_COLS)
    dc = np.clip(c[None, :] - c[:, None] + WIN_COLS - 1, 0, 2 * WIN_COLS - 2)
    pick = (dc[:, :, None] == np.arange(2 * WIN_COLS - 1)).astype(np.float32)
    m = jnp.einsum('hrd,ckd->hcrk', rpb * LOG2_E, jnp.asarray(pick), precision=lax.Precision.HIGHEST)
    m = jnp.where(valid[None, :, None, :], m, NEG_INF)
    t = jnp.stack([m[:, :, o:o + WIN_ROWS] for o in range(WIN_ROWS)], axis=1)
    return t.reshape(rpb.shape[0], WIN_ROWS, GRID_W, NA_KEYS).astype(F32)


NA_ROWS_PER_STEP = 32
NA_Q_SCALE = NA_HEAD_DIM ** -0.5 * LOG2_E


def _natten_kernel(q_ref, k_ref, v_ref, t_ref, o_ref):
    def rows(i, carry):
        rs = [i * NA_ROWS_PER_STEP + u for u in range(NA_ROWS_PER_STEP)]
        kr0s = [jnp.clip(r - WIN_ROWS // 2, 0, GRID_ROWS - WIN_ROWS) for r in rs]
        q0s = [pl.multiple_of(r * GRID_W, GRID_W) for r in rs]
        k0s = [pl.multiple_of(kr0 * GRID_W, GRID_W) for kr0 in kr0s]
        ss = [lax.dot_general(q_ref[pl.ds(q0, GRID_W), :], k_ref[pl.ds(k0, NA_KEYS), :],
                              (((1,), (1,)), ((), ())), preferred_element_type=F32)
              for q0, k0 in zip(q0s, k0s)]
        ps, ls = [], []
        for s, r, kr0 in zip(ss, rs, kr0s):
            s = s + t_ref[kr0 - r + WIN_ROWS - 1]
            p = jnp.exp2(s - jnp.max(s, axis=-1, keepdims=True))
            ls.append(jnp.sum(p, axis=-1, keepdims=True))
            ps.append(p.astype(BF16))
        for p, l, q0, k0 in zip(ps, ls, q0s, k0s):
            o = jnp.dot(p, v_ref[pl.ds(k0, NA_KEYS), :], preferred_element_type=F32) / l
            o_ref[pl.ds(q0, GRID_W), :] = o.astype(o_ref.dtype)
        return carry

    lax.fori_loop(0, GRID_ROWS // NA_ROWS_PER_STEP, rows, 0)


def _natten(p, table):
    blk = (SEQ, NA_HEAD_DIM)
    return pl.pallas_call(
        _natten_kernel,
        out_shape=jax.ShapeDtypeStruct((N_TOK, NA_WIDTH), BF16),
        grid=(BATCH, NA_HEADS),
        in_specs=[
            pl.BlockSpec(blk, lambda b, h: (b, h)),
            pl.BlockSpec(blk, lambda b, h: (b, NA_HEADS + h)),
            pl.BlockSpec(blk, lambda b, h: (b, 2 * NA_HEADS + h)),
            pl.BlockSpec((None, WIN_ROWS, GRID_W, NA_KEYS), lambda b, h: (h, 0, 0, 0)),
        ],
        out_specs=pl.BlockSpec(blk, lambda b, h: (b, h)),
        compiler_params=_params("parallel", "parallel"),
        name="natten",
    )(p, p, p, table)


CONV_CB = 256


def _conv_kernel(gb_ref, gc_ref, hc_ref, w_ref, b_ref, o_ref):
    z = gc_ref[...].astype(F32) * hc_ref[...].astype(F32)
    pos = lax.broadcasted_iota(jnp.int32, z.shape, 0)
    z_prev = jnp.where(pos == 0, 0.0, pltpu.roll(z, 1, 0))
    z_next = jnp.where(pos == SEQ - 1, 0.0, pltpu.roll(z, SEQ - 1, 0))
    y = b_ref[...] + z_prev * w_ref[0:1, :] + z * w_ref[1:2, :] + z_next * w_ref[2:3, :]
    o_ref[...] = (gb_ref[...].astype(F32) * y).astype(o_ref.dtype)


def _gated_conv(p, w, b):
    base = 3 * NA_WIDTH // CONV_CB
    step = CONV_CH // CONV_CB
    blk = (SEQ, CONV_CB)
    return pl.pallas_call(
        _conv_kernel,
        out_shape=jax.ShapeDtypeStruct((N_TOK, CONV_CH), BF16),
        grid=(BATCH, step),
        in_specs=[
            pl.BlockSpec(blk, lambda bi, c: (bi, base + c)),
            pl.BlockSpec(blk, lambda bi, c: (bi, base + step + c)),
            pl.BlockSpec(blk, lambda bi, c: (bi, base + 2 * step + c)),
            pl.BlockSpec((3, CONV_CB), lambda bi, c: (0, c)),
            pl.BlockSpec((1, CONV_CB), lambda bi, c: (0, c)),
        ],
        out_specs=pl.BlockSpec(blk, lambda bi, c: (bi, c)),
        compiler_params=_params("parallel", "parallel"),
        name="gated_conv",
    )(p, p, p, w, b.reshape(1, CONV_CH))


def _out_proj_kernel(a_ref, b_ref, wa_ref, wb_ref, r_ref, o_ref):
    acc = jnp.dot(a_ref[...], wa_ref[...], preferred_element_type=F32)
    acc = acc + jnp.dot(b_ref[...], wb_ref[...], preferred_element_type=F32)
    o_ref[...] = r_ref[...] + acc


def _out_proj(a, b, w, res, tm=512):
    ka, kb = a.shape[1], b.shape[1]
    assert ka == kb
    return pl.pallas_call(
        _out_proj_kernel,
        out_shape=jax.ShapeDtypeStruct((N_TOK, D_MODEL), F32),
        grid=(N_TOK // tm,),
        in_specs=[
            pl.BlockSpec((tm, ka), lambda i: (i, 0)),
            pl.BlockSpec((tm, kb), lambda i: (i, 0)),
            pl.BlockSpec((ka, D_MODEL), lambda i: (0, 0)),
            pl.BlockSpec((kb, D_MODEL), lambda i: (1, 0)),
            pl.BlockSpec((tm, D_MODEL), lambda i: (i, 0)),
        ],
        out_specs=pl.BlockSpec((tm, D_MODEL), lambda i: (i, 0)),
        compiler_params=_params("parallel"),
        name="out_proj",
    )(a, b, w, w, res)


MLA_TM = 512


def _rope_tables():
    f32 = np.float32
    pos = np.arange(SEQ)
    row = (pos // GRID_W).astype(f32)
    col = (pos % GRID_W).astype(f32)
    half = QK_ROPE // 2
    inv = np.power(f32(ROPE_THETA), -np.arange(0, half, 2, dtype=f32) / f32(half)).astype(f32)
    ar, ac = row[:, None] * inv, col[:, None] * inv
    cos_t = jnp.asarray(np.concatenate([np.cos(ar), np.cos(ar), np.cos(ac), np.cos(ac)], axis=-1).astype(f32))
    sin_t = jnp.asarray(np.concatenate([-np.sin(ar), np.sin(ar), -np.sin(ac), np.sin(ac)], axis=-1).astype(f32))
    quarter = half // 2
    src = np.arange(QK_ROPE) + np.where((np.arange(QK_ROPE) // quarter) % 2 == 0, quarter, -quarter)
    swap = np.zeros((QK_ROPE, QK_ROPE), np.float32)
    swap[src, np.arange(QK_ROPE)] = 1.0
    swap_all = np.kron(np.eye(MLA_HEADS, dtype=np.float32), swap)
    return cos_t, sin_t, jnp.asarray(swap_all, BF16)


def _rope(x, cos_t, sin_t, swap):
    hi = x.astype(BF16)
    lo = (x - hi.astype(F32)).astype(BF16)
    xs = jnp.dot(hi, swap, preferred_element_type=F32) + jnp.dot(lo, swap, preferred_element_type=F32)
    return x * cos_t + xs * sin_t


ROPE_ALL = MLA_HEADS * QK_ROPE
NOPE_ALL = MLA_HEADS * QK_NOPE


def _mla_prep_kernel(cq_ref, ckv_ref, kr_ref, gq_ref, gkv_ref, wq_ref, wkv_ref, cos_ref, sin_ref,
                     swap_ref, q_ref, k_ref, v_ref):
    scale = QK_DIM ** -0.5 * LOG2_E
    cos_t, sin_t, swap = cos_ref[...], sin_ref[...], swap_ref[...]
    cqn = _rms(cq_ref[...].astype(F32), gq_ref[...]).astype(BF16)
    ckvn = _rms(ckv_ref[...].astype(F32), gkv_ref[...]).astype(BF16)
    kr = kr_ref[:, :QK_ROPE].astype(F32)
    krot = _rope(kr, cos_t, sin_t, swap[:QK_ROPE, :QK_ROPE]).astype(BF16)
    q = jnp.dot(cqn, wq_ref[...], preferred_element_type=F32)
    qn = (q[:, :NOPE_ALL] * scale).astype(BF16)
    cos_all = jnp.concatenate([cos_t] * MLA_HEADS, axis=1)
    sin_all = jnp.concatenate([sin_t] * MLA_HEADS, axis=1)
    qr = (_rope(q[:, NOPE_ALL:], cos_all, sin_all, swap) * scale).astype(BF16)
    kv = jnp.dot(ckvn, wkv_ref[...], preferred_element_type=F32)
    hw = QK_NOPE + V_DIM
    for h in range(MLA_HEADS):
        q_ref[h, :, :QK_NOPE] = qn[:, h * QK_NOPE:(h + 1) * QK_NOPE]
        q_ref[h, :, QK_NOPE:] = qr[:, h * QK_ROPE:(h + 1) * QK_ROPE]
        k_ref[h, :, :QK_NOPE] = kv[:, h * hw:h * hw + QK_NOPE].astype(BF16)
        k_ref[h, :, QK_NOPE:] = krot
        v_ref[h, 0] = kv[:, h * hw + QK_NOPE:(h + 1) * hw].T.astype(BF16)


def _mla_prep(p, k_rope, q_norm, kv_norm, wq, wkv):
    cos_t, sin_t, swap = _rope_tables()
    tm = MLA_TM
    seq_blocks = SEQ // tm
    return pl.pallas_call(
        _mla_prep_kernel,
        out_shape=(
            jax.ShapeDtypeStruct((MLA_HEADS, N_TOK, QK_DIM), BF16),
            jax.ShapeDtypeStruct((MLA_HEADS, N_TOK, QK_DIM), BF16),
            jax.ShapeDtypeStruct((MLA_HEADS, N_TOK // tm, V_DIM, tm), BF16),
        ),
        grid=(N_TOK // tm,),
        in_specs=[
            pl.BlockSpec((tm, Q_LORA), lambda i: (i, 0)),
            pl.BlockSpec((tm, KV_LORA), lambda i: (i, 1)),
            pl.BlockSpec((tm, LANES), lambda i: (i, 0)),
            pl.BlockSpec((1, Q_LORA), lambda i: (0, 0)),
            pl.BlockSpec((1, KV_LORA), lambda i: (0, 0)),
            pl.BlockSpec((Q_LORA, NOPE_ALL + ROPE_ALL), lambda i: (0, 0)),
            pl.BlockSpec((KV_LORA, MLA_HEADS * (QK_NOPE + V_DIM)), lambda i: (0, 0)),
            pl.BlockSpec((tm, QK_ROPE), lambda i: (i % seq_blocks, 0)),
            pl.BlockSpec((tm, QK_ROPE), lambda i: (i % seq_blocks, 0)),
            pl.BlockSpec((ROPE_ALL, ROPE_ALL), lambda i: (0, 0)),
        ],
        out_specs=(
            pl.BlockSpec((MLA_HEADS, tm, QK_DIM), lambda i: (0, i, 0)),
            pl.BlockSpec((MLA_HEADS, tm, QK_DIM), lambda i: (0, i, 0)),
            pl.BlockSpec((MLA_HEADS, 1, V_DIM, tm), lambda i: (0, i, 0, 0)),
        ),
        compiler_params=_params("parallel"),
        name="mla_prep",
    )(p, p, k_rope, q_norm.reshape(1, Q_LORA), kv_norm.reshape(1, KV_LORA), wq, wkv, cos_t, sin_t, swap)


MLA_TQ = 4096
MLA_TK = 512
MLA_QH = 16
assert MLA_TK == MLA_TM


def _mla_attn_kernel(q_ref, k_ref, vt_ref, o_ref, s_ref, m_ref, l_ref, acc_ref):
    n_chunks = SEQ // MLA_TK
    cols = MLA_TQ // MLA_QH

    def scores(h, c):
        k0 = c * MLA_TK if isinstance(c, int) else pl.multiple_of(c * MLA_TK, MLA_TK)
        return lax.dot_general(k_ref[pl.ds(k0, MLA_TK), :], q_ref[h * cols:(h + 1) * cols, :],
                               (((1,), (1,)), ((), ())), preferred_element_type=F32)

    m_ref[...] = jnp.full(m_ref.shape, NEG_INF, F32)
    l_ref[...] = jnp.zeros(l_ref.shape, F32)
    acc_ref[...] = jnp.zeros(acc_ref.shape, F32)
    for h in range(MLA_QH):
        s_ref[0, h] = scores(h, 0)

    def step(c, slot, prefetch=True):
        for h in range(MLA_QH):
            if prefetch:
                s_ref[1 - slot, h] = scores(h, c + 1)
            s = s_ref[slot, h]
            m_old = m_ref[h]
            m_new = jnp.maximum(m_old, jnp.max(s, axis=0, keepdims=True))
            p = jnp.exp2(s - m_new)
            alpha = jnp.exp2(m_old - m_new)
            l_ref[h] = alpha * l_ref[h] + jnp.sum(p, axis=0, keepdims=True)
            pv = jnp.dot(vt_ref[c], p.astype(BF16), preferred_element_type=F32)
            acc_ref[h] = alpha * acc_ref[h] + pv
            m_ref[h] = m_new

    def pair(i, carry):
        step(2 * i, 0)
        step(2 * i + 1, 1)
        return carry

    lax.fori_loop(0, n_chunks // 2 - 1, pair, 0)
    step(n_chunks - 2, 0)
    step(n_chunks - 1, 1, prefetch=False)
    for h in range(MLA_QH):
        o_ref[h * cols:(h + 1) * cols, :] = (acc_ref[h] / l_ref[h]).T.astype(o_ref.dtype)


def _mla_attn(q, k, v):
    nq = SEQ // MLA_TQ
    return pl.pallas_call(
        _mla_attn_kernel,
        out_shape=jax.ShapeDtypeStruct((N_TOK, MLA_HEADS * V_DIM), BF16),
        grid=(BATCH, MLA_HEADS, nq),
        in_specs=[
            pl.BlockSpec((None, MLA_TQ, QK_DIM), lambda b, h, i: (h, b * nq + i, 0)),
            pl.BlockSpec((None, SEQ, QK_DIM), lambda b, h, i: (h, b, 0)),
            pl.BlockSpec((None, SEQ // MLA_TK, V_DIM, MLA_TK), lambda b, h, i: (h, b, 0, 0)),
        ],
        out_specs=pl.BlockSpec((MLA_TQ, V_DIM), lambda b, h, i: (b * nq + i, h)),
        scratch_shapes=[
            pltpu.VMEM((2, MLA_QH, MLA_TK, MLA_TQ // MLA_QH), F32),
            pltpu.VMEM((MLA_QH, 1, MLA_TQ // MLA_QH), F32),
            pltpu.VMEM((MLA_QH, 1, MLA_TQ // MLA_QH), F32),
            pltpu.VMEM((MLA_QH, V_DIM, MLA_TQ // MLA_QH), F32),
        ],
        compiler_params=_params("parallel", "parallel", "parallel"),
        name="mla_attn",
    )(q, k, v)


SG_TM = 512


def _sg_kernel(u_ref, v_ref, g_ref, w_ref, bt_ref, o_ref):
    v = jax.nn.gelu(v_ref[...].astype(F32))
    vn = _rms(v, g_ref[...]).astype(BF16)
    u = jax.nn.gelu(u_ref[...].astype(F32))
    gc = SG_CH // SG_GROUPS
    for n in range(SG_TM // CHUNK):
        rows = slice(n * CHUNK, (n + 1) * CHUNK)
        for g in range(SG_GROUPS):
            cols = slice(g * gc, (g + 1) * gc)
            mixed = jnp.dot(w_ref[g], vn[rows, cols], preferred_element_type=F32) + bt_ref[:, g:g + 1]
            o_ref[rows, cols] = (u[rows, cols] * mixed).astype(o_ref.dtype)


def _spatial_gating(p, g_norm, w_s, b_s):
    u_blk = (Q_LORA + KV_LORA) // SG_CH
    return pl.pallas_call(
        _sg_kernel,
        out_shape=jax.ShapeDtypeStruct((N_TOK, SG_CH), BF16),
        grid=(N_TOK // SG_TM,),
        in_specs=[
            pl.BlockSpec((SG_TM, SG_CH), lambda i: (i, u_blk)),
            pl.BlockSpec((SG_TM, SG_CH), lambda i: (i, u_blk + 1)),
            pl.BlockSpec((1, SG_CH), lambda i: (0, 0)),
            pl.BlockSpec((SG_GROUPS, CHUNK, CHUNK), lambda i: (0, 0, 0)),
            pl.BlockSpec((CHUNK, SG_GROUPS), lambda i: (0, 0)),
        ],
        out_specs=pl.BlockSpec((SG_TM, SG_CH), lambda i: (i, 0)),
        compiler_params=_params("parallel"),
        name="spatial_gating",
    )(p, p, g_norm.reshape(1, SG_CH), w_s.astype(BF16), b_s.T)


ROUTER_TM = 512


def _router_kernel(x_ref, g_ref, w_ref, b_ref, o_ref, xn_ref):
    xn = _rms(x_ref[...], g_ref[...])
    xn_ref[...] = _pack_halves(xn)
    w = w_ref[...]
    xh, wh = xn.astype(BF16), w.astype(BF16)
    xl, wl = (xn - xh.astype(F32)).astype(BF16), (w - wh.astype(F32)).astype(BF16)
    logits = (jnp.dot(xh, wh, preferred_element_type=F32)
              + (jnp.dot(xh, wl, preferred_element_type=F32) + jnp.dot(xl, wh, preferred_element_type=F32))
              + b_ref[...])
    lane = lax.broadcasted_iota(jnp.int32, logits.shape, 1).astype(F32)
    low = jnp.float32(-3.0e38)

    def first_max(vals):
        top = jnp.max(vals, axis=-1, keepdims=True)
        idx = jnp.min(jnp.where(vals == top, lane, float(LANES)), axis=-1, keepdims=True)
        return top, idx

    is_group = lane < N_GROUPS
    g_top, g_idx = first_max(jnp.where(is_group, logits, low))
    g_prob = 1.0 / jnp.sum(jnp.where(is_group, jnp.exp(logits - g_top), 0.0), axis=-1, keepdims=True)
    lo = N_GROUPS + g_idx * EXPERTS_PER_GROUP
    e_vals = jnp.where(lane >= lo, jnp.where(lane < lo + EXPERTS_PER_GROUP, logits, low), low)
    v1, i1 = first_max(e_vals)
    v2, i2 = first_max(jnp.where(lane == i1, low, e_vals))
    e21 = jnp.exp(v2 - v1)
    w1 = g_prob / (1.0 + e21)
    w2 = g_prob * e21 / (1.0 + e21)
    out = jnp.where(lane == 0, i1 - N_GROUPS,
                    jnp.where(lane == 1, i2 - N_GROUPS,
                              jnp.where(lane == 2, w1, jnp.where(lane == 3, w2, 0.0))))
    o_ref[...] = out


def _router(x, g, w, b):
    tm = ROUTER_TM
    return pl.pallas_call(
        _router_kernel,
        out_shape=(jax.ShapeDtypeStruct((N_TOK, LANES), F32), jax.ShapeDtypeStruct((N_TOK, HALF), jnp.uint32)),
        grid=(N_TOK // tm,),
        in_specs=[
            pl.BlockSpec((tm, D_MODEL), lambda i: (i, 0)),
            pl.BlockSpec((1, D_MODEL), lambda i: (0, 0)),
            pl.BlockSpec((D_MODEL, LANES), lambda i: (0, 0)),
            pl.BlockSpec((1, LANES), lambda i: (0, 0)),
        ],
        out_specs=(pl.BlockSpec((tm, LANES), lambda i: (i, 0)), pl.BlockSpec((tm, HALF), lambda i: (i, 0))),
        compiler_params=_params("parallel"),
        name="router",
    )(x, g.reshape(1, D_MODEL), w, b)


def _dispatch_plan(eid):
    tok = jnp.arange(N_TOK, dtype=jnp.int32)
    keys = jnp.concatenate([eid[:, k] * N_ASSIGN + (k * N_TOK + tok) for k in range(TOP_K)])
    row_asg = jnp.sort(keys) & (N_ASSIGN - 1)
    row_asg = jnp.concatenate([row_asg, jnp.zeros((SUBLANES,), jnp.int32)])
    counts = jnp.sum(eid.reshape(-1)[:, None] == jnp.arange(N_EXPERTS, dtype=jnp.int32)[None, :], axis=0,
                     dtype=jnp.int32)
    starts = jnp.cumsum(counts) - counts
    nblk_e = (counts + MOE_BLOCK - 1) // MOE_BLOCK
    blk_end = jnp.cumsum(nblk_e)
    blk_start = blk_end - nblk_e
    n_active = blk_end[-1]
    bidx = jnp.arange(MOE_NBLK, dtype=jnp.int32)
    blk_x = jnp.minimum(bidx, n_active - 1)
    blk_e = jnp.minimum(jnp.sum(blk_end[None, :] <= blk_x[:, None], axis=1, dtype=jnp.int32), N_EXPERTS - 1)
    first = (blk_x - blk_start[blk_e]) * MOE_BLOCK
    blk_rows = jnp.where(bidx < n_active, jnp.clip(counts[blk_e] - first, 0, MOE_BLOCK), 0).astype(jnp.int32)
    blk_src = (starts[blk_e] + first).astype(jnp.int32)
    return row_asg, blk_e, blk_rows, blk_src


GATHER_STEP = min(1, N_FT - 1)
_ROW_BITS = tuple(1 << k for k in range(MOE_BLOCK.bit_length() - 1, -1, -1))


def _expert_kernel(row_asg, blk_e, blk_rows, blk_src, x_hbm, w1_ref, w3_ref, w2_ref, ys_hbm,
                   xbuf, acc, ypk, xn_ref, gsem, ssem):
    b = pl.program_id(0)
    j = pl.program_id(1)
    rows = blk_rows[b]
    slot = b % 2

    def n_fetch(blk):
        return (blk_rows[blk] + SUBLANES - 1) & -SUBLANES

    def issue_gather(blk, s):
        base = blk_src[blk]

        def group(q, c):
            for u in range(SUBLANES):
                t = row_asg[base + q * SUBLANES + u] & (N_TOK - 1)
                pltpu.make_async_copy(x_hbm.at[pl.ds(t, 1), :], xbuf.at[s, q, pl.ds(u, 1), :], gsem.at[s]).start()
            return c

        lax.fori_loop(0, n_fetch(blk) >> SUBLANE_SHIFT, group, 0)

    def wait_rows(n, copy_of):
        for bit in _ROW_BITS:
            @pl.when((n & bit) != 0)
            def _():
                copy_of(bit).wait()

    def gather_block(s):
        def copy_of(k):
            tiles = xbuf.at[s, pl.ds(0, k // SUBLANES)]
            return pltpu.make_async_copy(tiles, tiles, gsem.at[s])
        return copy_of

    def scatter_block(s):
        def copy_of(k):
            part = ypk.at[s, pl.ds(0, k // SUBLANES)] if k >= SUBLANES else ypk.at[s, 0, pl.ds(0, k), :]
            return pltpu.make_async_copy(part, part, ssem.at[s])
        return copy_of

    def issue_scatter(blk, s, n):
        base = blk_src[blk]

        def one(q, u, a):
            pltpu.make_async_copy(ypk.at[s, q, pl.ds(u, 1), :], ys_hbm.at[pl.ds(a, 1), :], ssem.at[s]).start()

        def group(q, c):
            for u in range(SUBLANES):
                one(q, u, row_asg[base + q * SUBLANES + u])
            return c

        def tail(i, c):
            one(i >> SUBLANE_SHIFT, i & (SUBLANES - 1), row_asg[base + i])
            return c

        full = n >> SUBLANE_SHIFT
        lax.fori_loop(0, full, group, 0)
        lax.fori_loop(full * SUBLANES, n, tail, 0)

    def for_row_tile(fn):
        lo = 0
        for m in MOE_ROW_TILES:
            @pl.when(jnp.logical_and(rows > lo, rows <= m))
            def _():
                fn(m)
            lo = m

    nxt = jnp.minimum(b + 1, MOE_NBLK - 1)
    has_next = jnp.logical_and(b + 1 < MOE_NBLK, blk_rows[nxt] > 0)

    @pl.when(rows > 0)
    def _():
        @pl.when(j == 0)
        def _():
            @pl.when(b == 0)
            def _():
                xbuf[...] = jnp.zeros(xbuf.shape, jnp.uint32)
                acc[...] = jnp.zeros(acc.shape, F32)
                issue_gather(0, 0)

            wait_rows(n_fetch(b), gather_block(slot))

            def unpack_tile(m):
                lo, hi = _unpack_halves(xbuf[slot, :m // SUBLANES].reshape(m, HALF))
                xn_ref[:m, :HALF] = lo.astype(BF16)
                xn_ref[:m, HALF:] = hi.astype(BF16)

            for_row_tile(unpack_tile)

        @pl.when(jnp.logical_and(j == GATHER_STEP, has_next))
        def _():
            issue_gather(nxt, 1 - slot)

        def ffn_tile(m):
            xs = xn_ref[:m, :]
            h1 = jnp.dot(xs, w1_ref[...].astype(BF16), preferred_element_type=F32)
            h3 = jnp.dot(xs, w3_ref[...].astype(BF16), preferred_element_type=F32)
            h = (jax.nn.silu(h1) * h3).astype(BF16)
            y = jnp.dot(h, w2_ref[...].astype(BF16), preferred_element_type=F32)
            acc[:m, :] = jnp.where(j > 0, acc[:m, :], 0.0) + y

        for_row_tile(ffn_tile)

        @pl.when(j == N_FT - 1)
        def _():
            @pl.when(b > 0)
            def _():
                wait_rows(blk_rows[jnp.maximum(b - 1, 0)], scatter_block(1 - slot))

            def pack_tile(m):
                ypk[slot, :m // SUBLANES] = _pack_halves(acc[:m, :]).reshape(m // SUBLANES, SUBLANES, HALF)

            for_row_tile(pack_tile)

            issue_scatter(b, slot, rows)

            @pl.when(jnp.logical_not(has_next))
            def _():
                wait_rows(rows, scatter_block(slot))


def _moe_experts(xn, w1, w3, w2, layer, row_asg, blk_e, blk_rows, blk_src):
    def w_col(b, j, ra, be, br, bs):
        return (layer, be[b], 0, jnp.where(br[b] > 0, j, N_FT - 1))

    def w_row(b, j, ra, be, br, bs):
        return (layer, be[b], jnp.where(br[b] > 0, j, N_FT - 1), 0)

    return pl.pallas_call(
        _expert_kernel,
        out_shape=jax.ShapeDtypeStruct((N_ASSIGN, HALF), jnp.uint32),
        grid_spec=pltpu.PrefetchScalarGridSpec(
            num_scalar_prefetch=4,
            grid=(MOE_NBLK, N_FT),
            in_specs=[
                pl.BlockSpec(memory_space=pl.ANY),
                pl.BlockSpec((None, None, D_MODEL, MOE_FT), w_col),
                pl.BlockSpec((None, None, D_MODEL, MOE_FT), w_col),
                pl.BlockSpec((None, None, MOE_FT, D_MODEL), w_row),
            ],
            out_specs=pl.BlockSpec(memory_space=pl.ANY),
            scratch_shapes=[
                pltpu.VMEM((2, MOE_BLOCK // SUBLANES, SUBLANES, HALF), jnp.uint32),
                pltpu.VMEM((MOE_BLOCK, D_MODEL), F32),
                pltpu.VMEM((2, MOE_BLOCK // SUBLANES, SUBLANES, HALF), jnp.uint32),
                pltpu.VMEM((MOE_BLOCK, D_MODEL), BF16),
                pltpu.SemaphoreType.DMA((2,)),
                pltpu.SemaphoreType.DMA((2,)),
            ],
        ),
        compiler_params=_params("arbitrary", "arbitrary"),
        name="moe_experts",
    )(row_asg, blk_e, blk_rows, blk_src, xn, w1, w3, w2)


def _combine_kernel(final_norm, x_ref, y0_ref, y1_ref, gate_ref, gf_ref, o_ref):
    gates = gate_ref[...]
    g0, g1 = gates[:, 2:3], gates[:, 3:4]
    lo0, hi0 = _unpack_halves(y0_ref[...])
    lo1, hi1 = _unpack_halves(y1_ref[...])
    lo = x_ref[:, :HALF] + (g0 * lo0 + g1 * lo1)
    hi = x_ref[:, HALF:] + (g0 * hi0 + g1 * hi1)
    if final_norm:
        ms = (jnp.sum(lo * lo, axis=-1, keepdims=True) + jnp.sum(hi * hi, axis=-1, keepdims=True)) / D_MODEL
        r = lax.rsqrt(ms + EPS)
        lo, hi = lo * r * gf_ref[:, :HALF], hi * r * gf_ref[:, HALF:]
    o_ref[:, :HALF] = lo
    o_ref[:, HALF:] = hi


def _moe_combine(x, ys, route, g_final, final_norm):
    tm = COMBINE_TOK
    return pl.pallas_call(
        functools.partial(_combine_kernel, final_norm),
        out_shape=jax.ShapeDtypeStruct((N_TOK, D_MODEL), F32),
        grid=(N_TOK // tm,),
        in_specs=[
            pl.BlockSpec((tm, D_MODEL), lambda i: (i, 0)),
            pl.BlockSpec((tm, HALF), lambda i: (i, 0)),
            pl.BlockSpec((tm, HALF), lambda i: (N_TOK // tm + i, 0)),
            pl.BlockSpec((tm, LANES), lambda i: (i, 0)),
            pl.BlockSpec((1, D_MODEL), lambda i: (0, 0)),
        ],
        out_specs=pl.BlockSpec((tm, D_MODEL), lambda i: (i, 0)),
        compiler_params=_params("parallel"),
        name="moe_combine",
    )(x, ys, ys, route, g_final.reshape(1, D_MODEL))


def _hier_moe(x, g, wg, bg, we, be, w1, w3, w2, layer, g_final, final_norm):
    pad = LANES - N_GROUPS - N_EXPERTS
    w_r = jnp.concatenate([wg, we, jnp.zeros((D_MODEL, pad), F32)], axis=1)
    b_r = jnp.concatenate([bg, be, jnp.zeros((pad,), F32)]).reshape(1, LANES)
    route, xn = _router(x, g, w_r, b_r)
    eid = route[:, :TOP_K].astype(jnp.int32)
    row_asg, blk_e, blk_rows, blk_src = _dispatch_plan(eid)
    ys = _moe_experts(xn, w1, w3, w2, layer, row_asg, blk_e, blk_rows, blk_src)
    return _moe_combine(x, ys, route, g_final, final_norm)


def kernel(x, norm_mix, norm_ffn, norm_final, w_in_ab, na_rpb, conv_w, conv_b, w_out_ab, w_in_cd,
           q_norm, kv_norm, w_uq, w_ukv, sg_norm, sg_w, sg_b, w_out_cd, router_group_w,
           router_group_b, router_expert_w, router_expert_b, w1, w3, w2):
    xt = x.reshape(N_TOK, D_MODEL)
    for layer in range(DEPTH):
        i = layer // 2
        if layer % 2 == 0:
            q_scale = np.where(np.arange(AB_IN) < NA_WIDTH, NA_Q_SCALE, 1.0).astype(np.float32)
            p = _norm_matmul(xt, norm_mix[layer], (w_in_ab[i] * q_scale).astype(BF16), tm=1024, tn=1536)
            a_out = _natten(p, _natten_bias_table(na_rpb[i]))
            b_out = _gated_conv(p, conv_w[i], conv_b[i])
            xt = _out_proj(a_out, b_out, w_out_ab[i].astype(BF16), xt)
        else:
            p, k_rope = _cd_in_proj(xt, norm_mix[layer], w_in_cd[i])
            wq = w_uq[i].reshape(Q_LORA, MLA_HEADS, QK_DIM)
            wq = jnp.concatenate([wq[:, :, :QK_NOPE].reshape(Q_LORA, NOPE_ALL),
                                  wq[:, :, QK_NOPE:].reshape(Q_LORA, ROPE_ALL)], axis=1).astype(BF16)
            q, k, v = _mla_prep(p, k_rope, q_norm[i], kv_norm[i], wq, w_ukv[i].astype(BF16))
            c_out = _mla_attn(q, k, v)
            d_out = _spatial_gating(p, sg_norm[i], sg_w[i], sg_b[i])
            xt = _out_proj(c_out, d_out, w_out_cd[i].astype(BF16), xt)
        xt = _hier_moe(xt, norm_ffn[layer], router_group_w[layer], router_group_b[layer],
                       router_expert_w[layer], router_expert_b[layer], w1, w3, w2, layer,
                       norm_final, final_norm=(layer == DEPTH - 1))
    return xt.reshape(BATCH, SEQ, D_MODEL)
```

```python
import functools

import numpy as np
import jax
import jax.numpy as jnp
from jax import lax
from jax.experimental import pallas as pl
from jax.experimental.pallas import tpu as pltpu

F32 = jnp.float32
BF16 = jnp.bfloat16

D_MODEL = 2048
BATCH = 2
SEQ = 4096
N_TOK = BATCH * SEQ
DEPTH = 2
GRID_W = 64
GRID_ROWS = SEQ // GRID_W
WIN_ROWS = 8
WIN_COLS = 16
NA_HEADS = 8
NA_HEAD_DIM = 128
NA_WIDTH = NA_HEADS * NA_HEAD_DIM
NA_KEYS = WIN_ROWS * GRID_W
CONV_CH = D_MODEL // 2
MLA_HEADS = 8
Q_LORA = 512
KV_LORA = 512
QK_NOPE = 128
QK_ROPE = 64
QK_DIM = QK_NOPE + QK_ROPE
V_DIM = 128
ROPE_THETA = 10000.0
SG_CH = D_MODEL // 2
SG_GROUPS = 8
CHUNK = 128
N_GROUPS = 8
EXPERTS_PER_GROUP = 8
N_EXPERTS = N_GROUPS * EXPERTS_PER_GROUP
TOP_K = 2
D_EXPERT = 768
EPS = 1e-6
NEG_INF = -1e30
LOG2_E = 1.4426950408889634
AB_IN = 3 * NA_WIDTH + 3 * CONV_CH

LANES = 128
SUBLANES = 8
SUBLANE_SHIFT = 3
MOE_BLOCK = 512
MOE_ROW_TILES = (256, 288, 320, 384, 512)
MOE_FT = 256
N_FT = D_EXPERT // MOE_FT
N_ASSIGN = N_TOK * TOP_K
MOE_NBLK = N_ASSIGN // MOE_BLOCK + N_EXPERTS
COMBINE_TOK = 512
VMEM_LIMIT = 52 * 1024 * 1024


def _params(*sem):
    return pltpu.CompilerParams(dimension_semantics=sem, vmem_limit_bytes=VMEM_LIMIT)


def _rms(x, g):
    return x * lax.rsqrt(jnp.mean(x * x, axis=-1, keepdims=True) + EPS) * g


HALF = D_MODEL // 2
BF16_BITS = 16
HIGH_HALF = 0xFFFF0000


def _pack_halves(x):
    lo = pltpu.bitcast(x[:, :HALF].astype(BF16).astype(F32), jnp.uint32)
    hi = pltpu.bitcast(x[:, HALF:].astype(BF16).astype(F32), jnp.uint32)
    return (lo >> BF16_BITS) | (hi & jnp.uint32(HIGH_HALF))


def _unpack_halves(w):
    return pltpu.bitcast(w << BF16_BITS, F32), pltpu.bitcast(w & jnp.uint32(HIGH_HALF), F32)


def _norm_mm_kernel(x_ref, g_ref, w_ref, o_ref, xn_ref):
    @pl.when(pl.program_id(1) == 0)
    def _():
        xn_ref[...] = _rms(x_ref[...].astype(F32), g_ref[...]).astype(BF16)

    o_ref[...] = jnp.dot(xn_ref[...], w_ref[...], preferred_element_type=F32).astype(o_ref.dtype)


def _norm_matmul(x, g, w, tm, tn):
    m, k = x.shape
    nout = w.shape[1]
    return pl.pallas_call(
        _norm_mm_kernel,
        out_shape=jax.ShapeDtypeStruct((m, nout), BF16),
        grid=(m // tm, nout // tn),
        in_specs=[
            pl.BlockSpec((tm, k), lambda i, j: (i, 0)),
            pl.BlockSpec((1, k), lambda i, j: (0, 0)),
            pl.BlockSpec((k, tn), lambda i, j: (0, j)),
        ],
        out_specs=pl.BlockSpec((tm, tn), lambda i, j: (i, j)),
        scratch_shapes=[pltpu.VMEM((tm, k), BF16)],
        compiler_params=_params("parallel", "arbitrary"),
        name="norm_matmul",
    )(x, g.reshape(1, k), w)


CD_TN = 1024
CD_MAIN = Q_LORA + KV_LORA + 2 * SG_CH
CD_A_TILES = (Q_LORA + KV_LORA) // CD_TN
CD_MAIN_TILES = CD_MAIN // CD_TN


def _cd_proj_kernel(x_ref, g_ref, wa_ref, wb_ref, wc_ref, main_ref, kr_ref, xn_ref):
    j = pl.program_id(1)

    @pl.when(j == 0)
    def _():
        xn_ref[...] = _rms(x_ref[...], g_ref[...]).astype(BF16)

    @pl.when(j < CD_A_TILES)
    def _():
        main_ref[...] = jnp.dot(xn_ref[...], wa_ref[...], preferred_element_type=F32).astype(BF16)

    @pl.when(j >= CD_A_TILES)
    def _():
        main_ref[...] = jnp.dot(xn_ref[...], wb_ref[...], preferred_element_type=F32).astype(BF16)

    @pl.when(j == 0)
    def _():
        kr_ref[...] = jnp.dot(xn_ref[...], wc_ref[...], preferred_element_type=F32).astype(BF16)


def _cd_in_proj(x, g, w, tm=1024):
    c0, c1 = Q_LORA + KV_LORA, Q_LORA + KV_LORA + QK_ROPE
    wa = w[:, :c0].astype(BF16)
    wb = w[:, c1:].astype(BF16)
    wc = jnp.pad(w[:, c0:c1], ((0, 0), (0, LANES - QK_ROPE))).astype(BF16)
    return pl.pallas_call(
        _cd_proj_kernel,
        out_shape=(jax.ShapeDtypeStruct((N_TOK, CD_MAIN), BF16), jax.ShapeDtypeStruct((N_TOK, LANES), BF16)),
        grid=(N_TOK // tm, CD_MAIN_TILES),
        in_specs=[
            pl.BlockSpec((tm, D_MODEL), lambda i, j: (i, 0)),
            pl.BlockSpec((1, D_MODEL), lambda i, j: (0, 0)),
            pl.BlockSpec((D_MODEL, CD_TN), lambda i, j: (0, jnp.minimum(j, CD_A_TILES - 1))),
            pl.BlockSpec((D_MODEL, CD_TN), lambda i, j: (0, jnp.maximum(j - CD_A_TILES, 0))),
            pl.BlockSpec((D_MODEL, LANES), lambda i, j: (0, 0)),
        ],
        out_specs=(
            pl.BlockSpec((tm, CD_TN), lambda i, j: (i, j)),
            pl.BlockSpec((tm, LANES), lambda i, j: (i, 0)),
        ),
        scratch_shapes=[pltpu.VMEM((tm, D_MODEL), BF16)],
        compiler_params=_params("parallel", "arbitrary"),
        name="cd_in_proj",
    )(x, g.reshape(1, D_MODEL), wa, wb, wc)


def _natten_bias_table(rpb):
    c = np.arange(GRID_W)
    col_start = np.clip(c - WIN_COLS // 2, 0, GRID_W - WIN_COLS)
    valid = (c[None, :] >= col_start[:, None]) & (c[None, :] < col_start[:, None] + WIN_COLS)
    dc = np.clip(c[None, :] - c[:, None] + WIN_COLS - 1, 0, 2 * WIN_COLS - 2)
    pick = (dc[:, :, None] == np.arange(2 * WIN_COLS - 1)).astype(np.float32)
    m = jnp.einsum('hrd,ckd->hcrk', rpb * LOG2_E, jnp.asarray(pick), precision=lax.Precision.HIGHEST)
    m = jnp.where(valid[None, :, None, :], m, NEG_INF)
    t = jnp.stack([m[:, :, o:o + WIN_ROWS] for o in range(WIN_ROWS)], axis=1)
    return t.reshape(rpb.shape[0], WIN_ROWS, GRID_W, NA_KEYS).astype(F32)


NA_ROWS_PER_STEP = 32
NA_Q_SCALE = NA_HEAD_DIM ** -0.5 * LOG2_E


def _natten_kernel(q_ref, k_ref, v_ref, t_ref, o_ref):
    def rows(i, carry):
        rs = [i * NA_ROWS_PER_STEP + u for u in range(NA_ROWS_PER_STEP)]
        kr0s = [jnp.clip(r - WIN_ROWS // 2, 0, GRID_ROWS - WIN_ROWS) for r in rs]
        q0s = [pl.multiple_of(r * GRID_W, GRID_W) for r in rs]
        k0s = [pl.multiple_of(kr0 * GRID_W, GRID_W) for kr0 in kr0s]
        ss = [lax.dot_general(q_ref[pl.ds(q0, GRID_W), :], k_ref[pl.ds(k0, NA_KEYS), :],
                              (((1,), (1,)), ((), ())), preferred_element_type=F32)
              for q0, k0 in zip(q0s, k0s)]
        ps, ls = [], []
        for s, r, kr0 in zip(ss, rs, kr0s):
            s = s + t_ref[kr0 - r + WIN_ROWS - 1]
            p = jnp.exp2(s - jnp.max(s, axis=-1, keepdims=True))
            ls.append(jnp.sum(p, axis=-1, keepdims=True))
            ps.append(p.astype(BF16))
        for p, l, q0, k0 in zip(ps, ls, q0s, k0s):
            o = jnp.dot(p, v_ref[pl.ds(k0, NA_KEYS), :], preferred_element_type=F32) / l
            o_ref[pl.ds(q0, GRID_W), :] = o.astype(o_ref.dtype)
        return carry

    lax.fori_loop(0, GRID_ROWS // NA_ROWS_PER_STEP, rows, 0)


def _natten(p, table):
    blk = (SEQ, NA_HEAD_DIM)
    return pl.pallas_call(
        _natten_kernel,
        out_shape=jax.ShapeDtypeStruct((N_TOK, NA_WIDTH), BF16),
        grid=(BATCH, NA_HEADS),
        in_specs=[
            pl.BlockSpec(blk, lambda b, h: (b, h)),
            pl.BlockSpec(blk, lambda b, h: (b, NA_HEADS + h)),
            pl.BlockSpec(blk, lambda b, h: (b, 2 * NA_HEADS + h)),
            pl.BlockSpec((None, WIN_ROWS, GRID_W, NA_KEYS), lambda b, h: (h, 0, 0, 0)),
        ],
        out_specs=pl.BlockSpec(blk, lambda b, h: (b, h)),
        compiler_params=_params("parallel", "parallel"),
        name="natten",
    )(p, p, p, table)


CONV_CB = 256


def _conv_kernel(gb_ref, gc_ref, hc_ref, w_ref, b_ref, o_ref):
    z = gc_ref[...].astype(F32) * hc_ref[...].astype(F32)
    pos = lax.broadcasted_iota(jnp.int32, z.shape, 0)
    z_prev = jnp.where(pos == 0, 0.0, pltpu.roll(z, 1, 0))
    z_next = jnp.where(pos == SEQ - 1, 0.0, pltpu.roll(z, SEQ - 1, 0))
    y = b_ref[...] + z_prev * w_ref[0:1, :] + z * w_ref[1:2, :] + z_next * w_ref[2:3, :]
    o_ref[...] = (gb_ref[...].astype(F32) * y).astype(o_ref.dtype)


def _gated_conv(p, w, b):
    base = 3 * NA_WIDTH // CONV_CB
    step = CONV_CH // CONV_CB
    blk = (SEQ, CONV_CB)
    return pl.pallas_call(
        _conv_kernel,
        out_shape=jax.ShapeDtypeStruct((N_TOK, CONV_CH), BF16),
        grid=(BATCH, step),
        in_specs=[
            pl.BlockSpec(blk, lambda bi, c: (bi, base + c)),
            pl.BlockSpec(blk, lambda bi, c: (bi, base + step + c)),
            pl.BlockSpec(blk, lambda bi, c: (bi, base + 2 * step + c)),
            pl.BlockSpec((3, CONV_CB), lambda bi, c: (0, c)),
            pl.BlockSpec((1, CONV_CB), lambda bi, c: (0, c)),
        ],
        out_specs=pl.BlockSpec(blk, lambda bi, c: (bi, c)),
        compiler_params=_params("parallel", "parallel"),
        name="gated_conv",
    )(p, p, p, w, b.reshape(1, CONV_CH))


def _out_proj_kernel(a_ref, b_ref, wa_ref, wb_ref, r_ref, o_ref):
    acc = jnp.dot(a_ref[...], wa_ref[...], preferred_element_type=F32)
    acc = acc + jnp.dot(b_ref[...], wb_ref[...], preferred_element_type=F32)
    o_ref[...] = r_ref[...] + acc


def _out_proj(a, b, w, res, tm=512):
    ka, kb = a.shape[1], b.shape[1]
    assert ka == kb
    return pl.pallas_call(
        _out_proj_kernel,
        out_shape=jax.ShapeDtypeStruct((N_TOK, D_MODEL), F32),
        grid=(N_TOK // tm,),
        in_specs=[
            pl.BlockSpec((tm, ka), lambda i: (i, 0)),
            pl.BlockSpec((tm, kb), lambda i: (i, 0)),
            pl.BlockSpec((ka, D_MODEL), lambda i: (0, 0)),
            pl.BlockSpec((kb, D_MODEL), lambda i: (1, 0)),
            pl.BlockSpec((tm, D_MODEL), lambda i: (i, 0)),
        ],
        out_specs=pl.BlockSpec((tm, D_MODEL), lambda i: (i, 0)),
        compiler_params=_params("parallel"),
        name="out_proj",
    )(a, b, w, w, res)


MLA_TM = 512


def _rope_tables():
    f32 = np.float32
    pos = np.arange(SEQ)
    row = (pos // GRID_W).astype(f32)
    col = (pos % GRID_W).astype(f32)
    half = QK_ROPE // 2
    inv = np.power(f32(ROPE_THETA), -np.arange(0, half, 2, dtype=f32) / f32(half)).astype(f32)
    ar, ac = row[:, None] * inv, col[:, None] * inv
    cos_t = jnp.asarray(np.concatenate([np.cos(ar), np.cos(ar), np.cos(ac), np.cos(ac)], axis=-1).astype(f32))
    sin_t = jnp.asarray(np.concatenate([-np.sin(ar), np.sin(ar), -np.sin(ac), np.sin(ac)], axis=-1).astype(f32))
    quarter = half // 2
    src = np.arange(QK_ROPE) + np.where((np.arange(QK_ROPE) // quarter) % 2 == 0, quarter, -quarter)
    swap = np.zeros((QK_ROPE, QK_ROPE), np.float32)
    swap[src, np.arange(QK_ROPE)] = 1.0
    swap_all = np.kron(np.eye(MLA_HEADS, dtype=np.float32), swap)
    return cos_t, sin_t, jnp.asarray(swap_all, BF16)


def _rope(x, cos_t, sin_t, swap):
    hi = x.astype(BF16)
    lo = (x - hi.astype(F32)).astype(BF16)
    xs = jnp.dot(hi, swap, preferred_element_type=F32) + jnp.dot(lo, swap, preferred_element_type=F32)
    return x * cos_t + xs * sin_t


ROPE_ALL = MLA_HEADS * QK_ROPE
NOPE_ALL = MLA_HEADS * QK_NOPE


def _mla_prep_kernel(cq_ref, ckv_ref, kr_ref, gq_ref, gkv_ref, wq_ref, wkv_ref, cos_ref, sin_ref,
                     swap_ref, q_ref, k_ref, v_ref):
    scale = QK_DIM ** -0.5 * LOG2_E
    cos_t, sin_t, swap = cos_ref[...], sin_ref[...], swap_ref[...]
    cqn = _rms(cq_ref[...].astype(F32), gq_ref[...]).astype(BF16)
    ckvn = _rms(ckv_ref[...].astype(F32), gkv_ref[...]).astype(BF16)
    kr = kr_ref[:, :QK_ROPE].astype(F32)
    krot = _rope(kr, cos_t, sin_t, swap[:QK_ROPE, :QK_ROPE]).astype(BF16)
    q = jnp.dot(cqn, wq_ref[...], preferred_element_type=F32)
    qn = (q[:, :NOPE_ALL] * scale).astype(BF16)
    cos_all = jnp.concatenate([cos_t] * MLA_HEADS, axis=1)
    sin_all = jnp.concatenate([sin_t] * MLA_HEADS, axis=1)
    qr = (_rope(q[:, NOPE_ALL:], cos_all, sin_all, swap) * scale).astype(BF16)
    kv = jnp.dot(ckvn, wkv_ref[...], preferred_element_type=F32)
    hw = QK_NOPE + V_DIM
    for h in range(MLA_HEADS):
        q_ref[h, :, :QK_NOPE] = qn[:, h * QK_NOPE:(h + 1) * QK_NOPE]
        q_ref[h, :, QK_NOPE:] = qr[:, h * QK_ROPE:(h + 1) * QK_ROPE]
        k_ref[h, :, :QK_NOPE] = kv[:, h * hw:h * hw + QK_NOPE].astype(BF16)
        k_ref[h, :, QK_NOPE:] = krot
        v_ref[h, 0] = kv[:, h * hw + QK_NOPE:(h + 1) * hw].T.astype(BF16)


def _mla_prep(p, k_rope, q_norm, kv_norm, wq, wkv):
    cos_t, sin_t, swap = _rope_tables()
    tm = MLA_TM
    seq_blocks = SEQ // tm
    return pl.pallas_call(
        _mla_prep_kernel,
        out_shape=(
            jax.ShapeDtypeStruct((MLA_HEADS, N_TOK, QK_DIM), BF16),
            jax.ShapeDtypeStruct((MLA_HEADS, N_TOK, QK_DIM), BF16),
            jax.ShapeDtypeStruct((MLA_HEADS, N_TOK // tm, V_DIM, tm), BF16),
        ),
        grid=(N_TOK // tm,),
        in_specs=[
            pl.BlockSpec((tm, Q_LORA), lambda i: (i, 0)),
            pl.BlockSpec((tm, KV_LORA), lambda i: (i, 1)),
            pl.BlockSpec((tm, LANES), lambda i: (i, 0)),
            pl.BlockSpec((1, Q_LORA), lambda i: (0, 0)),
            pl.BlockSpec((1, KV_LORA), lambda i: (0, 0)),
            pl.BlockSpec((Q_LORA, NOPE_ALL + ROPE_ALL), lambda i: (0, 0)),
            pl.BlockSpec((KV_LORA, MLA_HEADS * (QK_NOPE + V_DIM)), lambda i: (0, 0)),
            pl.BlockSpec((tm, QK_ROPE), lambda i: (i % seq_blocks, 0)),
            pl.BlockSpec((tm, QK_ROPE), lambda i: (i % seq_blocks, 0)),
            pl.BlockSpec((ROPE_ALL, ROPE_ALL), lambda i: (0, 0)),
        ],
        out_specs=(
            pl.BlockSpec((MLA_HEADS, tm, QK_DIM), lambda i: (0, i, 0)),
            pl.BlockSpec((MLA_HEADS, tm, QK_DIM), lambda i: (0, i, 0)),
            pl.BlockSpec((MLA_HEADS, 1, V_DIM, tm), lambda i: (0, i, 0, 0)),
        ),
        compiler_params=_params("parallel"),
        name="mla_prep",
    )(p, p, k_rope, q_norm.reshape(1, Q_LORA), kv_norm.reshape(1, KV_LORA), wq, wkv, cos_t, sin_t, swap)


MLA_TQ = 4096
MLA_TK = 512
MLA_QH = 16
assert MLA_TK == MLA_TM


def _mla_attn_kernel(q_ref, k_ref, vt_ref, o_ref, s_ref, m_ref, l_ref, acc_ref):
    n_chunks = SEQ // MLA_TK
    cols = MLA_TQ // MLA_QH

    def scores(h, c):
        k0 = c * MLA_TK if isinstance(c, int) else pl.multiple_of(c * MLA_TK, MLA_TK)
        return lax.dot_general(k_ref[pl.ds(k0, MLA_TK), :], q_ref[h * cols:(h + 1) * cols, :],
                               (((1,), (1,)), ((), ())), preferred_element_type=F32)

    m_ref[...] = jnp.full(m_ref.shape, NEG_INF, F32)
    l_ref[...] = jnp.zeros(l_ref.shape, F32)
    acc_ref[...] = jnp.zeros(acc_ref.shape, F32)
    for h in range(MLA_QH):
        s_ref[0, h] = scores(h, 0)

    def step(c, slot, prefetch=True):
        for h in range(MLA_QH):
            if prefetch:
                s_ref[1 - slot, h] = scores(h, c + 1)
            s = s_ref[slot, h]
            m_old = m_ref[h]
            m_new = jnp.maximum(m_old, jnp.max(s, axis=0, keepdims=True))
            p = jnp.exp2(s - m_new)
            alpha = jnp.exp2(m_old - m_new)
            l_ref[h] = alpha * l_ref[h] + jnp.sum(p, axis=0, keepdims=True)
            pv = jnp.dot(vt_ref[c], p.astype(BF16), preferred_element_type=F32)
            acc_ref[h] = alpha * acc_ref[h] + pv
            m_ref[h] = m_new

    def pair(i, carry):
        step(2 * i, 0)
        step(2 * i + 1, 1)
        return carry

    lax.fori_loop(0, n_chunks // 2 - 1, pair, 0)
    step(n_chunks - 2, 0)
    step(n_chunks - 1, 1, prefetch=False)
    for h in range(MLA_QH):
        o_ref[h * cols:(h + 1) * cols, :] = (acc_ref[h] / l_ref[h]).T.astype(o_ref.dtype)


def _mla_attn(q, k, v):
    nq = SEQ // MLA_TQ
    return pl.pallas_call(
        _mla_attn_kernel,
        out_shape=jax.ShapeDtypeStruct((N_TOK, MLA_HEADS * V_DIM), BF16),
        grid=(BATCH, MLA_HEADS, nq),
        in_specs=[
            pl.BlockSpec((None, MLA_TQ, QK_DIM), lambda b, h, i: (h, b * nq + i, 0)),
            pl.BlockSpec((None, SEQ, QK_DIM), lambda b, h, i: (h, b, 0)),
            pl.BlockSpec((None, SEQ // MLA_TK, V_DIM, MLA_TK), lambda b, h, i: (h, b, 0, 0)),
        ],
        out_specs=pl.BlockSpec((MLA_TQ, V_DIM), lambda b, h, i: (b * nq + i, h)),
        scratch_shapes=[
            pltpu.VMEM((2, MLA_QH, MLA_TK, MLA_TQ // MLA_QH), F32),
            pltpu.VMEM((MLA_QH, 1, MLA_TQ // MLA_QH), F32),
            pltpu.VMEM((MLA_QH, 1, MLA_TQ // MLA_QH), F32),
            pltpu.VMEM((MLA_QH, V_DIM, MLA_TQ // MLA_QH), F32),
        ],
        compiler_params=_params("parallel", "parallel", "parallel"),
        name="mla_attn",
    )(q, k, v)


SG_TM = 512


def _sg_kernel(u_ref, v_ref, g_ref, w_ref, bt_ref, o_ref):
    v = jax.nn.gelu(v_ref[...].astype(F32))
    vn = _rms(v, g_ref[...]).astype(BF16)
    u = jax.nn.gelu(u_ref[...].astype(F32))
    gc = SG_CH // SG_GROUPS
    for n in range(SG_TM // CHUNK):
        rows = slice(n * CHUNK, (n + 1) * CHUNK)
        for g in range(SG_GROUPS):
            cols = slice(g * gc, (g + 1) * gc)
            mixed = jnp.dot(w_ref[g], vn[rows, cols], preferred_element_type=F32) + bt_ref[:, g:g + 1]
            o_ref[rows, cols] = (u[rows, cols] * mixed).astype(o_ref.dtype)


def _spatial_gating(p, g_norm, w_s, b_s):
    u_blk = (Q_LORA + KV_LORA) // SG_CH
    return pl.pallas_call(
        _sg_kernel,
        out_shape=jax.ShapeDtypeStruct((N_TOK, SG_CH), BF16),
        grid=(N_TOK // SG_TM,),
        in_specs=[
            pl.BlockSpec((SG_TM, SG_CH), lambda i: (i, u_blk)),
            pl.BlockSpec((SG_TM, SG_CH), lambda i: (i, u_blk + 1)),
            pl.BlockSpec((1, SG_CH), lambda i: (0, 0)),
            pl.BlockSpec((SG_GROUPS, CHUNK, CHUNK), lambda i: (0, 0, 0)),
            pl.BlockSpec((CHUNK, SG_GROUPS), lambda i: (0, 0)),
        ],
        out_specs=pl.BlockSpec((SG_TM, SG_CH), lambda i: (i, 0)),
        compiler_params=_params("parallel"),
        name="spatial_gating",
    )(p, p, g_norm.reshape(1, SG_CH), w_s.astype(BF16), b_s.T)


ROUTER_TM = 512


def _router_kernel(x_ref, g_ref, w_ref, b_ref, o_ref, xn_ref):
    xn = _rms(x_ref[...], g_ref[...])
    xn_ref[...] = _pack_halves(xn)
    w = w_ref[...]
    xh, wh = xn.astype(BF16), w.astype(BF16)
    xl, wl = (xn - xh.astype(F32)).astype(BF16), (w - wh.astype(F32)).astype(BF16)
    hh_hl = jnp.dot(xh, jnp.concatenate([wh, wl], axis=1), preferred_element_type=F32)
    logits = (hh_hl[:, :LANES] + (hh_hl[:, LANES:] + jnp.dot(xl, wh, preferred_element_type=F32))
              + b_ref[...])
    lane = lax.broadcasted_iota(jnp.int32, logits.shape, 1).astype(F32)
    low = jnp.float32(-3.0e38)

    def first_max(vals):
        top = jnp.max(vals, axis=-1, keepdims=True)
        idx = jnp.min(jnp.where(vals == top, lane, float(LANES)), axis=-1, keepdims=True)
        return top, idx

    is_group = lane < N_GROUPS
    g_top, g_idx = first_max(jnp.where(is_group, logits, low))
    g_prob = 1.0 / jnp.sum(jnp.where(is_group, jnp.exp(logits - g_top), 0.0), axis=-1, keepdims=True)
    lo = N_GROUPS + g_idx * EXPERTS_PER_GROUP
    e_vals = jnp.where(lane >= lo, jnp.where(lane < lo + EXPERTS_PER_GROUP, logits, low), low)
    v1, i1 = first_max(e_vals)
    v2, i2 = first_max(jnp.where(lane == i1, low, e_vals))
    e21 = jnp.exp(v2 - v1)
    w1 = g_prob / (1.0 + e21)
    w2 = g_prob * e21 / (1.0 + e21)
    out = jnp.where(lane == 0, i1 - N_GROUPS,
                    jnp.where(lane == 1, i2 - N_GROUPS,
                              jnp.where(lane == 2, w1, jnp.where(lane == 3, w2, 0.0))))
    o_ref[...] = out


def _router(x, g, w, b):
    tm = ROUTER_TM
    return pl.pallas_call(
        _router_kernel,
        out_shape=(jax.ShapeDtypeStruct((N_TOK, LANES), F32), jax.ShapeDtypeStruct((N_TOK, HALF), jnp.uint32)),
        grid=(N_TOK // tm,),
        in_specs=[
            pl.BlockSpec((tm, D_MODEL), lambda i: (i, 0)),
            pl.BlockSpec((1, D_MODEL), lambda i: (0, 0)),
            pl.BlockSpec((D_MODEL, LANES), lambda i: (0, 0)),
            pl.BlockSpec((1, LANES), lambda i: (0, 0)),
        ],
        out_specs=(pl.BlockSpec((tm, LANES), lambda i: (i, 0)), pl.BlockSpec((tm, HALF), lambda i: (i, 0))),
        compiler_params=_params("parallel"),
        name="router",
    )(x, g.reshape(1, D_MODEL), w, b)


def _dispatch_plan(eid):
    tok = jnp.arange(N_TOK, dtype=jnp.int32)
    keys = jnp.concatenate([eid[:, k] * N_ASSIGN + (k * N_TOK + tok) for k in range(TOP_K)])
    row_asg = jnp.sort(keys) & (N_ASSIGN - 1)
    row_asg = jnp.concatenate([row_asg, jnp.zeros((SUBLANES,), jnp.int32)])
    counts = jnp.sum(eid.reshape(-1)[:, None] == jnp.arange(N_EXPERTS, dtype=jnp.int32)[None, :], axis=0,
                     dtype=jnp.int32)
    starts = jnp.cumsum(counts) - counts
    nblk_e = (counts + MOE_BLOCK - 1) // MOE_BLOCK
    blk_end = jnp.cumsum(nblk_e)
    blk_start = blk_end - nblk_e
    n_active = blk_end[-1]
    bidx = jnp.arange(MOE_NBLK, dtype=jnp.int32)
    blk_x = jnp.minimum(bidx, n_active - 1)
    blk_e = jnp.minimum(jnp.sum(blk_end[None, :] <= blk_x[:, None], axis=1, dtype=jnp.int32), N_EXPERTS - 1)
    first = (blk_x - blk_start[blk_e]) * MOE_BLOCK
    blk_rows = jnp.where(bidx < n_active, jnp.clip(counts[blk_e] - first, 0, MOE_BLOCK), 0).astype(jnp.int32)
    blk_src = (starts[blk_e] + first).astype(jnp.int32)
    return row_asg, blk_e, blk_rows, blk_src


GATHER_STEP = min(1, N_FT - 1)
_ROW_BITS = tuple(1 << k for k in range(MOE_BLOCK.bit_length() - 1, -1, -1))


def _expert_kernel(row_asg, blk_e, blk_rows, blk_src, x_hbm, w1_ref, w3_ref, w2_ref, ys_hbm,
                   xbuf, acc, ypk, xn_ref, gsem, ssem):
    b = pl.program_id(0)
    j = pl.program_id(1)
    rows = blk_rows[b]
    slot = b % 2

    def n_fetch(blk):
        return (blk_rows[blk] + SUBLANES - 1) & -SUBLANES

    def issue_gather(blk, s):
        base = blk_src[blk]

        def group(q, c):
            for u in range(SUBLANES):
                t = row_asg[base + q * SUBLANES + u] & (N_TOK - 1)
                pltpu.make_async_copy(x_hbm.at[pl.ds(t, 1), :], xbuf.at[s, q, pl.ds(u, 1), :], gsem.at[s]).start()
            return c

        lax.fori_loop(0, n_fetch(blk) >> SUBLANE_SHIFT, group, 0)

    def wait_rows(n, copy_of):
        for bit in _ROW_BITS:
            @pl.when((n & bit) != 0)
            def _():
                copy_of(bit).wait()

    def gather_block(s):
        def copy_of(k):
            tiles = xbuf.at[s, pl.ds(0, k // SUBLANES)]
            return pltpu.make_async_copy(tiles, tiles, gsem.at[s])
        return copy_of

    def scatter_block(s):
        def copy_of(k):
            part = ypk.at[s, pl.ds(0, k // SUBLANES)] if k >= SUBLANES else ypk.at[s, 0, pl.ds(0, k), :]
            return pltpu.make_async_copy(part, part, ssem.at[s])
        return copy_of

    def issue_scatter(blk, s, n):
        base = blk_src[blk]

        def one(q, u, a):
            pltpu.make_async_copy(ypk.at[s, q, pl.ds(u, 1), :], ys_hbm.at[pl.ds(a, 1), :], ssem.at[s]).start()

        def group(q, c):
            for u in range(SUBLANES):
                one(q, u, row_asg[base + q * SUBLANES + u])
            return c

        def tail(i, c):
            one(i >> SUBLANE_SHIFT, i & (SUBLANES - 1), row_asg[base + i])
            return c

        full = n >> SUBLANE_SHIFT
        lax.fori_loop(0, full, group, 0)
        lax.fori_loop(full * SUBLANES, n, tail, 0)

    def for_row_tile(fn):
        lo = 0
        for m in MOE_ROW_TILES:
            @pl.when(jnp.logical_and(rows > lo, rows <= m))
            def _():
                fn(m)
            lo = m

    nxt = jnp.minimum(b + 1, MOE_NBLK - 1)
    has_next = jnp.logical_and(b + 1 < MOE_NBLK, blk_rows[nxt] > 0)

    @pl.when(rows > 0)
    def _():
        @pl.when(j == 0)
        def _():
            @pl.when(b == 0)
            def _():
                xbuf[...] = jnp.zeros(xbuf.shape, jnp.uint32)
                acc[...] = jnp.zeros(acc.shape, F32)
                issue_gather(0, 0)

            wait_rows(n_fetch(b), gather_block(slot))

            def unpack_tile(m):
                lo, hi = _unpack_halves(xbuf[slot, :m // SUBLANES].reshape(m, HALF))
                xn_ref[:m, :HALF] = lo.astype(BF16)
                xn_ref[:m, HALF:] = hi.astype(BF16)

            for_row_tile(unpack_tile)

        @pl.when(jnp.logical_and(j == GATHER_STEP, has_next))
        def _():
            issue_gather(nxt, 1 - slot)

        def ffn_tile(m):
            xs = xn_ref[:m, :]
            h1 = jnp.dot(xs, w1_ref[...].astype(BF16), preferred_element_type=F32)
            h3 = jnp.dot(xs, w3_ref[...].astype(BF16), preferred_element_type=F32)
            h = (jax.nn.silu(h1) * h3).astype(BF16)
            y = jnp.dot(h, w2_ref[...].astype(BF16), preferred_element_type=F32)
            acc[:m, :] = jnp.where(j > 0, acc[:m, :], 0.0) + y

        for_row_tile(ffn_tile)

        @pl.when(j == N_FT - 1)
        def _():
            @pl.when(b > 0)
            def _():
                wait_rows(blk_rows[jnp.maximum(b - 1, 0)], scatter_block(1 - slot))

            def pack_tile(m):
                ypk[slot, :m // SUBLANES] = _pack_halves(acc[:m, :]).reshape(m // SUBLANES, SUBLANES, HALF)

            for_row_tile(pack_tile)

            issue_scatter(b, slot, rows)

            @pl.when(jnp.logical_not(has_next))
            def _():
                wait_rows(rows, scatter_block(slot))


def _moe_experts(xn, w1, w3, w2, layer, row_asg, blk_e, blk_rows, blk_src):
    def w_col(b, j, ra, be, br, bs):
        return (layer, be[b], 0, jnp.where(br[b] > 0, j, N_FT - 1))

    def w_row(b, j, ra, be, br, bs):
        return (layer, be[b], jnp.where(br[b] > 0, j, N_FT - 1), 0)

    return pl.pallas_call(
        _expert_kernel,
        out_shape=jax.ShapeDtypeStruct((N_ASSIGN, HALF), jnp.uint32),
        grid_spec=pltpu.PrefetchScalarGridSpec(
            num_scalar_prefetch=4,
            grid=(MOE_NBLK, N_FT),
            in_specs=[
                pl.BlockSpec(memory_space=pl.ANY),
                pl.BlockSpec((None, None, D_MODEL, MOE_FT), w_col),
                pl.BlockSpec((None, None, D_MODEL, MOE_FT), w_col),
                pl.BlockSpec((None, None, MOE_FT, D_MODEL), w_row),
            ],
            out_specs=pl.BlockSpec(memory_space=pl.ANY),
            scratch_shapes=[
                pltpu.VMEM((2, MOE_BLOCK // SUBLANES, SUBLANES, HALF), jnp.uint32),
                pltpu.VMEM((MOE_BLOCK, D_MODEL), F32),
                pltpu.VMEM((2, MOE_BLOCK // SUBLANES, SUBLANES, HALF), jnp.uint32),
                pltpu.VMEM((MOE_BLOCK, D_MODEL), BF16),
                pltpu.SemaphoreType.DMA((2,)),
                pltpu.SemaphoreType.DMA((2,)),
            ],
        ),
        compiler_params=_params("arbitrary", "arbitrary"),
        name="moe_experts",
    )(row_asg, blk_e, blk_rows, blk_src, xn, w1, w3, w2)


def _combine_kernel(final_norm, x_ref, y0_ref, y1_ref, gate_ref, gf_ref, o_ref):
    gates = gate_ref[...]
    g0, g1 = gates[:, 2:3], gates[:, 3:4]
    lo0, hi0 = _unpack_halves(y0_ref[...])
    lo1, hi1 = _unpack_halves(y1_ref[...])
    lo = x_ref[:, :HALF] + (g0 * lo0 + g1 * lo1)
    hi = x_ref[:, HALF:] + (g0 * hi0 + g1 * hi1)
    if final_norm:
        ms = (jnp.sum(lo * lo, axis=-1, keepdims=True) + jnp.sum(hi * hi, axis=-1, keepdims=True)) / D_MODEL
        r = lax.rsqrt(ms + EPS)
        lo, hi = lo * r * gf_ref[:, :HALF], hi * r * gf_ref[:, HALF:]
    o_ref[:, :HALF] = lo
    o_ref[:, HALF:] = hi


def _moe_combine(x, ys, route, g_final, final_norm):
    tm = COMBINE_TOK
    return pl.pallas_call(
        functools.partial(_combine_kernel, final_norm),
        out_shape=jax.ShapeDtypeStruct((N_TOK, D_MODEL), F32),
        grid=(N_TOK // tm,),
        in_specs=[
            pl.BlockSpec((tm, D_MODEL), lambda i: (i, 0)),
            pl.BlockSpec((tm, HALF), lambda i: (i, 0)),
            pl.BlockSpec((tm, HALF), lambda i: (N_TOK // tm + i, 0)),
            pl.BlockSpec((tm, LANES), lambda i: (i, 0)),
            pl.BlockSpec((1, D_MODEL), lambda i: (0, 0)),
        ],
        out_specs=pl.BlockSpec((tm, D_MODEL), lambda i: (i, 0)),
        compiler_params=_params("parallel"),
        name="moe_combine",
    )(x, ys, ys, route, g_final.reshape(1, D_MODEL))


def _hier_moe(x, g, wg, bg, we, be, w1, w3, w2, layer, g_final, final_norm):
    pad = LANES - N_GROUPS - N_EXPERTS
    w_r = jnp.concatenate([wg, we, jnp.zeros((D_MODEL, pad), F32)], axis=1)
    b_r = jnp.concatenate([bg, be, jnp.zeros((pad,), F32)]).reshape(1, LANES)
    route, xn = _router(x, g, w_r, b_r)
    eid = route[:, :TOP_K].astype(jnp.int32)
    row_asg, blk_e, blk_rows, blk_src = _dispatch_plan(eid)
    ys = _moe_experts(xn, w1, w3, w2, layer, row_asg, blk_e, blk_rows, blk_src)
    return _moe_combine(x, ys, route, g_final, final_norm)


def kernel(x, norm_mix, norm_ffn, norm_final, w_in_ab, na_rpb, conv_w, conv_b, w_out_ab, w_in_cd,
           q_norm, kv_norm, w_uq, w_ukv, sg_norm, sg_w, sg_b, w_out_cd, router_group_w,
           router_group_b, router_expert_w, router_expert_b, w1, w3, w2):
    xt = x.reshape(N_TOK, D_MODEL)
    for layer in range(DEPTH):
        i = layer // 2
        if layer % 2 == 0:
            q_scale = np.where(np.arange(AB_IN) < NA_WIDTH, NA_Q_SCALE, 1.0).astype(np.float32)
            p = _norm_matmul(xt, norm_mix[layer], (w_in_ab[i] * q_scale).astype(BF16), tm=1024, tn=1536)
            a_out = _natten(p, _natten_bias_table(na_rpb[i]))
            b_out = _gated_conv(p, conv_w[i], conv_b[i])
            xt = _out_proj(a_out, b_out, w_out_ab[i].astype(BF16), xt)
        else:
            p, k_rope = _cd_in_proj(xt, norm_mix[layer], w_in_cd[i])
            wq = w_uq[i].reshape(Q_LORA, MLA_HEADS, QK_DIM)
            wq = jnp.concatenate([wq[:, :, :QK_NOPE].reshape(Q_LORA, NOPE_ALL),
                                  wq[:, :, QK_NOPE:].reshape(Q_LORA, ROPE_ALL)], axis=1).astype(BF16)
            q, k, v = _mla_prep(p, k_rope, q_norm[i], kv_norm[i], wq, w_ukv[i].astype(BF16))
            c_out = _mla_attn(q, k, v)
            d_out = _spatial_gating(p, sg_norm[i], sg_w[i], sg_b[i])
            xt = _out_proj(c_out, d_out, w_out_cd[i].astype(BF16), xt)
        xt = _hier_moe(xt, norm_ffn[layer], router_group_w[layer], router_group_b[layer],
                       router_expert_w[layer], router_expert_b[layer], w1, w3, w2, layer,
                       norm_final, final_norm=(layer == DEPTH - 1))
    return xt.reshape(BATCH, SEQ, D_MODEL)
```

```python
import functools

import numpy as np
import jax
import jax.numpy as jnp
from jax import lax
from jax.experimental import pallas as pl
from jax.experimental.pallas import tpu as pltpu

F32 = jnp.float32
BF16 = jnp.bfloat16

D_MODEL = 2048
BATCH = 2
SEQ = 4096
N_TOK = BATCH * SEQ
DEPTH = 2
GRID_W = 64
GRID_ROWS = SEQ // GRID_W
WIN_ROWS = 8
WIN_COLS = 16
NA_HEADS = 8
NA_HEAD_DIM = 128
NA_WIDTH = NA_HEADS * NA_HEAD_DIM
NA_KEYS = WIN_ROWS * GRID_W
CONV_CH = D_MODEL // 2
MLA_HEADS = 8
Q_LORA = 512
KV_LORA = 512
QK_NOPE = 128
QK_ROPE = 64
QK_DIM = QK_NOPE + QK_ROPE
V_DIM = 128
ROPE_THETA = 10000.0
SG_CH = D_MODEL // 2
SG_GROUPS = 8
CHUNK = 128
N_GROUPS = 8
EXPERTS_PER_GROUP = 8
N_EXPERTS = N_GROUPS * EXPERTS_PER_GROUP
TOP_K = 2
D_EXPERT = 768
EPS = 1e-6
NEG_INF = -1e30
LOG2_E = 1.4426950408889634
AB_IN = 3 * NA_WIDTH + 3 * CONV_CH

LANES = 128
SUBLANES = 8
SUBLANE_SHIFT = 3
MOE_BLOCK = 512
MOE_ROW_TILES = (256, 288, 320, 384, 512)
MOE_FT = 256
N_FT = D_EXPERT // MOE_FT
N_ASSIGN = N_TOK * TOP_K
MOE_NBLK = N_ASSIGN // MOE_BLOCK + N_EXPERTS
COMBINE_TOK = 512
VMEM_LIMIT = 52 * 1024 * 1024


def _params(*sem):
    return pltpu.CompilerParams(dimension_semantics=sem, vmem_limit_bytes=VMEM_LIMIT)


def _rms(x, g):
    return x * lax.rsqrt(jnp.mean(x * x, axis=-1, keepdims=True) + EPS) * g


HALF = D_MODEL // 2
BF16_BITS = 16
HIGH_HALF = 0xFFFF0000


def _pack_halves(x):
    lo = pltpu.bitcast(x[:, :HALF].astype(BF16).astype(F32), jnp.uint32)
    hi = pltpu.bitcast(x[:, HALF:].astype(BF16).astype(F32), jnp.uint32)
    return (lo >> BF16_BITS) | (hi & jnp.uint32(HIGH_HALF))


def _unpack_halves(w):
    return pltpu.bitcast(w << BF16_BITS, F32), pltpu.bitcast(w & jnp.uint32(HIGH_HALF), F32)


def _norm_mm_kernel(x_ref, g_ref, w_ref, o_ref, xn_ref):
    @pl.when(pl.program_id(1) == 0)
    def _():
        xn_ref[...] = _rms(x_ref[...].astype(F32), g_ref[...]).astype(BF16)

    o_ref[...] = jnp.dot(xn_ref[...], w_ref[...], preferred_element_type=F32).astype(o_ref.dtype)


def _norm_matmul(x, g, w, tm, tn):
    m, k = x.shape
    nout = w.shape[1]
    return pl.pallas_call(
        _norm_mm_kernel,
        out_shape=jax.ShapeDtypeStruct((m, nout), BF16),
        grid=(m // tm, nout // tn),
        in_specs=[
            pl.BlockSpec((tm, k), lambda i, j: (i, 0)),
            pl.BlockSpec((1, k), lambda i, j: (0, 0)),
            pl.BlockSpec((k, tn), lambda i, j: (0, j)),
        ],
        out_specs=pl.BlockSpec((tm, tn), lambda i, j: (i, j)),
        scratch_shapes=[pltpu.VMEM((tm, k), BF16)],
        compiler_params=_params("parallel", "arbitrary"),
        name="norm_matmul",
    )(x, g.reshape(1, k), w)


CD_TN = 1024
CD_MAIN = Q_LORA + KV_LORA + 2 * SG_CH
CD_A_TILES = (Q_LORA + KV_LORA) // CD_TN
CD_MAIN_TILES = CD_MAIN // CD_TN


def _cd_proj_kernel(x_ref, g_ref, wa_ref, wb_ref, wc_ref, main_ref, kr_ref, xn_ref):
    j = pl.program_id(1)

    @pl.when(j == 0)
    def _():
        xn_ref[...] = _rms(x_ref[...], g_ref[...]).astype(BF16)

    @pl.when(j < CD_A_TILES)
    def _():
        main_ref[...] = jnp.dot(xn_ref[...], wa_ref[...], preferred_element_type=F32).astype(BF16)

    @pl.when(j >= CD_A_TILES)
    def _():
        main_ref[...] = jnp.dot(xn_ref[...], wb_ref[...], preferred_element_type=F32).astype(BF16)

    @pl.when(j == 0)
    def _():
        kr_ref[...] = jnp.dot(xn_ref[...], wc_ref[...], preferred_element_type=F32).astype(BF16)


def _cd_in_proj(x, g, w, tm=1024):
    c0, c1 = Q_LORA + KV_LORA, Q_LORA + KV_LORA + QK_ROPE
    wa = w[:, :c0].astype(BF16)
    wb = w[:, c1:].astype(BF16)
    wc = jnp.pad(w[:, c0:c1], ((0, 0), (0, LANES - QK_ROPE))).astype(BF16)
    return pl.pallas_call(
        _cd_proj_kernel,
        out_shape=(jax.ShapeDtypeStruct((N_TOK, CD_MAIN), BF16), jax.ShapeDtypeStruct((N_TOK, LANES), BF16)),
        grid=(N_TOK // tm, CD_MAIN_TILES),
        in_specs=[
            pl.BlockSpec((tm, D_MODEL), lambda i, j: (i, 0)),
            pl.BlockSpec((1, D_MODEL), lambda i, j: (0, 0)),
            pl.BlockSpec((D_MODEL, CD_TN), lambda i, j: (0, jnp.minimum(j, CD_A_TILES - 1))),
            pl.BlockSpec((D_MODEL, CD_TN), lambda i, j: (0, jnp.maximum(j - CD_A_TILES, 0))),
            pl.BlockSpec((D_MODEL, LANES), lambda i, j: (0, 0)),
        ],
        out_specs=(
            pl.BlockSpec((tm, CD_TN), lambda i, j: (i, j)),
            pl.BlockSpec((tm, LANES), lambda i, j: (i, 0)),
        ),
        scratch_shapes=[pltpu.VMEM((tm, D_MODEL), BF16)],
        compiler_params=_params("parallel", "arbitrary"),
        name="cd_in_proj",
    )(x, g.reshape(1, D_MODEL), wa, wb, wc)


def _natten_bias_table(rpb):
    c = np.arange(GRID_W)
    col_start = np.clip(c - WIN_COLS // 2, 0, GRID_W - WIN_COLS)
    valid = (c[None, :] >= col_start[:, None]) & (c[None, :] < col_start[:, None] + WIN_COLS)
    dc = np.clip(c[None, :] - c[:, None] + WIN_COLS - 1, 0, 2 * WIN_COLS - 2)
    pick = (dc[:, :, None] == np.arange(2 * WIN_COLS - 1)).astype(np.float32)
    m = jnp.einsum('hrd,ckd->hcrk', rpb * LOG2_E, jnp.asarray(pick), precision=lax.Precision.HIGHEST)
    m = jnp.where(valid[None, :, None, :], m, NEG_INF)
    t = jnp.stack([m[:, :, o:o + WIN_ROWS] for o in range(WIN_ROWS)], axis=1)
    return t.reshape(rpb.shape[0], WIN_ROWS, GRID_W, NA_KEYS).astype(F32)


NA_ROWS_PER_STEP = 32
NA_Q_SCALE = NA_HEAD_DIM ** -0.5 * LOG2_E


def _natten_kernel(q_ref, k_ref, v_ref, t_ref, o_ref):
    def rows(i, carry):
        rs = [i * NA_ROWS_PER_STEP + u for u in range(NA_ROWS_PER_STEP)]
        kr0s = [jnp.clip(r - WIN_ROWS // 2, 0, GRID_ROWS - WIN_ROWS) for r in rs]
        q0s = [pl.multiple_of(r * GRID_W, GRID_W) for r in rs]
        k0s = [pl.multiple_of(kr0 * GRID_W, GRID_W) for kr0 in kr0s]
        ss = [lax.dot_general(q_ref[pl.ds(q0, GRID_W), :], k_ref[pl.ds(k0, NA_KEYS), :],
                              (((1,), (1,)), ((), ())), preferred_element_type=F32)
              for q0, k0 in zip(q0s, k0s)]
        ps, ls = [], []
        for s, r, kr0 in zip(ss, rs, kr0s):
            s = s + t_ref[kr0 - r + WIN_ROWS - 1]
            p = jnp.exp2(s - jnp.max(s, axis=-1, keepdims=True))
            ls.append(jnp.sum(p, axis=-1, keepdims=True))
            ps.append(p.astype(BF16))
        for p, l, q0, k0 in zip(ps, ls, q0s, k0s):
            o = jnp.dot(p, v_ref[pl.ds(k0, NA_KEYS), :], preferred_element_type=F32) / l
            o_ref[pl.ds(q0, GRID_W), :] = o.astype(o_ref.dtype)
        return carry

    lax.fori_loop(0, GRID_ROWS // NA_ROWS_PER_STEP, rows, 0)


def _natten(p, table):
    blk = (SEQ, NA_HEAD_DIM)
    return pl.pallas_call(
        _natten_kernel,
        out_shape=jax.ShapeDtypeStruct((N_TOK, NA_WIDTH), BF16),
        grid=(BATCH, NA_HEADS),
        in_specs=[
            pl.BlockSpec(blk, lambda b, h: (b, h)),
            pl.BlockSpec(blk, lambda b, h: (b, NA_HEADS + h)),
            pl.BlockSpec(blk, lambda b, h: (b, 2 * NA_HEADS + h)),
            pl.BlockSpec((None, WIN_ROWS, GRID_W, NA_KEYS), lambda b, h: (h, 0, 0, 0)),
        ],
        out_specs=pl.BlockSpec(blk, lambda b, h: (b, h)),
        compiler_params=_params("parallel", "parallel"),
        name="natten",
    )(p, p, p, table)


CONV_CB = 256


def _conv_kernel(gb_ref, gc_ref, hc_ref, w_ref, b_ref, o_ref):
    z = gc_ref[...].astype(F32) * hc_ref[...].astype(F32)
    pos = lax.broadcasted_iota(jnp.int32, z.shape, 0)
    z_prev = jnp.where(pos == 0, 0.0, pltpu.roll(z, 1, 0))
    z_next = jnp.where(pos == SEQ - 1, 0.0, pltpu.roll(z, SEQ - 1, 0))
    y = b_ref[...] + z_prev * w_ref[0:1, :] + z * w_ref[1:2, :] + z_next * w_ref[2:3, :]
    o_ref[...] = (gb_ref[...].astype(F32) * y).astype(o_ref.dtype)


def _gated_conv(p, w, b):
    base = 3 * NA_WIDTH // CONV_CB
    step = CONV_CH // CONV_CB
    blk = (SEQ, CONV_CB)
    return pl.pallas_call(
        _conv_kernel,
        out_shape=jax.ShapeDtypeStruct((N_TOK, CONV_CH), BF16),
        grid=(BATCH, step),
        in_specs=[
            pl.BlockSpec(blk, lambda bi, c: (bi, base + c)),
            pl.BlockSpec(blk, lambda bi, c: (bi, base + step + c)),
            pl.BlockSpec(blk, lambda bi, c: (bi, base + 2 * step + c)),
            pl.BlockSpec((3, CONV_CB), lambda bi, c: (0, c)),
            pl.BlockSpec((1, CONV_CB), lambda bi, c: (0, c)),
        ],
        out_specs=pl.BlockSpec(blk, lambda bi, c: (bi, c)),
        compiler_params=_params("parallel", "parallel"),
        name="gated_conv",
    )(p, p, p, w, b.reshape(1, CONV_CH))


def _out_proj_kernel(a_ref, b_ref, wa_ref, wb_ref, r_ref, o_ref):
    acc = jnp.dot(a_ref[...], wa_ref[...], preferred_element_type=F32)
    acc = acc + jnp.dot(b_ref[...], wb_ref[...], preferred_element_type=F32)
    o_ref[...] = r_ref[...] + acc


def _out_proj(a, b, w, res, tm=512):
    ka, kb = a.shape[1], b.shape[1]
    assert ka == kb
    return pl.pallas_call(
        _out_proj_kernel,
        out_shape=jax.ShapeDtypeStruct((N_TOK, D_MODEL), F32),
        grid=(N_TOK // tm,),
        in_specs=[
            pl.BlockSpec((tm, ka), lambda i: (i, 0)),
            pl.BlockSpec((tm, kb), lambda i: (i, 0)),
            pl.BlockSpec((ka, D_MODEL), lambda i: (0, 0)),
            pl.BlockSpec((kb, D_MODEL), lambda i: (1, 0)),
            pl.BlockSpec((tm, D_MODEL), lambda i: (i, 0)),
        ],
        out_specs=pl.BlockSpec((tm, D_MODEL), lambda i: (i, 0)),
        compiler_params=_params("parallel"),
        name="out_proj",
    )(a, b, w, w, res)


MLA_TM = 512


def _rope_tables():
    f32 = np.float32
    pos = np.arange(SEQ)
    row = (pos // GRID_W).astype(f32)
    col = (pos % GRID_W).astype(f32)
    half = QK_ROPE // 2
    inv = np.power(f32(ROPE_THETA), -np.arange(0, half, 2, dtype=f32) / f32(half)).astype(f32)
    ar, ac = row[:, None] * inv, col[:, None] * inv
    cos_t = jnp.asarray(np.concatenate([np.cos(ar), np.cos(ar), np.cos(ac), np.cos(ac)], axis=-1).astype(f32))
    sin_t = jnp.asarray(np.concatenate([-np.sin(ar), np.sin(ar), -np.sin(ac), np.sin(ac)], axis=-1).astype(f32))
    quarter = half // 2
    src = np.arange(QK_ROPE) + np.where((np.arange(QK_ROPE) // quarter) % 2 == 0, quarter, -quarter)
    swap = np.zeros((QK_ROPE, QK_ROPE), np.float32)
    swap[src, np.arange(QK_ROPE)] = 1.0
    swap_all = np.kron(np.eye(MLA_HEADS, dtype=np.float32), swap)
    return cos_t, sin_t, jnp.asarray(swap_all, BF16)


def _rope(x, cos_t, sin_t, swap):
    hi = x.astype(BF16)
    lo = (x - hi.astype(F32)).astype(BF16)
    xs = jnp.dot(hi, swap, preferred_element_type=F32) + jnp.dot(lo, swap, preferred_element_type=F32)
    return x * cos_t + xs * sin_t


ROPE_ALL = MLA_HEADS * QK_ROPE
NOPE_ALL = MLA_HEADS * QK_NOPE


def _mla_prep_kernel(cq_ref, ckv_ref, kr_ref, gq_ref, gkv_ref, wq_ref, wkv_ref, cos_ref, sin_ref,
                     swap_ref, q_ref, k_ref, v_ref):
    scale = QK_DIM ** -0.5 * LOG2_E
    cos_t, sin_t, swap = cos_ref[...], sin_ref[...], swap_ref[...]
    cqn = _rms(cq_ref[...].astype(F32), gq_ref[...]).astype(BF16)
    ckvn = _rms(ckv_ref[...].astype(F32), gkv_ref[...]).astype(BF16)
    kr = kr_ref[:, :QK_ROPE].astype(F32)
    krot = _rope(kr, cos_t, sin_t, swap[:QK_ROPE, :QK_ROPE]).astype(BF16)
    q = jnp.dot(cqn, wq_ref[...], preferred_element_type=F32)
    qn = (q[:, :NOPE_ALL] * scale).astype(BF16)
    cos_all = jnp.concatenate([cos_t] * MLA_HEADS, axis=1)
    sin_all = jnp.concatenate([sin_t] * MLA_HEADS, axis=1)
    qr = (_rope(q[:, NOPE_ALL:], cos_all, sin_all, swap) * scale).astype(BF16)
    kv = jnp.dot(ckvn, wkv_ref[...], preferred_element_type=F32)
    hw = QK_NOPE + V_DIM
    for h in range(MLA_HEADS):
        q_ref[h, :, :QK_NOPE] = qn[:, h * QK_NOPE:(h + 1) * QK_NOPE]
        q_ref[h, :, QK_NOPE:] = qr[:, h * QK_ROPE:(h + 1) * QK_ROPE]
        k_ref[h, :, :QK_NOPE] = kv[:, h * hw:h * hw + QK_NOPE].astype(BF16)
        k_ref[h, :, QK_NOPE:] = krot
        v_ref[h, 0] = kv[:, h * hw + QK_NOPE:(h + 1) * hw].T.astype(BF16)


def _mla_prep(p, k_rope, q_norm, kv_norm, wq, wkv):
    cos_t, sin_t, swap = _rope_tables()
    tm = MLA_TM
    seq_blocks = SEQ // tm
    return pl.pallas_call(
        _mla_prep_kernel,
        out_shape=(
            jax.ShapeDtypeStruct((MLA_HEADS, N_TOK, QK_DIM), BF16),
            jax.ShapeDtypeStruct((MLA_HEADS, N_TOK, QK_DIM), BF16),
            jax.ShapeDtypeStruct((MLA_HEADS, N_TOK // tm, V_DIM, tm), BF16),
        ),
        grid=(N_TOK // tm,),
        in_specs=[
            pl.BlockSpec((tm, Q_LORA), lambda i: (i, 0)),
            pl.BlockSpec((tm, KV_LORA), lambda i: (i, 1)),
            pl.BlockSpec((tm, LANES), lambda i: (i, 0)),
            pl.BlockSpec((1, Q_LORA), lambda i: (0, 0)),
            pl.BlockSpec((1, KV_LORA), lambda i: (0, 0)),
            pl.BlockSpec((Q_LORA, NOPE_ALL + ROPE_ALL), lambda i: (0, 0)),
            pl.BlockSpec((KV_LORA, MLA_HEADS * (QK_NOPE + V_DIM)), lambda i: (0, 0)),
            pl.BlockSpec((tm, QK_ROPE), lambda i: (i % seq_blocks, 0)),
            pl.BlockSpec((tm, QK_ROPE), lambda i: (i % seq_blocks, 0)),
            pl.BlockSpec((ROPE_ALL, ROPE_ALL), lambda i: (0, 0)),
        ],
        out_specs=(
            pl.BlockSpec((MLA_HEADS, tm, QK_DIM), lambda i: (0, i, 0)),
            pl.BlockSpec((MLA_HEADS, tm, QK_DIM), lambda i: (0, i, 0)),
            pl.BlockSpec((MLA_HEADS, 1, V_DIM, tm), lambda i: (0, i, 0, 0)),
        ),
        compiler_params=_params("parallel"),
        name="mla_prep",
    )(p, p, k_rope, q_norm.reshape(1, Q_LORA), kv_norm.reshape(1, KV_LORA), wq, wkv, cos_t, sin_t, swap)


MLA_TQ = 4096
MLA_TK = 512
MLA_QH = 16
assert MLA_TK == MLA_TM


def _mla_attn_kernel(q_ref, k_ref, vt_ref, o_ref, s_ref, m_ref, l_ref, acc_ref):
    n_chunks = SEQ // MLA_TK
    cols = MLA_TQ // MLA_QH

    def scores(h, c):
        k0 = c * MLA_TK if isinstance(c, int) else pl.multiple_of(c * MLA_TK, MLA_TK)
        return lax.dot_general(k_ref[pl.ds(k0, MLA_TK), :], q_ref[h * cols:(h + 1) * cols, :],
                               (((1,), (1,)), ((), ())), preferred_element_type=F32)

    m_ref[...] = jnp.full(m_ref.shape, NEG_INF, F32)
    l_ref[...] = jnp.zeros(l_ref.shape, F32)
    acc_ref[...] = jnp.zeros(acc_ref.shape, F32)
    for h in range(MLA_QH):
        s_ref[0, h] = scores(h, 0)

    def step(c, slot, prefetch=True):
        for h in range(MLA_QH):
            if prefetch:
                s_ref[1 - slot, h] = scores(h, c + 1)
            s = s_ref[slot, h]
            m_old = m_ref[h]
            m_new = jnp.maximum(m_old, jnp.max(s, axis=0, keepdims=True))
            p = jnp.exp2(s - m_new)
            alpha = jnp.exp2(m_old - m_new)
            l_ref[h] = alpha * l_ref[h] + jnp.sum(p, axis=0, keepdims=True)
            pv = jnp.dot(vt_ref[c], p.astype(BF16), preferred_element_type=F32)
            acc_ref[h] = alpha * acc_ref[h] + pv
            m_ref[h] = m_new

    def pair(i, carry):
        step(2 * i, 0)
        step(2 * i + 1, 1)
        return carry

    lax.fori_loop(0, n_chunks // 2 - 1, pair, 0)
    step(n_chunks - 2, 0)
    step(n_chunks - 1, 1, prefetch=False)
    for h in range(MLA_QH):
        o_ref[h * cols:(h + 1) * cols, :] = (acc_ref[h] / l_ref[h]).T.astype(o_ref.dtype)


def _mla_attn(q, k, v):
    nq = SEQ // MLA_TQ
    return pl.pallas_call(
        _mla_attn_kernel,
        out_shape=jax.ShapeDtypeStruct((N_TOK, MLA_HEADS * V_DIM), BF16),
        grid=(BATCH, MLA_HEADS, nq),
        in_specs=[
            pl.BlockSpec((None, MLA_TQ, QK_DIM), lambda b, h, i: (h, b * nq + i, 0)),
            pl.BlockSpec((None, SEQ, QK_DIM), lambda b, h, i: (h, b, 0)),
            pl.BlockSpec((None, SEQ // MLA_TK, V_DIM, MLA_TK), lambda b, h, i: (h, b, 0, 0)),
        ],
        out_specs=pl.BlockSpec((MLA_TQ, V_DIM), lambda b, h, i: (b * nq + i, h)),
        scratch_shapes=[
            pltpu.VMEM((2, MLA_QH, MLA_TK, MLA_TQ // MLA_QH), F32),
            pltpu.VMEM((MLA_QH, 1, MLA_TQ // MLA_QH), F32),
            pltpu.VMEM((MLA_QH, 1, MLA_TQ // MLA_QH), F32),
            pltpu.VMEM((MLA_QH, V_DIM, MLA_TQ // MLA_QH), F32),
        ],
        compiler_params=_params("parallel", "parallel", "parallel"),
        name="mla_attn",
    )(q, k, v)


SG_TM = 1024


def _sg_kernel(u_ref, v_ref, g_ref, w_ref, bt_ref, o_ref):
    v = jax.nn.gelu(v_ref[...].astype(F32))
    vn = _rms(v, g_ref[...]).astype(BF16)
    u = jax.nn.gelu(u_ref[...].astype(F32))
    gc = SG_CH // SG_GROUPS
    for n in range(SG_TM // CHUNK):
        rows = slice(n * CHUNK, (n + 1) * CHUNK)
        for g in range(SG_GROUPS):
            cols = slice(g * gc, (g + 1) * gc)
            mixed = jnp.dot(w_ref[g], vn[rows, cols], preferred_element_type=F32) + bt_ref[:, g:g + 1]
            o_ref[rows, cols] = (u[rows, cols] * mixed).astype(o_ref.dtype)


def _spatial_gating(p, g_norm, w_s, b_s):
    u_blk = (Q_LORA + KV_LORA) // SG_CH
    return pl.pallas_call(
        _sg_kernel,
        out_shape=jax.ShapeDtypeStruct((N_TOK, SG_CH), BF16),
        grid=(N_TOK // SG_TM,),
        in_specs=[
            pl.BlockSpec((SG_TM, SG_CH), lambda i: (i, u_blk)),
            pl.BlockSpec((SG_TM, SG_CH), lambda i: (i, u_blk + 1)),
            pl.BlockSpec((1, SG_CH), lambda i: (0, 0)),
            pl.BlockSpec((SG_GROUPS, CHUNK, CHUNK), lambda i: (0, 0, 0)),
            pl.BlockSpec((CHUNK, SG_GROUPS), lambda i: (0, 0)),
        ],
        out_specs=pl.BlockSpec((SG_TM, SG_CH), lambda i: (i, 0)),
        compiler_params=_params("parallel"),
        name="spatial_gating",
    )(p, p, g_norm.reshape(1, SG_CH), w_s.astype(BF16), b_s.T)


ROUTER_TM = 1024


def _router_kernel(x_ref, g_ref, w_ref, b_ref, o_ref, xn_ref):
    xn = _rms(x_ref[...], g_ref[...])
    xn_ref[...] = _pack_halves(xn)
    w = w_ref[...]
    xh, wh = xn.astype(BF16), w.astype(BF16)
    xl, wl = (xn - xh.astype(F32)).astype(BF16), (w - wh.astype(F32)).astype(BF16)
    hh_hl = jnp.dot(xh, jnp.concatenate([wh, wl], axis=1), preferred_element_type=F32)
    logits = (hh_hl[:, :LANES] + (hh_hl[:, LANES:] + jnp.dot(xl, wh, preferred_element_type=F32))
              + b_ref[...])
    lane = lax.broadcasted_iota(jnp.int32, logits.shape, 1).astype(F32)
    low = jnp.float32(-3.0e38)

    def first_max(vals):
        top = jnp.max(vals, axis=-1, keepdims=True)
        idx = jnp.min(jnp.where(vals == top, lane, float(LANES)), axis=-1, keepdims=True)
        return top, idx

    is_group = lane < N_GROUPS
    g_top, g_idx = first_max(jnp.where(is_group, logits, low))
    g_prob = 1.0 / jnp.sum(jnp.where(is_group, jnp.exp(logits - g_top), 0.0), axis=-1, keepdims=True)
    lo = N_GROUPS + g_idx * EXPERTS_PER_GROUP
    e_vals = jnp.where(lane >= lo, jnp.where(lane < lo + EXPERTS_PER_GROUP, logits, low), low)
    v1, i1 = first_max(e_vals)
    v2, i2 = first_max(jnp.where(lane == i1, low, e_vals))
    e21 = jnp.exp(v2 - v1)
    w1 = g_prob / (1.0 + e21)
    w2 = g_prob * e21 / (1.0 + e21)
    out = jnp.where(lane == 0, i1 - N_GROUPS,
                    jnp.where(lane == 1, i2 - N_GROUPS,
                              jnp.where(lane == 2, w1, jnp.where(lane == 3, w2, 0.0))))
    o_ref[...] = out


def _router(x, g, w, b):
    tm = ROUTER_TM
    return pl.pallas_call(
        _router_kernel,
        out_shape=(jax.ShapeDtypeStruct((N_TOK, LANES), F32), jax.ShapeDtypeStruct((N_TOK, HALF), jnp.uint32)),
        grid=(N_TOK // tm,),
        in_specs=[
            pl.BlockSpec((tm, D_MODEL), lambda i: (i, 0)),
            pl.BlockSpec((1, D_MODEL), lambda i: (0, 0)),
            pl.BlockSpec((D_MODEL, LANES), lambda i: (0, 0)),
            pl.BlockSpec((1, LANES), lambda i: (0, 0)),
        ],
        out_specs=(pl.BlockSpec((tm, LANES), lambda i: (i, 0)), pl.BlockSpec((tm, HALF), lambda i: (i, 0))),
        compiler_params=_params("parallel"),
        name="router",
    )(x, g.reshape(1, D_MODEL), w, b)


def _dispatch_plan(eid):
    tok = jnp.arange(N_TOK, dtype=jnp.int32)
    keys = jnp.concatenate([eid[:, k] * N_ASSIGN + (k * N_TOK + tok) for k in range(TOP_K)])
    row_asg = jnp.sort(keys) & (N_ASSIGN - 1)
    row_asg = jnp.concatenate([row_asg, jnp.zeros((SUBLANES,), jnp.int32)])
    counts = jnp.sum(eid.reshape(-1)[:, None] == jnp.arange(N_EXPERTS, dtype=jnp.int32)[None, :], axis=0,
                     dtype=jnp.int32)
    starts = jnp.cumsum(counts) - counts
    nblk_e = (counts + MOE_BLOCK - 1) // MOE_BLOCK
    blk_end = jnp.cumsum(nblk_e)
    blk_start = blk_end - nblk_e
    n_active = blk_end[-1]
    bidx = jnp.arange(MOE_NBLK, dtype=jnp.int32)
    blk_x = jnp.minimum(bidx, n_active - 1)
    blk_e = jnp.minimum(jnp.sum(blk_end[None, :] <= blk_x[:, None], axis=1, dtype=jnp.int32), N_EXPERTS - 1)
    first = (blk_x - blk_start[blk_e]) * MOE_BLOCK
    blk_rows = jnp.where(bidx < n_active, jnp.clip(counts[blk_e] - first, 0, MOE_BLOCK), 0).astype(jnp.int32)
    blk_src = (starts[blk_e] + first).astype(jnp.int32)
    return row_asg, blk_e, blk_rows, blk_src


GATHER_STEP = min(1, N_FT - 1)
_ROW_BITS = tuple(1 << k for k in range(MOE_BLOCK.bit_length() - 1, -1, -1))


def _expert_kernel(row_asg, blk_e, blk_rows, blk_src, x_hbm, w1_ref, w3_ref, w2_ref, ys_hbm,
                   xbuf, acc, ypk, xn_ref, gsem, ssem):
    b = pl.program_id(0)
    j = pl.program_id(1)
    rows = blk_rows[b]
    slot = b % 2

    def n_fetch(blk):
        return (blk_rows[blk] + SUBLANES - 1) & -SUBLANES

    def issue_gather(blk, s):
        base = blk_src[blk]

        def group(q, c):
            for u in range(SUBLANES):
                t = row_asg[base + q * SUBLANES + u] & (N_TOK - 1)
                pltpu.make_async_copy(x_hbm.at[pl.ds(t, 1), :], xbuf.at[s, q, pl.ds(u, 1), :], gsem.at[s]).start()
            return c

        lax.fori_loop(0, n_fetch(blk) >> SUBLANE_SHIFT, group, 0)

    def wait_rows(n, copy_of):
        for bit in _ROW_BITS:
            @pl.when((n & bit) != 0)
            def _():
                copy_of(bit).wait()

    def gather_block(s):
        def copy_of(k):
            tiles = xbuf.at[s, pl.ds(0, k // SUBLANES)]
            return pltpu.make_async_copy(tiles, tiles, gsem.at[s])
        return copy_of

    def scatter_block(s):
        def copy_of(k):
            part = ypk.at[s, pl.ds(0, k // SUBLANES)] if k >= SUBLANES else ypk.at[s, 0, pl.ds(0, k), :]
            return pltpu.make_async_copy(part, part, ssem.at[s])
        return copy_of

    def issue_scatter(blk, s, n):
        base = blk_src[blk]

        def one(q, u, a):
            pltpu.make_async_copy(ypk.at[s, q, pl.ds(u, 1), :], ys_hbm.at[pl.ds(a, 1), :], ssem.at[s]).start()

        def group(q, c):
            for u in range(SUBLANES):
                one(q, u, row_asg[base + q * SUBLANES + u])
            return c

        def tail(i, c):
            one(i >> SUBLANE_SHIFT, i & (SUBLANES - 1), row_asg[base + i])
            return c

        full = n >> SUBLANE_SHIFT
        lax.fori_loop(0, full, group, 0)
        lax.fori_loop(full * SUBLANES, n, tail, 0)

    def for_row_tile(fn):
        lo = 0
        for m in MOE_ROW_TILES:
            @pl.when(jnp.logical_and(rows > lo, rows <= m))
            def _():
                fn(m)
            lo = m

    nxt = jnp.minimum(b + 1, MOE_NBLK - 1)
    has_next = jnp.logical_and(b + 1 < MOE_NBLK, blk_rows[nxt] > 0)

    @pl.when(rows > 0)
    def _():
        @pl.when(j == 0)
        def _():
            @pl.when(b == 0)
            def _():
                xbuf[...] = jnp.zeros(xbuf.shape, jnp.uint32)
                acc[...] = jnp.zeros(acc.shape, F32)
                issue_gather(0, 0)

            wait_rows(n_fetch(b), gather_block(slot))

            def unpack_tile(m):
                lo, hi = _unpack_halves(xbuf[slot, :m // SUBLANES].reshape(m, HALF))
                xn_ref[:m, :HALF] = lo.astype(BF16)
                xn_ref[:m, HALF:] = hi.astype(BF16)

            for_row_tile(unpack_tile)

        @pl.when(jnp.logical_and(j == GATHER_STEP, has_next))
        def _():
            issue_gather(nxt, 1 - slot)

        def ffn_tile(m):
            xs = xn_ref[:m, :]
            h1 = jnp.dot(xs, w1_ref[...].astype(BF16), preferred_element_type=F32)
            h3 = jnp.dot(xs, w3_ref[...].astype(BF16), preferred_element_type=F32)
            h = (jax.nn.silu(h1) * h3).astype(BF16)
            y = jnp.dot(h, w2_ref[...].astype(BF16), preferred_element_type=F32)
            acc[:m, :] = jnp.where(j > 0, acc[:m, :], 0.0) + y

        for_row_tile(ffn_tile)

        @pl.when(j == N_FT - 1)
        def _():
            @pl.when(b > 0)
            def _():
                wait_rows(blk_rows[jnp.maximum(b - 1, 0)], scatter_block(1 - slot))

            def pack_tile(m):
                ypk[slot, :m // SUBLANES] = _pack_halves(acc[:m, :]).reshape(m // SUBLANES, SUBLANES, HALF)

            for_row_tile(pack_tile)

            issue_scatter(b, slot, rows)

            @pl.when(jnp.logical_not(has_next))
            def _():
                wait_rows(rows, scatter_block(slot))


def _moe_experts(xn, w1, w3, w2, layer, row_asg, blk_e, blk_rows, blk_src):
    def w_col(b, j, ra, be, br, bs):
        return (layer, be[b], 0, jnp.where(br[b] > 0, j, N_FT - 1))

    def w_row(b, j, ra, be, br, bs):
        return (layer, be[b], jnp.where(br[b] > 0, j, N_FT - 1), 0)

    return pl.pallas_call(
        _expert_kernel,
        out_shape=jax.ShapeDtypeStruct((N_ASSIGN, HALF), jnp.uint32),
        grid_spec=pltpu.PrefetchScalarGridSpec(
            num_scalar_prefetch=4,
            grid=(MOE_NBLK, N_FT),
            in_specs=[
                pl.BlockSpec(memory_space=pl.ANY),
                pl.BlockSpec((None, None, D_MODEL, MOE_FT), w_col),
                pl.BlockSpec((None, None, D_MODEL, MOE_FT), w_col),
                pl.BlockSpec((None, None, MOE_FT, D_MODEL), w_row),
            ],
            out_specs=pl.BlockSpec(memory_space=pl.ANY),
            scratch_shapes=[
                pltpu.VMEM((2, MOE_BLOCK // SUBLANES, SUBLANES, HALF), jnp.uint32),
                pltpu.VMEM((MOE_BLOCK, D_MODEL), F32),
                pltpu.VMEM((2, MOE_BLOCK // SUBLANES, SUBLANES, HALF), jnp.uint32),
                pltpu.VMEM((MOE_BLOCK, D_MODEL), BF16),
                pltpu.SemaphoreType.DMA((2,)),
                pltpu.SemaphoreType.DMA((2,)),
            ],
        ),
        compiler_params=_params("arbitrary", "arbitrary"),
        name="moe_experts",
    )(row_asg, blk_e, blk_rows, blk_src, xn, w1, w3, w2)


def _combine_kernel(final_norm, x_ref, y0_ref, y1_ref, gate_ref, gf_ref, o_ref):
    gates = gate_ref[...]
    g0, g1 = gates[:, 2:3], gates[:, 3:4]
    lo0, hi0 = _unpack_halves(y0_ref[...])
    lo1, hi1 = _unpack_halves(y1_ref[...])
    lo = x_ref[:, :HALF] + (g0 * lo0 + g1 * lo1)
    hi = x_ref[:, HALF:] + (g0 * hi0 + g1 * hi1)
    if final_norm:
        ms = (jnp.sum(lo * lo, axis=-1, keepdims=True) + jnp.sum(hi * hi, axis=-1, keepdims=True)) / D_MODEL
        r = lax.rsqrt(ms + EPS)
        lo, hi = lo * r * gf_ref[:, :HALF], hi * r * gf_ref[:, HALF:]
    o_ref[:, :HALF] = lo
    o_ref[:, HALF:] = hi


def _moe_combine(x, ys, route, g_final, final_norm):
    tm = COMBINE_TOK
    return pl.pallas_call(
        functools.partial(_combine_kernel, final_norm),
        out_shape=jax.ShapeDtypeStruct((N_TOK, D_MODEL), F32),
        grid=(N_TOK // tm,),
        in_specs=[
            pl.BlockSpec((tm, D_MODEL), lambda i: (i, 0)),
            pl.BlockSpec((tm, HALF), lambda i: (i, 0)),
            pl.BlockSpec((tm, HALF), lambda i: (N_TOK // tm + i, 0)),
            pl.BlockSpec((tm, LANES), lambda i: (i, 0)),
            pl.BlockSpec((1, D_MODEL), lambda i: (0, 0)),
        ],
        out_specs=pl.BlockSpec((tm, D_MODEL), lambda i: (i, 0)),
        compiler_params=_params("parallel"),
        name="moe_combine",
    )(x, ys, ys, route, g_final.reshape(1, D_MODEL))


def _hier_moe(x, g, wg, bg, we, be, w1, w3, w2, layer, g_final, final_norm):
    pad = LANES - N_GROUPS - N_EXPERTS
    w_r = jnp.concatenate([wg, we, jnp.zeros((D_MODEL, pad), F32)], axis=1)
    b_r = jnp.concatenate([bg, be, jnp.zeros((pad,), F32)]).reshape(1, LANES)
    route, xn = _router(x, g, w_r, b_r)
    eid = route[:, :TOP_K].astype(jnp.int32)
    row_asg, blk_e, blk_rows, blk_src = _dispatch_plan(eid)
    ys = _moe_experts(xn, w1, w3, w2, layer, row_asg, blk_e, blk_rows, blk_src)
    return _moe_combine(x, ys, route, g_final, final_norm)


def kernel(x, norm_mix, norm_ffn, norm_final, w_in_ab, na_rpb, conv_w, conv_b, w_out_ab, w_in_cd,
           q_norm, kv_norm, w_uq, w_ukv, sg_norm, sg_w, sg_b, w_out_cd, router_group_w,
           router_group_b, router_expert_w, router_expert_b, w1, w3, w2):
    xt = x.reshape(N_TOK, D_MODEL)
    for layer in range(DEPTH):
        i = layer // 2
        if layer % 2 == 0:
            q_scale = np.where(np.arange(AB_IN) < NA_WIDTH, NA_Q_SCALE, 1.0).astype(np.float32)
            p = _norm_matmul(xt, norm_mix[layer], (w_in_ab[i] * q_scale).astype(BF16), tm=1024, tn=1536)
            a_out = _natten(p, _natten_bias_table(na_rpb[i]))
            b_out = _gated_conv(p, conv_w[i], conv_b[i])
            xt = _out_proj(a_out, b_out, w_out_ab[i].astype(BF16), xt)
        else:
            p, k_rope = _cd_in_proj(xt, norm_mix[layer], w_in_cd[i])
            wq = w_uq[i].reshape(Q_LORA, MLA_HEADS, QK_DIM)
            wq = jnp.concatenate([wq[:, :, :QK_NOPE].reshape(Q_LORA, NOPE_ALL),
                                  wq[:, :, QK_NOPE:].reshape(Q_LORA, ROPE_ALL)], axis=1).astype(BF16)
            q, k, v = _mla_prep(p, k_rope, q_norm[i], kv_norm[i], wq, w_ukv[i].astype(BF16))
            c_out = _mla_attn(q, k, v)
            d_out = _spatial_gating(p, sg_norm[i], sg_w[i], sg_b[i])
            xt = _out_proj(c_out, d_out, w_out_cd[i].astype(BF16), xt)
        xt = _hier_moe(xt, norm_ffn[layer], router_group_w[layer], router_group_b[layer],
                       router_expert_w[layer], router_expert_b[layer], w1, w3, w2, layer,
                       norm_final, final_norm=(layer == DEPTH - 1))
    return xt.reshape(BATCH, SEQ, D_MODEL)
```
